```python
import jax, jax.numpy as jnp
from jax import lax
import numpy as np

D_MODEL = 1024
BATCH = 4
SEQ = 4096
DEPTH = 1

D_MIX = D_MODEL
D_REC = D_MIX // 2
N_REC_BLOCKS = 8
REC_BLOCK = D_REC // N_REC_BLOCKS
CONV_WIDTH = 4
RGLRU_C = 8.0
D_ATT = D_MIX - D_REC
HEAD_DIM = 64
N_HEADS = D_ATT // HEAD_DIM
Q_BLOCK = 128
D_IN_PROJ = 2 * D_REC + 3 * D_ATT + N_HEADS
N_EXPERTS = 32
TOP_K = 4
D_EXPERT = D_MODEL
SWIGLU_LIMIT = 7.0
SWIGLU_ALPHA = 1.702
EXPERT_BLOCK = 128
EPS = 1e-6

kernel_name = "hybrid_rglru_fox_moe_layer"


def _rmsnorm(x, w):
    xf = x.astype(jnp.float32)
    y = xf * lax.rsqrt(jnp.mean(xf * xf, axis=-1, keepdims=True) + EPS)
    return (y * w.astype(jnp.float32)).astype(x.dtype)


def _rg_lru_group(u, gate, conv_w, conv_b, w_a, b_a, w_i, b_i, lam):
    B, S, _ = u.shape
    f32 = jnp.float32
    up = jnp.pad(u, ((0, 0), (CONV_WIDTH - 1, 0), (0, 0)))
    xc = conv_b + up[:, 0:S] * conv_w[0]
    for tap in range(1, CONV_WIDTH):
        xc = xc + up[:, tap:tap + S] * conv_w[tap]
    xb = xc.reshape(B, S, N_REC_BLOCKS, REC_BLOCK)
    r = jax.nn.sigmoid(jnp.einsum('bsgi,gij->bsgj', xb, w_a).reshape(B, S, D_REC) + b_a)
    i = jax.nn.sigmoid(jnp.einsum('bsgi,gij->bsgj', xb, w_i).reshape(B, S, D_REC) + b_i)
    log_a = -RGLRU_C * r.astype(f32) * jax.nn.softplus(-lam.astype(f32))
    a = jnp.exp(log_a)
    b = jnp.sqrt(-jnp.expm1(2.0 * log_a)) * (i * xc).astype(f32)

    def combine(left, right):
        a_l, b_l = left
        a_r, b_r = right
        return a_l * a_r, a_r * b_l + b_r

    _, h = lax.associative_scan(combine, (a, b), axis=1)
    return (h * jax.nn.gelu(gate.astype(f32))).astype(u.dtype)


def _forgetting_attention(q, k, v, f_logit, b_f, q_norm_w, k_norm_w):
    B, S, _ = q.shape
    f32 = jnp.float32
    q = _rmsnorm(q.reshape(B, S, N_HEADS, HEAD_DIM), q_norm_w).astype(f32)
    k = _rmsnorm(k.reshape(B, S, N_HEADS, HEAD_DIM), k_norm_w).astype(f32)
    v = v.reshape(B, S, N_HEADS, HEAD_DIM).astype(f32)
    log_f = jax.nn.log_sigmoid(f_logit.astype(f32) + b_f.astype(f32))
    c = jnp.cumsum(log_f, axis=1)
    c_k = jnp.transpose(c, (0, 2, 1))
    n_blocks = S // Q_BLOCK
    q_blocks = q.reshape(B, n_blocks, Q_BLOCK, N_HEADS, HEAD_DIM).transpose(1, 0, 2, 3, 4)
    c_blocks = c_k.reshape(B, N_HEADS, n_blocks, Q_BLOCK).transpose(2, 0, 1, 3)
    k_pos = jnp.arange(S)
    scale = HEAD_DIM ** -0.5

    def one_block(args):
        q_blk, c_q, blk = args
        logits = jnp.einsum('bqhd,bkhd->bhqk', q_blk, k) * scale + (c_q[..., :, None] - c_k[:, :, None, :])
        q_pos = blk * Q_BLOCK + jnp.arange(Q_BLOCK)
        causal = q_pos[:, None] >= k_pos[None, :]
        p = jax.nn.softmax(jnp.where(causal, logits, -jnp.inf), axis=-1)
        return jnp.einsum('bhqk,bkhd->bqhd', p, v)

    o = lax.map(one_block, (q_blocks, c_blocks, jnp.arange(n_blocks)))
    return o.transpose(1, 0, 2, 3, 4).reshape(B, S, D_ATT)


def _moe(h, w_router, b_router, w_gate_up, b_gate_up, w_down, b_down):
    B, S, D = h.shape
    f32 = jnp.float32
    T = B * S
    A = T * TOP_K
    xt = h.reshape(T, D)
    logits = (xt @ w_router + b_router).astype(f32)
    top_vals, top_idx = lax.top_k(logits, TOP_K)
    gates = jax.nn.softmax(top_vals, axis=-1)
    flat_e = top_idx.reshape(A)
    order = jnp.argsort(flat_e)
    sorted_e = flat_e[order]
    token = order // TOP_K
    gate_sorted = gates.reshape(A)[order]
    counts = jnp.bincount(flat_e, length=N_EXPERTS)
    group_start = jnp.cumsum(counts) - counts
    padded = (counts + EXPERT_BLOCK - 1) // EXPERT_BLOCK * EXPERT_BLOCK
    pad_end = jnp.cumsum(padded)
    pad_start = pad_end - padded
    dest = pad_start[sorted_e] + jnp.arange(A) - group_start[sorted_e]
    n_blocks = -(-A // EXPERT_BLOCK) + N_EXPERTS
    xs = jnp.zeros((n_blocks * EXPERT_BLOCK, D), xt.dtype).at[dest].set(xt[token])
    block_expert = jnp.minimum(
        jnp.searchsorted(pad_end, jnp.arange(n_blocks) * EXPERT_BLOCK, side='right'), N_EXPERTS - 1)

    def expert_block(args):
        x_blk, e = args
        gu = x_blk @ w_gate_up[e] + b_gate_up[e]
        g = jnp.minimum(gu[:, :D_EXPERT], SWIGLU_LIMIT)
        u = jnp.clip(gu[:, D_EXPERT:], -SWIGLU_LIMIT, SWIGLU_LIMIT)
        act = (u + 1.0) * (g * jax.nn.sigmoid(SWIGLU_ALPHA * g))
        return act @ w_down[e] + b_down[e]

    ys = lax.map(expert_block, (xs.reshape(n_blocks, EXPERT_BLOCK, D), block_expert))
    y = ys.reshape(-1, D)[dest].astype(f32) * gate_sorted[:, None]
    out = jax.ops.segment_sum(y, token, num_segments=T)
    return out.reshape(B, S, D).astype(h.dtype)


def setup_inputs(seed: int = 0) -> dict:
    key = jax.random.key(seed)
    ks = jax.random.split(key, 24)
    f32 = jnp.float32
    L = DEPTH

    def nrm(k, shape, scale):
        return jax.random.normal(k, shape, f32) * scale

    a0 = jax.random.uniform(ks[9], (L, D_REC), f32, 0.9, 0.999)
    sig = a0 ** (1.0 / RGLRU_C)
    rg_lambda = jnp.log(sig) - jnp.log1p(-sig)
    return {
        "x": nrm(ks[0], (BATCH, SEQ, D_MODEL), 1.0),
        "ln1_w": 1.0 + nrm(ks[1], (L, D_MODEL), 0.02),
        "w_in": nrm(ks[2], (L, D_MODEL, D_IN_PROJ), D_MODEL ** -0.5),
        "conv_w": nrm(ks[3], (L, CONV_WIDTH, D_REC), CONV_WIDTH ** -0.5),
        "conv_b": nrm(ks[4], (L, D_REC), 0.02),
        "w_rg_a": nrm(ks[5], (L, N_REC_BLOCKS, REC_BLOCK, REC_BLOCK), REC_BLOCK ** -0.5),
        "b_rg_a": nrm(ks[6], (L, D_REC), 0.02),
        "w_rg_i": nrm(ks[7], (L, N_REC_BLOCKS, REC_BLOCK, REC_BLOCK), REC_BLOCK ** -0.5),
        "b_rg_i": nrm(ks[8], (L, D_REC), 0.02),
        "rg_lambda": rg_lambda,
        "b_forget": jax.random.uniform(ks[10], (L, N_HEADS), f32, 1.0, 4.0),
        "q_norm_w": 1.0 + nrm(ks[11], (L, HEAD_DIM), 0.02),
        "k_norm_w": 1.0 + nrm(ks[12], (L, HEAD_DIM), 0.02),
        "rec_out_norm_w": 1.0 + nrm(ks[13], (L, D_REC), 0.02),
        "attn_out_norm_w": 1.0 + nrm(ks[14], (L, D_ATT), 0.02),
        "w_out": nrm(ks[15], (L, D_MIX, D_MODEL), D_MIX ** -0.5),
        "ln2_w": 1.0 + nrm(ks[16], (L, D_MODEL), 0.02),
        "w_router": nrm(ks[17], (L, D_MODEL, N_EXPERTS), D_MODEL ** -0.5),
        "b_router": nrm(ks[18], (L, N_EXPERTS), 0.01),
        "w_gate_up": nrm(ks[19], (L, N_EXPERTS, D_MODEL, 2 * D_EXPERT), D_MODEL ** -0.5),
        "b_gate_up": nrm(ks[20], (L, N_EXPERTS, 2 * D_EXPERT), 0.02),
        "w_down": nrm(ks[21], (L, N_EXPERTS, D_EXPERT, D_MODEL), D_EXPERT ** -0.5),
        "b_down": nrm(ks[22], (L, N_EXPERTS, D_MODEL), 0.02),
    }


def reference(x, ln1_w, w_in, conv_w, conv_b, w_rg_a, b_rg_a, w_rg_i, b_rg_i, rg_lambda, b_forget,
              q_norm_w, k_norm_w, rec_out_norm_w, attn_out_norm_w, w_out, ln2_w, w_router, b_router,
              w_gate_up, b_gate_up, w_down, b_down):
    o_q = 2 * D_REC
    o_k = o_q + D_ATT
    o_v = o_k + D_ATT
    o_f = o_v + D_ATT
    for l in range(DEPTH):
        h = _rmsnorm(x, ln1_w[l])
        proj = jnp.einsum('bsd,de->bse', h, w_in[l])
        u = proj[..., :D_REC]
        gate = proj[..., D_REC:o_q]
        q = proj[..., o_q:o_k]
        k = proj[..., o_k:o_v]
        v = proj[..., o_v:o_f]
        f_logit = proj[..., o_f:]
        rec = _rg_lru_group(u, gate, conv_w[l], conv_b[l], w_rg_a[l], b_rg_a[l],
                            w_rg_i[l], b_rg_i[l], rg_lambda[l])
        att = _forgetting_attention(q, k, v, f_logit, b_forget[l], q_norm_w[l], k_norm_w[l])
        mix = jnp.concatenate([_rmsnorm(rec, rec_out_norm_w[l]).astype(x.dtype),
                               _rmsnorm(att, attn_out_norm_w[l]).astype(x.dtype)], axis=-1)
        x = x + jnp.einsum('bse,ed->bsd', mix, w_out[l])
        x = x + _moe(_rmsnorm(x, ln2_w[l]), w_router[l], b_router[l], w_gate_up[l], b_gate_up[l],
                     w_down[l], b_down[l])
    return x
```

```python
import functools

import jax
import jax.numpy as jnp
from jax import lax
from jax.experimental import pallas as pl
from jax.experimental.pallas import tpu as pltpu

D_MODEL = 1024
D_REC = 512
N_REC_BLOCKS = 8
REC_BLOCK = 64
CONV_WIDTH = 4
RGLRU_C = 8.0
D_ATT = 512
HEAD_DIM = 64
N_HEADS = 8
N_EXPERTS = 32
TOP_K = 4
D_EXPERT = 1024
SWIGLU_LIMIT = 7.0
SWIGLU_ALPHA = 1.702
EPS = 1e-6

LANES = 128
SLAB = LANES
VMEM_LIMIT = 56 * 1024 * 1024

F32 = jnp.float32
BF16 = jnp.bfloat16

O_U = 0
O_G = O_U + D_REC
O_Q = O_G + D_REC
O_K = O_Q + N_HEADS * SLAB
O_V = O_K + N_HEADS * SLAB
O_F = O_V + N_HEADS * SLAB
N_PROJ = O_F + LANES

L_C = HEAD_DIM
L_ONE = HEAD_DIM + 3

TM_IN = 512
TS_REC = 512
TQ = 512
TK = 512
TM_OUT = 512
TM_E = 256
TM_C = 256


def _split3(v):
    hi = v.astype(BF16).astype(F32)
    r = v - hi
    mid = r.astype(BF16).astype(F32)
    lo = (r - mid).astype(BF16).astype(F32)
    return hi, mid, lo


def _dot(a, b):
    return jnp.dot(a, b, preferred_element_type=F32)


def _in_proj_kernel(x_ref, ln_ref, w_ref, bf_ref, qw_ref, kw_ref,
                    u_ref, g_ref, q_ref, k_ref, v_ref, carry_ref, *, tiles_per_seq):
    i = pl.program_id(0)
    tm = x_ref.shape[0]

    @pl.when(i % tiles_per_seq == 0)
    def _():
        carry_ref[...] = jnp.zeros_like(carry_ref)

    x = x_ref[...]
    ms = jnp.mean(x * x, axis=-1, keepdims=True)
    h = (x * lax.rsqrt(ms + EPS) * ln_ref[...]).astype(BF16)

    u_ref[...] = _dot(h, w_ref[:, O_U:O_U + D_REC])
    g_ref[...] = _dot(h, w_ref[:, O_G:O_G + D_REC])

    f = _dot(h, w_ref[:, O_F:O_F + LANES]) + bf_ref[...]
    logf = jnp.minimum(f, 0.0) - jnp.log1p(jnp.exp(-jnp.abs(f)))
    row = lax.broadcasted_iota(jnp.int32, (tm, tm), 0)
    col = lax.broadcasted_iota(jnp.int32, (tm, tm), 1)
    tri = (row >= col).astype(BF16)
    hi, mid, lo = _split3(logf)
    c = (_dot(tri, hi.astype(BF16)) + _dot(tri, mid.astype(BF16))
         + _dot(tri, lo.astype(BF16)) + carry_ref[...])
    carry_ref[...] = c[tm - 1:tm, :]

    lane = lax.broadcasted_iota(jnp.int32, (tm, SLAB), 1)
    q_all = _dot(h, w_ref[:, O_Q:O_Q + N_HEADS * SLAB])
    k_all = _dot(h, w_ref[:, O_K:O_K + N_HEADS * SLAB])
    v_all = _dot(h, w_ref[:, O_V:O_V + N_HEADS * SLAB])
    qw = qw_ref[...]
    kw = kw_ref[...]
    for hd in range(N_HEADS):
        sl = slice(hd * SLAB, (hd + 1) * SLAB)
        qs = q_all[:, sl]
        ks = k_all[:, sl]
        qn = qs * lax.rsqrt(jnp.sum(qs * qs, axis=-1, keepdims=True) * (1.0 / HEAD_DIM) + EPS) * qw
        kn = ks * lax.rsqrt(jnp.sum(ks * ks, axis=-1, keepdims=True) * (1.0 / HEAD_DIM) + EPS) * kw
        c_hi, c_mid, c_lo = _split3(c[:, hd:hd + 1])
        qp = jnp.where(lane == L_C, c_hi, qn)
        qp = jnp.where(lane == L_C + 1, c_mid, qp)
        qp = jnp.where(lane == L_C + 2, c_lo, qp)
        qp = jnp.where((lane >= L_ONE) & (lane < L_ONE + 3), 1.0, qp)
        kp = jnp.where((lane >= L_C) & (lane < L_C + 3), 1.0, kn)
        kp = jnp.where(lane == L_ONE, -c_hi, kp)
        kp = jnp.where(lane == L_ONE + 1, -c_mid, kp)
        kp = jnp.where(lane == L_ONE + 2, -c_lo, kp)
        q_ref[0, hd] = qp.astype(BF16)
        k_ref[0, hd] = kp.astype(BF16)
        v_ref[0, hd] = jnp.where(lane == HEAD_DIM, 1.0, v_all[:, sl]).astype(BF16)


def _in_proj(x2d, ln1_w, w_all, bf_pad, qw_slab, kw_slab, batch, seq):
    t = x2d.shape[0]
    tiles_per_seq = seq // TM_IN
    slab_shape = jax.ShapeDtypeStruct((batch, N_HEADS, seq, SLAB), BF16)
    slab_spec = pl.BlockSpec((1, N_HEADS, TM_IN, SLAB),
                             lambda i: (i // tiles_per_seq, 0, i % tiles_per_seq, 0))
    row_spec = pl.BlockSpec((TM_IN, D_REC), lambda i: (i, 0))
    const = lambda shape: pl.BlockSpec(shape, lambda i: (0,) * len(shape))
    return pl.pallas_call(
        functools.partial(_in_proj_kernel, tiles_per_seq=tiles_per_seq),
        out_shape=(jax.ShapeDtypeStruct((t, D_REC), F32), jax.ShapeDtypeStruct((t, D_REC), F32),
                   slab_shape, slab_shape, slab_shape),
        grid=(t // TM_IN,),
        in_specs=[pl.BlockSpec((TM_IN, D_MODEL), lambda i: (i, 0)),
                  const((1, D_MODEL)), const((D_MODEL, N_PROJ)), const((1, LANES)),
                  const((1, SLAB)), const((1, SLAB))],
        out_specs=(row_spec, row_spec, slab_spec, slab_spec, slab_spec),
        scratch_shapes=[pltpu.VMEM((1, LANES), F32)],
        compiler_params=pltpu.CompilerParams(dimension_semantics=("arbitrary",),
                                             vmem_limit_bytes=VMEM_LIMIT),
        name="in_proj",
    )(x2d, ln1_w, w_all, bf_pad, qw_slab, kw_slab)


def _rglru_kernel(u_ref, g_ref, cw_ref, cb_ref, wa_ref, ba_ref, wi_ref, bi_ref, lam_ref, nw_ref,
                  o_ref, ext_ref, a_ref, b_ref, h_ref, hcar_ref):
    s = pl.program_id(1)
    ts = u_ref.shape[1]
    pad = 8

    @pl.when(s == 0)
    def _():
        ext_ref[0:pad, :] = jnp.zeros((pad, D_REC), F32)
        hcar_ref[...] = jnp.zeros_like(hcar_ref)

    @pl.when(s != 0)
    def _():
        ext_ref[0:pad, :] = ext_ref[ts:ts + pad, :]

    u = u_ref[0]
    ext_ref[pad:pad + ts, :] = u
    xc = cb_ref[...] + u * cw_ref[CONV_WIDTH - 1:CONV_WIDTH, :]
    for tap in range(CONV_WIDTH - 1):
        d = CONV_WIDTH - 1 - tap
        xc = xc + ext_ref[pad - d:pad - d + ts, :] * cw_ref[tap:tap + 1, :]

    xb = xc.astype(BF16)
    r = jax.nn.sigmoid(_dot(xb, wa_ref[...]) + ba_ref[...])
    ig = jax.nn.sigmoid(_dot(xb, wi_ref[...]) + bi_ref[...])
    nlam = -lam_ref[...]
    softplus = jnp.maximum(nlam, 0.0) + jnp.log1p(jnp.exp(-jnp.abs(nlam)))
    log_a = -RGLRU_C * r * softplus
    a = jnp.exp(log_a)
    a_ref[...] = a
    b_ref[...] = jnp.sqrt(1.0 - jnp.exp(2.0 * log_a)) * (ig * xc)

    def step(t, h):
        h = a_ref[pl.ds(t, 1), :] * h + b_ref[pl.ds(t, 1), :]
        h_ref[pl.ds(t, 1), :] = h
        return h

    hcar_ref[...] = lax.fori_loop(0, ts, step, hcar_ref[...], unroll=8)

    rec = h_ref[...] * jax.nn.gelu(g_ref[0])
    ms = jnp.mean(rec * rec, axis=-1, keepdims=True)
    o_ref[0] = (rec * lax.rsqrt(ms + EPS) * nw_ref[...]).astype(BF16)


def _rglru(u, g, conv_w, conv_b, wa_bd, b_a, wi_bd, b_i, lam, norm_w):
    batch, seq, _ = u.shape
    seq_spec = pl.BlockSpec((1, TS_REC, D_REC), lambda b, s: (b, s, 0))
    const = lambda shape: pl.BlockSpec(shape, lambda b, s: (0,) * len(shape))
    return pl.pallas_call(
        _rglru_kernel,
        out_shape=jax.ShapeDtypeStruct((batch, seq, D_REC), BF16),
        grid=(batch, seq // TS_REC),
        in_specs=[seq_spec, seq_spec, const((CONV_WIDTH, D_REC)), const((1, D_REC)),
                  const((D_REC, D_REC)), const((1, D_REC)), const((D_REC, D_REC)), const((1, D_REC)),
                  const((1, D_REC)), const((1, D_REC))],
        out_specs=seq_spec,
        scratch_shapes=[pltpu.VMEM((TS_REC + 8, D_REC), F32), pltpu.VMEM((TS_REC, D_REC), F32),
                        pltpu.VMEM((TS_REC, D_REC), F32), pltpu.VMEM((TS_REC, D_REC), F32),
                        pltpu.VMEM((1, D_REC), F32)],
        compiler_params=pltpu.CompilerParams(dimension_semantics=("arbitrary", "arbitrary"),
                                             vmem_limit_bytes=VMEM_LIMIT),
        name="rglru",
    )(u, g, conv_w, conv_b, wa_bd, b_a, wi_bd, b_i, lam, norm_w)


def _attn_kernel(q_ref, k_ref, v_ref, o_ref):
    qi = pl.program_id(2)
    q = q_ref[0, 0]
    tq = q.shape[0]

    def step(j, carry, masked):
        m, acc = carry
        start = pl.multiple_of(j * TK, TK)
        kj = k_ref[0, 0, pl.ds(start, TK), :]
        vj = v_ref[0, 0, pl.ds(start, TK), :]
        s = lax.dot_general(q, kj, (((1,), (1,)), ((), ())), preferred_element_type=F32)
        if masked:
            row = lax.broadcasted_iota(jnp.int32, (tq, TK), 0)
            col = lax.broadcasted_iota(jnp.int32, (tq, TK), 1)
            s = jnp.where(row >= col, s, -jnp.inf)
        m_new = jnp.maximum(m, jnp.max(s, axis=-1, keepdims=True))
        p = jnp.exp(s - m_new)
        acc = jnp.exp(m - m_new) * acc + _dot(p.astype(BF16), vj)
        return m_new, acc

    init = (jnp.full((tq, 1), -jnp.inf, F32), jnp.zeros((tq, SLAB), F32))
    carry = lax.fori_loop(0, qi, lambda j, c: step(j, c, False), init)
    _, acc = step(qi, carry, True)
    lane = lax.broadcasted_iota(jnp.int32, (tq, SLAB), 1)
    o_ref[0] = jnp.where(lane < HEAD_DIM, acc / acc[:, HEAD_DIM:HEAD_DIM + 1], 0.0)


def _attention(qp, kp, vp):
    batch, _, seq, _ = qp.shape
    kv_spec = pl.BlockSpec((1, 1, seq, SLAB), lambda b, h, i: (b, h, 0, 0))
    return pl.pallas_call(
        _attn_kernel,
        out_shape=jax.ShapeDtypeStruct((batch, seq, N_HEADS * SLAB), F32),
        grid=(batch, N_HEADS, seq // TQ),
        in_specs=[pl.BlockSpec((1, 1, TQ, SLAB), lambda b, h, i: (b, h, i, 0)), kv_spec, kv_spec],
        out_specs=pl.BlockSpec((1, TQ, SLAB), lambda b, h, i: (b, i, h)),
        compiler_params=pltpu.CompilerParams(
            dimension_semantics=("arbitrary", "arbitrary", "arbitrary"),
            vmem_limit_bytes=VMEM_LIMIT),
        name="attn",
    )(qp, kp, vp)


def _out_proj_kernel(x_ref, rec_ref, att_ref, aw_ref, wr_ref, wa_ref, ln_ref, rwh_ref, rwl_ref, rb_ref,
                     x1_ref, h2_ref, meta_ref, cnt_ref, carry_ref):
    i = pl.program_id(0)
    tm = x_ref.shape[0]

    @pl.when(i == 0)
    def _():
        carry_ref[...] = jnp.zeros_like(carry_ref)

    att = att_ref[...]
    ms = jnp.sum(att * att, axis=-1, keepdims=True) * (1.0 / D_ATT)
    att_n = (att * lax.rsqrt(ms + EPS) * aw_ref[...]).astype(BF16)
    x1 = x_ref[...] + _dot(rec_ref[...], wr_ref[...]) + _dot(att_n, wa_ref[...])
    x1_ref[...] = x1

    ms2 = jnp.mean(x1 * x1, axis=-1, keepdims=True)
    h2 = x1 * lax.rsqrt(ms2 + EPS) * ln_ref[...]
    h2_ref[...] = h2

    h_hi = h2.astype(BF16)
    h_lo = (h2 - h_hi.astype(F32)).astype(BF16)
    logits = (_dot(h_hi, rwh_ref[...]) + _dot(h_lo, rwh_ref[...]) + _dot(h_hi, rwl_ref[...])
              + rb_ref[...])
    lane = lax.broadcasted_iota(jnp.int32, (tm, LANES), 1)
    lane_f = lane.astype(F32)
    l = jnp.where(lane < N_EXPERTS, logits, -jnp.inf)

    vals, idxs, sels = [], [], []
    for _ in range(TOP_K):
        m = jnp.max(l, axis=-1, keepdims=True)
        idx = jnp.min(jnp.where(l == m, lane_f, float(LANES)), axis=-1, keepdims=True)
        sel = lane_f == idx
        vals.append(m)
        idxs.append(idx)
        sels.append(sel)
        l = jnp.where(sel, -jnp.inf, l)
    exps = [jnp.exp(v - vals[0]) for v in vals]
    denom = exps[0] + exps[1] + exps[2] + exps[3]

    any_sel = sels[0] | sels[1] | sels[2] | sels[3]
    onehot = jnp.where(any_sel, 1.0, 0.0)
    row = lax.broadcasted_iota(jnp.int32, (tm, tm), 0)
    col = lax.broadcasted_iota(jnp.int32, (tm, tm), 1)
    before = (row > col).astype(BF16)
    seen = _dot(before, onehot.astype(BF16)) + carry_ref[...]
    carry_ref[...] = carry_ref[...] + jnp.sum(onehot, axis=0, keepdims=True)
    cnt_ref[...] = jnp.broadcast_to(carry_ref[...], cnt_ref.shape)

    meta = jnp.zeros((tm, LANES), F32)
    for k in range(TOP_K):
        rank = jnp.sum(jnp.where(sels[k], seen, 0.0), axis=-1, keepdims=True)
        meta = jnp.where(lane == k, idxs[k], meta)
        meta = jnp.where(lane == TOP_K + k, exps[k] / denom, meta)
        meta = jnp.where(lane == 2 * TOP_K + k, rank, meta)
    meta_ref[...] = meta


def _out_proj(x2d, rec_n, att, aw_slab, w_rec, w_att, ln2_w, rw_hi, rw_lo, rb_pad):
    t = x2d.shape[0]
    rows = lambda n: pl.BlockSpec((TM_OUT, n), lambda i: (i, 0))
    const = lambda shape: pl.BlockSpec(shape, lambda i: (0,) * len(shape))
    return pl.pallas_call(
        _out_proj_kernel,
        out_shape=(jax.ShapeDtypeStruct((t, D_MODEL), F32), jax.ShapeDtypeStruct((t, D_MODEL), F32),
                   jax.ShapeDtypeStruct((t, LANES), F32), jax.ShapeDtypeStruct((8, LANES), F32)),
        grid=(t // TM_OUT,),
        in_specs=[rows(D_MODEL), rows(D_REC), rows(N_HEADS * SLAB), const((1, N_HEADS * SLAB)),
                  const((D_REC, D_MODEL)), const((N_HEADS * SLAB, D_MODEL)), const((1, D_MODEL)),
                  const((D_MODEL, LANES)), const((D_MODEL, LANES)), const((1, LANES))],
        out_specs=(rows(D_MODEL), rows(D_MODEL), rows(LANES), const((8, LANES))),
        scratch_shapes=[pltpu.VMEM((1, LANES), F32)],
        compiler_params=pltpu.CompilerParams(dimension_semantics=("arbitrary",),
                                             vmem_limit_bytes=VMEM_LIMIT),
        name="out_proj",
    )(x2d, rec_n, att, aw_slab, w_rec, w_att, ln2_w, rw_hi, rw_lo, rb_pad)


def _gather_rows(idx_ref, base, src_hbm, dst, sem, n_rows):
    def body(r, carry):
        tok = idx_ref[base + r]
        pltpu.make_async_copy(src_hbm.at[pl.ds(tok, 1), :], dst.at[pl.ds(r, 1), :], sem).start()
        return carry
    lax.fori_loop(0, n_rows, body, 0, unroll=8)


def _wait_rows(src_hbm, dst, sem, n_rows):
    pltpu.make_async_copy(src_hbm.at[pl.ds(0, n_rows), :], dst, sem).wait()


def _experts_kernel(be_ref, nblk_ref, src_ref, h2_hbm, wgu_ref, bgu_ref, wd_ref, bd_ref,
                    ys_ref, xbuf, wgu_bf, wd_bf, sem):
    b = pl.program_id(0)
    nblk = nblk_ref[0]
    slot = b % 2

    @pl.when(b == 0)
    def _():
        _gather_rows(src_ref, 0, h2_hbm, xbuf.at[0], sem.at[0], TM_E)

    @pl.when(b + 1 < nblk)
    def _():
        _gather_rows(src_ref, (b + 1) * TM_E, h2_hbm, xbuf.at[1 - slot], sem.at[1 - slot], TM_E)

    new_expert = jnp.logical_or(b == 0, be_ref[b] != be_ref[jnp.maximum(b - 1, 0)])

    @pl.when(jnp.logical_and(b < nblk, new_expert))
    def _():
        wgu_bf[...] = wgu_ref[0].astype(BF16)
        wd_bf[...] = wd_ref[0].astype(BF16)

    @pl.when(b < nblk)
    def _():
        _wait_rows(h2_hbm, xbuf.at[slot], sem.at[slot], TM_E)
        x = xbuf[slot].astype(BF16)
        gu = _dot(x, wgu_bf[...]) + bgu_ref[0]
        g = jnp.minimum(gu[:, :D_EXPERT], SWIGLU_LIMIT)
        u = jnp.clip(gu[:, D_EXPERT:], -SWIGLU_LIMIT, SWIGLU_LIMIT)
        act = (u + 1.0) * (g * jax.nn.sigmoid(SWIGLU_ALPHA * g))
        ys_ref[...] = _dot(act.astype(BF16), wd_bf[...]) + bd_ref[0]

    @pl.when(b >= nblk)
    def _():
        ys_ref[...] = jnp.zeros_like(ys_ref)


def _experts(block_expert, n_blocks_used, src_token, h2, w_gate_up, b_gate_up, w_down, b_down):
    n_blocks = block_expert.shape[0]
    grid_spec = pltpu.PrefetchScalarGridSpec(
        num_scalar_prefetch=3,
        grid=(n_blocks,),
        in_specs=[pl.BlockSpec(memory_space=pl.ANY),
                  pl.BlockSpec((1, D_MODEL, 2 * D_EXPERT), lambda b, be, nb, src: (be[b], 0, 0)),
                  pl.BlockSpec((1, 1, 2 * D_EXPERT), lambda b, be, nb, src: (be[b], 0, 0)),
                  pl.BlockSpec((1, D_EXPERT, D_MODEL), lambda b, be, nb, src: (be[b], 0, 0)),
                  pl.BlockSpec((1, 1, D_MODEL), lambda b, be, nb, src: (be[b], 0, 0))],
        out_specs=pl.BlockSpec((TM_E, D_MODEL), lambda b, be, nb, src: (b, 0)),
        scratch_shapes=[pltpu.VMEM((2, TM_E, D_MODEL), F32),
                        pltpu.VMEM((D_MODEL, 2 * D_EXPERT), BF16),
                        pltpu.VMEM((D_EXPERT, D_MODEL), BF16),
                        pltpu.SemaphoreType.DMA((2,))],
    )
    return pl.pallas_call(
        _experts_kernel,
        out_shape=jax.ShapeDtypeStruct((n_blocks * TM_E, D_MODEL), F32),
        grid_spec=grid_spec,
        compiler_params=pltpu.CompilerParams(dimension_semantics=("arbitrary",),
                                             vmem_limit_bytes=VMEM_LIMIT),
        name="experts",
    )(block_expert, n_blocks_used, src_token, h2, w_gate_up,
      b_gate_up.reshape(N_EXPERTS, 1, 2 * D_EXPERT), w_down, b_down.reshape(N_EXPERTS, 1, D_MODEL))


def _combine_kernel(dest_ref, x1_ref, meta_ref, ys_hbm, o_ref, buf, sem):
    i = pl.program_id(0)
    n = pl.num_programs(0)
    slot = i % 2
    rows = TM_C * TOP_K

    @pl.when(i == 0)
    def _():
        _gather_rows(dest_ref, 0, ys_hbm, buf.at[0], sem.at[0], rows)

    @pl.when(i + 1 < n)
    def _():
        _gather_rows(dest_ref, (i + 1) * rows, ys_hbm, buf.at[1 - slot], sem.at[1 - slot], rows)

    _wait_rows(ys_hbm, buf.at[slot], sem.at[slot], rows)
    meta = meta_ref[...]
    out = x1_ref[...]
    for k in range(TOP_K):
        out = out + meta[:, TOP_K + k:TOP_K + k + 1] * buf[slot, k * TM_C:(k + 1) * TM_C, :]
    o_ref[...] = out


def _combine(dest_kmajor, x1, meta, ys):
    t = x1.shape[0]
    grid_spec = pltpu.PrefetchScalarGridSpec(
        num_scalar_prefetch=1,
        grid=(t // TM_C,),
        in_specs=[pl.BlockSpec((TM_C, D_MODEL), lambda i, d: (i, 0)),
                  pl.BlockSpec((TM_C, LANES), lambda i, d: (i, 0)),
                  pl.BlockSpec(memory_space=pl.ANY)],
        out_specs=pl.BlockSpec((TM_C, D_MODEL), lambda i, d: (i, 0)),
        scratch_shapes=[pltpu.VMEM((2, TM_C * TOP_K, D_MODEL), F32),
                        pltpu.SemaphoreType.DMA((2,))],
    )
    return pl.pallas_call(
        _combine_kernel,
        out_shape=jax.ShapeDtypeStruct((t, D_MODEL), F32),
        grid_spec=grid_spec,
        compiler_params=pltpu.CompilerParams(dimension_semantics=("arbitrary",),
                                             vmem_limit_bytes=VMEM_LIMIT),
        name="combine",
    )(dest_kmajor, x1, meta, ys)


def _head_slabs(w):
    d = w.shape[0]
    w = w.reshape(d, N_HEADS, HEAD_DIM)
    return jnp.pad(w, ((0, 0), (0, 0), (0, SLAB - HEAD_DIM))).reshape(d, N_HEADS * SLAB)


def _block_diag(w):
    g = w.shape[0]
    eye = jnp.eye(g, dtype=w.dtype)
    return jnp.einsum('gij,gh->gihj', w, eye).reshape(g * REC_BLOCK, g * REC_BLOCK)


def _pad_lanes(v, n=LANES):
    return jnp.pad(v, (0, n - v.shape[0])).reshape(1, n)


def _layer(x, ln1_w, w_in, conv_w, conv_b, w_rg_a, b_rg_a, w_rg_i, b_rg_i, rg_lambda, b_forget,
           q_norm_w, k_norm_w, rec_out_norm_w, attn_out_norm_w, w_out, ln2_w, w_router, b_router,
           w_gate_up, b_gate_up, w_down, b_down):
    batch, seq, _ = x.shape
    t = batch * seq
    x2d = x.reshape(t, D_MODEL)

    o_q = 2 * D_REC
    o_k = o_q + D_ATT
    o_v = o_k + D_ATT
    o_f = o_v + D_ATT
    w_all = jnp.concatenate(
        [w_in[:, :o_q], _head_slabs(w_in[:, o_q:o_k]), _head_slabs(w_in[:, o_k:o_v]),
         _head_slabs(w_in[:, o_v:o_f]), jnp.pad(w_in[:, o_f:], ((0, 0), (0, LANES - N_HEADS)))],
        axis=1).astype(BF16)
    scale = HEAD_DIM ** -0.5
    u, g, qp, kp, vp = _in_proj(x2d, ln1_w.reshape(1, D_MODEL), w_all, _pad_lanes(b_forget),
                                _pad_lanes(q_norm_w * scale), _pad_lanes(k_norm_w), batch, seq)

    rec_n = _rglru(u.reshape(batch, seq, D_REC), g.reshape(batch, seq, D_REC), conv_w,
                   conv_b.reshape(1, D_REC), _block_diag(w_rg_a).astype(BF16), b_rg_a.reshape(1, D_REC),
                   _block_diag(w_rg_i).astype(BF16), b_rg_i.reshape(1, D_REC),
                   rg_lambda.reshape(1, D_REC), rec_out_norm_w.reshape(1, D_REC))

    att = _attention(qp, kp, vp)

    aw_slab = _head_slabs(attn_out_norm_w.reshape(1, D_ATT))
    w_att = jnp.pad(w_out[D_REC:].reshape(N_HEADS, HEAD_DIM, D_MODEL),
                    ((0, 0), (0, SLAB - HEAD_DIM), (0, 0))).reshape(N_HEADS * SLAB, D_MODEL)
    rw = jnp.pad(w_router, ((0, 0), (0, LANES - N_EXPERTS)))
    rw_hi = rw.astype(BF16)
    rw_lo = (rw - rw_hi.astype(F32)).astype(BF16)
    x1, h2, meta, counts = _out_proj(
        x2d, rec_n.reshape(t, D_REC), att.reshape(t, N_HEADS * SLAB), aw_slab,
        w_out[:D_REC].astype(BF16), w_att.astype(BF16), ln2_w.reshape(1, D_MODEL),
        rw_hi, rw_lo, _pad_lanes(b_router))

    counts = counts[0, :N_EXPERTS].astype(jnp.int32)
    padded = (counts + TM_E - 1) // TM_E * TM_E
    pad_end = jnp.cumsum(padded)
    pad_start = pad_end - padded
    idx = meta[:, 0:TOP_K].astype(jnp.int32)
    rank = meta[:, 2 * TOP_K:3 * TOP_K].astype(jnp.int32)
    dest = pad_start[idx] + rank
    n_blocks = t * TOP_K // TM_E + N_EXPERTS
    block_pos = jnp.arange(n_blocks, dtype=jnp.int32) * TM_E
    block_expert = jnp.minimum(
        jnp.sum((pad_end[None, :] <= block_pos[:, None]).astype(jnp.int32), axis=1), N_EXPERTS - 1)
    n_used = (pad_end[-1:] // TM_E).astype(jnp.int32)
    token = jnp.broadcast_to(jnp.arange(t, dtype=jnp.int32)[:, None], (t, TOP_K))
    src_token = jnp.zeros((n_blocks * TM_E,), jnp.int32).at[dest.reshape(-1)].set(token.reshape(-1))

    ys = _experts(block_expert, n_used, src_token, h2, w_gate_up, b_gate_up, w_down, b_down)

    dest_kmajor = dest.reshape(t // TM_C, TM_C, TOP_K).transpose(0, 2, 1).reshape(-1)
    out = _combine(dest_kmajor, x1, meta, ys)
    return out.reshape(batch, seq, D_MODEL)


def kernel(x, ln1_w, w_in, conv_w, conv_b, w_rg_a, b_rg_a, w_rg_i, b_rg_i, rg_lambda, b_forget,
           q_norm_w, k_norm_w, rec_out_norm_w, attn_out_norm_w, w_out, ln2_w, w_router, b_router,
           w_gate_up, b_gate_up, w_down, b_down):
    for l in range(ln1_w.shape[0]):
        x = _layer(x, ln1_w[l], w_in[l], conv_w[l], conv_b[l], w_rg_a[l], b_rg_a[l], w_rg_i[l],
                   b_rg_i[l], rg_lambda[l], b_forget[l], q_norm_w[l], k_norm_w[l], rec_out_norm_w[l],
                   attn_out_norm_w[l], w_out[l], ln2_w[l], w_router[l], b_router[l], w_gate_up[l],
                   b_gate_up[l], w_down[l], b_down[l])
    return x
```

```python
import functools

import jax
import jax.numpy as jnp
from jax import lax
from jax.experimental import pallas as pl
from jax.experimental.pallas import tpu as pltpu

D_MODEL = 1024
D_REC = 512
N_REC_BLOCKS = 8
REC_BLOCK = 64
CONV_WIDTH = 4
RGLRU_C = 8.0
D_ATT = 512
HEAD_DIM = 64
N_HEADS = 8
N_EXPERTS = 32
TOP_K = 4
D_EXPERT = 1024
SWIGLU_LIMIT = 7.0
SWIGLU_ALPHA = 1.702
EPS = 1e-6

LANES = 128
SLAB = LANES
CHUNKS = D_MODEL // LANES
VMEM_LIMIT = 56 * 1024 * 1024

F32 = jnp.float32
BF16 = jnp.bfloat16

O_U = 0
O_G = O_U + D_REC
O_Q = O_G + D_REC
O_K = O_Q + N_HEADS * SLAB
O_V = O_K + N_HEADS * SLAB
O_F = O_V + N_HEADS * SLAB
N_PROJ = O_F + LANES

L_C = HEAD_DIM
L_ONE = HEAD_DIM + 3

TM_IN = 512
TS_REC = 512
TQ = 512
TK = 512
TM_OUT = 512
TM_E = 256
TM_C = 256

TOKEN_BITS = 14
TOKEN_MASK = (1 << TOKEN_BITS) - 1
DUMP_GROUPS = 2 * TM_E
FAKE_BLOCKS = 2


def _split3(v):
    hi = v.astype(BF16).astype(F32)
    r = v - hi
    mid = r.astype(BF16).astype(F32)
    lo = (r - mid).astype(BF16).astype(F32)
    return hi, mid, lo


def _dot(a, b):
    return jnp.dot(a, b, preferred_element_type=F32)


def _in_proj_kernel(x_ref, ln_ref, w_ref, bf_ref, qw_ref, kw_ref,
                    u_ref, g_ref, q_ref, k_ref, v_ref, carry_ref, *, tiles_per_seq):
    i = pl.program_id(0)
    tm = x_ref.shape[0]

    @pl.when(i % tiles_per_seq == 0)
    def _():
        carry_ref[...] = jnp.zeros_like(carry_ref)

    x = x_ref[...]
    ms = jnp.mean(x * x, axis=-1, keepdims=True)
    h = (x * lax.rsqrt(ms + EPS) * ln_ref[...]).astype(BF16)

    u_ref[...] = _dot(h, w_ref[:, O_U:O_U + D_REC])
    g_ref[...] = _dot(h, w_ref[:, O_G:O_G + D_REC])

    f = _dot(h, w_ref[:, O_F:O_F + LANES]) + bf_ref[...]
    logf = jnp.minimum(f, 0.0) - jnp.log1p(jnp.exp(-jnp.abs(f)))
    row = lax.broadcasted_iota(jnp.int32, (tm, tm), 0)
    col = lax.broadcasted_iota(jnp.int32, (tm, tm), 1)
    tri = (row >= col).astype(BF16)
    hi, mid, lo = _split3(logf)
    c = (_dot(tri, hi.astype(BF16)) + _dot(tri, mid.astype(BF16))
         + _dot(tri, lo.astype(BF16)) + carry_ref[...])
    carry_ref[...] = c[tm - 1:tm, :]

    lane = lax.broadcasted_iota(jnp.int32, (tm, SLAB), 1)
    q_all = _dot(h, w_ref[:, O_Q:O_Q + N_HEADS * SLAB])
    k_all = _dot(h, w_ref[:, O_K:O_K + N_HEADS * SLAB])
    v_all = _dot(h, w_ref[:, O_V:O_V + N_HEADS * SLAB])
    qw = qw_ref[...]
    kw = kw_ref[...]
    for hd in range(N_HEADS):
        sl = slice(hd * SLAB, (hd + 1) * SLAB)
        qs = q_all[:, sl]
        ks = k_all[:, sl]
        qn = qs * lax.rsqrt(jnp.sum(qs * qs, axis=-1, keepdims=True) * (1.0 / HEAD_DIM) + EPS) * qw
        kn = ks * lax.rsqrt(jnp.sum(ks * ks, axis=-1, keepdims=True) * (1.0 / HEAD_DIM) + EPS) * kw
        c_hi, c_mid, c_lo = _split3(c[:, hd:hd + 1])
        qp = jnp.where(lane == L_C, c_hi, qn)
        qp = jnp.where(lane == L_C + 1, c_mid, qp)
        qp = jnp.where(lane == L_C + 2, c_lo, qp)
        qp = jnp.where((lane >= L_ONE) & (lane < L_ONE + 3), 1.0, qp)
        kp = jnp.where((lane >= L_C) & (lane < L_C + 3), 1.0, kn)
        kp = jnp.where(lane == L_ONE, -c_hi, kp)
        kp = jnp.where(lane == L_ONE + 1, -c_mid, kp)
        kp = jnp.where(lane == L_ONE + 2, -c_lo, kp)
        q_ref[0, hd] = qp.astype(BF16)
        k_ref[0, hd] = kp.astype(BF16)
        v_ref[0, hd] = jnp.where(lane == HEAD_DIM, 1.0, v_all[:, sl]).astype(BF16)


def _in_proj(x2d, ln1_w, w_all, bf_pad, qw_slab, kw_slab, batch, seq):
    t = x2d.shape[0]
    tiles_per_seq = seq // TM_IN
    slab_shape = jax.ShapeDtypeStruct((batch, N_HEADS, seq, SLAB), BF16)
    slab_spec = pl.BlockSpec((1, N_HEADS, TM_IN, SLAB),
                             lambda i: (i // tiles_per_seq, 0, i % tiles_per_seq, 0))
    row_spec = pl.BlockSpec((TM_IN, D_REC), lambda i: (i, 0))
    const = lambda shape: pl.BlockSpec(shape, lambda i: (0,) * len(shape))
    return pl.pallas_call(
        functools.partial(_in_proj_kernel, tiles_per_seq=tiles_per_seq),
        out_shape=(jax.ShapeDtypeStruct((t, D_REC), F32), jax.ShapeDtypeStruct((t, D_REC), F32),
                   slab_shape, slab_shape, slab_shape),
        grid=(t // TM_IN,),
        in_specs=[pl.BlockSpec((TM_IN, D_MODEL), lambda i: (i, 0)),
                  const((1, D_MODEL)), const((D_MODEL, N_PROJ)), const((1, LANES)),
                  const((1, SLAB)), const((1, SLAB))],
        out_specs=(row_spec, row_spec, slab_spec, slab_spec, slab_spec),
        scratch_shapes=[pltpu.VMEM((1, LANES), F32)],
        compiler_params=pltpu.CompilerParams(dimension_semantics=("arbitrary",),
                                             vmem_limit_bytes=VMEM_LIMIT),
        name="in_proj",
    )(x2d, ln1_w, w_all, bf_pad, qw_slab, kw_slab)


def _rglru_kernel(u_ref, g_ref, cw_ref, cb_ref, wa_ref, ba_ref, wi_ref, bi_ref, lam_ref, nw_ref,
                  o_ref, ext_ref, a_ref, b_ref, h_ref, hcar_ref):
    s = pl.program_id(1)
    ts = u_ref.shape[1]
    pad = 8

    @pl.when(s == 0)
    def _():
        ext_ref[0:pad, :] = jnp.zeros((pad, D_REC), F32)
        hcar_ref[...] = jnp.zeros_like(hcar_ref)

    @pl.when(s != 0)
    def _():
        ext_ref[0:pad, :] = ext_ref[ts:ts + pad, :]

    u = u_ref[0]
    ext_ref[pad:pad + ts, :] = u
    xc = cb_ref[...] + u * cw_ref[CONV_WIDTH - 1:CONV_WIDTH, :]
    for tap in range(CONV_WIDTH - 1):
        d = CONV_WIDTH - 1 - tap
        xc = xc + ext_ref[pad - d:pad - d + ts, :] * cw_ref[tap:tap + 1, :]

    xb = xc.astype(BF16)
    r = jax.nn.sigmoid(_dot(xb, wa_ref[...]) + ba_ref[...])
    ig = jax.nn.sigmoid(_dot(xb, wi_ref[...]) + bi_ref[...])
    nlam = -lam_ref[...]
    softplus = jnp.maximum(nlam, 0.0) + jnp.log1p(jnp.exp(-jnp.abs(nlam)))
    log_a = -RGLRU_C * r * softplus
    a = jnp.exp(log_a)
    a_ref[...] = a
    b_ref[...] = jnp.sqrt(1.0 - jnp.exp(2.0 * log_a)) * (ig * xc)

    def step(t, h):
        h = a_ref[pl.ds(t, 1), :] * h + b_ref[pl.ds(t, 1), :]
        h_ref[pl.ds(t, 1), :] = h
        return h

    hcar_ref[...] = lax.fori_loop(0, ts, step, hcar_ref[...], unroll=8)

    rec = h_ref[...] * jax.nn.gelu(g_ref[0])
    ms = jnp.mean(rec * rec, axis=-1, keepdims=True)
    o_ref[0] = (rec * lax.rsqrt(ms + EPS) * nw_ref[...]).astype(BF16)


def _rglru(u, g, conv_w, conv_b, wa_bd, b_a, wi_bd, b_i, lam, norm_w):
    batch, seq, _ = u.shape
    seq_spec = pl.BlockSpec((1, TS_REC, D_REC), lambda b, s: (b, s, 0))
    const = lambda shape: pl.BlockSpec(shape, lambda b, s: (0,) * len(shape))
    return pl.pallas_call(
        _rglru_kernel,
        out_shape=jax.ShapeDtypeStruct((batch, seq, D_REC), BF16),
        grid=(batch, seq // TS_REC),
        in_specs=[seq_spec, seq_spec, const((CONV_WIDTH, D_REC)), const((1, D_REC)),
                  const((D_REC, D_REC)), const((1, D_REC)), const((D_REC, D_REC)), const((1, D_REC)),
                  const((1, D_REC)), const((1, D_REC))],
        out_specs=seq_spec,
        scratch_shapes=[pltpu.VMEM((TS_REC + 8, D_REC), F32), pltpu.VMEM((TS_REC, D_REC), F32),
                        pltpu.VMEM((TS_REC, D_REC), F32), pltpu.VMEM((TS_REC, D_REC), F32),
                        pltpu.VMEM((1, D_REC), F32)],
        compiler_params=pltpu.CompilerParams(dimension_semantics=("arbitrary", "arbitrary"),
                                             vmem_limit_bytes=VMEM_LIMIT),
        name="rglru",
    )(u, g, conv_w, conv_b, wa_bd, b_a, wi_bd, b_i, lam, norm_w)


def _attn_kernel(q_ref, k_ref, v_ref, o_ref):
    qi = pl.program_id(2)
    q = q_ref[0, 0]
    tq = q.shape[0]

    def step(j, carry, masked):
        m, acc = carry
        start = pl.multiple_of(j * TK, TK)
        kj = k_ref[0, 0, pl.ds(start, TK), :]
        vj = v_ref[0, 0, pl.ds(start, TK), :]
        s = lax.dot_general(q, kj, (((1,), (1,)), ((), ())), preferred_element_type=F32)
        if masked:
            row = lax.broadcasted_iota(jnp.int32, (tq, TK), 0)
            col = lax.broadcasted_iota(jnp.int32, (tq, TK), 1)
            s = jnp.where(row >= col, s, -jnp.inf)
        m_new = jnp.maximum(m, jnp.max(s, axis=-1, keepdims=True))
        p = jnp.exp(s - m_new)
        acc = jnp.exp(m - m_new) * acc + _dot(p.astype(BF16), vj)
        return m_new, acc

    init = (jnp.full((tq, 1), -jnp.inf, F32), jnp.zeros((tq, SLAB), F32))
    carry = lax.fori_loop(0, qi, lambda j, c: step(j, c, False), init)
    _, acc = step(qi, carry, True)
    lane = lax.broadcasted_iota(jnp.int32, (tq, SLAB), 1)
    o_ref[0] = jnp.where(lane < HEAD_DIM, acc / acc[:, HEAD_DIM:HEAD_DIM + 1], 0.0)


def _attention(qp, kp, vp):
    batch, _, seq, _ = qp.shape
    kv_spec = pl.BlockSpec((1, 1, seq, SLAB), lambda b, h, i: (b, h, 0, 0))
    return pl.pallas_call(
        _attn_kernel,
        out_shape=jax.ShapeDtypeStruct((batch, seq, N_HEADS * SLAB), F32),
        grid=(batch, N_HEADS, seq // TQ),
        in_specs=[pl.BlockSpec((1, 1, TQ, SLAB), lambda b, h, i: (b, h, i, 0)), kv_spec, kv_spec],
        out_specs=pl.BlockSpec((1, TQ, SLAB), lambda b, h, i: (b, i, h)),
        compiler_params=pltpu.CompilerParams(
            dimension_semantics=("arbitrary", "arbitrary", "arbitrary"),
            vmem_limit_bytes=VMEM_LIMIT),
        name="attn",
    )(qp, kp, vp)


def _out_proj_kernel(x_ref, rec_ref, att_ref, aw_ref, wr_ref, wa_ref, ln_ref, rwh_ref, rwl_ref, rb_ref,
                     x1_ref, h2_ref, meta_ref, cnt_ref, carry_ref):
    i = pl.program_id(0)
    tm = x_ref.shape[0]

    @pl.when(i == 0)
    def _():
        carry_ref[...] = jnp.zeros_like(carry_ref)

    att = att_ref[...]
    ms = jnp.sum(att * att, axis=-1, keepdims=True) * (1.0 / D_ATT)
    att_n = (att * lax.rsqrt(ms + EPS) * aw_ref[...]).astype(BF16)
    x1 = x_ref[...] + _dot(rec_ref[...], wr_ref[...]) + _dot(att_n, wa_ref[...])
    x1_ref[...] = x1

    ms2 = jnp.mean(x1 * x1, axis=-1, keepdims=True)
    h2 = x1 * lax.rsqrt(ms2 + EPS) * ln_ref[...]
    for s in range(CHUNKS):
        h2_ref[pl.ds(s, tm, stride=CHUNKS), :] = h2[:, s * LANES:(s + 1) * LANES]

    h_hi = h2.astype(BF16)
    h_lo = (h2 - h_hi.astype(F32)).astype(BF16)
    logits = (_dot(h_hi, rwh_ref[...]) + _dot(h_lo, rwh_ref[...]) + _dot(h_hi, rwl_ref[...])
              + rb_ref[...])
    lane = lax.broadcasted_iota(jnp.int32, (tm, LANES), 1)
    lane_f = lane.astype(F32)
    l = jnp.where(lane < N_EXPERTS, logits, -jnp.inf)

    vals, idxs, sels = [], [], []
    for _ in range(TOP_K):
        m = jnp.max(l, axis=-1, keepdims=True)
        idx = jnp.min(jnp.where(l == m, lane_f, float(LANES)), axis=-1, keepdims=True)
        sel = lane_f == idx
        vals.append(m)
        idxs.append(idx)
        sels.append(sel)
        l = jnp.where(sel, -jnp.inf, l)
    exps = [jnp.exp(v - vals[0]) for v in vals]
    denom = exps[0] + exps[1] + exps[2] + exps[3]

    any_sel = sels[0] | sels[1] | sels[2] | sels[3]
    onehot = jnp.where(any_sel, 1.0, 0.0)
    row = lax.broadcasted_iota(jnp.int32, (tm, tm), 0)
    col = lax.broadcasted_iota(jnp.int32, (tm, tm), 1)
    before = (row > col).astype(BF16)
    seen = _dot(before, onehot.astype(BF16)) + carry_ref[...]
    carry_ref[...] = carry_ref[...] + jnp.sum(onehot, axis=0, keepdims=True)
    cnt_ref[...] = jnp.broadcast_to(carry_ref[...], cnt_ref.shape)

    meta = jnp.zeros((tm, LANES), F32)
    for k in range(TOP_K):
        rank = jnp.sum(jnp.where(sels[k], seen, 0.0), axis=-1, keepdims=True)
        meta = jnp.where(lane == k, idxs[k], meta)
        meta = jnp.where(lane == TOP_K + k, exps[k] / denom, meta)
        meta = jnp.where(lane == 2 * TOP_K + k, rank, meta)
    meta_ref[...] = meta


def _out_proj(x2d, rec_n, att, aw_slab, w_rec, w_att, ln2_w, rw_hi, rw_lo, rb_pad):
    t = x2d.shape[0]
    rows = lambda n: pl.BlockSpec((TM_OUT, n), lambda i: (i, 0))
    const = lambda shape: pl.BlockSpec(shape, lambda i: (0,) * len(shape))
    return pl.pallas_call(
        _out_proj_kernel,
        out_shape=(jax.ShapeDtypeStruct((t, D_MODEL), F32), jax.ShapeDtypeStruct((t * CHUNKS, LANES), F32),
                   jax.ShapeDtypeStruct((t, LANES), F32), jax.ShapeDtypeStruct((8, LANES), F32)),
        grid=(t // TM_OUT,),
        in_specs=[rows(D_MODEL), rows(D_REC), rows(N_HEADS * SLAB), const((1, N_HEADS * SLAB)),
                  const((D_REC, D_MODEL)), const((N_HEADS * SLAB, D_MODEL)), const((1, D_MODEL)),
                  const((D_MODEL, LANES)), const((D_MODEL, LANES)), const((1, LANES))],
        out_specs=(rows(D_MODEL), pl.BlockSpec((TM_OUT * CHUNKS, LANES), lambda i: (i, 0)), rows(LANES),
                   const((8, LANES))),
        scratch_shapes=[pltpu.VMEM((1, LANES), F32)],
        compiler_params=pltpu.CompilerParams(dimension_semantics=("arbitrary",),
                                             vmem_limit_bytes=VMEM_LIMIT),
        name="out_proj",
    )(x2d, rec_n, att, aw_slab, w_rec, w_att, ln2_w, rw_hi, rw_lo, rb_pad)


def _experts_kernel(be_ref, nblk_ref, info_ref, h2_hbm, wgu_ref, bgu_ref, wd_ref, bd_ref,
                    y_hbm, x_a, x_b, o_a, o_b, wgu_bf, wd_bf, gsem, ssem, *, n_blocks):
    b = pl.program_id(0)
    nblk = nblk_ref[0]
    rows = TM_E * CHUNKS

    def start_gather(v, x_dst, r, sem, priority=0):
        tok_row = pl.multiple_of((v & TOKEN_MASK) * CHUNKS, CHUNKS)
        dst_row = r * CHUNKS if isinstance(r, int) else pl.multiple_of(r * CHUNKS, CHUNKS)
        pltpu.make_async_copy(h2_hbm.at[pl.ds(tok_row, CHUNKS), :],
                              x_dst.at[pl.ds(dst_row, CHUNKS), :], sem).start(priority=priority)

    def start_scatter(v, o_src, r, sem, priority=0):
        out_row = pl.multiple_of(lax.shift_right_logical(v, TOKEN_BITS) * CHUNKS, CHUNKS)
        src_row = r * CHUNKS if isinstance(r, int) else pl.multiple_of(r * CHUNKS, CHUNKS)
        pltpu.make_async_copy(o_src.at[pl.ds(src_row, CHUNKS), :],
                              y_hbm.at[pl.ds(out_row, CHUNKS), :], sem).start(priority=priority)

    def wait_gather(x_dst, sem):
        pltpu.make_async_copy(h2_hbm.at[pl.ds(0, rows), :], x_dst, sem).wait()

    def wait_scatter(o_src, sem):
        pltpu.make_async_copy(o_src, y_hbm.at[pl.ds(0, rows), :], sem).wait()

    def info_base(block):
        return (block + FAKE_BLOCKS) * TM_E

    @pl.when(b == 0)
    def _():
        o_a[...] = jnp.zeros_like(o_a)
        o_b[...] = jnp.zeros_like(o_b)

        def body(r, carry):
            start_gather(info_ref[info_base(0) + r], x_a, r, gsem.at[0])
            start_scatter(info_ref[info_base(-2) + r], o_a, r, ssem.at[0])
            return carry
        lax.fori_loop(0, TM_E, body, 0, unroll=8)

    new_expert = jnp.logical_or(b == 0, be_ref[b] != be_ref[jnp.maximum(b - 1, 0)])

    @pl.when(jnp.logical_and(b < nblk, new_expert))
    def _():
        wgu_bf[...] = wgu_ref[0].astype(BF16)
        wd_bf[...] = wd_ref[0].astype(BF16)

    def block_step(x_cur, x_nxt, o_cur, o_prev, g_cur, g_nxt, s_cur, s_prev):
        wait_gather(x_cur, g_cur)
        nxt_base = info_base(jnp.minimum(b + 1, n_blocks - 1))
        prev_base = info_base(b - 1)
        for r in range(TM_E):
            start_gather(info_ref[nxt_base + r], x_nxt, r, g_nxt, r % 2)
            start_scatter(info_ref[prev_base + r], o_prev, r, s_prev, (r + 1) % 2)

        x = jnp.concatenate([x_cur[pl.ds(s, TM_E, stride=CHUNKS), :] for s in range(CHUNKS)],
                            axis=1).astype(BF16)
        gu = _dot(x, wgu_bf[...]) + bgu_ref[0]
        g = jnp.minimum(gu[:, :D_EXPERT], SWIGLU_LIMIT)
        u = jnp.clip(gu[:, D_EXPERT:], -SWIGLU_LIMIT, SWIGLU_LIMIT)
        act = (u + 1.0) * (g * jax.nn.sigmoid(SWIGLU_ALPHA * g))
        y = _dot(act.astype(BF16), wd_bf[...]) + bd_ref[0]
        wait_scatter(o_cur, s_cur)
        for s in range(CHUNKS):
            o_cur[pl.ds(s, TM_E, stride=CHUNKS), :] = y[:, s * LANES:(s + 1) * LANES]

    def drain_step(x_cur, o_cur, o_prev, g_cur, s_cur, s_prev):
        wait_gather(x_cur, g_cur)
        prev_base = info_base(b - 1)

        def body(r, carry):
            start_scatter(info_ref[prev_base + r], o_prev, r, s_prev)
            return carry
        lax.fori_loop(0, TM_E, body, 0, unroll=8)
        wait_scatter(o_cur, s_cur)
        wait_scatter(o_prev, s_prev)

    even = b % 2 == 0
    odd = b % 2 == 1

    @pl.when(jnp.logical_and(b < nblk, even))
    def _():
        block_step(x_a, x_b, o_a, o_b, gsem.at[0], gsem.at[1], ssem.at[0], ssem.at[1])

    @pl.when(jnp.logical_and(b < nblk, odd))
    def _():
        block_step(x_b, x_a, o_b, o_a, gsem.at[1], gsem.at[0], ssem.at[1], ssem.at[0])

    @pl.when(jnp.logical_and(b == nblk, even))
    def _():
        drain_step(x_a, o_a, o_b, gsem.at[0], ssem.at[0], ssem.at[1])

    @pl.when(jnp.logical_and(b == nblk, odd))
    def _():
        drain_step(x_b, o_b, o_a, gsem.at[1], ssem.at[1], ssem.at[0])


def _experts(block_expert, n_blocks_used, slot_info, h2c, w_gate_up, b_gate_up, w_down, b_down,
             n_out_groups):
    n_blocks = slot_info.shape[0] // TM_E - FAKE_BLOCKS
    grid_spec = pltpu.PrefetchScalarGridSpec(
        num_scalar_prefetch=3,
        grid=(n_blocks + 1,),
        in_specs=[pl.BlockSpec(memory_space=pl.ANY),
                  pl.BlockSpec((1, D_MODEL, 2 * D_EXPERT), lambda b, be, nb, info: (be[b], 0, 0)),
                  pl.BlockSpec((1, 1, 2 * D_EXPERT), lambda b, be, nb, info: (be[b], 0, 0)),
                  pl.BlockSpec((1, D_EXPERT, D_MODEL), lambda b, be, nb, info: (be[b], 0, 0)),
                  pl.BlockSpec((1, 1, D_MODEL), lambda b, be, nb, info: (be[b], 0, 0))],
        out_specs=pl.BlockSpec(memory_space=pl.ANY),
        scratch_shapes=[pltpu.VMEM((TM_E * CHUNKS, LANES), F32), pltpu.VMEM((TM_E * CHUNKS, LANES), F32),
                        pltpu.VMEM((TM_E * CHUNKS, LANES), F32), pltpu.VMEM((TM_E * CHUNKS, LANES), F32),
                        pltpu.VMEM((D_MODEL, 2 * D_EXPERT), BF16),
                        pltpu.VMEM((D_EXPERT, D_MODEL), BF16),
                        pltpu.SemaphoreType.DMA((2,)), pltpu.SemaphoreType.DMA((2,))],
    )
    return pl.pallas_call(
        functools.partial(_experts_kernel, n_blocks=n_blocks),
        out_shape=jax.ShapeDtypeStruct((n_out_groups * CHUNKS, LANES), F32),
        grid_spec=grid_spec,
        compiler_params=pltpu.CompilerParams(dimension_semantics=("arbitrary",),
                                             vmem_limit_bytes=VMEM_LIMIT),
        name="experts",
    )(block_expert, n_blocks_used, slot_info, h2c, w_gate_up,
      b_gate_up.reshape(N_EXPERTS, 1, 2 * D_EXPERT), w_down, b_down.reshape(N_EXPERTS, 1, D_MODEL))


def _combine_kernel(x1_ref, meta_ref, y0_ref, y1_ref, y2_ref, y3_ref, o_ref):
    tm = x1_ref.shape[0]
    meta = meta_ref[...]
    gates = [jnp.broadcast_to(meta[:, TOP_K + k:TOP_K + k + 1], (tm, LANES)) for k in range(TOP_K)]
    for s in range(CHUNKS):
        acc = x1_ref[:, s * LANES:(s + 1) * LANES]
        for k, y_ref in enumerate((y0_ref, y1_ref, y2_ref, y3_ref)):
            acc = acc + gates[k] * y_ref[pl.ds(s, tm, stride=CHUNKS), :]
        o_ref[:, s * LANES:(s + 1) * LANES] = acc


def _combine(x1, meta, y):
    t = x1.shape[0]
    tiles = t // TM_C
    y_spec = lambda k: pl.BlockSpec((TM_C * CHUNKS, LANES), lambda i: (k * tiles + i, 0))
    return pl.pallas_call(
        _combine_kernel,
        out_shape=jax.ShapeDtypeStruct((t, D_MODEL), F32),
        grid=(tiles,),
        in_specs=[pl.BlockSpec((TM_C, D_MODEL), lambda i: (i, 0)),
                  pl.BlockSpec((TM_C, LANES), lambda i: (i, 0)),
                  y_spec(0), y_spec(1), y_spec(2), y_spec(3)],
        out_specs=pl.BlockSpec((TM_C, D_MODEL), lambda i: (i, 0)),
        compiler_params=pltpu.CompilerParams(dimension_semantics=("arbitrary",),
                                             vmem_limit_bytes=VMEM_LIMIT),
        name="combine",
    )(x1, meta, y, y, y, y)


def _head_slabs(w):
    d = w.shape[0]
    w = w.reshape(d, N_HEADS, HEAD_DIM)
    return jnp.pad(w, ((0, 0), (0, 0), (0, SLAB - HEAD_DIM))).reshape(d, N_HEADS * SLAB)


def _block_diag(w):
    g = w.shape[0]
    eye = jnp.eye(g, dtype=w.dtype)
    return jnp.einsum('gij,gh->gihj', w, eye).reshape(g * REC_BLOCK, g * REC_BLOCK)


def _pad_lanes(v, n=LANES):
    return jnp.pad(v, (0, n - v.shape[0])).reshape(1, n)


def _layer(x, ln1_w, w_in, conv_w, conv_b, w_rg_a, b_rg_a, w_rg_i, b_rg_i, rg_lambda, b_forget,
           q_norm_w, k_norm_w, rec_out_norm_w, attn_out_norm_w, w_out, ln2_w, w_router, b_router,
           w_gate_up, b_gate_up, w_down, b_down):
    batch, seq, _ = x.shape
    t = batch * seq
    x2d = x.reshape(t, D_MODEL)

    o_q = 2 * D_REC
    o_k = o_q + D_ATT
    o_v = o_k + D_ATT
    o_f = o_v + D_ATT
    w_all = jnp.concatenate(
        [w_in[:, :o_q], _head_slabs(w_in[:, o_q:o_k]), _head_slabs(w_in[:, o_k:o_v]),
         _head_slabs(w_in[:, o_v:o_f]), jnp.pad(w_in[:, o_f:], ((0, 0), (0, LANES - N_HEADS)))],
        axis=1).astype(BF16)
    scale = HEAD_DIM ** -0.5
    u, g, qp, kp, vp = _in_proj(x2d, ln1_w.reshape(1, D_MODEL), w_all, _pad_lanes(b_forget),
                                _pad_lanes(q_norm_w * scale), _pad_lanes(k_norm_w), batch, seq)

    rec_n = _rglru(u.reshape(batch, seq, D_REC), g.reshape(batch, seq, D_REC), conv_w,
                   conv_b.reshape(1, D_REC), _block_diag(w_rg_a).astype(BF16), b_rg_a.reshape(1, D_REC),
                   _block_diag(w_rg_i).astype(BF16), b_rg_i.reshape(1, D_REC),
                   rg_lambda.reshape(1, D_REC), rec_out_norm_w.reshape(1, D_REC))

    att = _attention(qp, kp, vp)

    aw_slab = _head_slabs(attn_out_norm_w.reshape(1, D_ATT))
    w_att = jnp.pad(w_out[D_REC:].reshape(N_HEADS, HEAD_DIM, D_MODEL),
                    ((0, 0), (0, SLAB - HEAD_DIM), (0, 0))).reshape(N_HEADS * SLAB, D_MODEL)
    rw = jnp.pad(w_router, ((0, 0), (0, LANES - N_EXPERTS)))
    rw_hi = rw.astype(BF16)
    rw_lo = (rw - rw_hi.astype(F32)).astype(BF16)
    x1, h2c, meta, counts = _out_proj(
        x2d, rec_n.reshape(t, D_REC), att.reshape(t, N_HEADS * SLAB), aw_slab,
        w_out[:D_REC].astype(BF16), w_att.astype(BF16), ln2_w.reshape(1, D_MODEL),
        rw_hi, rw_lo, _pad_lanes(b_router))

    counts = counts[0, :N_EXPERTS].astype(jnp.int32)
    padded = (counts + TM_E - 1) // TM_E * TM_E
    pad_end = jnp.cumsum(padded)
    pad_start = pad_end - padded
    idx = meta[:, 0:TOP_K].astype(jnp.int32)
    rank = meta[:, 2 * TOP_K:3 * TOP_K].astype(jnp.int32)
    dest = pad_start[idx] + rank
    n_blocks = t * TOP_K // TM_E + N_EXPERTS
    block_pos = jnp.arange(n_blocks, dtype=jnp.int32) * TM_E
    block_expert = jnp.minimum(
        jnp.sum((pad_end[None, :] <= block_pos[:, None]).astype(jnp.int32), axis=1), N_EXPERTS - 1)
    n_used = (pad_end[-1:] // TM_E).astype(jnp.int32)
    token = jnp.broadcast_to(jnp.arange(t, dtype=jnp.int32)[:, None], (t, TOP_K))
    group = jnp.arange(TOP_K, dtype=jnp.int32)[None, :] * t + token
    n_groups = TOP_K * t
    slot_ids = jnp.arange((n_blocks + FAKE_BLOCKS) * TM_E, dtype=jnp.int32)
    dump = (n_groups + slot_ids % DUMP_GROUPS) << TOKEN_BITS
    slot_info = dump.at[dest.reshape(-1) + FAKE_BLOCKS * TM_E].set(
        (token | (group << TOKEN_BITS)).reshape(-1))
    block_expert = jnp.concatenate([block_expert, block_expert[-1:]])

    y = _experts(block_expert, n_used, slot_info, h2c, w_gate_up, b_gate_up, w_down, b_down,
                 n_groups + DUMP_GROUPS)
    out = _combine(x1, meta, y)
    return out.reshape(batch, seq, D_MODEL)


def kernel(x, ln1_w, w_in, conv_w, conv_b, w_rg_a, b_rg_a, w_rg_i, b_rg_i, rg_lambda, b_forget,
           q_norm_w, k_norm_w, rec_out_norm_w, attn_out_norm_w, w_out, ln2_w, w_router, b_router,
           w_gate_up, b_gate_up, w_down, b_down):
    for l in range(ln1_w.shape[0]):
        x = _layer(x, ln1_w[l], w_in[l], conv_w[l], conv_b[l], w_rg_a[l], b_rg_a[l], w_rg_i[l],
                   b_rg_i[l], rg_lambda[l], b_forget[l], q_norm_w[l], k_norm_w[l], rec_out_norm_w[l],
                   attn_out_norm_w[l], w_out[l], ln2_w[l], w_router[l], b_router[l], w_gate_up[l],
                   b_gate_up[l], w_down[l], b_down[l])
    return x
```

```python
import functools

import jax
import jax.numpy as jnp
from jax import lax
from jax.experimental import pallas as pl
from jax.experimental.pallas import tpu as pltpu

D_MODEL = 1024
D_REC = 512
N_REC_BLOCKS = 8
REC_BLOCK = 64
CONV_WIDTH = 4
RGLRU_C = 8.0
D_ATT = 512
HEAD_DIM = 64
N_HEADS = 8
N_EXPERTS = 32
TOP_K = 4
D_EXPERT = 1024
SWIGLU_LIMIT = 7.0
SWIGLU_ALPHA = 1.702
EPS = 1e-6

LANES = 128
SLAB = LANES
VMEM_LIMIT = 56 * 1024 * 1024

F32 = jnp.float32
BF16 = jnp.bfloat16

O_U = 0
O_G = O_U + D_REC
O_Q = O_G + D_REC
O_K = O_Q + N_HEADS * SLAB
O_V = O_K + N_HEADS * SLAB
O_F = O_V + N_HEADS * SLAB
N_PROJ = O_F + LANES

L_C = HEAD_DIM
L_ONE = HEAD_DIM + 3

TM_IN = 512
TS_REC = 512
TQ = 512
TK = 512
TM_OUT = 512
TM_E = 256

UNIT = 16
TILE_SLOTS = -(-(TM_OUT * TOP_K + N_EXPERTS * (UNIT - 1)) // LANES) * LANES
TILE_UNITS = TILE_SLOTS // UNIT
BLOCK_UNITS = TM_E // UNIT
SLOT_CHUNK = 512


def _split3(v):
    hi = v.astype(BF16).astype(F32)
    r = v - hi
    mid = r.astype(BF16).astype(F32)
    lo = (r - mid).astype(BF16).astype(F32)
    return hi, mid, lo


def _dot(a, b):
    return jnp.dot(a, b, preferred_element_type=F32)


def _dot_t(a, b):
    return lax.dot_general(a, b, (((0,), (0,)), ((), ())), preferred_element_type=F32)


def _in_proj_kernel(x_ref, ln_ref, w_ref, bf_ref, qw_ref, kw_ref,
                    u_ref, g_ref, q_ref, k_ref, v_ref, carry_ref, *, tiles_per_seq):
    i = pl.program_id(0)
    tm = x_ref.shape[0]

    @pl.when(i % tiles_per_seq == 0)
    def _():
        carry_ref[...] = jnp.zeros_like(carry_ref)

    x = x_ref[...]
    ms = jnp.mean(x * x, axis=-1, keepdims=True)
    h = (x * lax.rsqrt(ms + EPS) * ln_ref[...]).astype(BF16)

    u_ref[...] = _dot(h, w_ref[:, O_U:O_U + D_REC])
    g_ref[...] = _dot(h, w_ref[:, O_G:O_G + D_REC])

    f = _dot(h, w_ref[:, O_F:O_F + LANES]) + bf_ref[...]
    logf = jnp.minimum(f, 0.0) - jnp.log1p(jnp.exp(-jnp.abs(f)))
    row = lax.broadcasted_iota(jnp.int32, (tm, tm), 0)
    col = lax.broadcasted_iota(jnp.int32, (tm, tm), 1)
    tri = (row >= col).astype(BF16)
    hi, mid, lo = _split3(logf)
    c = (_dot(tri, hi.astype(BF16)) + _dot(tri, mid.astype(BF16))
         + _dot(tri, lo.astype(BF16)) + carry_ref[...])
    carry_ref[...] = c[tm - 1:tm, :]

    lane = lax.broadcasted_iota(jnp.int32, (tm, SLAB), 1)
    q_all = _dot(h, w_ref[:, O_Q:O_Q + N_HEADS * SLAB])
    k_all = _dot(h, w_ref[:, O_K:O_K + N_HEADS * SLAB])
    v_all = _dot(h, w_ref[:, O_V:O_V + N_HEADS * SLAB])
    qw = qw_ref[...]
    kw = kw_ref[...]
    for hd in range(N_HEADS):
        sl = slice(hd * SLAB, (hd + 1) * SLAB)
        qs = q_all[:, sl]
        ks = k_all[:, sl]
        qn = qs * lax.rsqrt(jnp.sum(qs * qs, axis=-1, keepdims=True) * (1.0 / HEAD_DIM) + EPS) * qw
        kn = ks * lax.rsqrt(jnp.sum(ks * ks, axis=-1, keepdims=True) * (1.0 / HEAD_DIM) + EPS) * kw
        c_hi, c_mid, c_lo = _split3(c[:, hd:hd + 1])
        qp = jnp.where(lane == L_C, c_hi, qn)
        qp = jnp.where(lane == L_C + 1, c_mid, qp)
        qp = jnp.where(lane == L_C + 2, c_lo, qp)
        qp = jnp.where((lane >= L_ONE) & (lane < L_ONE + 3), 1.0, qp)
        kp = jnp.where((lane >= L_C) & (lane < L_C + 3), 1.0, kn)
        kp = jnp.where(lane == L_ONE, -c_hi, kp)
        kp = jnp.where(lane == L_ONE + 1, -c_mid, kp)
        kp = jnp.where(lane == L_ONE + 2, -c_lo, kp)
        q_ref[0, hd] = qp.astype(BF16)
        k_ref[0, hd] = kp.astype(BF16)
        v_ref[0, hd] = jnp.where(lane == HEAD_DIM, 1.0, v_all[:, sl]).astype(BF16)


def _in_proj(x2d, ln1_w, w_all, bf_pad, qw_slab, kw_slab, batch, seq):
    t = x2d.shape[0]
    tiles_per_seq = seq // TM_IN
    slab_shape = jax.ShapeDtypeStruct((batch, N_HEADS, seq, SLAB), BF16)
    slab_spec = pl.BlockSpec((1, N_HEADS, TM_IN, SLAB),
                             lambda i: (i // tiles_per_seq, 0, i % tiles_per_seq, 0))
    row_spec = pl.BlockSpec((TM_IN, D_REC), lambda i: (i, 0))
    const = lambda shape: pl.BlockSpec(shape, lambda i: (0,) * len(shape))
    return pl.pallas_call(
        functools.partial(_in_proj_kernel, tiles_per_seq=tiles_per_seq),
        out_shape=(jax.ShapeDtypeStruct((t, D_REC), F32), jax.ShapeDtypeStruct((t, D_REC), F32),
                   slab_shape, slab_shape, slab_shape),
        grid=(t // TM_IN,),
        in_specs=[pl.BlockSpec((TM_IN, D_MODEL), lambda i: (i, 0)),
                  const((1, D_MODEL)), const((D_MODEL, N_PROJ)), const((1, LANES)),
                  const((1, SLAB)), const((1, SLAB))],
        out_specs=(row_spec, row_spec, slab_spec, slab_spec, slab_spec),
        scratch_shapes=[pltpu.VMEM((1, LANES), F32)],
        compiler_params=pltpu.CompilerParams(dimension_semantics=("arbitrary",),
                                             vmem_limit_bytes=VMEM_LIMIT),
        name="in_proj",
    )(x2d, ln1_w, w_all, bf_pad, qw_slab, kw_slab)


def _rglru_kernel(u_ref, g_ref, cw_ref, cb_ref, wa_ref, ba_ref, wi_ref, bi_ref, lam_ref, nw_ref,
                  o_ref, ext_ref, a_ref, b_ref, h_ref, hcar_ref):
    s = pl.program_id(1)
    ts = u_ref.shape[1]
    pad = 8

    @pl.when(s == 0)
    def _():
        ext_ref[0:pad, :] = jnp.zeros((pad, D_REC), F32)
        hcar_ref[...] = jnp.zeros_like(hcar_ref)

    @pl.when(s != 0)
    def _():
        ext_ref[0:pad, :] = ext_ref[ts:ts + pad, :]

    u = u_ref[0]
    ext_ref[pad:pad + ts, :] = u
    xc = cb_ref[...] + u * cw_ref[CONV_WIDTH - 1:CONV_WIDTH, :]
    for tap in range(CONV_WIDTH - 1):
        d = CONV_WIDTH - 1 - tap
        xc = xc + ext_ref[pad - d:pad - d + ts, :] * cw_ref[tap:tap + 1, :]

    xb = xc.astype(BF16)
    r = jax.nn.sigmoid(_dot(xb, wa_ref[...]) + ba_ref[...])
    ig = jax.nn.sigmoid(_dot(xb, wi_ref[...]) + bi_ref[...])
    nlam = -lam_ref[...]
    softplus = jnp.maximum(nlam, 0.0) + jnp.log1p(jnp.exp(-jnp.abs(nlam)))
    log_a = -RGLRU_C * r * softplus
    a = jnp.exp(log_a)
    a_ref[...] = a
    b_ref[...] = jnp.sqrt(1.0 - jnp.exp(2.0 * log_a)) * (ig * xc)

    def step(t, h):
        h = a_ref[pl.ds(t, 1), :] * h + b_ref[pl.ds(t, 1), :]
        h_ref[pl.ds(t, 1), :] = h
        return h

    hcar_ref[...] = lax.fori_loop(0, ts, step, hcar_ref[...], unroll=8)

    rec = h_ref[...] * jax.nn.gelu(g_ref[0])
    ms = jnp.mean(rec * rec, axis=-1, keepdims=True)
    o_ref[0] = (rec * lax.rsqrt(ms + EPS) * nw_ref[...]).astype(BF16)


def _rglru(u, g, conv_w, conv_b, wa_bd, b_a, wi_bd, b_i, lam, norm_w):
    batch, seq, _ = u.shape
    seq_spec = pl.BlockSpec((1, TS_REC, D_REC), lambda b, s: (b, s, 0))
    const = lambda shape: pl.BlockSpec(shape, lambda b, s: (0,) * len(shape))
    return pl.pallas_call(
        _rglru_kernel,
        out_shape=jax.ShapeDtypeStruct((batch, seq, D_REC), BF16),
        grid=(batch, seq // TS_REC),
        in_specs=[seq_spec, seq_spec, const((CONV_WIDTH, D_REC)), const((1, D_REC)),
                  const((D_REC, D_REC)), const((1, D_REC)), const((D_REC, D_REC)), const((1, D_REC)),
                  const((1, D_REC)), const((1, D_REC))],
        out_specs=seq_spec,
        scratch_shapes=[pltpu.VMEM((TS_REC + 8, D_REC), F32), pltpu.VMEM((TS_REC, D_REC), F32),
                        pltpu.VMEM((TS_REC, D_REC), F32), pltpu.VMEM((TS_REC, D_REC), F32),
                        pltpu.VMEM((1, D_REC), F32)],
        compiler_params=pltpu.CompilerParams(dimension_semantics=("arbitrary", "arbitrary"),
                                             vmem_limit_bytes=VMEM_LIMIT),
        name="rglru",
    )(u, g, conv_w, conv_b, wa_bd, b_a, wi_bd, b_i, lam, norm_w)


def _attn_kernel(q_ref, k_ref, v_ref, o_ref):
    qi = pl.program_id(2)
    q = q_ref[0, 0]
    tq = q.shape[0]

    def step(j, carry, masked):
        m, acc = carry
        start = pl.multiple_of(j * TK, TK)
        kj = k_ref[0, 0, pl.ds(start, TK), :]
        vj = v_ref[0, 0, pl.ds(start, TK), :]
        s = lax.dot_general(q, kj, (((1,), (1,)), ((), ())), preferred_element_type=F32)
        if masked:
            row = lax.broadcasted_iota(jnp.int32, (tq, TK), 0)
            col = lax.broadcasted_iota(jnp.int32, (tq, TK), 1)
            s = jnp.where(row >= col, s, -jnp.inf)
        m_new = jnp.maximum(m, jnp.max(s, axis=-1, keepdims=True))
        p = jnp.exp(s - m_new)
        acc = jnp.exp(m - m_new) * acc + _dot(p.astype(BF16), vj)
        return m_new, acc

    init = (jnp.full((tq, 1), -jnp.inf, F32), jnp.zeros((tq, SLAB), F32))
    carry = lax.fori_loop(0, qi, lambda j, c: step(j, c, False), init)
    _, acc = step(qi, carry, True)
    lane = lax.broadcasted_iota(jnp.int32, (tq, SLAB), 1)
    o_ref[0] = jnp.where(lane < HEAD_DIM, acc / acc[:, HEAD_DIM:HEAD_DIM + 1], 0.0)


def _attention(qp, kp, vp):
    batch, _, seq, _ = qp.shape
    kv_spec = pl.BlockSpec((1, 1, seq, SLAB), lambda b, h, i: (b, h, 0, 0))
    return pl.pallas_call(
        _attn_kernel,
        out_shape=jax.ShapeDtypeStruct((batch, seq, N_HEADS * SLAB), F32),
        grid=(batch, N_HEADS, seq // TQ),
        in_specs=[pl.BlockSpec((1, 1, TQ, SLAB), lambda b, h, i: (b, h, i, 0)), kv_spec, kv_spec],
        out_specs=pl.BlockSpec((1, TQ, SLAB), lambda b, h, i: (b, i, h)),
        compiler_params=pltpu.CompilerParams(
            dimension_semantics=("arbitrary", "arbitrary", "arbitrary"),
            vmem_limit_bytes=VMEM_LIMIT),
        name="attn",
    )(qp, kp, vp)


def _slot_onehot(slots, first, width):
    tm = slots[0].shape[0]
    col = lax.broadcasted_iota(jnp.int32, (tm, width), 1) + first
    hit = (col == slots[0]) | (col == slots[1]) | (col == slots[2]) | (col == slots[3])
    return jnp.where(hit, 1.0, 0.0).astype(BF16)


def _out_proj_kernel(x_ref, rec_ref, att_ref, aw_ref, wr_ref, wa_ref, ln_ref, rwh_ref, rwl_ref, rb_ref,
                     x1_ref, xs_ref, meta_ref, units_ref):
    tm = x_ref.shape[0]

    att = att_ref[...]
    ms = jnp.sum(att * att, axis=-1, keepdims=True) * (1.0 / D_ATT)
    att_n = (att * lax.rsqrt(ms + EPS) * aw_ref[...]).astype(BF16)
    x1 = x_ref[...] + _dot(rec_ref[...], wr_ref[...]) + _dot(att_n, wa_ref[...])
    x1_ref[...] = x1

    ms2 = jnp.mean(x1 * x1, axis=-1, keepdims=True)
    h2 = x1 * lax.rsqrt(ms2 + EPS) * ln_ref[...]

    h_hi = h2.astype(BF16)
    h_lo = (h2 - h_hi.astype(F32)).astype(BF16)
    logits = (_dot(h_hi, rwh_ref[...]) + _dot(h_lo, rwh_ref[...]) + _dot(h_hi, rwl_ref[...])
              + rb_ref[...])
    lane = lax.broadcasted_iota(jnp.int32, (tm, LANES), 1)
    lane_f = lane.astype(F32)
    l = jnp.where(lane < N_EXPERTS, logits, -jnp.inf)

    vals, idxs, sels = [], [], []
    for _ in range(TOP_K):
        m = jnp.max(l, axis=-1, keepdims=True)
        idx = jnp.min(jnp.where(l == m, lane_f, float(LANES)), axis=-1, keepdims=True)
        sel = lane_f == idx
        vals.append(m)
        idxs.append(idx)
        sels.append(sel)
        l = jnp.where(sel, -jnp.inf, l)
    exps = [jnp.exp(v - vals[0]) for v in vals]
    denom = exps[0] + exps[1] + exps[2] + exps[3]

    any_sel = sels[0] | sels[1] | sels[2] | sels[3]
    onehot = jnp.where(any_sel, 1.0, 0.0)
    row = lax.broadcasted_iota(jnp.int32, (tm, tm), 0)
    col = lax.broadcasted_iota(jnp.int32, (tm, tm), 1)
    before = (row > col).astype(BF16)
    rank = _dot(before, onehot.astype(BF16))
    count = jnp.sum(onehot, axis=0, keepdims=True)
    units = jnp.floor((count + (UNIT - 1)) * (1.0 / UNIT))
    e_row = lax.broadcasted_iota(jnp.int32, (LANES, LANES), 0)
    e_col = lax.broadcasted_iota(jnp.int32, (LANES, LANES), 1)
    earlier = (e_row < e_col).astype(BF16)
    units8 = jnp.broadcast_to(units, (8, LANES))
    first_unit = _dot(units8.astype(BF16), earlier)[0:1, :]
    units_ref[0] = units8
    place = first_unit * float(UNIT) + rank

    meta = jnp.zeros((tm, LANES), F32)
    slots = []
    for k in range(TOP_K):
        slot = jnp.sum(jnp.where(sels[k], place, 0.0), axis=-1, keepdims=True)
        slots.append(slot.astype(jnp.int32))
        meta = jnp.where(lane == k, idxs[k], meta)
        meta = jnp.where(lane == TOP_K + k, exps[k] / denom, meta)
        meta = jnp.where(lane == 2 * TOP_K + k, slot, meta)
    meta_ref[...] = meta

    for c in range(TILE_SLOTS // SLOT_CHUNK):
        sel_t = _slot_onehot(slots, c * SLOT_CHUNK, SLOT_CHUNK)
        xs_ref[c * SLOT_CHUNK:(c + 1) * SLOT_CHUNK, :] = _dot_t(sel_t, h_hi).astype(BF16)


def _out_proj(x2d, rec_n, att, aw_slab, w_rec, w_att, ln2_w, rw_hi, rw_lo, rb_pad):
    t = x2d.shape[0]
    n_tiles = t // TM_OUT
    rows = lambda n: pl.BlockSpec((TM_OUT, n), lambda i: (i, 0))
    const = lambda shape: pl.BlockSpec(shape, lambda i: (0,) * len(shape))
    return pl.pallas_call(
        _out_proj_kernel,
        out_shape=(jax.ShapeDtypeStruct((t, D_MODEL), F32),
                   jax.ShapeDtypeStruct((n_tiles * TILE_SLOTS, D_MODEL), BF16),
                   jax.ShapeDtypeStruct((t, LANES), F32),
                   jax.ShapeDtypeStruct((n_tiles, 8, LANES), F32)),
        grid=(n_tiles,),
        in_specs=[rows(D_MODEL), rows(D_REC), rows(N_HEADS * SLAB), const((1, N_HEADS * SLAB)),
                  const((D_REC, D_MODEL)), const((N_HEADS * SLAB, D_MODEL)), const((1, D_MODEL)),
                  const((D_MODEL, LANES)), const((D_MODEL, LANES)), const((1, LANES))],
        out_specs=(rows(D_MODEL), pl.BlockSpec((TILE_SLOTS, D_MODEL), lambda i: (i, 0)), rows(LANES),
                   pl.BlockSpec((1, 8, LANES), lambda i: (i, 0, 0))),
        compiler_params=pltpu.CompilerParams(dimension_semantics=("arbitrary",),
                                             vmem_limit_bytes=VMEM_LIMIT),
        name="out_proj",
    )(x2d, rec_n, att, aw_slab, w_rec, w_att, ln2_w, rw_hi, rw_lo, rb_pad)


def _start_units(unit_ref, base, n_units, src_hbm, dst, sem):
    def body(j, carry):
        src_row = pl.multiple_of(unit_ref[base + j] * UNIT, UNIT)
        dst_row = pl.multiple_of(j * UNIT, UNIT)
        pltpu.make_async_copy(src_hbm.at[pl.ds(src_row, UNIT), :],
                              dst.at[pl.ds(dst_row, UNIT), :], sem).start()
        return carry
    lax.fori_loop(0, n_units, body, 0, unroll=8)


def _wait_units(n_units, src_hbm, dst, sem):
    pltpu.make_async_copy(src_hbm.at[pl.ds(0, n_units * UNIT), :], dst, sem).wait()


def _experts_kernel(be_ref, nblk_ref, unit_ref, xs_hbm, wgu_ref, bgu_ref, wd_ref, bd_ref,
                    ys_ref, xbuf, wgu_bf, wd_bf, sem):
    b = pl.program_id(0)
    nblk = nblk_ref[0]
    slot = b % 2

    @pl.when(b == 0)
    def _():
        _start_units(unit_ref, 0, BLOCK_UNITS, xs_hbm, xbuf.at[0], sem.at[0])

    @pl.when(b + 1 < nblk)
    def _():
        _start_units(unit_ref, (b + 1) * BLOCK_UNITS, BLOCK_UNITS, xs_hbm, xbuf.at[1 - slot],
                     sem.at[1 - slot])

    new_expert = jnp.logical_or(b == 0, be_ref[b] != be_ref[jnp.maximum(b - 1, 0)])

    @pl.when(jnp.logical_and(b < nblk, new_expert))
    def _():
        wgu_bf[...] = wgu_ref[0].astype(BF16)
        wd_bf[...] = wd_ref[0].astype(BF16)

    @pl.when(b < nblk)
    def _():
        _wait_units(BLOCK_UNITS, xs_hbm, xbuf.at[slot], sem.at[slot])
        gu = _dot(xbuf[slot], wgu_bf[...]) + bgu_ref[0]
        g = jnp.minimum(gu[:, :D_EXPERT], SWIGLU_LIMIT)
        u = jnp.clip(gu[:, D_EXPERT:], -SWIGLU_LIMIT, SWIGLU_LIMIT)
        act = (u + 1.0) * (g * jax.nn.sigmoid(SWIGLU_ALPHA * g))
        ys_ref[...] = _dot(act.astype(BF16), wd_bf[...]) + bd_ref[0]

    @pl.when(b >= nblk)
    def _():
        ys_ref[...] = jnp.zeros_like(ys_ref)


def _experts(block_expert, n_blocks_used, unit_src, xs, w_gate_up, b_gate_up, w_down, b_down):
    n_blocks = block_expert.shape[0]
    grid_spec = pltpu.PrefetchScalarGridSpec(
        num_scalar_prefetch=3,
        grid=(n_blocks,),
        in_specs=[pl.BlockSpec(memory_space=pl.ANY),
                  pl.BlockSpec((1, D_MODEL, 2 * D_EXPERT), lambda b, be, nb, us: (be[b], 0, 0)),
                  pl.BlockSpec((1, 1, 2 * D_EXPERT), lambda b, be, nb, us: (be[b], 0, 0)),
                  pl.BlockSpec((1, D_EXPERT, D_MODEL), lambda b, be, nb, us: (be[b], 0, 0)),
                  pl.BlockSpec((1, 1, D_MODEL), lambda b, be, nb, us: (be[b], 0, 0))],
        out_specs=pl.BlockSpec((TM_E, D_MODEL), lambda b, be, nb, us: (b, 0)),
        scratch_shapes=[pltpu.VMEM((2, TM_E, D_MODEL), BF16),
                        pltpu.VMEM((D_MODEL, 2 * D_EXPERT), BF16),
                        pltpu.VMEM((D_EXPERT, D_MODEL), BF16),
                        pltpu.SemaphoreType.DMA((2,))],
    )
    return pl.pallas_call(
        _experts_kernel,
        out_shape=jax.ShapeDtypeStruct((n_blocks * TM_E, D_MODEL), F32),
        grid_spec=grid_spec,
        compiler_params=pltpu.CompilerParams(dimension_semantics=("arbitrary",),
                                             vmem_limit_bytes=VMEM_LIMIT),
        name="experts",
    )(block_expert, n_blocks_used, unit_src, xs, w_gate_up,
      b_gate_up.reshape(N_EXPERTS, 1, 2 * D_EXPERT), w_down, b_down.reshape(N_EXPERTS, 1, D_MODEL))


def _combine_kernel(unit_ref, x1_ref, meta_ref, ys_hbm, o_ref, buf, sem):
    i = pl.program_id(0)
    n = pl.num_programs(0)
    slot = i % 2
    tm = x1_ref.shape[0]

    @pl.when(i == 0)
    def _():
        _start_units(unit_ref, 0, TILE_UNITS, ys_hbm, buf.at[0], sem.at[0])

    @pl.when(i + 1 < n)
    def _():
        _start_units(unit_ref, (i + 1) * TILE_UNITS, TILE_UNITS, ys_hbm, buf.at[1 - slot],
                     sem.at[1 - slot])

    meta = meta_ref[...]
    slots = [meta[:, 2 * TOP_K + k:2 * TOP_K + k + 1].astype(jnp.int32) for k in range(TOP_K)]
    gates = [meta[:, TOP_K + k:TOP_K + k + 1] for k in range(TOP_K)]
    ones = jnp.ones((tm, LANES), BF16)
    _wait_units(TILE_UNITS, ys_hbm, buf.at[slot], sem.at[slot])

    acc = x1_ref[...]
    for c in range(TILE_SLOTS // SLOT_CHUNK):
        first = c * SLOT_CHUNK
        col = lax.broadcasted_iota(jnp.int32, (tm, SLOT_CHUNK), 1) + first
        weight = jnp.zeros((tm, SLOT_CHUNK), F32)
        for k in range(TOP_K):
            weight = jnp.where(col == slots[k], gates[k], weight)
        w_hi = weight.astype(BF16)
        w_lo = (weight - w_hi.astype(F32)).astype(BF16)
        slot_gate = _dot_t(w_hi, ones) + _dot_t(w_lo, ones)
        y = buf[slot, first:first + SLOT_CHUNK, :] * slot_gate[:, 0:1]
        y_hi = y.astype(BF16)
        y_lo = (y - y_hi.astype(F32)).astype(BF16)
        sel_t = _slot_onehot(slots, first, SLOT_CHUNK)
        acc = acc + _dot(sel_t, y_hi) + _dot(sel_t, y_lo)
    o_ref[...] = acc


def _combine(tile_units, x1, meta, ys):
    t = x1.shape[0]
    grid_spec = pltpu.PrefetchScalarGridSpec(
        num_scalar_prefetch=1,
        grid=(t // TM_OUT,),
        in_specs=[pl.BlockSpec((TM_OUT, D_MODEL), lambda i, us: (i, 0)),
                  pl.BlockSpec((TM_OUT, LANES), lambda i, us: (i, 0)),
                  pl.BlockSpec(memory_space=pl.ANY)],
        out_specs=pl.BlockSpec((TM_OUT, D_MODEL), lambda i, us: (i, 0)),
        scratch_shapes=[pltpu.VMEM((2, TILE_SLOTS, D_MODEL), F32),
                        pltpu.SemaphoreType.DMA((2,))],
    )
    return pl.pallas_call(
        _combine_kernel,
        out_shape=jax.ShapeDtypeStruct((t, D_MODEL), F32),
        grid_spec=grid_spec,
        compiler_params=pltpu.CompilerParams(dimension_semantics=("arbitrary",),
                                             vmem_limit_bytes=VMEM_LIMIT),
        name="combine",
    )(tile_units, x1, meta, ys)


def _head_slabs(w):
    d = w.shape[0]
    w = w.reshape(d, N_HEADS, HEAD_DIM)
    return jnp.pad(w, ((0, 0), (0, 0), (0, SLAB - HEAD_DIM))).reshape(d, N_HEADS * SLAB)


def _block_diag(w):
    g = w.shape[0]
    eye = jnp.eye(g, dtype=w.dtype)
    return jnp.einsum('gij,gh->gihj', w, eye).reshape(g * REC_BLOCK, g * REC_BLOCK)


def _pad_lanes(v, n=LANES):
    return jnp.pad(v, (0, n - v.shape[0])).reshape(1, n)


def _excl_cumsum(a, axis):
    return jnp.cumsum(a, axis=axis) - a


def _dispatch_tables(units, n_blocks):
    n_tiles = units.shape[0]
    tile_first = _excl_cumsum(units, 1)
    run_first = _excl_cumsum(units, 0)
    run_end = run_first + units
    region_units = jnp.sum(units, axis=0)
    region_blocks = (region_units + BLOCK_UNITS - 1) // BLOCK_UNITS
    block_end = jnp.cumsum(region_blocks)
    region_first = (block_end - region_blocks) * BLOCK_UNITS
    n_used = block_end[-1:]

    blocks = jnp.arange(n_blocks, dtype=jnp.int32)
    block_expert = jnp.minimum(
        jnp.sum((block_end[None, :] <= blocks[:, None]).astype(jnp.int32), axis=1), N_EXPERTS - 1)

    g = jnp.arange(n_blocks * BLOCK_UNITS, dtype=jnp.int32)
    e = block_expert[g // BLOCK_UNITS]
    local = g - region_first[e]
    valid = (local < region_units[e]) & (g // BLOCK_UNITS < n_used[0])
    ends = run_end.T[e]
    tile = jnp.minimum(jnp.sum((ends <= local[:, None]).astype(jnp.int32), axis=1), n_tiles - 1)
    src = tile * TILE_UNITS + tile_first[tile, e] + (local - run_first[tile, e])
    unit_src = jnp.where(valid, src, 0).astype(jnp.int32)

    j = jnp.arange(TILE_UNITS, dtype=jnp.int32)
    tile_end = tile_first + units
    ej = jnp.minimum(jnp.sum((tile_end[:, None, :] <= j[None, :, None]).astype(jnp.int32), axis=2),
                     N_EXPERTS - 1)
    used = j[None, :] < jnp.sum(units, axis=1)[:, None]
    rows = jnp.arange(n_tiles, dtype=jnp.int32)[:, None]
    dst = region_first[ej] + run_first[rows, ej] + (j[None, :] - tile_first[rows, ej])
    tile_units = jnp.where(used, dst, 0).astype(jnp.int32).reshape(-1)
    return block_expert.astype(jnp.int32), n_used.astype(jnp.int32), unit_src, tile_units


def _layer(x, ln1_w, w_in, conv_w, conv_b, w_rg_a, b_rg_a, w_rg_i, b_rg_i, rg_lambda, b_forget,
           q_norm_w, k_norm_w, rec_out_norm_w, attn_out_norm_w, w_out, ln2_w, w_router, b_router,
           w_gate_up, b_gate_up, w_down, b_down):
    batch, seq, _ = x.shape
    t = batch * seq
    x2d = x.reshape(t, D_MODEL)

    o_q = 2 * D_REC
    o_k = o_q + D_ATT
    o_v = o_k + D_ATT
    o_f = o_v + D_ATT
    w_all = jnp.concatenate(
        [w_in[:, :o_q], _head_slabs(w_in[:, o_q:o_k]), _head_slabs(w_in[:, o_k:o_v]),
         _head_slabs(w_in[:, o_v:o_f]), jnp.pad(w_in[:, o_f:], ((0, 0), (0, LANES - N_HEADS)))],
        axis=1).astype(BF16)
    scale = HEAD_DIM ** -0.5
    u, g, qp, kp, vp = _in_proj(x2d, ln1_w.reshape(1, D_MODEL), w_all, _pad_lanes(b_forget),
                                _pad_lanes(q_norm_w * scale), _pad_lanes(k_norm_w), batch, seq)

    rec_n = _rglru(u.reshape(batch, seq, D_REC), g.reshape(batch, seq, D_REC), conv_w,
                   conv_b.reshape(1, D_REC), _block_diag(w_rg_a).astype(BF16), b_rg_a.reshape(1, D_REC),
                   _block_diag(w_rg_i).astype(BF16), b_rg_i.reshape(1, D_REC),
                   rg_lambda.reshape(1, D_REC), rec_out_norm_w.reshape(1, D_REC))

    att = _attention(qp, kp, vp)

    aw_slab = _head_slabs(attn_out_norm_w.reshape(1, D_ATT))
    w_att = jnp.pad(w_out[D_REC:].reshape(N_HEADS, HEAD_DIM, D_MODEL),
                    ((0, 0), (0, SLAB - HEAD_DIM), (0, 0))).reshape(N_HEADS * SLAB, D_MODEL)
    rw = jnp.pad(w_router, ((0, 0), (0, LANES - N_EXPERTS)))
    rw_hi = rw.astype(BF16)
    rw_lo = (rw - rw_hi.astype(F32)).astype(BF16)
    x1, xs, meta, units = _out_proj(
        x2d, rec_n.reshape(t, D_REC), att.reshape(t, N_HEADS * SLAB), aw_slab,
        w_out[:D_REC].astype(BF16), w_att.astype(BF16), ln2_w.reshape(1, D_MODEL),
        rw_hi, rw_lo, _pad_lanes(b_router))

    n_tiles = t // TM_OUT
    max_units = t * TOP_K // UNIT + n_tiles * N_EXPERTS
    n_blocks = -(-max_units // BLOCK_UNITS) + N_EXPERTS
    block_expert, n_used, unit_src, tile_units = _dispatch_tables(
        units[:, 0, :N_EXPERTS].astype(jnp.int32), n_blocks)

    ys = _experts(block_expert, n_used, unit_src, xs, w_gate_up, b_gate_up, w_down, b_down)
    out = _combine(tile_units, x1, meta, ys)
    return out.reshape(batch, seq, D_MODEL)


def kernel(x, ln1_w, w_in, conv_w, conv_b, w_rg_a, b_rg_a, w_rg_i, b_rg_i, rg_lambda, b_forget,
           q_norm_w, k_norm_w, rec_out_norm_w, attn_out_norm_w, w_out, ln2_w, w_router, b_router,
           w_gate_up, b_gate_up, w_down, b_down):
    for l in range(ln1_w.shape[0]):
        x = _layer(x, ln1_w[l], w_in[l], conv_w[l], conv_b[l], w_rg_a[l], b_rg_a[l], w_rg_i[l],
                   b_rg_i[l], rg_lambda[l], b_forget[l], q_norm_w[l], k_norm_w[l], rec_out_norm_w[l],
                   attn_out_norm_w[l], w_out[l], ln2_w[l], w_router[l], b_router[l], w_gate_up[l],
                   b_gate_up[l], w_down[l], b_down[l])
    return x
```

```python
import functools

import jax
import jax.numpy as jnp
from jax import lax
from jax.experimental import pallas as pl
from jax.experimental.pallas import tpu as pltpu

D_MODEL = 1024
D_REC = 512
N_REC_BLOCKS = 8
REC_BLOCK = 64
CONV_WIDTH = 4
RGLRU_C = 8.0
D_ATT = 512
HEAD_DIM = 64
N_HEADS = 8
N_EXPERTS = 32
TOP_K = 4
D_EXPERT = 1024
SWIGLU_LIMIT = 7.0
SWIGLU_ALPHA = 1.702
EPS = 1e-6

LANES = 128
SLAB = LANES
VMEM_LIMIT = 56 * 1024 * 1024

F32 = jnp.float32
BF16 = jnp.bfloat16

O_U = 0
O_G = O_U + D_REC
O_Q = O_G + D_REC
O_K = O_Q + N_HEADS * SLAB
O_V = O_K + N_HEADS * SLAB
O_F = O_V + N_HEADS * SLAB
N_PROJ = O_F + LANES

L_C = HEAD_DIM
L_ONE = HEAD_DIM + 3

TM_IN = 512
TS_REC = 512
TQ = 512
TK = 512
TM_OUT = 512
TM_E = 256

UNIT = 16
TILE_SLOTS = -(-(TM_OUT * TOP_K + N_EXPERTS * (UNIT - 1)) // LANES) * LANES
TILE_UNITS = TILE_SLOTS // UNIT
BLOCK_UNITS = TM_E // UNIT
SLOT_CHUNK = 512


def _split3(v):
    hi = v.astype(BF16).astype(F32)
    r = v - hi
    mid = r.astype(BF16).astype(F32)
    lo = (r - mid).astype(BF16).astype(F32)
    return hi, mid, lo


def _dot(a, b):
    return jnp.dot(a, b, preferred_element_type=F32)


def _dot_t(a, b):
    return lax.dot_general(a, b, (((0,), (0,)), ((), ())), preferred_element_type=F32)


def _in_proj_kernel(x_ref, ln_ref, w_ref, bf_ref, qw_ref, kw_ref,
                    u_ref, g_ref, q_ref, k_ref, v_ref, carry_ref, *, tiles_per_seq):
    i = pl.program_id(0)
    tm = x_ref.shape[0]

    @pl.when(i % tiles_per_seq == 0)
    def _():
        carry_ref[...] = jnp.zeros_like(carry_ref)

    x = x_ref[...]
    ms = jnp.mean(x * x, axis=-1, keepdims=True)
    h = (x * lax.rsqrt(ms + EPS) * ln_ref[...]).astype(BF16)

    u_ref[...] = _dot(h, w_ref[:, O_U:O_U + D_REC])
    g_ref[...] = _dot(h, w_ref[:, O_G:O_G + D_REC])

    f = _dot(h, w_ref[:, O_F:O_F + LANES]) + bf_ref[...]
    logf = jnp.minimum(f, 0.0) - jnp.log1p(jnp.exp(-jnp.abs(f)))
    row = lax.broadcasted_iota(jnp.int32, (tm, tm), 0)
    col = lax.broadcasted_iota(jnp.int32, (tm, tm), 1)
    tri = (row >= col).astype(BF16)
    hi, mid, lo = _split3(logf)
    c = (_dot(tri, hi.astype(BF16)) + _dot(tri, mid.astype(BF16))
         + _dot(tri, lo.astype(BF16)) + carry_ref[...])
    carry_ref[...] = c[tm - 1:tm, :]

    lane = lax.broadcasted_iota(jnp.int32, (tm, SLAB), 1)
    q_all = _dot(h, w_ref[:, O_Q:O_Q + N_HEADS * SLAB])
    k_all = _dot(h, w_ref[:, O_K:O_K + N_HEADS * SLAB])
    v_all = _dot(h, w_ref[:, O_V:O_V + N_HEADS * SLAB])
    qw = qw_ref[...]
    kw = kw_ref[...]
    for hd in range(N_HEADS):
        sl = slice(hd * SLAB, (hd + 1) * SLAB)
        qs = q_all[:, sl]
        ks = k_all[:, sl]
        qn = qs * lax.rsqrt(jnp.sum(qs * qs, axis=-1, keepdims=True) * (1.0 / HEAD_DIM) + EPS) * qw
        kn = ks * lax.rsqrt(jnp.sum(ks * ks, axis=-1, keepdims=True) * (1.0 / HEAD_DIM) + EPS) * kw
        c_hi, c_mid, c_lo = _split3(c[:, hd:hd + 1])
        qp = jnp.where(lane == L_C, c_hi, qn)
        qp = jnp.where(lane == L_C + 1, c_mid, qp)
        qp = jnp.where(lane == L_C + 2, c_lo, qp)
        qp = jnp.where((lane >= L_ONE) & (lane < L_ONE + 3), 1.0, qp)
        kp = jnp.where((lane >= L_C) & (lane < L_C + 3), 1.0, kn)
        kp = jnp.where(lane == L_ONE, -c_hi, kp)
        kp = jnp.where(lane == L_ONE + 1, -c_mid, kp)
        kp = jnp.where(lane == L_ONE + 2, -c_lo, kp)
        q_ref[0, hd] = qp.astype(BF16)
        k_ref[0, hd] = kp.astype(BF16)
        v_ref[0, hd] = jnp.where(lane == HEAD_DIM, 1.0, v_all[:, sl]).astype(BF16)


def _in_proj(x2d, ln1_w, w_all, bf_pad, qw_slab, kw_slab, batch, seq):
    t = x2d.shape[0]
    tiles_per_seq = seq // TM_IN
    slab_shape = jax.ShapeDtypeStruct((batch, N_HEADS, seq, SLAB), BF16)
    slab_spec = pl.BlockSpec((1, N_HEADS, TM_IN, SLAB),
                             lambda i: (i // tiles_per_seq, 0, i % tiles_per_seq, 0))
    row_spec = pl.BlockSpec((TM_IN, D_REC), lambda i: (i, 0))
    const = lambda shape: pl.BlockSpec(shape, lambda i: (0,) * len(shape))
    return pl.pallas_call(
        functools.partial(_in_proj_kernel, tiles_per_seq=tiles_per_seq),
        out_shape=(jax.ShapeDtypeStruct((t, D_REC), F32), jax.ShapeDtypeStruct((t, D_REC), F32),
                   slab_shape, slab_shape, slab_shape),
        grid=(t // TM_IN,),
        in_specs=[pl.BlockSpec((TM_IN, D_MODEL), lambda i: (i, 0)),
                  const((1, D_MODEL)), const((D_MODEL, N_PROJ)), const((1, LANES)),
                  const((1, SLAB)), const((1, SLAB))],
        out_specs=(row_spec, row_spec, slab_spec, slab_spec, slab_spec),
        scratch_shapes=[pltpu.VMEM((1, LANES), F32)],
        compiler_params=pltpu.CompilerParams(dimension_semantics=("arbitrary",),
                                             vmem_limit_bytes=VMEM_LIMIT),
        name="in_proj",
    )(x2d, ln1_w, w_all, bf_pad, qw_slab, kw_slab)


def _rglru_kernel(u_ref, g_ref, cw_ref, cb_ref, wa_ref, ba_ref, wi_ref, bi_ref, lam_ref, nw_ref,
                  o_ref, ext_ref, a_ref, b_ref, h_ref, hcar_ref):
    s = pl.program_id(1)
    ts = u_ref.shape[1]
    pad = 8

    @pl.when(s == 0)
    def _():
        ext_ref[0:pad, :] = jnp.zeros((pad, D_REC), F32)
        hcar_ref[...] = jnp.zeros_like(hcar_ref)

    @pl.when(s != 0)
    def _():
        ext_ref[0:pad, :] = ext_ref[ts:ts + pad, :]

    u = u_ref[0]
    ext_ref[pad:pad + ts, :] = u
    xc = cb_ref[...] + u * cw_ref[CONV_WIDTH - 1:CONV_WIDTH, :]
    for tap in range(CONV_WIDTH - 1):
        d = CONV_WIDTH - 1 - tap
        xc = xc + ext_ref[pad - d:pad - d + ts, :] * cw_ref[tap:tap + 1, :]

    xb = xc.astype(BF16)
    r = jax.nn.sigmoid(_dot(xb, wa_ref[...]) + ba_ref[...])
    ig = jax.nn.sigmoid(_dot(xb, wi_ref[...]) + bi_ref[...])
    nlam = -lam_ref[...]
    softplus = jnp.maximum(nlam, 0.0) + jnp.log1p(jnp.exp(-jnp.abs(nlam)))
    log_a = -RGLRU_C * r * softplus
    a = jnp.exp(log_a)
    a_ref[...] = a
    b_ref[...] = jnp.sqrt(1.0 - jnp.exp(2.0 * log_a)) * (ig * xc)

    def step(t, h):
        h = a_ref[pl.ds(t, 1), :] * h + b_ref[pl.ds(t, 1), :]
        h_ref[pl.ds(t, 1), :] = h
        return h

    hcar_ref[...] = lax.fori_loop(0, ts, step, hcar_ref[...], unroll=8)

    rec = h_ref[...] * jax.nn.gelu(g_ref[0])
    ms = jnp.mean(rec * rec, axis=-1, keepdims=True)
    o_ref[0] = (rec * lax.rsqrt(ms + EPS) * nw_ref[...]).astype(BF16)


def _rglru(u, g, conv_w, conv_b, wa_bd, b_a, wi_bd, b_i, lam, norm_w):
    batch, seq, _ = u.shape
    seq_spec = pl.BlockSpec((1, TS_REC, D_REC), lambda b, s: (b, s, 0))
    const = lambda shape: pl.BlockSpec(shape, lambda b, s: (0,) * len(shape))
    return pl.pallas_call(
        _rglru_kernel,
        out_shape=jax.ShapeDtypeStruct((batch, seq, D_REC), BF16),
        grid=(batch, seq // TS_REC),
        in_specs=[seq_spec, seq_spec, const((CONV_WIDTH, D_REC)), const((1, D_REC)),
                  const((D_REC, D_REC)), const((1, D_REC)), const((D_REC, D_REC)), const((1, D_REC)),
                  const((1, D_REC)), const((1, D_REC))],
        out_specs=seq_spec,
        scratch_shapes=[pltpu.VMEM((TS_REC + 8, D_REC), F32), pltpu.VMEM((TS_REC, D_REC), F32),
                        pltpu.VMEM((TS_REC, D_REC), F32), pltpu.VMEM((TS_REC, D_REC), F32),
                        pltpu.VMEM((1, D_REC), F32)],
        compiler_params=pltpu.CompilerParams(dimension_semantics=("arbitrary", "arbitrary"),
                                             vmem_limit_bytes=VMEM_LIMIT),
        name="rglru",
    )(u, g, conv_w, conv_b, wa_bd, b_a, wi_bd, b_i, lam, norm_w)


def _attn_kernel(q_ref, k_ref, v_ref, o_ref):
    qi = pl.program_id(2)
    q = q_ref[0, 0]
    tq = q.shape[0]

    def step(j, carry, masked):
        m, acc = carry
        start = pl.multiple_of(j * TK, TK)
        kj = k_ref[0, 0, pl.ds(start, TK), :]
        vj = v_ref[0, 0, pl.ds(start, TK), :]
        s = lax.dot_general(q, kj, (((1,), (1,)), ((), ())), preferred_element_type=F32)
        if masked:
            row = lax.broadcasted_iota(jnp.int32, (tq, TK), 0)
            col = lax.broadcasted_iota(jnp.int32, (tq, TK), 1)
            s = jnp.where(row >= col, s, -jnp.inf)
        m_new = jnp.maximum(m, jnp.max(s, axis=-1, keepdims=True))
        p = jnp.exp(s - m_new)
        acc = jnp.exp(m - m_new) * acc + _dot(p.astype(BF16), vj)
        return m_new, acc

    init = (jnp.full((tq, 1), -jnp.inf, F32), jnp.zeros((tq, SLAB), F32))
    carry = lax.fori_loop(0, qi, lambda j, c: step(j, c, False), init)
    _, acc = step(qi, carry, True)
    lane = lax.broadcasted_iota(jnp.int32, (tq, SLAB), 1)
    o_ref[0] = jnp.where(lane < HEAD_DIM, acc / acc[:, HEAD_DIM:HEAD_DIM + 1], 0.0)


def _attention(qp, kp, vp):
    batch, _, seq, _ = qp.shape
    kv_spec = pl.BlockSpec((1, 1, seq, SLAB), lambda b, h, i: (b, h, 0, 0))
    return pl.pallas_call(
        _attn_kernel,
        out_shape=jax.ShapeDtypeStruct((batch, seq, N_HEADS * SLAB), F32),
        grid=(batch, N_HEADS, seq // TQ),
        in_specs=[pl.BlockSpec((1, 1, TQ, SLAB), lambda b, h, i: (b, h, i, 0)), kv_spec, kv_spec],
        out_specs=pl.BlockSpec((1, TQ, SLAB), lambda b, h, i: (b, i, h)),
        compiler_params=pltpu.CompilerParams(
            dimension_semantics=("arbitrary", "arbitrary", "arbitrary"),
            vmem_limit_bytes=VMEM_LIMIT),
        name="attn",
    )(qp, kp, vp)


def _slot_onehot(slots, first, width):
    tm = slots[0].shape[0]
    col = lax.broadcasted_iota(jnp.int32, (tm, width), 1) + first
    hit = jnp.zeros((tm, width), F32)
    for slot in slots:
        hit = jnp.where(col == slot, 1.0, hit)
    return hit.astype(BF16)


def _out_proj_kernel(x_ref, rec_ref, att_ref, aw_ref, wr_ref, wa_ref, ln_ref, rwh_ref, rwl_ref, rb_ref,
                     x1_ref, xs_ref, meta_ref, units_ref):
    tm = x_ref.shape[0]

    att = att_ref[...]
    ms = jnp.sum(att * att, axis=-1, keepdims=True) * (1.0 / D_ATT)
    att_n = (att * lax.rsqrt(ms + EPS) * aw_ref[...]).astype(BF16)
    x1 = x_ref[...] + _dot(rec_ref[...], wr_ref[...]) + _dot(att_n, wa_ref[...])
    x1_ref[...] = x1

    ms2 = jnp.mean(x1 * x1, axis=-1, keepdims=True)
    h2 = x1 * lax.rsqrt(ms2 + EPS) * ln_ref[...]

    h_hi = h2.astype(BF16)
    h_lo = (h2 - h_hi.astype(F32)).astype(BF16)
    logits = (_dot(h_hi, rwh_ref[...]) + _dot(h_lo, rwh_ref[...]) + _dot(h_hi, rwl_ref[...])
              + rb_ref[...])
    lane = lax.broadcasted_iota(jnp.int32, (tm, LANES), 1)
    lane_f = lane.astype(F32)
    l = jnp.where(lane < N_EXPERTS, logits, -jnp.inf)

    vals, idxs, sels = [], [], []
    for _ in range(TOP_K):
        m = jnp.max(l, axis=-1, keepdims=True)
        idx = jnp.min(jnp.where(l == m, lane_f, float(LANES)), axis=-1, keepdims=True)
        sel = lane_f == idx
        vals.append(m)
        idxs.append(idx)
        sels.append(sel)
        l = jnp.where(sel, -jnp.inf, l)
    exps = [jnp.exp(v - vals[0]) for v in vals]
    denom = exps[0] + exps[1] + exps[2] + exps[3]

    any_sel = sels[0] | sels[1] | sels[2] | sels[3]
    onehot = jnp.where(any_sel, 1.0, 0.0)
    row = lax.broadcasted_iota(jnp.int32, (tm, tm), 0)
    col = lax.broadcasted_iota(jnp.int32, (tm, tm), 1)
    before = (row > col).astype(BF16)
    rank = _dot(before, onehot.astype(BF16))
    count = jnp.sum(onehot, axis=0, keepdims=True)
    units = jnp.floor((count + (UNIT - 1)) * (1.0 / UNIT))
    e_row = lax.broadcasted_iota(jnp.int32, (LANES, LANES), 0)
    e_col = lax.broadcasted_iota(jnp.int32, (LANES, LANES), 1)
    earlier = (e_row < e_col).astype(BF16)
    units8 = jnp.broadcast_to(units, (8, LANES))
    first_unit = _dot(units8.astype(BF16), earlier)[0:1, :]
    units_ref[0] = units8
    place = first_unit * float(UNIT) + rank

    meta = jnp.zeros((tm, LANES), F32)
    slots = []
    for k in range(TOP_K):
        slot = jnp.sum(jnp.where(sels[k], place, 0.0), axis=-1, keepdims=True)
        slots.append(slot.astype(jnp.int32))
        meta = jnp.where(lane == k, idxs[k], meta)
        meta = jnp.where(lane == TOP_K + k, exps[k] / denom, meta)
        meta = jnp.where(lane == 2 * TOP_K + k, slot, meta)
    meta_ref[...] = meta

    for c in range(TILE_SLOTS // SLOT_CHUNK):
        sel_t = _slot_onehot(slots, c * SLOT_CHUNK, SLOT_CHUNK)
        xs_ref[c * SLOT_CHUNK:(c + 1) * SLOT_CHUNK, :] = _dot_t(sel_t, h_hi).astype(BF16)


def _out_proj(x2d, rec_n, att, aw_slab, w_rec, w_att, ln2_w, rw_hi, rw_lo, rb_pad):
    t = x2d.shape[0]
    n_tiles = t // TM_OUT
    rows = lambda n: pl.BlockSpec((TM_OUT, n), lambda i: (i, 0))
    const = lambda shape: pl.BlockSpec(shape, lambda i: (0,) * len(shape))
    return pl.pallas_call(
        _out_proj_kernel,
        out_shape=(jax.ShapeDtypeStruct((t, D_MODEL), F32),
                   jax.ShapeDtypeStruct((n_tiles * TILE_SLOTS, D_MODEL), BF16),
                   jax.ShapeDtypeStruct((t, LANES), F32),
                   jax.ShapeDtypeStruct((n_tiles, 8, LANES), F32)),
        grid=(n_tiles,),
        in_specs=[rows(D_MODEL), rows(D_REC), rows(N_HEADS * SLAB), const((1, N_HEADS * SLAB)),
                  const((D_REC, D_MODEL)), const((N_HEADS * SLAB, D_MODEL)), const((1, D_MODEL)),
                  const((D_MODEL, LANES)), const((D_MODEL, LANES)), const((1, LANES))],
        out_specs=(rows(D_MODEL), pl.BlockSpec((TILE_SLOTS, D_MODEL), lambda i: (i, 0)), rows(LANES),
                   pl.BlockSpec((1, 8, LANES), lambda i: (i, 0, 0))),
        compiler_params=pltpu.CompilerParams(dimension_semantics=("arbitrary",),
                                             vmem_limit_bytes=VMEM_LIMIT),
        name="out_proj",
    )(x2d, rec_n, att, aw_slab, w_rec, w_att, ln2_w, rw_hi, rw_lo, rb_pad)


def _start_units(unit_ref, base, n_units, src_hbm, dst, sem):
    def body(j, carry):
        src_row = pl.multiple_of(unit_ref[base + j] * UNIT, UNIT)
        dst_row = pl.multiple_of(j * UNIT, UNIT)
        pltpu.make_async_copy(src_hbm.at[pl.ds(src_row, UNIT), :],
                              dst.at[pl.ds(dst_row, UNIT), :], sem).start()
        return carry
    lax.fori_loop(0, n_units, body, 0, unroll=8)


def _wait_units(n_units, src_hbm, dst, sem):
    pltpu.make_async_copy(src_hbm.at[pl.ds(0, n_units * UNIT), :], dst, sem).wait()


def _experts_kernel(be_ref, nblk_ref, unit_ref, xs_hbm, wgu_ref, bgu_ref, wd_ref, bd_ref,
                    ys_ref, xbuf, wgu_bf, wd_bf, sem):
    b = pl.program_id(0)
    nblk = nblk_ref[0]
    slot = b % 2

    @pl.when(b == 0)
    def _():
        _start_units(unit_ref, 0, BLOCK_UNITS, xs_hbm, xbuf.at[0], sem.at[0])

    @pl.when(b + 1 < nblk)
    def _():
        _start_units(unit_ref, (b + 1) * BLOCK_UNITS, BLOCK_UNITS, xs_hbm, xbuf.at[1 - slot],
                     sem.at[1 - slot])

    new_expert = jnp.logical_or(b == 0, be_ref[b] != be_ref[jnp.maximum(b - 1, 0)])

    @pl.when(jnp.logical_and(b < nblk, new_expert))
    def _():
        wgu_bf[...] = wgu_ref[0].astype(BF16)
        wd_bf[...] = wd_ref[0].astype(BF16)

    @pl.when(b < nblk)
    def _():
        _wait_units(BLOCK_UNITS, xs_hbm, xbuf.at[slot], sem.at[slot])
        gu = _dot(xbuf[slot], wgu_bf[...]) + bgu_ref[0]
        g = jnp.minimum(gu[:, :D_EXPERT], SWIGLU_LIMIT)
        u = jnp.clip(gu[:, D_EXPERT:], -SWIGLU_LIMIT, SWIGLU_LIMIT)
        act = (u + 1.0) * (g * jax.nn.sigmoid(SWIGLU_ALPHA * g))
        ys_ref[...] = (_dot(act.astype(BF16), wd_bf[...]) + bd_ref[0]).astype(BF16)

    @pl.when(b >= nblk)
    def _():
        ys_ref[...] = jnp.zeros_like(ys_ref)


def _experts(block_expert, n_blocks_used, unit_src, xs, w_gate_up, b_gate_up, w_down, b_down):
    n_blocks = block_expert.shape[0]
    grid_spec = pltpu.PrefetchScalarGridSpec(
        num_scalar_prefetch=3,
        grid=(n_blocks,),
        in_specs=[pl.BlockSpec(memory_space=pl.ANY),
                  pl.BlockSpec((1, D_MODEL, 2 * D_EXPERT), lambda b, be, nb, us: (be[b], 0, 0)),
                  pl.BlockSpec((1, 1, 2 * D_EXPERT), lambda b, be, nb, us: (be[b], 0, 0)),
                  pl.BlockSpec((1, D_EXPERT, D_MODEL), lambda b, be, nb, us: (be[b], 0, 0)),
                  pl.BlockSpec((1, 1, D_MODEL), lambda b, be, nb, us: (be[b], 0, 0))],
        out_specs=pl.BlockSpec((TM_E, D_MODEL), lambda b, be, nb, us: (b, 0)),
        scratch_shapes=[pltpu.VMEM((2, TM_E, D_MODEL), BF16),
                        pltpu.VMEM((D_MODEL, 2 * D_EXPERT), BF16),
                        pltpu.VMEM((D_EXPERT, D_MODEL), BF16),
                        pltpu.SemaphoreType.DMA((2,))],
    )
    return pl.pallas_call(
        _experts_kernel,
        out_shape=jax.ShapeDtypeStruct((n_blocks * TM_E, D_MODEL), BF16),
        grid_spec=grid_spec,
        compiler_params=pltpu.CompilerParams(dimension_semantics=("arbitrary",),
                                             vmem_limit_bytes=VMEM_LIMIT),
        name="experts",
    )(block_expert, n_blocks_used, unit_src, xs, w_gate_up,
      b_gate_up.reshape(N_EXPERTS, 1, 2 * D_EXPERT), w_down, b_down.reshape(N_EXPERTS, 1, D_MODEL))


def _combine_kernel(unit_ref, x1_ref, meta_ref, ys_hbm, o_ref, buf, sem):
    i = pl.program_id(0)
    n = pl.num_programs(0)
    slot = i % 2
    tm = x1_ref.shape[0]

    @pl.when(i == 0)
    def _():
        _start_units(unit_ref, 0, TILE_UNITS, ys_hbm, buf.at[0], sem.at[0])

    @pl.when(i + 1 < n)
    def _():
        _start_units(unit_ref, (i + 1) * TILE_UNITS, TILE_UNITS, ys_hbm, buf.at[1 - slot],
                     sem.at[1 - slot])

    meta = meta_ref[...]
    slots = [meta[:, 2 * TOP_K + k:2 * TOP_K + k + 1].astype(jnp.int32) for k in range(TOP_K)]
    gates = [meta[:, TOP_K + k:TOP_K + k + 1] for k in range(TOP_K)]
    _wait_units(TILE_UNITS, ys_hbm, buf.at[slot], sem.at[slot])

    acc = x1_ref[...]
    for c in range(TILE_SLOTS // SLOT_CHUNK):
        first = c * SLOT_CHUNK
        col = lax.broadcasted_iota(jnp.int32, (tm, SLOT_CHUNK), 1) + first
        weight = jnp.zeros((tm, SLOT_CHUNK), F32)
        for k in range(TOP_K):
            weight = jnp.where(col == slots[k], gates[k], weight)
        w_hi = weight.astype(BF16)
        w_lo = (weight - w_hi.astype(F32)).astype(BF16)
        y = buf[slot, first:first + SLOT_CHUNK, :]
        acc = acc + _dot(w_hi, y) + _dot(w_lo, y)
    o_ref[...] = acc


def _combine(tile_units, x1, meta, ys):
    t = x1.shape[0]
    grid_spec = pltpu.PrefetchScalarGridSpec(
        num_scalar_prefetch=1,
        grid=(t // TM_OUT,),
        in_specs=[pl.BlockSpec((TM_OUT, D_MODEL), lambda i, us: (i, 0)),
                  pl.BlockSpec((TM_OUT, LANES), lambda i, us: (i, 0)),
                  pl.BlockSpec(memory_space=pl.ANY)],
        out_specs=pl.BlockSpec((TM_OUT, D_MODEL), lambda i, us: (i, 0)),
        scratch_shapes=[pltpu.VMEM((2, TILE_SLOTS, D_MODEL), BF16),
                        pltpu.SemaphoreType.DMA((2,))],
    )
    return pl.pallas_call(
        _combine_kernel,
        out_shape=jax.ShapeDtypeStruct((t, D_MODEL), F32),
        grid_spec=grid_spec,
        compiler_params=pltpu.CompilerParams(dimension_semantics=("arbitrary",),
                                             vmem_limit_bytes=VMEM_LIMIT),
        name="combine",
    )(tile_units, x1, meta, ys)


def _head_slabs(w):
    d = w.shape[0]
    w = w.reshape(d, N_HEADS, HEAD_DIM)
    return jnp.pad(w, ((0, 0), (0, 0), (0, SLAB - HEAD_DIM))).reshape(d, N_HEADS * SLAB)


def _block_diag(w):
    g = w.shape[0]
    eye = jnp.eye(g, dtype=w.dtype)
    return jnp.einsum('gij,gh->gihj', w, eye).reshape(g * REC_BLOCK, g * REC_BLOCK)


def _pad_lanes(v, n=LANES):
    return jnp.pad(v, (0, n - v.shape[0])).reshape(1, n)


def _excl_cumsum(a, axis):
    return jnp.cumsum(a, axis=axis) - a


def _pick(onehot, table):
    if table.ndim == 1:
        return jnp.sum(onehot * table[None, :], axis=1)
    return jnp.sum(onehot[:, :, None] * table[None, :, :], axis=1)


def _dispatch_tables(units, n_blocks):
    n_tiles = units.shape[0]
    i32 = jnp.int32
    tile_first = _excl_cumsum(units, 1)
    tile_end = tile_first + units
    run_first = _excl_cumsum(units, 0)
    run_end = run_first + units
    region_units = jnp.sum(units, axis=0)
    region_blocks = (region_units + BLOCK_UNITS - 1) // BLOCK_UNITS
    block_end = jnp.cumsum(region_blocks)
    region_first = (block_end - region_blocks) * BLOCK_UNITS
    n_used = block_end[-1:]
    experts = jnp.arange(N_EXPERTS, dtype=i32)
    tiles = jnp.arange(n_tiles, dtype=i32)

    blocks = jnp.arange(n_blocks, dtype=i32)
    block_expert = jnp.minimum(
        jnp.sum((block_end[None, :] <= blocks[:, None]).astype(i32), axis=1), N_EXPERTS - 1)

    g = jnp.arange(n_blocks * BLOCK_UNITS, dtype=i32)
    e = jnp.minimum(jnp.sum((block_end[None, :] * BLOCK_UNITS <= g[:, None]).astype(i32), axis=1),
                    N_EXPERTS - 1)
    e_hot = (e[:, None] == experts[None, :]).astype(i32)
    local = g - _pick(e_hot, region_first)
    valid = (local < _pick(e_hot, region_units)) & (g < n_used[0] * BLOCK_UNITS)
    ends = _pick(e_hot, run_end.T)
    tile = jnp.minimum(jnp.sum((ends <= local[:, None]).astype(i32), axis=1), n_tiles - 1)
    t_hot = (tile[:, None] == tiles[None, :]).astype(i32)
    src = (tile * TILE_UNITS + jnp.sum(_pick(t_hot, tile_first - run_first) * e_hot, axis=1) + local)
    unit_src = jnp.where(valid, src, 0).astype(i32)

    j = jnp.arange(TILE_UNITS, dtype=i32)
    ej = jnp.minimum(jnp.sum((tile_end[:, None, :] <= j[None, :, None]).astype(i32), axis=2),
                     N_EXPERTS - 1)
    ej_hot = (ej[:, :, None] == experts[None, None, :]).astype(i32)
    used = j[None, :] < jnp.sum(units, axis=1)[:, None]
    base = region_first[None, :] + run_first - tile_first
    dst = jnp.sum(ej_hot * base[:, None, :], axis=2) + j[None, :]
    tile_units = jnp.where(used, dst, 0).astype(i32).reshape(-1)
    return block_expert.astype(i32), n_used.astype(i32), unit_src, tile_units


def _layer(x, ln1_w, w_in, conv_w, conv_b, w_rg_a, b_rg_a, w_rg_i, b_rg_i, rg_lambda, b_forget,
           q_norm_w, k_norm_w, rec_out_norm_w, attn_out_norm_w, w_out, ln2_w, w_router, b_router,
           w_gate_up, b_gate_up, w_down, b_down):
    batch, seq, _ = x.shape
    t = batch * seq
    x2d = x.reshape(t, D_MODEL)

    o_q = 2 * D_REC
    o_k = o_q + D_ATT
    o_v = o_k + D_ATT
    o_f = o_v + D_ATT
    w_all = jnp.concatenate(
        [w_in[:, :o_q], _head_slabs(w_in[:, o_q:o_k]), _head_slabs(w_in[:, o_k:o_v]),
         _head_slabs(w_in[:, o_v:o_f]), jnp.pad(w_in[:, o_f:], ((0, 0), (0, LANES - N_HEADS)))],
        axis=1).astype(BF16)
    scale = HEAD_DIM ** -0.5
    u, g, qp, kp, vp = _in_proj(x2d, ln1_w.reshape(1, D_MODEL), w_all, _pad_lanes(b_forget),
                                _pad_lanes(q_norm_w * scale), _pad_lanes(k_norm_w), batch, seq)

    rec_n = _rglru(u.reshape(batch, seq, D_REC), g.reshape(batch, seq, D_REC), conv_w,
                   conv_b.reshape(1, D_REC), _block_diag(w_rg_a).astype(BF16), b_rg_a.reshape(1, D_REC),
                   _block_diag(w_rg_i).astype(BF16), b_rg_i.reshape(1, D_REC),
                   rg_lambda.reshape(1, D_REC), rec_out_norm_w.reshape(1, D_REC))

    att = _attention(qp, kp, vp)

    aw_slab = _head_slabs(attn_out_norm_w.reshape(1, D_ATT))
    w_att = jnp.pad(w_out[D_REC:].reshape(N_HEADS, HEAD_DIM, D_MODEL),
                    ((0, 0), (0, SLAB - HEAD_DIM), (0, 0))).reshape(N_HEADS * SLAB, D_MODEL)
    rw = jnp.pad(w_router, ((0, 0), (0, LANES - N_EXPERTS)))
    rw_hi = rw.astype(BF16)
    rw_lo = (rw - rw_hi.astype(F32)).astype(BF16)
    x1, xs, meta, units = _out_proj(
        x2d, rec_n.reshape(t, D_REC), att.reshape(t, N_HEADS * SLAB), aw_slab,
        w_out[:D_REC].astype(BF16), w_att.astype(BF16), ln2_w.reshape(1, D_MODEL),
        rw_hi, rw_lo, _pad_lanes(b_router))

    n_tiles = t // TM_OUT
    max_units = t * TOP_K // UNIT + n_tiles * N_EXPERTS
    n_blocks = -(-max_units // BLOCK_UNITS) + N_EXPERTS
    block_expert, n_used, unit_src, tile_units = _dispatch_tables(
        units[:, 0, :N_EXPERTS].astype(jnp.int32), n_blocks)

    ys = _experts(block_expert, n_used, unit_src, xs, w_gate_up, b_gate_up, w_down, b_down)
    out = _combine(tile_units, x1, meta, ys)
    return out.reshape(batch, seq, D_MODEL)


def kernel(x, ln1_w, w_in, conv_w, conv_b, w_rg_a, b_rg_a, w_rg_i, b_rg_i, rg_lambda, b_forget,
           q_norm_w, k_norm_w, rec_out_norm_w, attn_out_norm_w, w_out, ln2_w, w_router, b_router,
           w_gate_up, b_gate_up, w_down, b_down):
    for l in range(ln1_w.shape[0]):
        x = _layer(x, ln1_w[l], w_in[l], conv_w[l], conv_b[l], w_rg_a[l], b_rg_a[l], w_rg_i[l],
                   b_rg_i[l], rg_lambda[l], b_forget[l], q_norm_w[l], k_norm_w[l], rec_out_norm_w[l],
                   attn_out_norm_w[l], w_out[l], ln2_w[l], w_router[l], b_router[l], w_gate_up[l],
                   b_gate_up[l], w_down[l], b_down[l])
    return x
```

```python
import functools

import jax
import jax.numpy as jnp
from jax import lax
from jax.experimental import pallas as pl
from jax.experimental.pallas import tpu as pltpu

D_MODEL = 1024
D_REC = 512
N_REC_BLOCKS = 8
REC_BLOCK = 64
CONV_WIDTH = 4
RGLRU_C = 8.0
D_ATT = 512
HEAD_DIM = 64
N_HEADS = 8
N_EXPERTS = 32
TOP_K = 4
D_EXPERT = 1024
SWIGLU_LIMIT = 7.0
SWIGLU_ALPHA = 1.702
EPS = 1e-6

LANES = 128
SLAB = LANES
VMEM_LIMIT = 56 * 1024 * 1024

F32 = jnp.float32
BF16 = jnp.bfloat16

O_U = 0
O_G = O_U + D_REC
O_Q = O_G + D_REC
O_K = O_Q + N_HEADS * SLAB
O_V = O_K + N_HEADS * SLAB
O_F = O_V + N_HEADS * SLAB
N_PROJ = O_F + LANES

L_C = HEAD_DIM
L_ONE = HEAD_DIM + 3

TM_IN = 512
TS_REC = 512
TQ = 512
TK = 512
HEADS_PER_STEP = 4
TM_OUT = 512
TM_E = 256

UNIT = 16
TILE_SLOTS = -(-(TM_OUT * TOP_K + N_EXPERTS * (UNIT - 1)) // LANES) * LANES
TILE_UNITS = TILE_SLOTS // UNIT
BLOCK_UNITS = TM_E // UNIT
SLOT_CHUNK = 512


def _split3(v):
    hi = v.astype(BF16).astype(F32)
    r = v - hi
    mid = r.astype(BF16).astype(F32)
    lo = (r - mid).astype(BF16).astype(F32)
    return hi, mid, lo


def _dot(a, b):
    return jnp.dot(a, b, preferred_element_type=F32)


def _dot_t(a, b):
    return lax.dot_general(a, b, (((0,), (0,)), ((), ())), preferred_element_type=F32)


def _in_proj_kernel(x_ref, ln_ref, w_ref, bf_ref, qw_ref, kw_ref,
                    u_ref, g_ref, q_ref, k_ref, v_ref, carry_ref, *, tiles_per_seq):
    i = pl.program_id(0)
    tm = x_ref.shape[0]

    @pl.when(i % tiles_per_seq == 0)
    def _():
        carry_ref[...] = jnp.zeros_like(carry_ref)

    x = x_ref[...]
    ms = jnp.mean(x * x, axis=-1, keepdims=True)
    h = (x * lax.rsqrt(ms + EPS) * ln_ref[...]).astype(BF16)

    u_ref[...] = _dot(h, w_ref[:, O_U:O_U + D_REC])
    g_ref[...] = _dot(h, w_ref[:, O_G:O_G + D_REC])

    f = _dot(h, w_ref[:, O_F:O_F + LANES]) + bf_ref[...]
    logf = jnp.minimum(f, 0.0) - jnp.log1p(jnp.exp(-jnp.abs(f)))
    row = lax.broadcasted_iota(jnp.int32, (tm, tm), 0)
    col = lax.broadcasted_iota(jnp.int32, (tm, tm), 1)
    tri = (row >= col).astype(BF16)
    hi, mid, lo = _split3(logf)
    c = (_dot(tri, hi.astype(BF16)) + _dot(tri, mid.astype(BF16))
         + _dot(tri, lo.astype(BF16)) + carry_ref[...])
    carry_ref[...] = c[tm - 1:tm, :]

    lane = lax.broadcasted_iota(jnp.int32, (tm, SLAB), 1)
    q_all = _dot(h, w_ref[:, O_Q:O_Q + N_HEADS * SLAB])
    k_all = _dot(h, w_ref[:, O_K:O_K + N_HEADS * SLAB])
    v_all = _dot(h, w_ref[:, O_V:O_V + N_HEADS * SLAB])
    qw = qw_ref[...]
    kw = kw_ref[...]
    g_row = lax.broadcasted_iota(jnp.int32, (N_HEADS * SLAB, LANES), 0)
    g_col = lax.broadcasted_iota(jnp.int32, (N_HEADS * SLAB, LANES), 1)
    group = ((g_row >= g_col * SLAB) & (g_row < (g_col + 1) * SLAB)).astype(BF16)
    q_inv = lax.rsqrt(_dot((q_all * q_all).astype(BF16), group) * (1.0 / HEAD_DIM) + EPS)
    k_inv = lax.rsqrt(_dot((k_all * k_all).astype(BF16), group) * (1.0 / HEAD_DIM) + EPS)
    for hd in range(N_HEADS):
        sl = slice(hd * SLAB, (hd + 1) * SLAB)
        qn = q_all[:, sl] * q_inv[:, hd:hd + 1] * qw
        kn = k_all[:, sl] * k_inv[:, hd:hd + 1] * kw
        c_hi, c_mid, c_lo = _split3(c[:, hd:hd + 1])
        qp = jnp.where(lane == L_C, c_hi, qn)
        qp = jnp.where(lane == L_C + 1, c_mid, qp)
        qp = jnp.where(lane == L_C + 2, c_lo, qp)
        qp = jnp.where((lane >= L_ONE) & (lane < L_ONE + 3), 1.0, qp)
        kp = jnp.where((lane >= L_C) & (lane < L_C + 3), 1.0, kn)
        kp = jnp.where(lane == L_ONE, -c_hi, kp)
        kp = jnp.where(lane == L_ONE + 1, -c_mid, kp)
        kp = jnp.where(lane == L_ONE + 2, -c_lo, kp)
        q_ref[0, hd] = qp.astype(BF16)
        k_ref[0, hd] = kp.astype(BF16)
        v_ref[0, hd] = jnp.where(lane == HEAD_DIM, 1.0, v_all[:, sl]).astype(BF16)


def _in_proj(x2d, ln1_w, w_all, bf_pad, qw_slab, kw_slab, batch, seq):
    t = x2d.shape[0]
    tiles_per_seq = seq // TM_IN
    slab_shape = jax.ShapeDtypeStruct((batch, N_HEADS, seq, SLAB), BF16)
    slab_spec = pl.BlockSpec((1, N_HEADS, TM_IN, SLAB),
                             lambda i: (i // tiles_per_seq, 0, i % tiles_per_seq, 0))
    row_spec = pl.BlockSpec((TM_IN, D_REC), lambda i: (i, 0))
    const = lambda shape: pl.BlockSpec(shape, lambda i: (0,) * len(shape))
    return pl.pallas_call(
        functools.partial(_in_proj_kernel, tiles_per_seq=tiles_per_seq),
        out_shape=(jax.ShapeDtypeStruct((t, D_REC), F32), jax.ShapeDtypeStruct((t, D_REC), F32),
                   slab_shape, slab_shape, slab_shape),
        grid=(t // TM_IN,),
        in_specs=[pl.BlockSpec((TM_IN, D_MODEL), lambda i: (i, 0)),
                  const((1, D_MODEL)), const((D_MODEL, N_PROJ)), const((1, LANES)),
                  const((1, SLAB)), const((1, SLAB))],
        out_specs=(row_spec, row_spec, slab_spec, slab_spec, slab_spec),
        scratch_shapes=[pltpu.VMEM((1, LANES), F32)],
        compiler_params=pltpu.CompilerParams(dimension_semantics=("arbitrary",),
                                             vmem_limit_bytes=VMEM_LIMIT),
        name="in_proj",
    )(x2d, ln1_w, w_all, bf_pad, qw_slab, kw_slab)


def _rglru_kernel(u_ref, g_ref, cw_ref, cb_ref, wa_ref, ba_ref, wi_ref, bi_ref, lam_ref, nw_ref,
                  o_ref, ext_ref, a_ref, b_ref, h_ref, hcar_ref):
    s = pl.program_id(1)
    ts = u_ref.shape[1]
    pad = 8

    @pl.when(s == 0)
    def _():
        ext_ref[0:pad, :] = jnp.zeros((pad, D_REC), F32)
        hcar_ref[...] = jnp.zeros_like(hcar_ref)

    @pl.when(s != 0)
    def _():
        ext_ref[0:pad, :] = ext_ref[ts:ts + pad, :]

    u = u_ref[0]
    ext_ref[pad:pad + ts, :] = u
    xc = cb_ref[...] + u * cw_ref[CONV_WIDTH - 1:CONV_WIDTH, :]
    for tap in range(CONV_WIDTH - 1):
        d = CONV_WIDTH - 1 - tap
        xc = xc + ext_ref[pad - d:pad - d + ts, :] * cw_ref[tap:tap + 1, :]

    xb = xc.astype(BF16)
    r = jax.nn.sigmoid(_dot(xb, wa_ref[...]) + ba_ref[...])
    ig = jax.nn.sigmoid(_dot(xb, wi_ref[...]) + bi_ref[...])
    nlam = -lam_ref[...]
    softplus = jnp.maximum(nlam, 0.0) + jnp.log1p(jnp.exp(-jnp.abs(nlam)))
    log_a = -RGLRU_C * r * softplus
    a = jnp.exp(log_a)
    a_ref[...] = a
    b_ref[...] = jnp.sqrt(1.0 - jnp.exp(2.0 * log_a)) * (ig * xc)

    def step(t, h):
        h = a_ref[pl.ds(t, 1), :] * h + b_ref[pl.ds(t, 1), :]
        h_ref[pl.ds(t, 1), :] = h
        return h

    hcar_ref[...] = lax.fori_loop(0, ts, step, hcar_ref[...], unroll=8)

    rec = h_ref[...] * jax.nn.gelu(g_ref[0])
    ms = jnp.mean(rec * rec, axis=-1, keepdims=True)
    o_ref[0] = (rec * lax.rsqrt(ms + EPS) * nw_ref[...]).astype(BF16)


def _rglru(u, g, conv_w, conv_b, wa_bd, b_a, wi_bd, b_i, lam, norm_w):
    batch, seq, _ = u.shape
    seq_spec = pl.BlockSpec((1, TS_REC, D_REC), lambda b, s: (b, s, 0))
    const = lambda shape: pl.BlockSpec(shape, lambda b, s: (0,) * len(shape))
    return pl.pallas_call(
        _rglru_kernel,
        out_shape=jax.ShapeDtypeStruct((batch, seq, D_REC), BF16),
        grid=(batch, seq // TS_REC),
        in_specs=[seq_spec, seq_spec, const((CONV_WIDTH, D_REC)), const((1, D_REC)),
                  const((D_REC, D_REC)), const((1, D_REC)), const((D_REC, D_REC)), const((1, D_REC)),
                  const((1, D_REC)), const((1, D_REC))],
        out_specs=seq_spec,
        scratch_shapes=[pltpu.VMEM((TS_REC + 8, D_REC), F32), pltpu.VMEM((TS_REC, D_REC), F32),
                        pltpu.VMEM((TS_REC, D_REC), F32), pltpu.VMEM((TS_REC, D_REC), F32),
                        pltpu.VMEM((1, D_REC), F32)],
        compiler_params=pltpu.CompilerParams(dimension_semantics=("arbitrary", "arbitrary"),
                                             vmem_limit_bytes=VMEM_LIMIT),
        name="rglru",
    )(u, g, conv_w, conv_b, wa_bd, b_a, wi_bd, b_i, lam, norm_w)


def _attn_kernel(q_ref, k_ref, v_ref, o_ref):
    qi = pl.program_id(2)
    hb = q_ref.shape[1]
    tq = q_ref.shape[2]
    qs = [q_ref[0, h] for h in range(hb)]

    def step(j, carry, masked):
        start = pl.multiple_of(j * TK, TK)
        out = []
        for h in range(hb):
            m, acc = carry[h]
            kj = k_ref[0, h, pl.ds(start, TK), :]
            vj = v_ref[0, h, pl.ds(start, TK), :]
            s = lax.dot_general(qs[h], kj, (((1,), (1,)), ((), ())), preferred_element_type=F32)
            if masked:
                row = lax.broadcasted_iota(jnp.int32, (tq, TK), 0)
                col = lax.broadcasted_iota(jnp.int32, (tq, TK), 1)
                s = jnp.where(row >= col, s, -jnp.inf)
            m_new = jnp.maximum(m, jnp.max(s, axis=-1, keepdims=True))
            p = jnp.exp(s - m_new)
            acc = jnp.exp(m - m_new) * acc + _dot(p.astype(BF16), vj)
            out.append((m_new, acc))
        return tuple(out)

    init = tuple((jnp.full((tq, 1), -jnp.inf, F32), jnp.zeros((tq, SLAB), F32)) for _ in range(hb))
    carry = lax.fori_loop(0, qi, lambda j, c: step(j, c, False), init)
    carry = step(qi, carry, True)
    lane = lax.broadcasted_iota(jnp.int32, (tq, SLAB), 1)
    for h in range(hb):
        acc = carry[h][1]
        o_ref[0, :, h * SLAB:(h + 1) * SLAB] = jnp.where(
            lane < HEAD_DIM, acc / acc[:, HEAD_DIM:HEAD_DIM + 1], 0.0)


def _attention(qp, kp, vp):
    batch, _, seq, _ = qp.shape
    hb = HEADS_PER_STEP
    kv_spec = pl.BlockSpec((1, hb, seq, SLAB), lambda b, h, i: (b, h, 0, 0))
    return pl.pallas_call(
        _attn_kernel,
        out_shape=jax.ShapeDtypeStruct((batch, seq, N_HEADS * SLAB), F32),
        grid=(batch, N_HEADS // hb, seq // TQ),
        in_specs=[pl.BlockSpec((1, hb, TQ, SLAB), lambda b, h, i: (b, h, i, 0)), kv_spec, kv_spec],
        out_specs=pl.BlockSpec((1, TQ, hb * SLAB), lambda b, h, i: (b, i, h)),
        compiler_params=pltpu.CompilerParams(
            dimension_semantics=("arbitrary", "arbitrary", "arbitrary"),
            vmem_limit_bytes=VMEM_LIMIT),
        name="attn",
    )(qp, kp, vp)


def _slot_onehot(slots, first, width):
    tm = slots[0].shape[0]
    col = lax.broadcasted_iota(jnp.int32, (tm, width), 1) + first
    hit = jnp.zeros((tm, width), F32)
    for slot in slots:
        hit = jnp.where(col == slot, 1.0, hit)
    return hit.astype(BF16)


def _out_proj_kernel(x_ref, rec_ref, att_ref, aw_ref, wr_ref, wa_ref, ln_ref, rwh_ref, rwl_ref, rb_ref,
                     x1_ref, xs_ref, meta_ref, units_ref):
    tm = x_ref.shape[0]

    att = att_ref[...]
    ms = jnp.sum(att * att, axis=-1, keepdims=True) * (1.0 / D_ATT)
    att_n = (att * lax.rsqrt(ms + EPS) * aw_ref[...]).astype(BF16)
    x1 = x_ref[...] + _dot(rec_ref[...], wr_ref[...]) + _dot(att_n, wa_ref[...])
    x1_ref[...] = x1

    ms2 = jnp.mean(x1 * x1, axis=-1, keepdims=True)
    h2 = x1 * lax.rsqrt(ms2 + EPS) * ln_ref[...]

    h_hi = h2.astype(BF16)
    h_lo = (h2 - h_hi.astype(F32)).astype(BF16)
    logits = (_dot(h_hi, rwh_ref[...]) + _dot(h_lo, rwh_ref[...]) + _dot(h_hi, rwl_ref[...])
              + rb_ref[...])
    lane = lax.broadcasted_iota(jnp.int32, (tm, LANES), 1)
    lane_f = lane.astype(F32)
    l = jnp.where(lane < N_EXPERTS, logits, -jnp.inf)

    vals, idxs, sels = [], [], []
    for _ in range(TOP_K):
        m = jnp.max(l, axis=-1, keepdims=True)
        idx = jnp.min(jnp.where(l == m, lane_f, float(LANES)), axis=-1, keepdims=True)
        sel = lane_f == idx
        vals.append(m)
        idxs.append(idx)
        sels.append(sel)
        l = jnp.where(sel, -jnp.inf, l)
    exps = [jnp.exp(v - vals[0]) for v in vals]
    denom = exps[0] + exps[1] + exps[2] + exps[3]

    any_sel = sels[0] | sels[1] | sels[2] | sels[3]
    onehot = jnp.where(any_sel, 1.0, 0.0)
    row = lax.broadcasted_iota(jnp.int32, (tm, tm), 0)
    col = lax.broadcasted_iota(jnp.int32, (tm, tm), 1)
    before = (row > col).astype(BF16)
    rank = _dot(before, onehot.astype(BF16))
    count = jnp.sum(onehot, axis=0, keepdims=True)
    units = jnp.floor((count + (UNIT - 1)) * (1.0 / UNIT))
    e_row = lax.broadcasted_iota(jnp.int32, (LANES, LANES), 0)
    e_col = lax.broadcasted_iota(jnp.int32, (LANES, LANES), 1)
    earlier = (e_row < e_col).astype(BF16)
    units8 = jnp.broadcast_to(units, (8, LANES))
    first_unit = _dot(units8.astype(BF16), earlier)[0:1, :]
    units_ref[0] = units8
    place = first_unit * float(UNIT) + rank

    meta = jnp.zeros((tm, LANES), F32)
    slots = []
    for k in range(TOP_K):
        slot = jnp.sum(jnp.where(sels[k], place, 0.0), axis=-1, keepdims=True)
        slots.append(slot.astype(jnp.int32))
        meta = jnp.where(lane == k, idxs[k], meta)
        meta = jnp.where(lane == TOP_K + k, exps[k] / denom, meta)
        meta = jnp.where(lane == 2 * TOP_K + k, slot, meta)
    meta_ref[...] = meta

    for c in range(TILE_SLOTS // SLOT_CHUNK):
        sel_t = _slot_onehot(slots, c * SLOT_CHUNK, SLOT_CHUNK)
        xs_ref[c * SLOT_CHUNK:(c + 1) * SLOT_CHUNK, :] = _dot_t(sel_t, h_hi).astype(BF16)


def _out_proj(x2d, rec_n, att, aw_slab, w_rec, w_att, ln2_w, rw_hi, rw_lo, rb_pad):
    t = x2d.shape[0]
    n_tiles = t // TM_OUT
    rows = lambda n: pl.BlockSpec((TM_OUT, n), lambda i: (i, 0))
    const = lambda shape: pl.BlockSpec(shape, lambda i: (0,) * len(shape))
    return pl.pallas_call(
        _out_proj_kernel,
        out_shape=(jax.ShapeDtypeStruct((t, D_MODEL), F32),
                   jax.ShapeDtypeStruct((n_tiles * TILE_SLOTS, D_MODEL), BF16),
                   jax.ShapeDtypeStruct((t, LANES), F32),
                   jax.ShapeDtypeStruct((n_tiles, 8, LANES), F32)),
        grid=(n_tiles,),
        in_specs=[rows(D_MODEL), rows(D_REC), rows(N_HEADS * SLAB), const((1, N_HEADS * SLAB)),
                  const((D_REC, D_MODEL)), const((N_HEADS * SLAB, D_MODEL)), const((1, D_MODEL)),
                  const((D_MODEL, LANES)), const((D_MODEL, LANES)), const((1, LANES))],
        out_specs=(rows(D_MODEL), pl.BlockSpec((TILE_SLOTS, D_MODEL), lambda i: (i, 0)), rows(LANES),
                   pl.BlockSpec((1, 8, LANES), lambda i: (i, 0, 0))),
        compiler_params=pltpu.CompilerParams(dimension_semantics=("arbitrary",),
                                             vmem_limit_bytes=VMEM_LIMIT),
        name="out_proj",
    )(x2d, rec_n, att, aw_slab, w_rec, w_att, ln2_w, rw_hi, rw_lo, rb_pad)


def _start_units(unit_ref, base, n_units, src_hbm, dst, sem):
    def body(j, carry):
        src_row = pl.multiple_of(unit_ref[base + j] * UNIT, UNIT)
        dst_row = pl.multiple_of(j * UNIT, UNIT)
        pltpu.make_async_copy(src_hbm.at[pl.ds(src_row, UNIT), :],
                              dst.at[pl.ds(dst_row, UNIT), :], sem).start()
        return carry
    lax.fori_loop(0, n_units, body, 0, unroll=8)


def _wait_units(n_units, src_hbm, dst, sem):
    pltpu.make_async_copy(src_hbm.at[pl.ds(0, n_units * UNIT), :], dst, sem).wait()


def _experts_kernel(be_ref, nblk_ref, unit_ref, order_ref, next_ref, xs_hbm, wgu_hbm, bgu_ref, wd_hbm,
                    bd_ref, ys_ref, xbuf, wgu_f32, wd_f32, wgu_bf, wd_bf, sem, wsem):
    b = pl.program_id(0)
    nblk = nblk_ref[0]
    slot = b % 2
    e = be_ref[b]

    def weight_copies(expert, wslot):
        return (pltpu.make_async_copy(wgu_hbm.at[expert], wgu_f32.at[wslot], wsem.at[wslot]),
                pltpu.make_async_copy(wd_hbm.at[expert], wd_f32.at[wslot], wsem.at[wslot]))

    @pl.when(b == 0)
    def _():
        _start_units(unit_ref, 0, BLOCK_UNITS, xs_hbm, xbuf.at[0], sem.at[0])
        for copy in weight_copies(e, 0):
            copy.start()

    @pl.when(b + 1 < nblk)
    def _():
        _start_units(unit_ref, (b + 1) * BLOCK_UNITS, BLOCK_UNITS, xs_hbm, xbuf.at[1 - slot],
                     sem.at[1 - slot])

    new_expert = jnp.logical_or(b == 0, e != be_ref[jnp.maximum(b - 1, 0)])

    @pl.when(jnp.logical_and(b < nblk, new_expert))
    def _():
        wslot = order_ref[e] % 2
        for copy in weight_copies(e, wslot):
            copy.wait()
        wgu_bf[...] = wgu_f32[wslot].astype(BF16)
        wd_bf[...] = wd_f32[wslot].astype(BF16)
        nxt = next_ref[e]

        @pl.when(nxt >= 0)
        def _():
            for copy in weight_copies(nxt, 1 - wslot):
                copy.start()

    @pl.when(b < nblk)
    def _():
        _wait_units(BLOCK_UNITS, xs_hbm, xbuf.at[slot], sem.at[slot])
        gu = _dot(xbuf[slot], wgu_bf[...]) + bgu_ref[0]
        g = jnp.minimum(gu[:, :D_EXPERT], SWIGLU_LIMIT)
        u = jnp.clip(gu[:, D_EXPERT:], -SWIGLU_LIMIT, SWIGLU_LIMIT)
        act = (u + 1.0) * (g * jax.nn.sigmoid(SWIGLU_ALPHA * g))
        ys_ref[...] = (_dot(act.astype(BF16), wd_bf[...]) + bd_ref[0]).astype(BF16)

    @pl.when(b >= nblk)
    def _():
        ys_ref[...] = jnp.zeros_like(ys_ref)


def _experts(block_expert, n_blocks_used, unit_src, expert_order, expert_next, xs, w_gate_up,
             b_gate_up, w_down, b_down):
    n_blocks = block_expert.shape[0]
    bias_map = lambda b, be, nb, us, eo, en: (be[b], 0, 0)
    grid_spec = pltpu.PrefetchScalarGridSpec(
        num_scalar_prefetch=5,
        grid=(n_blocks,),
        in_specs=[pl.BlockSpec(memory_space=pl.ANY),
                  pl.BlockSpec(memory_space=pl.ANY),
                  pl.BlockSpec((1, 1, 2 * D_EXPERT), bias_map),
                  pl.BlockSpec(memory_space=pl.ANY),
                  pl.BlockSpec((1, 1, D_MODEL), bias_map)],
        out_specs=pl.BlockSpec((TM_E, D_MODEL), lambda b, be, nb, us, eo, en: (b, 0)),
        scratch_shapes=[pltpu.VMEM((2, TM_E, D_MODEL), BF16),
                        pltpu.VMEM((2, D_MODEL, 2 * D_EXPERT), F32),
                        pltpu.VMEM((2, D_EXPERT, D_MODEL), F32),
                        pltpu.VMEM((D_MODEL, 2 * D_EXPERT), BF16),
                        pltpu.VMEM((D_EXPERT, D_MODEL), BF16),
                        pltpu.SemaphoreType.DMA((2,)),
                        pltpu.SemaphoreType.DMA((2,))],
    )
    return pl.pallas_call(
        _experts_kernel,
        out_shape=jax.ShapeDtypeStruct((n_blocks * TM_E, D_MODEL), BF16),
        grid_spec=grid_spec,
        compiler_params=pltpu.CompilerParams(dimension_semantics=("arbitrary",),
                                             vmem_limit_bytes=VMEM_LIMIT),
        name="experts",
    )(block_expert, n_blocks_used, unit_src, expert_order, expert_next, xs, w_gate_up,
      b_gate_up.reshape(N_EXPERTS, 1, 2 * D_EXPERT), w_down, b_down.reshape(N_EXPERTS, 1, D_MODEL))


def _combine_kernel(unit_ref, x1_ref, meta_ref, ys_hbm, o_ref, buf, sem):
    i = pl.program_id(0)
    n = pl.num_programs(0)
    slot = i % 2
    tm = x1_ref.shape[0]

    @pl.when(i == 0)
    def _():
        _start_units(unit_ref, 0, TILE_UNITS, ys_hbm, buf.at[0], sem.at[0])

    @pl.when(i + 1 < n)
    def _():
        _start_units(unit_ref, (i + 1) * TILE_UNITS, TILE_UNITS, ys_hbm, buf.at[1 - slot],
                     sem.at[1 - slot])

    meta = meta_ref[...]
    slots = [meta[:, 2 * TOP_K + k:2 * TOP_K + k + 1].astype(jnp.int32) for k in range(TOP_K)]
    gates = [meta[:, TOP_K + k:TOP_K + k + 1] for k in range(TOP_K)]
    _wait_units(TILE_UNITS, ys_hbm, buf.at[slot], sem.at[slot])

    acc = x1_ref[...]
    for c in range(TILE_SLOTS // SLOT_CHUNK):
        first = c * SLOT_CHUNK
        col = lax.broadcasted_iota(jnp.int32, (tm, SLOT_CHUNK), 1) + first
        weight = jnp.zeros((tm, SLOT_CHUNK), F32)
        for k in range(TOP_K):
            weight = jnp.where(col == slots[k], gates[k], weight)
        w_hi = weight.astype(BF16)
        w_lo = (weight - w_hi.astype(F32)).astype(BF16)
        y = buf[slot, first:first + SLOT_CHUNK, :]
        acc = acc + _dot(w_hi, y) + _dot(w_lo, y)
    o_ref[...] = acc


def _combine(tile_units, x1, meta, ys):
    t = x1.shape[0]
    grid_spec = pltpu.PrefetchScalarGridSpec(
        num_scalar_prefetch=1,
        grid=(t // TM_OUT,),
        in_specs=[pl.BlockSpec((TM_OUT, D_MODEL), lambda i, us: (i, 0)),
                  pl.BlockSpec((TM_OUT, LANES), lambda i, us: (i, 0)),
                  pl.BlockSpec(memory_space=pl.ANY)],
        out_specs=pl.BlockSpec((TM_OUT, D_MODEL), lambda i, us: (i, 0)),
        scratch_shapes=[pltpu.VMEM((2, TILE_SLOTS, D_MODEL), BF16),
                        pltpu.SemaphoreType.DMA((2,))],
    )
    return pl.pallas_call(
        _combine_kernel,
        out_shape=jax.ShapeDtypeStruct((t, D_MODEL), F32),
        grid_spec=grid_spec,
        compiler_params=pltpu.CompilerParams(dimension_semantics=("arbitrary",),
                                             vmem_limit_bytes=VMEM_LIMIT),
        name="combine",
    )(tile_units, x1, meta, ys)


def _head_slabs(w):
    d = w.shape[0]
    w = w.reshape(d, N_HEADS, HEAD_DIM)
    return jnp.pad(w, ((0, 0), (0, 0), (0, SLAB - HEAD_DIM))).reshape(d, N_HEADS * SLAB)


def _block_diag(w):
    g = w.shape[0]
    eye = jnp.eye(g, dtype=w.dtype)
    return jnp.einsum('gij,gh->gihj', w, eye).reshape(g * REC_BLOCK, g * REC_BLOCK)


def _pad_lanes(v, n=LANES):
    return jnp.pad(v, (0, n - v.shape[0])).reshape(1, n)


def _excl_cumsum(a, axis):
    return jnp.cumsum(a, axis=axis) - a


def _pick(onehot, table):
    if table.ndim == 1:
        return jnp.sum(onehot * table[None, :], axis=1)
    return jnp.sum(onehot[:, :, None] * table[None, :, :], axis=1)


def _dispatch_tables(units, n_blocks):
    n_tiles = units.shape[0]
    i32 = jnp.int32
    tile_first = _excl_cumsum(units, 1)
    tile_end = tile_first + units
    run_first = _excl_cumsum(units, 0)
    run_end = run_first + units
    region_units = jnp.sum(units, axis=0)
    region_blocks = (region_units + BLOCK_UNITS - 1) // BLOCK_UNITS
    block_end = jnp.cumsum(region_blocks)
    region_first = (block_end - region_blocks) * BLOCK_UNITS
    n_used = block_end[-1:]
    experts = jnp.arange(N_EXPERTS, dtype=i32)
    tiles = jnp.arange(n_tiles, dtype=i32)

    blocks = jnp.arange(n_blocks, dtype=i32)
    block_expert = jnp.minimum(
        jnp.sum((block_end[None, :] <= blocks[:, None]).astype(i32), axis=1), N_EXPERTS - 1)

    g = jnp.arange(n_blocks * BLOCK_UNITS, dtype=i32)
    e = jnp.minimum(jnp.sum((block_end[None, :] * BLOCK_UNITS <= g[:, None]).astype(i32), axis=1),
                    N_EXPERTS - 1)
    e_hot = (e[:, None] == experts[None, :]).astype(i32)
    local = g - _pick(e_hot, region_first)
    valid = (local < _pick(e_hot, region_units)) & (g < n_used[0] * BLOCK_UNITS)
    ends = _pick(e_hot, run_end.T)
    tile = jnp.minimum(jnp.sum((ends <= local[:, None]).astype(i32), axis=1), n_tiles - 1)
    t_hot = (tile[:, None] == tiles[None, :]).astype(i32)
    src = (tile * TILE_UNITS + jnp.sum(_pick(t_hot, tile_first - run_first) * e_hot, axis=1) + local)
    unit_src = jnp.where(valid, src, 0).astype(i32)

    j = jnp.arange(TILE_UNITS, dtype=i32)
    ej = jnp.minimum(jnp.sum((tile_end[:, None, :] <= j[None, :, None]).astype(i32), axis=2),
                     N_EXPERTS - 1)
    ej_hot = (ej[:, :, None] == experts[None, None, :]).astype(i32)
    used = j[None, :] < jnp.sum(units, axis=1)[:, None]
    base = region_first[None, :] + run_first - tile_first
    dst = jnp.sum(ej_hot * base[:, None, :], axis=2) + j[None, :]
    tile_units = jnp.where(used, dst, 0).astype(i32).reshape(-1)
    owns = region_blocks > 0
    expert_order = _excl_cumsum(owns.astype(i32), 0)
    later = (experts[None, :] > experts[:, None]) & owns[None, :]
    expert_next = jnp.min(jnp.where(later, experts[None, :], N_EXPERTS), axis=1)
    expert_next = jnp.where(expert_next < N_EXPERTS, expert_next, -1).astype(i32)
    return (block_expert.astype(i32), n_used.astype(i32), unit_src, tile_units,
            expert_order.astype(i32), expert_next)


def _layer(x, ln1_w, w_in, conv_w, conv_b, w_rg_a, b_rg_a, w_rg_i, b_rg_i, rg_lambda, b_forget,
           q_norm_w, k_norm_w, rec_out_norm_w, attn_out_norm_w, w_out, ln2_w, w_router, b_router,
           w_gate_up, b_gate_up, w_down, b_down):
    batch, seq, _ = x.shape
    t = batch * seq
    x2d = x.reshape(t, D_MODEL)

    o_q = 2 * D_REC
    o_k = o_q + D_ATT
    o_v = o_k + D_ATT
    o_f = o_v + D_ATT
    w_all = jnp.concatenate(
        [w_in[:, :o_q], _head_slabs(w_in[:, o_q:o_k]), _head_slabs(w_in[:, o_k:o_v]),
         _head_slabs(w_in[:, o_v:o_f]), jnp.pad(w_in[:, o_f:], ((0, 0), (0, LANES - N_HEADS)))],
        axis=1).astype(BF16)
    scale = HEAD_DIM ** -0.5
    u, g, qp, kp, vp = _in_proj(x2d, ln1_w.reshape(1, D_MODEL), w_all, _pad_lanes(b_forget),
                                _pad_lanes(q_norm_w * scale), _pad_lanes(k_norm_w), batch, seq)

    rec_n = _rglru(u.reshape(batch, seq, D_REC), g.reshape(batch, seq, D_REC), conv_w,
                   conv_b.reshape(1, D_REC), _block_diag(w_rg_a).astype(BF16), b_rg_a.reshape(1, D_REC),
                   _block_diag(w_rg_i).astype(BF16), b_rg_i.reshape(1, D_REC),
                   rg_lambda.reshape(1, D_REC), rec_out_norm_w.reshape(1, D_REC))

    att = _attention(qp, kp, vp)

    aw_slab = _head_slabs(attn_out_norm_w.reshape(1, D_ATT))
    w_att = jnp.pad(w_out[D_REC:].reshape(N_HEADS, HEAD_DIM, D_MODEL),
                    ((0, 0), (0, SLAB - HEAD_DIM), (0, 0))).reshape(N_HEADS * SLAB, D_MODEL)
    rw = jnp.pad(w_router, ((0, 0), (0, LANES - N_EXPERTS)))
    rw_hi = rw.astype(BF16)
    rw_lo = (rw - rw_hi.astype(F32)).astype(BF16)
    x1, xs, meta, units = _out_proj(
        x2d, rec_n.reshape(t, D_REC), att.reshape(t, N_HEADS * SLAB), aw_slab,
        w_out[:D_REC].astype(BF16), w_att.astype(BF16), ln2_w.reshape(1, D_MODEL),
        rw_hi, rw_lo, _pad_lanes(b_router))

    n_tiles = t // TM_OUT
    max_units = t * TOP_K // UNIT + n_tiles * N_EXPERTS
    n_blocks = -(-max_units // BLOCK_UNITS) + N_EXPERTS
    block_expert, n_used, unit_src, tile_units, expert_order, expert_next = _dispatch_tables(
        units[:, 0, :N_EXPERTS].astype(jnp.int32), n_blocks)

    ys = _experts(block_expert, n_used, unit_src, expert_order, expert_next, xs, w_gate_up, b_gate_up,
                  w_down, b_down)
    out = _combine(tile_units, x1, meta, ys)
    return out.reshape(batch, seq, D_MODEL)


def kernel(x, ln1_w, w_in, conv_w, conv_b, w_rg_a, b_rg_a, w_rg_i, b_rg_i, rg_lambda, b_forget,
           q_norm_w, k_norm_w, rec_out_norm_w, attn_out_norm_w, w_out, ln2_w, w_router, b_router,
           w_gate_up, b_gate_up, w_down, b_down):
    for l in range(ln1_w.shape[0]):
        x = _layer(x, ln1_w[l], w_in[l], conv_w[l], conv_b[l], w_rg_a[l], b_rg_a[l], w_rg_i[l],
                   b_rg_i[l], rg_lambda[l], b_forget[l], q_norm_w[l], k_norm_w[l], rec_out_norm_w[l],
                   attn_out_norm_w[l], w_out[l], ln2_w[l], w_router[l], b_router[l], w_gate_up[l],
                   b_gate_up[l], w_down[l], b_down[l])
    return x
```

```python
import functools

import jax
import jax.numpy as jnp
from jax import lax
from jax.experimental import pallas as pl
from jax.experimental.pallas import tpu as pltpu

D_MODEL = 1024
D_REC = 512
N_REC_BLOCKS = 8
REC_BLOCK = 64
CONV_WIDTH = 4
RGLRU_C = 8.0
D_ATT = 512
HEAD_DIM = 64
N_HEADS = 8
N_EXPERTS = 32
TOP_K = 4
D_EXPERT = 1024
SWIGLU_LIMIT = 7.0
SWIGLU_ALPHA = 1.702
EPS = 1e-6

LANES = 128
SLAB = LANES
VMEM_LIMIT = 56 * 1024 * 1024

F32 = jnp.float32
BF16 = jnp.bfloat16

O_U = 0
O_G = O_U + D_REC
O_Q = O_G + D_REC
O_K = O_Q + N_HEADS * SLAB
O_V = O_K + N_HEADS * SLAB
O_F = O_V + N_HEADS * SLAB
N_PROJ = O_F + LANES

L_C = HEAD_DIM
L_ONE = HEAD_DIM + 3

TM_IN = 512
TS_REC = 512
TQ = 512
TK = 512
HEADS_PER_STEP = 4
TM_OUT = 512
TM_E = 512

UNIT = 16
TILE_SLOTS = -(-(TM_OUT * TOP_K + N_EXPERTS * (UNIT - 1)) // LANES) * LANES
TILE_UNITS = TILE_SLOTS // UNIT
BLOCK_UNITS = TM_E // UNIT
SLOT_CHUNK = 512


def _split3(v):
    hi = v.astype(BF16).astype(F32)
    r = v - hi
    mid = r.astype(BF16).astype(F32)
    lo = (r - mid).astype(BF16).astype(F32)
    return hi, mid, lo


def _dot(a, b):
    return jnp.dot(a, b, preferred_element_type=F32)


def _dot_t(a, b):
    return lax.dot_general(a, b, (((0,), (0,)), ((), ())), preferred_element_type=F32)


def _in_proj_kernel(x_ref, ln_ref, w_ref, bf_ref, qw_ref, kw_ref,
                    u_ref, g_ref, q_ref, k_ref, v_ref, carry_ref, *, tiles_per_seq):
    i = pl.program_id(0)
    tm = x_ref.shape[0]

    @pl.when(i % tiles_per_seq == 0)
    def _():
        carry_ref[...] = jnp.zeros_like(carry_ref)

    x = x_ref[...]
    ms = jnp.mean(x * x, axis=-1, keepdims=True)
    h = (x * lax.rsqrt(ms + EPS) * ln_ref[...]).astype(BF16)

    u_ref[...] = _dot(h, w_ref[:, O_U:O_U + D_REC])
    g_ref[...] = _dot(h, w_ref[:, O_G:O_G + D_REC])

    f = _dot(h, w_ref[:, O_F:O_F + LANES]) + bf_ref[...]
    logf = jnp.minimum(f, 0.0) - jnp.log1p(jnp.exp(-jnp.abs(f)))
    row = lax.broadcasted_iota(jnp.int32, (tm, tm), 0)
    col = lax.broadcasted_iota(jnp.int32, (tm, tm), 1)
    tri = (row >= col).astype(BF16)
    hi, mid, lo = _split3(logf)
    c = (_dot(tri, hi.astype(BF16)) + _dot(tri, mid.astype(BF16))
         + _dot(tri, lo.astype(BF16)) + carry_ref[...])
    carry_ref[...] = c[tm - 1:tm, :]

    lane = lax.broadcasted_iota(jnp.int32, (tm, SLAB), 1)
    q_all = _dot(h, w_ref[:, O_Q:O_Q + N_HEADS * SLAB])
    k_all = _dot(h, w_ref[:, O_K:O_K + N_HEADS * SLAB])
    v_all = _dot(h, w_ref[:, O_V:O_V + N_HEADS * SLAB])
    qw = qw_ref[...]
    kw = kw_ref[...]
    g_row = lax.broadcasted_iota(jnp.int32, (N_HEADS * SLAB, LANES), 0)
    g_col = lax.broadcasted_iota(jnp.int32, (N_HEADS * SLAB, LANES), 1)
    group = ((g_row >= g_col * SLAB) & (g_row < (g_col + 1) * SLAB)).astype(BF16)
    q_inv = lax.rsqrt(_dot((q_all * q_all).astype(BF16), group) * (1.0 / HEAD_DIM) + EPS)
    k_inv = lax.rsqrt(_dot((k_all * k_all).astype(BF16), group) * (1.0 / HEAD_DIM) + EPS)
    for hd in range(N_HEADS):
        sl = slice(hd * SLAB, (hd + 1) * SLAB)
        qn = q_all[:, sl] * q_inv[:, hd:hd + 1] * qw
        kn = k_all[:, sl] * k_inv[:, hd:hd + 1] * kw
        c_hi, c_mid, c_lo = _split3(c[:, hd:hd + 1])
        qp = jnp.where(lane == L_C, c_hi, qn)
        qp = jnp.where(lane == L_C + 1, c_mid, qp)
        qp = jnp.where(lane == L_C + 2, c_lo, qp)
        qp = jnp.where((lane >= L_ONE) & (lane < L_ONE + 3), 1.0, qp)
        kp = jnp.where((lane >= L_C) & (lane < L_C + 3), 1.0, kn)
        kp = jnp.where(lane == L_ONE, -c_hi, kp)
        kp = jnp.where(lane == L_ONE + 1, -c_mid, kp)
        kp = jnp.where(lane == L_ONE + 2, -c_lo, kp)
        q_ref[0, hd] = qp.astype(BF16)
        k_ref[0, hd] = kp.astype(BF16)
        v_ref[0, hd] = jnp.where(lane == HEAD_DIM, 1.0, v_all[:, sl]).astype(BF16)


def _in_proj(x2d, ln1_w, w_all, bf_pad, qw_slab, kw_slab, batch, seq):
    t = x2d.shape[0]
    tiles_per_seq = seq // TM_IN
    slab_shape = jax.ShapeDtypeStruct((batch, N_HEADS, seq, SLAB), BF16)
    slab_spec = pl.BlockSpec((1, N_HEADS, TM_IN, SLAB),
                             lambda i: (i // tiles_per_seq, 0, i % tiles_per_seq, 0))
    row_spec = pl.BlockSpec((TM_IN, D_REC), lambda i: (i, 0))
    const = lambda shape: pl.BlockSpec(shape, lambda i: (0,) * len(shape))
    return pl.pallas_call(
        functools.partial(_in_proj_kernel, tiles_per_seq=tiles_per_seq),
        out_shape=(jax.ShapeDtypeStruct((t, D_REC), F32), jax.ShapeDtypeStruct((t, D_REC), F32),
                   slab_shape, slab_shape, slab_shape),
        grid=(t // TM_IN,),
        in_specs=[pl.BlockSpec((TM_IN, D_MODEL), lambda i: (i, 0)),
                  const((1, D_MODEL)), const((D_MODEL, N_PROJ)), const((1, LANES)),
                  const((1, SLAB)), const((1, SLAB))],
        out_specs=(row_spec, row_spec, slab_spec, slab_spec, slab_spec),
        scratch_shapes=[pltpu.VMEM((1, LANES), F32)],
        compiler_params=pltpu.CompilerParams(dimension_semantics=("arbitrary",),
                                             vmem_limit_bytes=VMEM_LIMIT),
        name="in_proj",
    )(x2d, ln1_w, w_all, bf_pad, qw_slab, kw_slab)


def _rglru_kernel(u_ref, g_ref, cw_ref, cb_ref, wa_ref, ba_ref, wi_ref, bi_ref, lam_ref, nw_ref,
                  o_ref, ext_ref, a_ref, b_ref, h_ref, hcar_ref):
    s = pl.program_id(1)
    ts = u_ref.shape[1]
    pad = 8

    @pl.when(s == 0)
    def _():
        ext_ref[0:pad, :] = jnp.zeros((pad, D_REC), F32)
        hcar_ref[...] = jnp.zeros_like(hcar_ref)

    @pl.when(s != 0)
    def _():
        ext_ref[0:pad, :] = ext_ref[ts:ts + pad, :]

    u = u_ref[0]
    ext_ref[pad:pad + ts, :] = u
    xc = cb_ref[...] + u * cw_ref[CONV_WIDTH - 1:CONV_WIDTH, :]
    for tap in range(CONV_WIDTH - 1):
        d = CONV_WIDTH - 1 - tap
        xc = xc + ext_ref[pad - d:pad - d + ts, :] * cw_ref[tap:tap + 1, :]

    xb = xc.astype(BF16)
    r = jax.nn.sigmoid(_dot(xb, wa_ref[...]) + ba_ref[...])
    ig = jax.nn.sigmoid(_dot(xb, wi_ref[...]) + bi_ref[...])
    nlam = -lam_ref[...]
    softplus = jnp.maximum(nlam, 0.0) + jnp.log1p(jnp.exp(-jnp.abs(nlam)))
    log_a = -RGLRU_C * r * softplus
    a = jnp.exp(log_a)
    a_ref[...] = a
    b_ref[...] = jnp.sqrt(1.0 - jnp.exp(2.0 * log_a)) * (ig * xc)

    def step(t, h):
        h = a_ref[pl.ds(t, 1), :] * h + b_ref[pl.ds(t, 1), :]
        h_ref[pl.ds(t, 1), :] = h
        return h

    hcar_ref[...] = lax.fori_loop(0, ts, step, hcar_ref[...], unroll=8)

    rec = h_ref[...] * jax.nn.gelu(g_ref[0])
    ms = jnp.mean(rec * rec, axis=-1, keepdims=True)
    o_ref[0] = (rec * lax.rsqrt(ms + EPS) * nw_ref[...]).astype(BF16)


def _rglru(u, g, conv_w, conv_b, wa_bd, b_a, wi_bd, b_i, lam, norm_w):
    batch, seq, _ = u.shape
    seq_spec = pl.BlockSpec((1, TS_REC, D_REC), lambda b, s: (b, s, 0))
    const = lambda shape: pl.BlockSpec(shape, lambda b, s: (0,) * len(shape))
    return pl.pallas_call(
        _rglru_kernel,
        out_shape=jax.ShapeDtypeStruct((batch, seq, D_REC), BF16),
        grid=(batch, seq // TS_REC),
        in_specs=[seq_spec, seq_spec, const((CONV_WIDTH, D_REC)), const((1, D_REC)),
                  const((D_REC, D_REC)), const((1, D_REC)), const((D_REC, D_REC)), const((1, D_REC)),
                  const((1, D_REC)), const((1, D_REC))],
        out_specs=seq_spec,
        scratch_shapes=[pltpu.VMEM((TS_REC + 8, D_REC), F32), pltpu.VMEM((TS_REC, D_REC), F32),
                        pltpu.VMEM((TS_REC, D_REC), F32), pltpu.VMEM((TS_REC, D_REC), F32),
                        pltpu.VMEM((1, D_REC), F32)],
        compiler_params=pltpu.CompilerParams(dimension_semantics=("arbitrary", "arbitrary"),
                                             vmem_limit_bytes=VMEM_LIMIT),
        name="rglru",
    )(u, g, conv_w, conv_b, wa_bd, b_a, wi_bd, b_i, lam, norm_w)


def _attn_kernel(q_ref, k_ref, v_ref, o_ref):
    qi = pl.program_id(2)
    hb = q_ref.shape[1]
    tq = q_ref.shape[2]
    qs = [q_ref[0, h] for h in range(hb)]

    def step(j, carry, masked):
        start = pl.multiple_of(j * TK, TK)
        out = []
        for h in range(hb):
            m, acc = carry[h]
            kj = k_ref[0, h, pl.ds(start, TK), :]
            vj = v_ref[0, h, pl.ds(start, TK), :]
            s = lax.dot_general(qs[h], kj, (((1,), (1,)), ((), ())), preferred_element_type=F32)
            if masked:
                row = lax.broadcasted_iota(jnp.int32, (tq, TK), 0)
                col = lax.broadcasted_iota(jnp.int32, (tq, TK), 1)
                s = jnp.where(row >= col, s, -jnp.inf)
            m_new = jnp.maximum(m, jnp.max(s, axis=-1, keepdims=True))
            p = jnp.exp(s - m_new)
            acc = jnp.exp(m - m_new) * acc + _dot(p.astype(BF16), vj)
            out.append((m_new, acc))
        return tuple(out)

    init = tuple((jnp.full((tq, 1), -jnp.inf, F32), jnp.zeros((tq, SLAB), F32)) for _ in range(hb))
    carry = lax.fori_loop(0, qi, lambda j, c: step(j, c, False), init)
    carry = step(qi, carry, True)
    lane = lax.broadcasted_iota(jnp.int32, (tq, SLAB), 1)
    for h in range(hb):
        acc = carry[h][1]
        o_ref[0, :, h * SLAB:(h + 1) * SLAB] = jnp.where(
            lane < HEAD_DIM, acc / acc[:, HEAD_DIM:HEAD_DIM + 1], 0.0)


def _attention(qp, kp, vp):
    batch, _, seq, _ = qp.shape
    hb = HEADS_PER_STEP
    kv_spec = pl.BlockSpec((1, hb, seq, SLAB), lambda b, h, i: (b, h, 0, 0))
    return pl.pallas_call(
        _attn_kernel,
        out_shape=jax.ShapeDtypeStruct((batch, seq, N_HEADS * SLAB), F32),
        grid=(batch, N_HEADS // hb, seq // TQ),
        in_specs=[pl.BlockSpec((1, hb, TQ, SLAB), lambda b, h, i: (b, h, i, 0)), kv_spec, kv_spec],
        out_specs=pl.BlockSpec((1, TQ, hb * SLAB), lambda b, h, i: (b, i, h)),
        compiler_params=pltpu.CompilerParams(
            dimension_semantics=("arbitrary", "arbitrary", "arbitrary"),
            vmem_limit_bytes=VMEM_LIMIT),
        name="attn",
    )(qp, kp, vp)


def _slot_onehot(slots, first, width):
    tm = slots[0].shape[0]
    col = lax.broadcasted_iota(jnp.int32, (tm, width), 1) + first
    hit = jnp.zeros((tm, width), F32)
    for slot in slots:
        hit = jnp.where(col == slot, 1.0, hit)
    return hit.astype(BF16)


def _out_proj_kernel(x_ref, rec_ref, att_ref, aw_ref, wr_ref, wa_ref, ln_ref, rwh_ref, rwl_ref, rb_ref,
                     x1_ref, xs_ref, meta_ref, units_ref):
    tm = x_ref.shape[0]

    att = att_ref[...]
    ms = jnp.sum(att * att, axis=-1, keepdims=True) * (1.0 / D_ATT)
    att_n = (att * lax.rsqrt(ms + EPS) * aw_ref[...]).astype(BF16)
    x1 = x_ref[...] + _dot(rec_ref[...], wr_ref[...]) + _dot(att_n, wa_ref[...])
    x1_ref[...] = x1

    ms2 = jnp.mean(x1 * x1, axis=-1, keepdims=True)
    h2 = x1 * lax.rsqrt(ms2 + EPS) * ln_ref[...]

    h_hi = h2.astype(BF16)
    h_lo = (h2 - h_hi.astype(F32)).astype(BF16)
    logits = (_dot(h_hi, rwh_ref[...]) + _dot(h_lo, rwh_ref[...]) + _dot(h_hi, rwl_ref[...])
              + rb_ref[...])
    lane = lax.broadcasted_iota(jnp.int32, (tm, LANES), 1)
    lane_f = lane.astype(F32)
    l = jnp.where(lane < N_EXPERTS, logits, -jnp.inf)

    vals, idxs, sels = [], [], []
    for _ in range(TOP_K):
        m = jnp.max(l, axis=-1, keepdims=True)
        idx = jnp.min(jnp.where(l == m, lane_f, float(LANES)), axis=-1, keepdims=True)
        sel = lane_f == idx
        vals.append(m)
        idxs.append(idx)
        sels.append(sel)
        l = jnp.where(sel, -jnp.inf, l)
    exps = [jnp.exp(v - vals[0]) for v in vals]
    denom = exps[0] + exps[1] + exps[2] + exps[3]

    any_sel = sels[0] | sels[1] | sels[2] | sels[3]
    onehot = jnp.where(any_sel, 1.0, 0.0)
    row = lax.broadcasted_iota(jnp.int32, (tm, tm), 0)
    col = lax.broadcasted_iota(jnp.int32, (tm, tm), 1)
    before = (row > col).astype(BF16)
    rank = _dot(before, onehot.astype(BF16))
    count = jnp.sum(onehot, axis=0, keepdims=True)
    units = jnp.floor((count + (UNIT - 1)) * (1.0 / UNIT))
    e_row = lax.broadcasted_iota(jnp.int32, (LANES, LANES), 0)
    e_col = lax.broadcasted_iota(jnp.int32, (LANES, LANES), 1)
    earlier = (e_row < e_col).astype(BF16)
    units8 = jnp.broadcast_to(units, (8, LANES))
    first_unit = _dot(units8.astype(BF16), earlier)[0:1, :]
    units_ref[0] = units8
    place = first_unit * float(UNIT) + rank

    meta = jnp.zeros((tm, LANES), F32)
    slots = []
    for k in range(TOP_K):
        slot = jnp.sum(jnp.where(sels[k], place, 0.0), axis=-1, keepdims=True)
        slots.append(slot.astype(jnp.int32))
        meta = jnp.where(lane == k, idxs[k], meta)
        meta = jnp.where(lane == TOP_K + k, exps[k] / denom, meta)
        meta = jnp.where(lane == 2 * TOP_K + k, slot, meta)
    meta_ref[...] = meta

    for c in range(TILE_SLOTS // SLOT_CHUNK):
        sel_t = _slot_onehot(slots, c * SLOT_CHUNK, SLOT_CHUNK)
        xs_ref[c * SLOT_CHUNK:(c + 1) * SLOT_CHUNK, :] = _dot_t(sel_t, h_hi).astype(BF16)


def _out_proj(x2d, rec_n, att, aw_slab, w_rec, w_att, ln2_w, rw_hi, rw_lo, rb_pad):
    t = x2d.shape[0]
    n_tiles = t // TM_OUT
    rows = lambda n: pl.BlockSpec((TM_OUT, n), lambda i: (i, 0))
    const = lambda shape: pl.BlockSpec(shape, lambda i: (0,) * len(shape))
    return pl.pallas_call(
        _out_proj_kernel,
        out_shape=(jax.ShapeDtypeStruct((t, D_MODEL), F32),
                   jax.ShapeDtypeStruct((n_tiles * TILE_SLOTS, D_MODEL), BF16),
                   jax.ShapeDtypeStruct((t, LANES), F32),
                   jax.ShapeDtypeStruct((n_tiles, 8, LANES), F32)),
        grid=(n_tiles,),
        in_specs=[rows(D_MODEL), rows(D_REC), rows(N_HEADS * SLAB), const((1, N_HEADS * SLAB)),
                  const((D_REC, D_MODEL)), const((N_HEADS * SLAB, D_MODEL)), const((1, D_MODEL)),
                  const((D_MODEL, LANES)), const((D_MODEL, LANES)), const((1, LANES))],
        out_specs=(rows(D_MODEL), pl.BlockSpec((TILE_SLOTS, D_MODEL), lambda i: (i, 0)), rows(LANES),
                   pl.BlockSpec((1, 8, LANES), lambda i: (i, 0, 0))),
        compiler_params=pltpu.CompilerParams(dimension_semantics=("arbitrary",),
                                             vmem_limit_bytes=VMEM_LIMIT),
        name="out_proj",
    )(x2d, rec_n, att, aw_slab, w_rec, w_att, ln2_w, rw_hi, rw_lo, rb_pad)


def _start_units(unit_ref, base, n_units, src_hbm, dst, sem):
    def body(j, carry):
        src_row = pl.multiple_of(unit_ref[base + j] * UNIT, UNIT)
        dst_row = pl.multiple_of(j * UNIT, UNIT)
        pltpu.make_async_copy(src_hbm.at[pl.ds(src_row, UNIT), :],
                              dst.at[pl.ds(dst_row, UNIT), :], sem).start()
        return carry
    lax.fori_loop(0, n_units, body, 0, unroll=8)


def _wait_units(n_units, src_hbm, dst, sem):
    pltpu.make_async_copy(src_hbm.at[pl.ds(0, n_units * UNIT), :], dst, sem).wait()


def _experts_kernel(be_ref, nblk_ref, unit_ref, order_ref, next_ref, xs_hbm, wgu_hbm, bgu_ref, wd_hbm,
                    bd_ref, ys_ref, xbuf, wgu_f32, wd_f32, wgu_bf, wd_bf, sem, wsem):
    b = pl.program_id(0)
    nblk = nblk_ref[0]
    slot = b % 2
    e = be_ref[b]

    def weight_copies(expert, wslot):
        return (pltpu.make_async_copy(wgu_hbm.at[expert], wgu_f32.at[wslot], wsem.at[wslot]),
                pltpu.make_async_copy(wd_hbm.at[expert], wd_f32.at[wslot], wsem.at[wslot]))

    @pl.when(b == 0)
    def _():
        _start_units(unit_ref, 0, BLOCK_UNITS, xs_hbm, xbuf.at[0], sem.at[0])
        for copy in weight_copies(e, 0):
            copy.start()

    @pl.when(b + 1 < nblk)
    def _():
        _start_units(unit_ref, (b + 1) * BLOCK_UNITS, BLOCK_UNITS, xs_hbm, xbuf.at[1 - slot],
                     sem.at[1 - slot])

    new_expert = jnp.logical_or(b == 0, e != be_ref[jnp.maximum(b - 1, 0)])

    @pl.when(jnp.logical_and(b < nblk, new_expert))
    def _():
        wslot = order_ref[e] % 2
        for copy in weight_copies(e, wslot):
            copy.wait()
        wgu_bf[...] = wgu_f32[wslot].astype(BF16)
        wd_bf[...] = wd_f32[wslot].astype(BF16)
        nxt = next_ref[e]

        @pl.when(nxt >= 0)
        def _():
            for copy in weight_copies(nxt, 1 - wslot):
                copy.start()

    @pl.when(b < nblk)
    def _():
        _wait_units(BLOCK_UNITS, xs_hbm, xbuf.at[slot], sem.at[slot])
        gu = _dot(xbuf[slot], wgu_bf[...]) + bgu_ref[0]
        g = jnp.minimum(gu[:, :D_EXPERT], SWIGLU_LIMIT)
        u = jnp.clip(gu[:, D_EXPERT:], -SWIGLU_LIMIT, SWIGLU_LIMIT)
        act = (u + 1.0) * (g * jax.nn.sigmoid(SWIGLU_ALPHA * g))
        ys_ref[...] = (_dot(act.astype(BF16), wd_bf[...]) + bd_ref[0]).astype(BF16)

    @pl.when(b >= nblk)
    def _():
        ys_ref[...] = jnp.zeros_like(ys_ref)


def _experts(block_expert, n_blocks_used, unit_src, expert_order, expert_next, xs, w_gate_up,
             b_gate_up, w_down, b_down):
    n_blocks = block_expert.shape[0]
    bias_map = lambda b, be, nb, us, eo, en: (be[b], 0, 0)
    grid_spec = pltpu.PrefetchScalarGridSpec(
        num_scalar_prefetch=5,
        grid=(n_blocks,),
        in_specs=[pl.BlockSpec(memory_space=pl.ANY),
                  pl.BlockSpec(memory_space=pl.ANY),
                  pl.BlockSpec((1, 1, 2 * D_EXPERT), bias_map),
                  pl.BlockSpec(memory_space=pl.ANY),
                  pl.BlockSpec((1, 1, D_MODEL), bias_map)],
        out_specs=pl.BlockSpec((TM_E, D_MODEL), lambda b, be, nb, us, eo, en: (b, 0)),
        scratch_shapes=[pltpu.VMEM((2, TM_E, D_MODEL), BF16),
                        pltpu.VMEM((2, D_MODEL, 2 * D_EXPERT), F32),
                        pltpu.VMEM((2, D_EXPERT, D_MODEL), F32),
                        pltpu.VMEM((D_MODEL, 2 * D_EXPERT), BF16),
                        pltpu.VMEM((D_EXPERT, D_MODEL), BF16),
                        pltpu.SemaphoreType.DMA((2,)),
                        pltpu.SemaphoreType.DMA((2,))],
    )
    return pl.pallas_call(
        _experts_kernel,
        out_shape=jax.ShapeDtypeStruct((n_blocks * TM_E, D_MODEL), BF16),
        grid_spec=grid_spec,
        compiler_params=pltpu.CompilerParams(dimension_semantics=("arbitrary",),
                                             vmem_limit_bytes=VMEM_LIMIT),
        name="experts",
    )(block_expert, n_blocks_used, unit_src, expert_order, expert_next, xs, w_gate_up,
      b_gate_up.reshape(N_EXPERTS, 1, 2 * D_EXPERT), w_down, b_down.reshape(N_EXPERTS, 1, D_MODEL))


def _combine_kernel(unit_ref, x1_ref, meta_ref, ys_hbm, o_ref, buf, sem):
    i = pl.program_id(0)
    n = pl.num_programs(0)
    slot = i % 2
    tm = x1_ref.shape[0]

    @pl.when(i == 0)
    def _():
        _start_units(unit_ref, 0, TILE_UNITS, ys_hbm, buf.at[0], sem.at[0])

    @pl.when(i + 1 < n)
    def _():
        _start_units(unit_ref, (i + 1) * TILE_UNITS, TILE_UNITS, ys_hbm, buf.at[1 - slot],
                     sem.at[1 - slot])

    meta = meta_ref[...]
    slots = [meta[:, 2 * TOP_K + k:2 * TOP_K + k + 1].astype(jnp.int32) for k in range(TOP_K)]
    gates = [meta[:, TOP_K + k:TOP_K + k + 1] for k in range(TOP_K)]
    _wait_units(TILE_UNITS, ys_hbm, buf.at[slot], sem.at[slot])

    acc = x1_ref[...]
    for c in range(TILE_SLOTS // SLOT_CHUNK):
        first = c * SLOT_CHUNK
        col = lax.broadcasted_iota(jnp.int32, (tm, SLOT_CHUNK), 1) + first
        weight = jnp.zeros((tm, SLOT_CHUNK), F32)
        for k in range(TOP_K):
            weight = jnp.where(col == slots[k], gates[k], weight)
        acc = acc + _dot(weight.astype(BF16), buf[slot, first:first + SLOT_CHUNK, :])
    o_ref[...] = acc


def _combine(tile_units, x1, meta, ys):
    t = x1.shape[0]
    grid_spec = pltpu.PrefetchScalarGridSpec(
        num_scalar_prefetch=1,
        grid=(t // TM_OUT,),
        in_specs=[pl.BlockSpec((TM_OUT, D_MODEL), lambda i, us: (i, 0)),
                  pl.BlockSpec((TM_OUT, LANES), lambda i, us: (i, 0)),
                  pl.BlockSpec(memory_space=pl.ANY)],
        out_specs=pl.BlockSpec((TM_OUT, D_MODEL), lambda i, us: (i, 0)),
        scratch_shapes=[pltpu.VMEM((2, TILE_SLOTS, D_MODEL), BF16),
                        pltpu.SemaphoreType.DMA((2,))],
    )
    return pl.pallas_call(
        _combine_kernel,
        out_shape=jax.ShapeDtypeStruct((t, D_MODEL), F32),
        grid_spec=grid_spec,
        compiler_params=pltpu.CompilerParams(dimension_semantics=("arbitrary",),
                                             vmem_limit_bytes=VMEM_LIMIT),
        name="combine",
    )(tile_units, x1, meta, ys)


def _head_slabs(w):
    d = w.shape[0]
    w = w.reshape(d, N_HEADS, HEAD_DIM)
    return jnp.pad(w, ((0, 0), (0, 0), (0, SLAB - HEAD_DIM))).reshape(d, N_HEADS * SLAB)


def _block_diag(w):
    g = w.shape[0]
    eye = jnp.eye(g, dtype=w.dtype)
    return jnp.einsum('gij,gh->gihj', w, eye).reshape(g * REC_BLOCK, g * REC_BLOCK)


def _pad_lanes(v, n=LANES):
    return jnp.pad(v, (0, n - v.shape[0])).reshape(1, n)


def _excl_cumsum(a, axis):
    return jnp.cumsum(a, axis=axis) - a


def _pick(onehot, table):
    if table.ndim == 1:
        return jnp.sum(onehot * table[None, :], axis=1)
    return jnp.sum(onehot[:, :, None] * table[None, :, :], axis=1)


def _dispatch_tables(units, n_blocks):
    n_tiles = units.shape[0]
    i32 = jnp.int32
    tile_first = _excl_cumsum(units, 1)
    tile_end = tile_first + units
    run_first = _excl_cumsum(units, 0)
    run_end = run_first + units
    region_units = jnp.sum(units, axis=0)
    region_blocks = (region_units + BLOCK_UNITS - 1) // BLOCK_UNITS
    block_end = jnp.cumsum(region_blocks)
    region_first = (block_end - region_blocks) * BLOCK_UNITS
    n_used = block_end[-1:]
    experts = jnp.arange(N_EXPERTS, dtype=i32)
    tiles = jnp.arange(n_tiles, dtype=i32)

    blocks = jnp.arange(n_blocks, dtype=i32)
    block_expert = jnp.minimum(
        jnp.sum((block_end[None, :] <= blocks[:, None]).astype(i32), axis=1), N_EXPERTS - 1)

    g = jnp.arange(n_blocks * BLOCK_UNITS, dtype=i32)
    e = jnp.minimum(jnp.sum((block_end[None, :] * BLOCK_UNITS <= g[:, None]).astype(i32), axis=1),
                    N_EXPERTS - 1)
    e_hot = (e[:, None] == experts[None, :]).astype(i32)
    local = g - _pick(e_hot, region_first)
    valid = (local < _pick(e_hot, region_units)) & (g < n_used[0] * BLOCK_UNITS)
    ends = _pick(e_hot, run_end.T)
    tile = jnp.minimum(jnp.sum((ends <= local[:, None]).astype(i32), axis=1), n_tiles - 1)
    t_hot = (tile[:, None] == tiles[None, :]).astype(i32)
    src = (tile * TILE_UNITS + jnp.sum(_pick(t_hot, tile_first - run_first) * e_hot, axis=1) + local)
    unit_src = jnp.where(valid, src, 0).astype(i32)

    j = jnp.arange(TILE_UNITS, dtype=i32)
    ej = jnp.minimum(jnp.sum((tile_end[:, None, :] <= j[None, :, None]).astype(i32), axis=2),
                     N_EXPERTS - 1)
    ej_hot = (ej[:, :, None] == experts[None, None, :]).astype(i32)
    used = j[None, :] < jnp.sum(units, axis=1)[:, None]
    base = region_first[None, :] + run_first - tile_first
    dst = jnp.sum(ej_hot * base[:, None, :], axis=2) + j[None, :]
    tile_units = jnp.where(used, dst, 0).astype(i32).reshape(-1)
    owns = region_blocks > 0
    expert_order = _excl_cumsum(owns.astype(i32), 0)
    later = (experts[None, :] > experts[:, None]) & owns[None, :]
    expert_next = jnp.min(jnp.where(later, experts[None, :], N_EXPERTS), axis=1)
    expert_next = jnp.where(expert_next < N_EXPERTS, expert_next, -1).astype(i32)
    return (block_expert.astype(i32), n_used.astype(i32), unit_src, tile_units,
            expert_order.astype(i32), expert_next)


def _layer(x, ln1_w, w_in, conv_w, conv_b, w_rg_a, b_rg_a, w_rg_i, b_rg_i, rg_lambda, b_forget,
           q_norm_w, k_norm_w, rec_out_norm_w, attn_out_norm_w, w_out, ln2_w, w_router, b_router,
           w_gate_up, b_gate_up, w_down, b_down):
    batch, seq, _ = x.shape
    t = batch * seq
    x2d = x.reshape(t, D_MODEL)

    o_q = 2 * D_REC
    o_k = o_q + D_ATT
    o_v = o_k + D_ATT
    o_f = o_v + D_ATT
    w_all = jnp.concatenate(
        [w_in[:, :o_q], _head_slabs(w_in[:, o_q:o_k]), _head_slabs(w_in[:, o_k:o_v]),
         _head_slabs(w_in[:, o_v:o_f]), jnp.pad(w_in[:, o_f:], ((0, 0), (0, LANES - N_HEADS)))],
        axis=1).astype(BF16)
    scale = HEAD_DIM ** -0.5
    u, g, qp, kp, vp = _in_proj(x2d, ln1_w.reshape(1, D_MODEL), w_all, _pad_lanes(b_forget),
                                _pad_lanes(q_norm_w * scale), _pad_lanes(k_norm_w), batch, seq)

    rec_n = _rglru(u.reshape(batch, seq, D_REC), g.reshape(batch, seq, D_REC), conv_w,
                   conv_b.reshape(1, D_REC), _block_diag(w_rg_a).astype(BF16), b_rg_a.reshape(1, D_REC),
                   _block_diag(w_rg_i).astype(BF16), b_rg_i.reshape(1, D_REC),
                   rg_lambda.reshape(1, D_REC), rec_out_norm_w.reshape(1, D_REC))

    att = _attention(qp, kp, vp)

    aw_slab = _head_slabs(attn_out_norm_w.reshape(1, D_ATT))
    w_att = jnp.pad(w_out[D_REC:].reshape(N_HEADS, HEAD_DIM, D_MODEL),
                    ((0, 0), (0, SLAB - HEAD_DIM), (0, 0))).reshape(N_HEADS * SLAB, D_MODEL)
    rw = jnp.pad(w_router, ((0, 0), (0, LANES - N_EXPERTS)))
    rw_hi = rw.astype(BF16)
    rw_lo = (rw - rw_hi.astype(F32)).astype(BF16)
    x1, xs, meta, units = _out_proj(
        x2d, rec_n.reshape(t, D_REC), att.reshape(t, N_HEADS * SLAB), aw_slab,
        w_out[:D_REC].astype(BF16), w_att.astype(BF16), ln2_w.reshape(1, D_MODEL),
        rw_hi, rw_lo, _pad_lanes(b_router))

    n_tiles = t // TM_OUT
    max_units = t * TOP_K // UNIT + n_tiles * N_EXPERTS
    n_blocks = -(-max_units // BLOCK_UNITS) + N_EXPERTS
    block_expert, n_used, unit_src, tile_units, expert_order, expert_next = _dispatch_tables(
        units[:, 0, :N_EXPERTS].astype(jnp.int32), n_blocks)

    ys = _experts(block_expert, n_used, unit_src, expert_order, expert_next, xs, w_gate_up, b_gate_up,
                  w_down, b_down)
    out = _combine(tile_units, x1, meta, ys)
    return out.reshape(batch, seq, D_MODEL)


def kernel(x, ln1_w, w_in, conv_w, conv_b, w_rg_a, b_rg_a, w_rg_i, b_rg_i, rg_lambda, b_forget,
           q_norm_w, k_norm_w, rec_out_norm_w, attn_out_norm_w, w_out, ln2_w, w_router, b_router,
           w_gate_up, b_gate_up, w_down, b_down):
    for l in range(ln1_w.shape[0]):
        x = _layer(x, ln1_w[l], w_in[l], conv_w[l], conv_b[l], w_rg_a[l], b_rg_a[l], w_rg_i[l],
                   b_rg_i[l], rg_lambda[l], b_forget[l], q_norm_w[l], k_norm_w[l], rec_out_norm_w[l],
                   attn_out_norm_w[l], w_out[l], ln2_w[l], w_router[l], b_router[l], w_gate_up[l],
                   b_gate_up[l], w_down[l], b_down[l])
    return x
```

```python
import functools

import jax
import jax.numpy as jnp
from jax import lax
from jax.experimental import pallas as pl
from jax.experimental.pallas import tpu as pltpu

D_MODEL = 1024
D_REC = 512
N_REC_BLOCKS = 8
REC_BLOCK = 64
CONV_WIDTH = 4
RGLRU_C = 8.0
D_ATT = 512
HEAD_DIM = 64
N_HEADS = 8
N_EXPERTS = 32
TOP_K = 4
D_EXPERT = 1024
SWIGLU_LIMIT = 7.0
SWIGLU_ALPHA = 1.702
EPS = 1e-6

LANES = 128
SLAB = LANES
VMEM_LIMIT = 56 * 1024 * 1024

F32 = jnp.float32
BF16 = jnp.bfloat16

O_U = 0
O_G = O_U + D_REC
O_Q = O_G + D_REC
O_K = O_Q + N_HEADS * SLAB
O_V = O_K + N_HEADS * SLAB
O_F = O_V + N_HEADS * SLAB
N_PROJ = O_F + LANES

L_C = HEAD_DIM
L_ONE = HEAD_DIM + 3

TM_IN = 512
TS_REC = 512
TQ = 512
TK = 512
HEADS_PER_STEP = 4
TM_OUT = 512
TM_E = 512

UNIT = 16
TILE_SLOTS = -(-(TM_OUT * TOP_K + N_EXPERTS * (UNIT - 1)) // LANES) * LANES
TILE_UNITS = TILE_SLOTS // UNIT
BLOCK_UNITS = TM_E // UNIT
SLOT_CHUNK = 512


def _split3(v):
    hi = v.astype(BF16).astype(F32)
    r = v - hi
    mid = r.astype(BF16).astype(F32)
    lo = (r - mid).astype(BF16).astype(F32)
    return hi, mid, lo


def _dot(a, b):
    return jnp.dot(a, b, preferred_element_type=F32)


def _dot_t(a, b):
    return lax.dot_general(a, b, (((0,), (0,)), ((), ())), preferred_element_type=F32)


def _in_proj_kernel(x_ref, ln_ref, w_ref, bf_ref, qw_ref, kw_ref,
                    u_ref, g_ref, q_ref, k_ref, v_ref, carry_ref, *, tiles_per_seq):
    i = pl.program_id(0)
    tm = x_ref.shape[0]

    @pl.when(i % tiles_per_seq == 0)
    def _():
        carry_ref[...] = jnp.zeros_like(carry_ref)

    x = x_ref[...]
    ms = jnp.mean(x * x, axis=-1, keepdims=True)
    h = (x * lax.rsqrt(ms + EPS) * ln_ref[...]).astype(BF16)

    f = _dot(h, w_ref[:, O_F:O_F + LANES]) + bf_ref[...]
    logf = jnp.minimum(f, 0.0) - jnp.log1p(jnp.exp(-jnp.abs(f)))
    row = lax.broadcasted_iota(jnp.int32, (tm, tm), 0)
    col = lax.broadcasted_iota(jnp.int32, (tm, tm), 1)
    tri = (row >= col).astype(BF16)
    hi, mid, lo = _split3(logf)
    c = (_dot(tri, hi.astype(BF16)) + _dot(tri, mid.astype(BF16))
         + _dot(tri, lo.astype(BF16)) + carry_ref[...])
    carry_ref[...] = c[tm - 1:tm, :]

    lane = lax.broadcasted_iota(jnp.int32, (tm, SLAB), 1)
    q_all = _dot(h, w_ref[:, O_Q:O_Q + N_HEADS * SLAB])
    k_all = _dot(h, w_ref[:, O_K:O_K + N_HEADS * SLAB])
    qw = qw_ref[...]
    kw = kw_ref[...]
    g_row = lax.broadcasted_iota(jnp.int32, (N_HEADS * SLAB, LANES), 0)
    g_col = lax.broadcasted_iota(jnp.int32, (N_HEADS * SLAB, LANES), 1)
    group = ((g_row >= g_col * SLAB) & (g_row < (g_col + 1) * SLAB)).astype(BF16)
    q_inv = lax.rsqrt(_dot((q_all * q_all).astype(BF16), group) * (1.0 / HEAD_DIM) + EPS)
    k_inv = lax.rsqrt(_dot((k_all * k_all).astype(BF16), group) * (1.0 / HEAD_DIM) + EPS)
    for hd in range(N_HEADS):
        sl = slice(hd * SLAB, (hd + 1) * SLAB)
        qn = q_all[:, sl] * q_inv[:, hd:hd + 1] * qw
        kn = k_all[:, sl] * k_inv[:, hd:hd + 1] * kw
        c_hi, c_mid, c_lo = _split3(c[:, hd:hd + 1])
        qp = jnp.where(lane == L_C, c_hi, qn)
        qp = jnp.where(lane == L_C + 1, c_mid, qp)
        qp = jnp.where(lane == L_C + 2, c_lo, qp)
        qp = jnp.where((lane >= L_ONE) & (lane < L_ONE + 3), 1.0, qp)
        kp = jnp.where((lane >= L_C) & (lane < L_C + 3), 1.0, kn)
        kp = jnp.where(lane == L_ONE, -c_hi, kp)
        kp = jnp.where(lane == L_ONE + 1, -c_mid, kp)
        kp = jnp.where(lane == L_ONE + 2, -c_lo, kp)
        q_ref[0, hd] = qp.astype(BF16)
        k_ref[0, hd] = kp.astype(BF16)

    v_all = _dot(h, w_ref[:, O_V:O_V + N_HEADS * SLAB])
    for hd in range(N_HEADS):
        v_ref[0, hd] = jnp.where(lane == HEAD_DIM, 1.0,
                                 v_all[:, hd * SLAB:(hd + 1) * SLAB]).astype(BF16)
    u_ref[...] = _dot(h, w_ref[:, O_U:O_U + D_REC])
    g_ref[...] = _dot(h, w_ref[:, O_G:O_G + D_REC])


def _in_proj(x2d, ln1_w, w_all, bf_pad, qw_slab, kw_slab, batch, seq):
    t = x2d.shape[0]
    tiles_per_seq = seq // TM_IN
    slab_shape = jax.ShapeDtypeStruct((batch, N_HEADS, seq, SLAB), BF16)
    slab_spec = pl.BlockSpec((1, N_HEADS, TM_IN, SLAB),
                             lambda i: (i // tiles_per_seq, 0, i % tiles_per_seq, 0))
    row_spec = pl.BlockSpec((TM_IN, D_REC), lambda i: (i, 0))
    const = lambda shape: pl.BlockSpec(shape, lambda i: (0,) * len(shape))
    return pl.pallas_call(
        functools.partial(_in_proj_kernel, tiles_per_seq=tiles_per_seq),
        out_shape=(jax.ShapeDtypeStruct((t, D_REC), F32), jax.ShapeDtypeStruct((t, D_REC), F32),
                   slab_shape, slab_shape, slab_shape),
        grid=(t // TM_IN,),
        in_specs=[pl.BlockSpec((TM_IN, D_MODEL), lambda i: (i, 0)),
                  const((1, D_MODEL)), const((D_MODEL, N_PROJ)), const((1, LANES)),
                  const((1, SLAB)), const((1, SLAB))],
        out_specs=(row_spec, row_spec, slab_spec, slab_spec, slab_spec),
        scratch_shapes=[pltpu.VMEM((1, LANES), F32)],
        compiler_params=pltpu.CompilerParams(dimension_semantics=("arbitrary",),
                                             vmem_limit_bytes=VMEM_LIMIT),
        name="in_proj",
    )(x2d, ln1_w, w_all, bf_pad, qw_slab, kw_slab)


def _rglru_kernel(u_ref, g_ref, cw_ref, cb_ref, wa_ref, ba_ref, wi_ref, bi_ref, lam_ref, nw_ref,
                  o_ref, ext_ref, a_ref, b_ref, h_ref, hcar_ref):
    s = pl.program_id(1)
    ts = u_ref.shape[1]
    pad = 8

    @pl.when(s == 0)
    def _():
        ext_ref[0:pad, :] = jnp.zeros((pad, D_REC), F32)
        hcar_ref[...] = jnp.zeros_like(hcar_ref)

    @pl.when(s != 0)
    def _():
        ext_ref[0:pad, :] = ext_ref[ts:ts + pad, :]

    u = u_ref[0]
    ext_ref[pad:pad + ts, :] = u
    xc = cb_ref[...] + u * cw_ref[CONV_WIDTH - 1:CONV_WIDTH, :]
    for tap in range(CONV_WIDTH - 1):
        d = CONV_WIDTH - 1 - tap
        xc = xc + ext_ref[pad - d:pad - d + ts, :] * cw_ref[tap:tap + 1, :]

    xb = xc.astype(BF16)
    r = jax.nn.sigmoid(_dot(xb, wa_ref[...]) + ba_ref[...])
    ig = jax.nn.sigmoid(_dot(xb, wi_ref[...]) + bi_ref[...])
    nlam = -lam_ref[...]
    softplus = jnp.maximum(nlam, 0.0) + jnp.log1p(jnp.exp(-jnp.abs(nlam)))
    log_a = -RGLRU_C * r * softplus
    a = jnp.exp(log_a)
    a_ref[...] = a
    b_ref[...] = jnp.sqrt(1.0 - jnp.exp(2.0 * log_a)) * (ig * xc)

    def step(t, h):
        h = a_ref[pl.ds(t, 1), :] * h + b_ref[pl.ds(t, 1), :]
        h_ref[pl.ds(t, 1), :] = h
        return h

    hcar_ref[...] = lax.fori_loop(0, ts, step, hcar_ref[...], unroll=8)

    rec = h_ref[...] * jax.nn.gelu(g_ref[0])
    ms = jnp.mean(rec * rec, axis=-1, keepdims=True)
    o_ref[0] = (rec * lax.rsqrt(ms + EPS) * nw_ref[...]).astype(BF16)


def _rglru(u, g, conv_w, conv_b, wa_bd, b_a, wi_bd, b_i, lam, norm_w):
    batch, seq, _ = u.shape
    seq_spec = pl.BlockSpec((1, TS_REC, D_REC), lambda b, s: (b, s, 0))
    const = lambda shape: pl.BlockSpec(shape, lambda b, s: (0,) * len(shape))
    return pl.pallas_call(
        _rglru_kernel,
        out_shape=jax.ShapeDtypeStruct((batch, seq, D_REC), BF16),
        grid=(batch, seq // TS_REC),
        in_specs=[seq_spec, seq_spec, const((CONV_WIDTH, D_REC)), const((1, D_REC)),
                  const((D_REC, D_REC)), const((1, D_REC)), const((D_REC, D_REC)), const((1, D_REC)),
                  const((1, D_REC)), const((1, D_REC))],
        out_specs=seq_spec,
        scratch_shapes=[pltpu.VMEM((TS_REC + 8, D_REC), F32), pltpu.VMEM((TS_REC, D_REC), F32),
                        pltpu.VMEM((TS_REC, D_REC), F32), pltpu.VMEM((TS_REC, D_REC), F32),
                        pltpu.VMEM((1, D_REC), F32)],
        compiler_params=pltpu.CompilerParams(dimension_semantics=("arbitrary", "arbitrary"),
                                             vmem_limit_bytes=VMEM_LIMIT),
        name="rglru",
    )(u, g, conv_w, conv_b, wa_bd, b_a, wi_bd, b_i, lam, norm_w)


def _attn_kernel(q_ref, k_ref, v_ref, o_ref):
    qi = pl.program_id(2)
    hb = q_ref.shape[1]
    tq = q_ref.shape[2]
    qs = [q_ref[0, h] for h in range(hb)]

    def step(j, carry, masked):
        start = pl.multiple_of(j * TK, TK)
        scores = []
        for h in range(hb):
            kj = k_ref[0, h, pl.ds(start, TK), :]
            s = lax.dot_general(qs[h], kj, (((1,), (1,)), ((), ())), preferred_element_type=F32)
            if masked:
                row = lax.broadcasted_iota(jnp.int32, (tq, TK), 0)
                col = lax.broadcasted_iota(jnp.int32, (tq, TK), 1)
                s = jnp.where(row >= col, s, -jnp.inf)
            scores.append(s)
        m_news = [jnp.maximum(carry[h][0], jnp.max(scores[h], axis=-1, keepdims=True))
                  for h in range(hb)]
        probs = [jnp.exp(scores[h] - m_news[h]).astype(BF16) for h in range(hb)]
        out = []
        for h in range(hb):
            m, acc = carry[h]
            vj = v_ref[0, h, pl.ds(start, TK), :]
            out.append((m_news[h], jnp.exp(m - m_news[h]) * acc + _dot(probs[h], vj)))
        return tuple(out)

    init = tuple((jnp.full((tq, 1), -jnp.inf, F32), jnp.zeros((tq, SLAB), F32)) for _ in range(hb))
    carry = lax.fori_loop(0, qi, lambda j, c: step(j, c, False), init)
    carry = step(qi, carry, True)
    lane = lax.broadcasted_iota(jnp.int32, (tq, SLAB), 1)
    for h in range(hb):
        acc = carry[h][1]
        o_ref[0, :, h * SLAB:(h + 1) * SLAB] = jnp.where(
            lane < HEAD_DIM, acc / acc[:, HEAD_DIM:HEAD_DIM + 1], 0.0)


def _attention(qp, kp, vp):
    batch, _, seq, _ = qp.shape
    hb = HEADS_PER_STEP
    kv_spec = pl.BlockSpec((1, hb, seq, SLAB), lambda b, h, i: (b, h, 0, 0))
    return pl.pallas_call(
        _attn_kernel,
        out_shape=jax.ShapeDtypeStruct((batch, seq, N_HEADS * SLAB), F32),
        grid=(batch, N_HEADS // hb, seq // TQ),
        in_specs=[pl.BlockSpec((1, hb, TQ, SLAB), lambda b, h, i: (b, h, i, 0)), kv_spec, kv_spec],
        out_specs=pl.BlockSpec((1, TQ, hb * SLAB), lambda b, h, i: (b, i, h)),
        compiler_params=pltpu.CompilerParams(
            dimension_semantics=("arbitrary", "arbitrary", "arbitrary"),
            vmem_limit_bytes=VMEM_LIMIT),
        name="attn",
    )(qp, kp, vp)


def _slot_onehot(slots, first, width):
    tm = slots[0].shape[0]
    col = lax.broadcasted_iota(jnp.int32, (tm, width), 1) + first
    hit = jnp.zeros((tm, width), F32)
    for slot in slots:
        hit = jnp.where(col == slot, 1.0, hit)
    return hit.astype(BF16)


def _out_proj_kernel(x_ref, rec_ref, att_ref, aw_ref, wr_ref, wa_ref, ln_ref, rwh_ref, rwl_ref, rb_ref,
                     x1_ref, xs_ref, meta_ref, units_ref):
    tm = x_ref.shape[0]

    att = att_ref[...]
    ms = jnp.sum(att * att, axis=-1, keepdims=True) * (1.0 / D_ATT)
    att_n = (att * lax.rsqrt(ms + EPS) * aw_ref[...]).astype(BF16)
    x1 = x_ref[...] + _dot(rec_ref[...], wr_ref[...]) + _dot(att_n, wa_ref[...])
    x1_ref[...] = x1

    ms2 = jnp.mean(x1 * x1, axis=-1, keepdims=True)
    h2 = x1 * lax.rsqrt(ms2 + EPS) * ln_ref[...]

    h_hi = h2.astype(BF16)
    h_lo = (h2 - h_hi.astype(F32)).astype(BF16)
    logits = (_dot(h_hi, rwh_ref[...]) + _dot(h_lo, rwh_ref[...]) + _dot(h_hi, rwl_ref[...])
              + rb_ref[...])
    lane = lax.broadcasted_iota(jnp.int32, (tm, LANES), 1)
    lane_f = lane.astype(F32)
    l = jnp.where(lane < N_EXPERTS, logits, -jnp.inf)

    vals, idxs, sels = [], [], []
    for _ in range(TOP_K):
        m = jnp.max(l, axis=-1, keepdims=True)
        idx = jnp.min(jnp.where(l == m, lane_f, float(LANES)), axis=-1, keepdims=True)
        sel = lane_f == idx
        vals.append(m)
        idxs.append(idx)
        sels.append(sel)
        l = jnp.where(sel, -jnp.inf, l)
    exps = [jnp.exp(v - vals[0]) for v in vals]
    denom = exps[0] + exps[1] + exps[2] + exps[3]

    any_sel = sels[0] | sels[1] | sels[2] | sels[3]
    onehot = jnp.where(any_sel, 1.0, 0.0)
    row = lax.broadcasted_iota(jnp.int32, (tm, tm), 0)
    col = lax.broadcasted_iota(jnp.int32, (tm, tm), 1)
    before = (row > col).astype(BF16)
    rank = _dot(before, onehot.astype(BF16))
    count = jnp.sum(onehot, axis=0, keepdims=True)
    units = jnp.floor((count + (UNIT - 1)) * (1.0 / UNIT))
    e_row = lax.broadcasted_iota(jnp.int32, (LANES, LANES), 0)
    e_col = lax.broadcasted_iota(jnp.int32, (LANES, LANES), 1)
    earlier = (e_row < e_col).astype(BF16)
    units8 = jnp.broadcast_to(units, (8, LANES))
    first_unit = _dot(units8.astype(BF16), earlier)[0:1, :]
    units_ref[0] = units8
    place = first_unit * float(UNIT) + rank

    meta = jnp.zeros((tm, LANES), F32)
    slots = []
    for k in range(TOP_K):
        slot = jnp.sum(jnp.where(sels[k], place, 0.0), axis=-1, keepdims=True)
        slots.append(slot.astype(jnp.int32))
        meta = jnp.where(lane == k, idxs[k], meta)
        meta = jnp.where(lane == TOP_K + k, exps[k] / denom, meta)
        meta = jnp.where(lane == 2 * TOP_K + k, slot, meta)
    meta_ref[...] = meta

    for c in range(TILE_SLOTS // SLOT_CHUNK):
        sel_t = _slot_onehot(slots, c * SLOT_CHUNK, SLOT_CHUNK)
        xs_ref[c * SLOT_CHUNK:(c + 1) * SLOT_CHUNK, :] = _dot_t(sel_t, h_hi).astype(BF16)


def _out_proj(x2d, rec_n, att, aw_slab, w_rec, w_att, ln2_w, rw_hi, rw_lo, rb_pad):
    t = x2d.shape[0]
    n_tiles = t // TM_OUT
    rows = lambda n: pl.BlockSpec((TM_OUT, n), lambda i: (i, 0))
    const = lambda shape: pl.BlockSpec(shape, lambda i: (0,) * len(shape))
    return pl.pallas_call(
        _out_proj_kernel,
        out_shape=(jax.ShapeDtypeStruct((t, D_MODEL), F32),
                   jax.ShapeDtypeStruct((n_tiles * TILE_SLOTS, D_MODEL), BF16),
                   jax.ShapeDtypeStruct((t, LANES), F32),
                   jax.ShapeDtypeStruct((n_tiles, 8, LANES), F32)),
        grid=(n_tiles,),
        in_specs=[rows(D_MODEL), rows(D_REC), rows(N_HEADS * SLAB), const((1, N_HEADS * SLAB)),
                  const((D_REC, D_MODEL)), const((N_HEADS * SLAB, D_MODEL)), const((1, D_MODEL)),
                  const((D_MODEL, LANES)), const((D_MODEL, LANES)), const((1, LANES))],
        out_specs=(rows(D_MODEL), pl.BlockSpec((TILE_SLOTS, D_MODEL), lambda i: (i, 0)), rows(LANES),
                   pl.BlockSpec((1, 8, LANES), lambda i: (i, 0, 0))),
        compiler_params=pltpu.CompilerParams(dimension_semantics=("arbitrary",),
                                             vmem_limit_bytes=VMEM_LIMIT),
        name="out_proj",
    )(x2d, rec_n, att, aw_slab, w_rec, w_att, ln2_w, rw_hi, rw_lo, rb_pad)


def _start_units(unit_ref, base, n_units, src_hbm, dst, sem):
    def body(j, carry):
        src_row = pl.multiple_of(unit_ref[base + j] * UNIT, UNIT)
        dst_row = pl.multiple_of(j * UNIT, UNIT)
        pltpu.make_async_copy(src_hbm.at[pl.ds(src_row, UNIT), :],
                              dst.at[pl.ds(dst_row, UNIT), :], sem).start()
        return carry
    lax.fori_loop(0, n_units, body, 0, unroll=8)


def _wait_units(n_units, src_hbm, dst, sem):
    pltpu.make_async_copy(src_hbm.at[pl.ds(0, n_units * UNIT), :], dst, sem).wait()


def _experts_kernel(be_ref, nblk_ref, unit_ref, order_ref, next_ref, xs_hbm, wgu_hbm, bgu_ref, wd_hbm,
                    bd_ref, ys_ref, xbuf, wgu_f32, wd_f32, wgu_bf, wd_bf, sem, wsem):
    b = pl.program_id(0)
    nblk = nblk_ref[0]
    slot = b % 2
    e = be_ref[b]

    def weight_copies(expert, wslot):
        return (pltpu.make_async_copy(wgu_hbm.at[expert], wgu_f32.at[wslot], wsem.at[wslot]),
                pltpu.make_async_copy(wd_hbm.at[expert], wd_f32.at[wslot], wsem.at[wslot]))

    @pl.when(b == 0)
    def _():
        _start_units(unit_ref, 0, BLOCK_UNITS, xs_hbm, xbuf.at[0], sem.at[0])
        for copy in weight_copies(e, 0):
            copy.start()

    @pl.when(b + 1 < nblk)
    def _():
        _start_units(unit_ref, (b + 1) * BLOCK_UNITS, BLOCK_UNITS, xs_hbm, xbuf.at[1 - slot],
                     sem.at[1 - slot])

    new_expert = jnp.logical_or(b == 0, e != be_ref[jnp.maximum(b - 1, 0)])

    @pl.when(jnp.logical_and(b < nblk, new_expert))
    def _():
        wslot = order_ref[e] % 2
        for copy in weight_copies(e, wslot):
            copy.wait()
        wgu_bf[...] = wgu_f32[wslot].astype(BF16)
        wd_bf[...] = wd_f32[wslot].astype(BF16)
        nxt = next_ref[e]

        @pl.when(nxt >= 0)
        def _():
            for copy in weight_copies(nxt, 1 - wslot):
                copy.start()

    @pl.when(b < nblk)
    def _():
        _wait_units(BLOCK_UNITS, xs_hbm, xbuf.at[slot], sem.at[slot])
        gu = _dot(xbuf[slot], wgu_bf[...]) + bgu_ref[0]
        g = jnp.minimum(gu[:, :D_EXPERT], SWIGLU_LIMIT)
        u = jnp.clip(gu[:, D_EXPERT:], -SWIGLU_LIMIT, SWIGLU_LIMIT)
        act = (u + 1.0) * (g * jax.nn.sigmoid(SWIGLU_ALPHA * g))
        ys_ref[...] = (_dot(act.astype(BF16), wd_bf[...]) + bd_ref[0]).astype(BF16)

    @pl.when(b >= nblk)
    def _():
        ys_ref[...] = jnp.zeros_like(ys_ref)


def _experts(block_expert, n_blocks_used, unit_src, expert_order, expert_next, xs, w_gate_up,
             b_gate_up, w_down, b_down):
    n_blocks = block_expert.shape[0]
    bias_map = lambda b, be, nb, us, eo, en: (be[b], 0, 0)
    grid_spec = pltpu.PrefetchScalarGridSpec(
        num_scalar_prefetch=5,
        grid=(n_blocks,),
        in_specs=[pl.BlockSpec(memory_space=pl.ANY),
                  pl.BlockSpec(memory_space=pl.ANY),
                  pl.BlockSpec((1, 1, 2 * D_EXPERT), bias_map),
                  pl.BlockSpec(memory_space=pl.ANY),
                  pl.BlockSpec((1, 1, D_MODEL), bias_map)],
        out_specs=pl.BlockSpec((TM_E, D_MODEL), lambda b, be, nb, us, eo, en: (b, 0)),
        scratch_shapes=[pltpu.VMEM((2, TM_E, D_MODEL), BF16),
                        pltpu.VMEM((2, D_MODEL, 2 * D_EXPERT), F32),
                        pltpu.VMEM((2, D_EXPERT, D_MODEL), F32),
                        pltpu.VMEM((D_MODEL, 2 * D_EXPERT), BF16),
                        pltpu.VMEM((D_EXPERT, D_MODEL), BF16),
                        pltpu.SemaphoreType.DMA((2,)),
                        pltpu.SemaphoreType.DMA((2,))],
    )
    return pl.pallas_call(
        _experts_kernel,
        out_shape=jax.ShapeDtypeStruct((n_blocks * TM_E, D_MODEL), BF16),
        grid_spec=grid_spec,
        compiler_params=pltpu.CompilerParams(dimension_semantics=("arbitrary",),
                                             vmem_limit_bytes=VMEM_LIMIT),
        name="experts",
    )(block_expert, n_blocks_used, unit_src, expert_order, expert_next, xs, w_gate_up,
      b_gate_up.reshape(N_EXPERTS, 1, 2 * D_EXPERT), w_down, b_down.reshape(N_EXPERTS, 1, D_MODEL))


def _combine_kernel(unit_ref, x1_ref, meta_ref, ys_hbm, o_ref, buf, sem):
    i = pl.program_id(0)
    n = pl.num_programs(0)
    slot = i % 2
    tm = x1_ref.shape[0]

    @pl.when(i == 0)
    def _():
        _start_units(unit_ref, 0, TILE_UNITS, ys_hbm, buf.at[0], sem.at[0])

    @pl.when(i + 1 < n)
    def _():
        _start_units(unit_ref, (i + 1) * TILE_UNITS, TILE_UNITS, ys_hbm, buf.at[1 - slot],
                     sem.at[1 - slot])

    meta = meta_ref[...]
    slots = [meta[:, 2 * TOP_K + k:2 * TOP_K + k + 1].astype(jnp.int32) for k in range(TOP_K)]
    gates = [meta[:, TOP_K + k:TOP_K + k + 1] for k in range(TOP_K)]
    _wait_units(TILE_UNITS, ys_hbm, buf.at[slot], sem.at[slot])

    acc = x1_ref[...]
    for c in range(TILE_SLOTS // SLOT_CHUNK):
        first = c * SLOT_CHUNK
        col = lax.broadcasted_iota(jnp.int32, (tm, SLOT_CHUNK), 1) + first
        weight = jnp.zeros((tm, SLOT_CHUNK), F32)
        for k in range(TOP_K):
            weight = jnp.where(col == slots[k], gates[k], weight)
        acc = acc + _dot(weight.astype(BF16), buf[slot, first:first + SLOT_CHUNK, :])
    o_ref[...] = acc


def _combine(tile_units, x1, meta, ys):
    t = x1.shape[0]
    grid_spec = pltpu.PrefetchScalarGridSpec(
        num_scalar_prefetch=1,
        grid=(t // TM_OUT,),
        in_specs=[pl.BlockSpec((TM_OUT, D_MODEL), lambda i, us: (i, 0)),
                  pl.BlockSpec((TM_OUT, LANES), lambda i, us: (i, 0)),
                  pl.BlockSpec(memory_space=pl.ANY)],
        out_specs=pl.BlockSpec((TM_OUT, D_MODEL), lambda i, us: (i, 0)),
        scratch_shapes=[pltpu.VMEM((2, TILE_SLOTS, D_MODEL), BF16),
                        pltpu.SemaphoreType.DMA((2,))],
    )
    return pl.pallas_call(
        _combine_kernel,
        out_shape=jax.ShapeDtypeStruct((t, D_MODEL), F32),
        grid_spec=grid_spec,
        compiler_params=pltpu.CompilerParams(dimension_semantics=("arbitrary",),
                                             vmem_limit_bytes=VMEM_LIMIT),
        name="combine",
    )(tile_units, x1, meta, ys)


def _head_slabs(w):
    d = w.shape[0]
    w = w.reshape(d, N_HEADS, HEAD_DIM)
    return jnp.pad(w, ((0, 0), (0, 0), (0, SLAB - HEAD_DIM))).reshape(d, N_HEADS * SLAB)


def _block_diag(w):
    g = w.shape[0]
    eye = jnp.eye(g, dtype=w.dtype)
    return jnp.einsum('gij,gh->gihj', w, eye).reshape(g * REC_BLOCK, g * REC_BLOCK)


def _pad_lanes(v, n=LANES):
    return jnp.pad(v, (0, n - v.shape[0])).reshape(1, n)


def _excl_cumsum(a, axis):
    return jnp.cumsum(a, axis=axis) - a


def _pick(onehot, table):
    if table.ndim == 1:
        return jnp.sum(onehot * table[None, :], axis=1)
    return jnp.sum(onehot[:, :, None] * table[None, :, :], axis=1)


def _dispatch_tables(units, n_blocks):
    n_tiles = units.shape[0]
    i32 = jnp.int32
    tile_first = _excl_cumsum(units, 1)
    tile_end = tile_first + units
    run_first = _excl_cumsum(units, 0)
    run_end = run_first + units
    region_units = jnp.sum(units, axis=0)
    region_blocks = (region_units + BLOCK_UNITS - 1) // BLOCK_UNITS
    block_end = jnp.cumsum(region_blocks)
    region_first = (block_end - region_blocks) * BLOCK_UNITS
    n_used = block_end[-1:]
    experts = jnp.arange(N_EXPERTS, dtype=i32)
    tiles = jnp.arange(n_tiles, dtype=i32)

    blocks = jnp.arange(n_blocks, dtype=i32)
    block_expert = jnp.minimum(
        jnp.sum((block_end[None, :] <= blocks[:, None]).astype(i32), axis=1), N_EXPERTS - 1)

    g = jnp.arange(n_blocks * BLOCK_UNITS, dtype=i32)
    e = jnp.minimum(jnp.sum((block_end[None, :] * BLOCK_UNITS <= g[:, None]).astype(i32), axis=1),
                    N_EXPERTS - 1)
    e_hot = (e[:, None] == experts[None, :]).astype(i32)
    local = g - _pick(e_hot, region_first)
    valid = (local < _pick(e_hot, region_units)) & (g < n_used[0] * BLOCK_UNITS)
    ends = _pick(e_hot, run_end.T)
    tile = jnp.minimum(jnp.sum((ends <= local[:, None]).astype(i32), axis=1), n_tiles - 1)
    t_hot = (tile[:, None] == tiles[None, :]).astype(i32)
    src = (tile * TILE_UNITS + jnp.sum(_pick(t_hot, tile_first - run_first) * e_hot, axis=1) + local)
    unit_src = jnp.where(valid, src, 0).astype(i32)

    j = jnp.arange(TILE_UNITS, dtype=i32)
    ej = jnp.minimum(jnp.sum((tile_end[:, None, :] <= j[None, :, None]).astype(i32), axis=2),
                     N_EXPERTS - 1)
    ej_hot = (ej[:, :, None] == experts[None, None, :]).astype(i32)
    used = j[None, :] < jnp.sum(units, axis=1)[:, None]
    base = region_first[None, :] + run_first - tile_first
    dst = jnp.sum(ej_hot * base[:, None, :], axis=2) + j[None, :]
    tile_units = jnp.where(used, dst, 0).astype(i32).reshape(-1)
    owns = region_blocks > 0
    expert_order = _excl_cumsum(owns.astype(i32), 0)
    later = (experts[None, :] > experts[:, None]) & owns[None, :]
    expert_next = jnp.min(jnp.where(later, experts[None, :], N_EXPERTS), axis=1)
    expert_next = jnp.where(expert_next < N_EXPERTS, expert_next, -1).astype(i32)
    return (block_expert.astype(i32), n_used.astype(i32), unit_src, tile_units,
            expert_order.astype(i32), expert_next)


def _layer(x, ln1_w, w_in, conv_w, conv_b, w_rg_a, b_rg_a, w_rg_i, b_rg_i, rg_lambda, b_forget,
           q_norm_w, k_norm_w, rec_out_norm_w, attn_out_norm_w, w_out, ln2_w, w_router, b_router,
           w_gate_up, b_gate_up, w_down, b_down):
    batch, seq, _ = x.shape
    t = batch * seq
    x2d = x.reshape(t, D_MODEL)

    o_q = 2 * D_REC
    o_k = o_q + D_ATT
    o_v = o_k + D_ATT
    o_f = o_v + D_ATT
    w_all = jnp.concatenate(
        [w_in[:, :o_q], _head_slabs(w_in[:, o_q:o_k]), _head_slabs(w_in[:, o_k:o_v]),
         _head_slabs(w_in[:, o_v:o_f]), jnp.pad(w_in[:, o_f:], ((0, 0), (0, LANES - N_HEADS)))],
        axis=1).astype(BF16)
    scale = HEAD_DIM ** -0.5
    u, g, qp, kp, vp = _in_proj(x2d, ln1_w.reshape(1, D_MODEL), w_all, _pad_lanes(b_forget),
                                _pad_lanes(q_norm_w * scale), _pad_lanes(k_norm_w), batch, seq)

    rec_n = _rglru(u.reshape(batch, seq, D_REC), g.reshape(batch, seq, D_REC), conv_w,
                   conv_b.reshape(1, D_REC), _block_diag(w_rg_a).astype(BF16), b_rg_a.reshape(1, D_REC),
                   _block_diag(w_rg_i).astype(BF16), b_rg_i.reshape(1, D_REC),
                   rg_lambda.reshape(1, D_REC), rec_out_norm_w.reshape(1, D_REC))

    att = _attention(qp, kp, vp)

    aw_slab = _head_slabs(attn_out_norm_w.reshape(1, D_ATT))
    w_att = jnp.pad(w_out[D_REC:].reshape(N_HEADS, HEAD_DIM, D_MODEL),
                    ((0, 0), (0, SLAB - HEAD_DIM), (0, 0))).reshape(N_HEADS * SLAB, D_MODEL)
    rw = jnp.pad(w_router, ((0, 0), (0, LANES - N_EXPERTS)))
    rw_hi = rw.astype(BF16)
    rw_lo = (rw - rw_hi.astype(F32)).astype(BF16)
    x1, xs, meta, units = _out_proj(
        x2d, rec_n.reshape(t, D_REC), att.reshape(t, N_HEADS * SLAB), aw_slab,
        w_out[:D_REC].astype(BF16), w_att.astype(BF16), ln2_w.reshape(1, D_MODEL),
        rw_hi, rw_lo, _pad_lanes(b_router))

    n_tiles = t // TM_OUT
    max_units = t * TOP_K // UNIT + n_tiles * N_EXPERTS
    n_blocks = -(-max_units // BLOCK_UNITS) + N_EXPERTS
    block_expert, n_used, unit_src, tile_units, expert_order, expert_next = _dispatch_tables(
        units[:, 0, :N_EXPERTS].astype(jnp.int32), n_blocks)

    ys = _experts(block_expert, n_used, unit_src, expert_order, expert_next, xs, w_gate_up, b_gate_up,
                  w_down, b_down)
    out = _combine(tile_units, x1, meta, ys)
    return out.reshape(batch, seq, D_MODEL)


def kernel(x, ln1_w, w_in, conv_w, conv_b, w_rg_a, b_rg_a, w_rg_i, b_rg_i, rg_lambda, b_forget,
           q_norm_w, k_norm_w, rec_out_norm_w, attn_out_norm_w, w_out, ln2_w, w_router, b_router,
           w_gate_up, b_gate_up, w_down, b_down):
    for l in range(ln1_w.shape[0]):
        x = _layer(x, ln1_w[l], w_in[l], conv_w[l], conv_b[l], w_rg_a[l], b_rg_a[l], w_rg_i[l],
                   b_rg_i[l], rg_lambda[l], b_forget[l], q_norm_w[l], k_norm_w[l], rec_out_norm_w[l],
                   attn_out_norm_w[l], w_out[l], ln2_w[l], w_router[l], b_router[l], w_gate_up[l],
                   b_gate_up[l], w_down[l], b_down[l])
    return x
```

```python
import functools

import jax
import jax.numpy as jnp
from jax import lax
from jax.experimental import pallas as pl
from jax.experimental.pallas import tpu as pltpu

D_MODEL = 1024
D_REC = 512
N_REC_BLOCKS = 8
REC_BLOCK = 64
CONV_WIDTH = 4
RGLRU_C = 8.0
D_ATT = 512
HEAD_DIM = 64
N_HEADS = 8
N_EXPERTS = 32
TOP_K = 4
D_EXPERT = 1024
SWIGLU_LIMIT = 7.0
SWIGLU_ALPHA = 1.702
EPS = 1e-6

LANES = 128
SLAB = LANES
VMEM_LIMIT = 56 * 1024 * 1024

F32 = jnp.float32
BF16 = jnp.bfloat16

O_U = 0
O_G = O_U + D_REC
O_Q = O_G + D_REC
O_K = O_Q + N_HEADS * SLAB
O_F = O_K + N_HEADS * SLAB
N_PROJ = O_F + LANES

L_C = HEAD_DIM
L_ONE = HEAD_DIM + 3

TM_IN = 512
TS_REC = 512
TQ = 512
TK = 512
HEADS_PER_STEP = 4
TM_OUT = 512
TM_E = 512

UNIT = 16
TILE_SLOTS = -(-(TM_OUT * TOP_K + N_EXPERTS * (UNIT - 1)) // LANES) * LANES
TILE_UNITS = TILE_SLOTS // UNIT
BLOCK_UNITS = TM_E // UNIT
SLOT_CHUNK = 512


def _split3(v):
    hi = v.astype(BF16).astype(F32)
    r = v - hi
    mid = r.astype(BF16).astype(F32)
    lo = (r - mid).astype(BF16).astype(F32)
    return hi, mid, lo


def _dot(a, b):
    return jnp.dot(a, b, preferred_element_type=F32)


def _dot_t(a, b):
    return lax.dot_general(a, b, (((0,), (0,)), ((), ())), preferred_element_type=F32)


def _in_proj_kernel(x_ref, ln_ref, w_ref, wvt_ref, bf_ref, qw_ref, kw_ref,
                    u_ref, g_ref, q_ref, k_ref, v_ref, carry_ref, *, tiles_per_seq):
    i = pl.program_id(0)
    tm = x_ref.shape[0]

    @pl.when(i % tiles_per_seq == 0)
    def _():
        carry_ref[...] = jnp.zeros_like(carry_ref)

    x = x_ref[...]
    ms = jnp.mean(x * x, axis=-1, keepdims=True)
    h = (x * lax.rsqrt(ms + EPS) * ln_ref[...]).astype(BF16)

    f = _dot(h, w_ref[:, O_F:O_F + LANES]) + bf_ref[...]
    logf = jnp.minimum(f, 0.0) - jnp.log1p(jnp.exp(-jnp.abs(f)))
    row = lax.broadcasted_iota(jnp.int32, (tm, tm), 0)
    col = lax.broadcasted_iota(jnp.int32, (tm, tm), 1)
    tri = (row >= col).astype(BF16)
    hi, mid, lo = _split3(logf)
    c = (_dot(tri, hi.astype(BF16)) + _dot(tri, mid.astype(BF16))
         + _dot(tri, lo.astype(BF16)) + carry_ref[...])
    carry_ref[...] = c[tm - 1:tm, :]

    lane = lax.broadcasted_iota(jnp.int32, (tm, SLAB), 1)
    q_all = _dot(h, w_ref[:, O_Q:O_Q + N_HEADS * SLAB])
    k_all = _dot(h, w_ref[:, O_K:O_K + N_HEADS * SLAB])
    qw = qw_ref[...]
    kw = kw_ref[...]
    g_row = lax.broadcasted_iota(jnp.int32, (N_HEADS * SLAB, LANES), 0)
    g_col = lax.broadcasted_iota(jnp.int32, (N_HEADS * SLAB, LANES), 1)
    group = ((g_row >= g_col * SLAB) & (g_row < (g_col + 1) * SLAB)).astype(BF16)
    q_inv = lax.rsqrt(_dot((q_all * q_all).astype(BF16), group) * (1.0 / HEAD_DIM) + EPS)
    k_inv = lax.rsqrt(_dot((k_all * k_all).astype(BF16), group) * (1.0 / HEAD_DIM) + EPS)
    for hd in range(N_HEADS):
        sl = slice(hd * SLAB, (hd + 1) * SLAB)
        qn = q_all[:, sl] * q_inv[:, hd:hd + 1] * qw
        kn = k_all[:, sl] * k_inv[:, hd:hd + 1] * kw
        c_hi, c_mid, c_lo = _split3(c[:, hd:hd + 1])
        qp = jnp.where(lane == L_C, c_hi, qn)
        qp = jnp.where(lane == L_C + 1, c_mid, qp)
        qp = jnp.where(lane == L_C + 2, c_lo, qp)
        qp = jnp.where((lane >= L_ONE) & (lane < L_ONE + 3), 1.0, qp)
        kp = jnp.where((lane >= L_C) & (lane < L_C + 3), 1.0, kn)
        kp = jnp.where(lane == L_ONE, -c_hi, kp)
        kp = jnp.where(lane == L_ONE + 1, -c_mid, kp)
        kp = jnp.where(lane == L_ONE + 2, -c_lo, kp)
        q_ref[0, hd] = qp.astype(BF16)
        k_ref[0, hd] = kp.astype(BF16)

    vt_all = lax.dot_general(wvt_ref[...], h, (((1,), (1,)), ((), ())), preferred_element_type=F32)
    feat = lax.broadcasted_iota(jnp.int32, (SLAB, tm), 0)
    for hd in range(N_HEADS):
        v_ref[0, hd] = jnp.where(feat == HEAD_DIM, 1.0,
                                 vt_all[hd * SLAB:(hd + 1) * SLAB, :]).astype(BF16)
    u_ref[...] = _dot(h, w_ref[:, O_U:O_U + D_REC])
    g_ref[...] = _dot(h, w_ref[:, O_G:O_G + D_REC])


def _in_proj(x2d, ln1_w, w_all, w_vt, bf_pad, qw_slab, kw_slab, batch, seq):
    t = x2d.shape[0]
    tiles_per_seq = seq // TM_IN
    slab_shape = jax.ShapeDtypeStruct((batch, N_HEADS, seq, SLAB), BF16)
    slab_spec = pl.BlockSpec((1, N_HEADS, TM_IN, SLAB),
                             lambda i: (i // tiles_per_seq, 0, i % tiles_per_seq, 0))
    slab_t_shape = jax.ShapeDtypeStruct((batch, N_HEADS, SLAB, seq), BF16)
    slab_t_spec = pl.BlockSpec((1, N_HEADS, SLAB, TM_IN),
                               lambda i: (i // tiles_per_seq, 0, 0, i % tiles_per_seq))
    row_spec = pl.BlockSpec((TM_IN, D_REC), lambda i: (i, 0))
    const = lambda shape: pl.BlockSpec(shape, lambda i: (0,) * len(shape))
    return pl.pallas_call(
        functools.partial(_in_proj_kernel, tiles_per_seq=tiles_per_seq),
        out_shape=(jax.ShapeDtypeStruct((t, D_REC), F32), jax.ShapeDtypeStruct((t, D_REC), F32),
                   slab_shape, slab_shape, slab_t_shape),
        grid=(t // TM_IN,),
        in_specs=[pl.BlockSpec((TM_IN, D_MODEL), lambda i: (i, 0)),
                  const((1, D_MODEL)), const((D_MODEL, N_PROJ)), const((N_HEADS * SLAB, D_MODEL)),
                  const((1, LANES)), const((1, SLAB)), const((1, SLAB))],
        out_specs=(row_spec, row_spec, slab_spec, slab_spec, slab_t_spec),
        scratch_shapes=[pltpu.VMEM((1, LANES), F32)],
        compiler_params=pltpu.CompilerParams(dimension_semantics=("arbitrary",),
                                             vmem_limit_bytes=VMEM_LIMIT),
        name="in_proj",
    )(x2d, ln1_w, w_all, w_vt, bf_pad, qw_slab, kw_slab)


def _rglru_kernel(u_ref, g_ref, cw_ref, cb_ref, wa_ref, ba_ref, wi_ref, bi_ref, lam_ref, nw_ref,
                  o_ref, ext_ref, a_ref, b_ref, h_ref, hcar_ref):
    s = pl.program_id(1)
    ts = u_ref.shape[1]
    pad = 8

    @pl.when(s == 0)
    def _():
        ext_ref[0:pad, :] = jnp.zeros((pad, D_REC), F32)
        hcar_ref[...] = jnp.zeros_like(hcar_ref)

    @pl.when(s != 0)
    def _():
        ext_ref[0:pad, :] = ext_ref[ts:ts + pad, :]

    u = u_ref[0]
    ext_ref[pad:pad + ts, :] = u
    xc = cb_ref[...] + u * cw_ref[CONV_WIDTH - 1:CONV_WIDTH, :]
    for tap in range(CONV_WIDTH - 1):
        d = CONV_WIDTH - 1 - tap
        xc = xc + ext_ref[pad - d:pad - d + ts, :] * cw_ref[tap:tap + 1, :]

    xb = xc.astype(BF16)
    r = jax.nn.sigmoid(_dot(xb, wa_ref[...]) + ba_ref[...])
    ig = jax.nn.sigmoid(_dot(xb, wi_ref[...]) + bi_ref[...])
    nlam = -lam_ref[...]
    softplus = jnp.maximum(nlam, 0.0) + jnp.log1p(jnp.exp(-jnp.abs(nlam)))
    log_a = -RGLRU_C * r * softplus
    a = jnp.exp(log_a)
    a_ref[...] = a
    b_ref[...] = jnp.sqrt(1.0 - jnp.exp(2.0 * log_a)) * (ig * xc)

    def step(t, h):
        h = a_ref[pl.ds(t, 1), :] * h + b_ref[pl.ds(t, 1), :]
        h_ref[pl.ds(t, 1), :] = h
        return h

    hcar_ref[...] = lax.fori_loop(0, ts, step, hcar_ref[...], unroll=8)

    rec = h_ref[...] * jax.nn.gelu(g_ref[0])
    ms = jnp.mean(rec * rec, axis=-1, keepdims=True)
    o_ref[0] = (rec * lax.rsqrt(ms + EPS) * nw_ref[...]).astype(BF16)


def _rglru(u, g, conv_w, conv_b, wa_bd, b_a, wi_bd, b_i, lam, norm_w):
    batch, seq, _ = u.shape
    seq_spec = pl.BlockSpec((1, TS_REC, D_REC), lambda b, s: (b, s, 0))
    const = lambda shape: pl.BlockSpec(shape, lambda b, s: (0,) * len(shape))
    return pl.pallas_call(
        _rglru_kernel,
        out_shape=jax.ShapeDtypeStruct((batch, seq, D_REC), BF16),
        grid=(batch, seq // TS_REC),
        in_specs=[seq_spec, seq_spec, const((CONV_WIDTH, D_REC)), const((1, D_REC)),
                  const((D_REC, D_REC)), const((1, D_REC)), const((D_REC, D_REC)), const((1, D_REC)),
                  const((1, D_REC)), const((1, D_REC))],
        out_specs=seq_spec,
        scratch_shapes=[pltpu.VMEM((TS_REC + 8, D_REC), F32), pltpu.VMEM((TS_REC, D_REC), F32),
                        pltpu.VMEM((TS_REC, D_REC), F32), pltpu.VMEM((TS_REC, D_REC), F32),
                        pltpu.VMEM((1, D_REC), F32)],
        compiler_params=pltpu.CompilerParams(dimension_semantics=("arbitrary", "arbitrary"),
                                             vmem_limit_bytes=VMEM_LIMIT),
        name="rglru",
    )(u, g, conv_w, conv_b, wa_bd, b_a, wi_bd, b_i, lam, norm_w)


def _attn_kernel(q_ref, k_ref, v_ref, o_ref):
    qi = pl.program_id(2)
    hb = q_ref.shape[1]
    tq = q_ref.shape[2]
    qs = [q_ref[0, h] for h in range(hb)]

    def step(j, carry, masked):
        start = pl.multiple_of(j * TK, TK)
        scores = []
        for h in range(hb):
            kj = k_ref[0, h, pl.ds(start, TK), :]
            s = lax.dot_general(kj, qs[h], (((1,), (1,)), ((), ())), preferred_element_type=F32)
            if masked:
                key = lax.broadcasted_iota(jnp.int32, (TK, tq), 0)
                qry = lax.broadcasted_iota(jnp.int32, (TK, tq), 1)
                s = jnp.where(qry >= key, s, -jnp.inf)
            scores.append(s)
        m_news = [jnp.maximum(carry[h][0], jnp.max(scores[h], axis=0, keepdims=True))
                  for h in range(hb)]
        probs = [jnp.exp(scores[h] - m_news[h]).astype(BF16) for h in range(hb)]
        out = []
        for h in range(hb):
            m, acc = carry[h]
            vj = v_ref[0, h, :, pl.ds(start, TK)]
            out.append((m_news[h], jnp.exp(m - m_news[h]) * acc + _dot(vj, probs[h])))
        return tuple(out)

    init = tuple((jnp.full((1, tq), -jnp.inf, F32), jnp.zeros((SLAB, tq), F32)) for _ in range(hb))
    carry = lax.fori_loop(0, qi, lambda j, c: step(j, c, False), init)
    carry = step(qi, carry, True)
    feat = lax.broadcasted_iota(jnp.int32, (SLAB, tq), 0)
    for h in range(hb):
        acc = carry[h][1]
        out_t = jnp.where(feat < HEAD_DIM, acc / acc[HEAD_DIM:HEAD_DIM + 1, :], 0.0)
        o_ref[0, :, h * SLAB:(h + 1) * SLAB] = out_t.T


def _attention(qp, kp, vp):
    batch, _, seq, _ = qp.shape
    hb = HEADS_PER_STEP
    kv_spec = pl.BlockSpec((1, hb, seq, SLAB), lambda b, h, i: (b, h, 0, 0))
    vt_spec = pl.BlockSpec((1, hb, SLAB, seq), lambda b, h, i: (b, h, 0, 0))
    return pl.pallas_call(
        _attn_kernel,
        out_shape=jax.ShapeDtypeStruct((batch, seq, N_HEADS * SLAB), F32),
        grid=(batch, N_HEADS // hb, seq // TQ),
        in_specs=[pl.BlockSpec((1, hb, TQ, SLAB), lambda b, h, i: (b, h, i, 0)), kv_spec, vt_spec],
        out_specs=pl.BlockSpec((1, TQ, hb * SLAB), lambda b, h, i: (b, i, h)),
        compiler_params=pltpu.CompilerParams(
            dimension_semantics=("arbitrary", "arbitrary", "arbitrary"),
            vmem_limit_bytes=VMEM_LIMIT),
        name="attn",
    )(qp, kp, vp)


def _slot_onehot(slots, first, width):
    tm = slots[0].shape[0]
    col = lax.broadcasted_iota(jnp.int32, (tm, width), 1) + first
    hit = jnp.zeros((tm, width), F32)
    for slot in slots:
        hit = jnp.where(col == slot, 1.0, hit)
    return hit.astype(BF16)


def _out_proj_kernel(x_ref, rec_ref, att_ref, aw_ref, wr_ref, wa_ref, ln_ref, rwh_ref, rwl_ref, rb_ref,
                     x1_ref, xs_ref, meta_ref, units_ref):
    tm = x_ref.shape[0]

    att = att_ref[...]
    ms = jnp.sum(att * att, axis=-1, keepdims=True) * (1.0 / D_ATT)
    att_n = (att * lax.rsqrt(ms + EPS) * aw_ref[...]).astype(BF16)
    x1 = x_ref[...] + _dot(rec_ref[...], wr_ref[...]) + _dot(att_n, wa_ref[...])
    x1_ref[...] = x1

    ms2 = jnp.mean(x1 * x1, axis=-1, keepdims=True)
    h2 = x1 * lax.rsqrt(ms2 + EPS) * ln_ref[...]

    h_hi = h2.astype(BF16)
    h_lo = (h2 - h_hi.astype(F32)).astype(BF16)
    logits = (_dot(h_hi, rwh_ref[...]) + _dot(h_lo, rwh_ref[...]) + _dot(h_hi, rwl_ref[...])
              + rb_ref[...])

    lane = lax.broadcasted_iota(jnp.int32, (tm, LANES), 1)
    lane_f = lane.astype(F32)
    l = jnp.where(lane < N_EXPERTS, logits, -jnp.inf)

    vals, idxs, sels = [], [], []
    for _ in range(TOP_K):
        m = jnp.max(l, axis=-1, keepdims=True)
        idx = jnp.min(jnp.where(l == m, lane_f, float(LANES)), axis=-1, keepdims=True)
        sel = lane_f == idx
        vals.append(m)
        idxs.append(idx)
        sels.append(sel)
        l = jnp.where(sel, -jnp.inf, l)
    exps = [jnp.exp(v - vals[0]) for v in vals]
    denom = exps[0] + exps[1] + exps[2] + exps[3]

    any_sel = sels[0] | sels[1] | sels[2] | sels[3]
    onehot = jnp.where(any_sel, 1.0, 0.0)
    row = lax.broadcasted_iota(jnp.int32, (tm, tm), 0)
    col = lax.broadcasted_iota(jnp.int32, (tm, tm), 1)
    before = (row > col).astype(BF16)
    rank = _dot(before, onehot.astype(BF16))
    count = jnp.sum(onehot, axis=0, keepdims=True)
    units = jnp.floor((count + (UNIT - 1)) * (1.0 / UNIT))
    e_row = lax.broadcasted_iota(jnp.int32, (LANES, LANES), 0)
    e_col = lax.broadcasted_iota(jnp.int32, (LANES, LANES), 1)
    earlier = (e_row < e_col).astype(BF16)
    units8 = jnp.broadcast_to(units, (8, LANES))
    first_unit = _dot(units8.astype(BF16), earlier)[0:1, :]
    units_ref[0] = units8
    place = first_unit * float(UNIT) + rank

    meta = jnp.zeros((tm, LANES), F32)
    slots = []
    for k in range(TOP_K):
        slot = jnp.sum(jnp.where(sels[k], place, 0.0), axis=-1, keepdims=True)
        slots.append(slot.astype(jnp.int32))
        meta = jnp.where(lane == k, idxs[k], meta)
        meta = jnp.where(lane == TOP_K + k, exps[k] / denom, meta)
        meta = jnp.where(lane == 2 * TOP_K + k, slot, meta)
    meta_ref[...] = meta

    for c in range(TILE_SLOTS // SLOT_CHUNK):
        sel_t = _slot_onehot(slots, c * SLOT_CHUNK, SLOT_CHUNK)
        xs_ref[c * SLOT_CHUNK:(c + 1) * SLOT_CHUNK, :] = _dot_t(sel_t, h_hi).astype(BF16)


def _out_proj(x2d, rec_n, att, aw_slab, w_rec, w_att, ln2_w, rw_hi, rw_lo, rb_pad):
    t = x2d.shape[0]
    n_tiles = t // TM_OUT
    rows = lambda n: pl.BlockSpec((TM_OUT, n), lambda i: (i, 0))
    const = lambda shape: pl.BlockSpec(shape, lambda i: (0,) * len(shape))
    return pl.pallas_call(
        _out_proj_kernel,
        out_shape=(jax.ShapeDtypeStruct((t, D_MODEL), F32),
                   jax.ShapeDtypeStruct((n_tiles * TILE_SLOTS, D_MODEL), BF16),
                   jax.ShapeDtypeStruct((t, LANES), F32),
                   jax.ShapeDtypeStruct((n_tiles, 8, LANES), F32)),
        grid=(n_tiles,),
        in_specs=[rows(D_MODEL), rows(D_REC), rows(N_HEADS * SLAB), const((1, N_HEADS * SLAB)),
                  const((D_REC, D_MODEL)), const((N_HEADS * SLAB, D_MODEL)), const((1, D_MODEL)),
                  const((D_MODEL, LANES)), const((D_MODEL, LANES)), const((1, LANES))],
        out_specs=(rows(D_MODEL), pl.BlockSpec((TILE_SLOTS, D_MODEL), lambda i: (i, 0)), rows(LANES),
                   pl.BlockSpec((1, 8, LANES), lambda i: (i, 0, 0))),
        compiler_params=pltpu.CompilerParams(dimension_semantics=("arbitrary",),
                                             vmem_limit_bytes=VMEM_LIMIT),
        name="out_proj",
    )(x2d, rec_n, att, aw_slab, w_rec, w_att, ln2_w, rw_hi, rw_lo, rb_pad)


def _start_units(unit_ref, base, n_units, src_hbm, dst, sem):
    def body(j, carry):
        src_row = pl.multiple_of(unit_ref[base + j] * UNIT, UNIT)
        dst_row = pl.multiple_of(j * UNIT, UNIT)
        pltpu.make_async_copy(src_hbm.at[pl.ds(src_row, UNIT), :],
                              dst.at[pl.ds(dst_row, UNIT), :], sem).start()
        return carry
    lax.fori_loop(0, n_units, body, 0, unroll=8)


def _wait_units(n_units, src_hbm, dst, sem):
    pltpu.make_async_copy(src_hbm.at[pl.ds(0, n_units * UNIT), :], dst, sem).wait()


def _experts_kernel(be_ref, nblk_ref, unit_ref, order_ref, next_ref, xs_hbm, wgu_hbm, bgu_ref, wd_hbm,
                    bd_ref, ys_ref, xbuf, wgu_f32, wd_f32, wgu_bf, wd_bf, sem, wsem):
    b = pl.program_id(0)
    nblk = nblk_ref[0]
    slot = b % 2
    e = be_ref[b]

    def weight_copies(expert, wslot):
        return (pltpu.make_async_copy(wgu_hbm.at[expert], wgu_f32.at[wslot], wsem.at[wslot]),
                pltpu.make_async_copy(wd_hbm.at[expert], wd_f32.at[wslot], wsem.at[wslot]))

    @pl.when(b == 0)
    def _():
        _start_units(unit_ref, 0, BLOCK_UNITS, xs_hbm, xbuf.at[0], sem.at[0])
        for copy in weight_copies(e, 0):
            copy.start()

    @pl.when(b + 1 < nblk)
    def _():
        _start_units(unit_ref, (b + 1) * BLOCK_UNITS, BLOCK_UNITS, xs_hbm, xbuf.at[1 - slot],
                     sem.at[1 - slot])

    new_expert = jnp.logical_or(b == 0, e != be_ref[jnp.maximum(b - 1, 0)])

    @pl.when(jnp.logical_and(b < nblk, new_expert))
    def _():
        wslot = order_ref[e] % 2
        for copy in weight_copies(e, wslot):
            copy.wait()
        wgu_bf[...] = wgu_f32[wslot].astype(BF16)
        wd_bf[...] = wd_f32[wslot].astype(BF16)
        nxt = next_ref[e]

        @pl.when(nxt >= 0)
        def _():
            for copy in weight_copies(nxt, 1 - wslot):
                copy.start()

    @pl.when(b < nblk)
    def _():
        _wait_units(BLOCK_UNITS, xs_hbm, xbuf.at[slot], sem.at[slot])
        gu = _dot(xbuf[slot], wgu_bf[...]) + bgu_ref[0]
        g = jnp.minimum(gu[:, :D_EXPERT], SWIGLU_LIMIT)
        u = jnp.clip(gu[:, D_EXPERT:], -SWIGLU_LIMIT, SWIGLU_LIMIT)
        act = (u + 1.0) * (g * jax.nn.sigmoid(SWIGLU_ALPHA * g))
        ys_ref[...] = (_dot(act.astype(BF16), wd_bf[...]) + bd_ref[0]).astype(BF16)

    @pl.when(b >= nblk)
    def _():
        ys_ref[...] = jnp.zeros_like(ys_ref)


def _experts(block_expert, n_blocks_used, unit_src, expert_order, expert_next, xs, w_gate_up,
             b_gate_up, w_down, b_down):
    n_blocks = block_expert.shape[0]
    bias_map = lambda b, be, nb, us, eo, en: (be[b], 0, 0)
    grid_spec = pltpu.PrefetchScalarGridSpec(
        num_scalar_prefetch=5,
        grid=(n_blocks,),
        in_specs=[pl.BlockSpec(memory_space=pl.ANY),
                  pl.BlockSpec(memory_space=pl.ANY),
                  pl.BlockSpec((1, 1, 2 * D_EXPERT), bias_map),
                  pl.BlockSpec(memory_space=pl.ANY),
                  pl.BlockSpec((1, 1, D_MODEL), bias_map)],
        out_specs=pl.BlockSpec((TM_E, D_MODEL), lambda b, be, nb, us, eo, en: (b, 0)),
        scratch_shapes=[pltpu.VMEM((2, TM_E, D_MODEL), BF16),
                        pltpu.VMEM((2, D_MODEL, 2 * D_EXPERT), F32),
                        pltpu.VMEM((2, D_EXPERT, D_MODEL), F32),
                        pltpu.VMEM((D_MODEL, 2 * D_EXPERT), BF16),
                        pltpu.VMEM((D_EXPERT, D_MODEL), BF16),
                        pltpu.SemaphoreType.DMA((2,)),
                        pltpu.SemaphoreType.DMA((2,))],
    )
    return pl.pallas_call(
        _experts_kernel,
        out_shape=jax.ShapeDtypeStruct((n_blocks * TM_E, D_MODEL), BF16),
        grid_spec=grid_spec,
        compiler_params=pltpu.CompilerParams(dimension_semantics=("arbitrary",),
                                             vmem_limit_bytes=VMEM_LIMIT),
        name="experts",
    )(block_expert, n_blocks_used, unit_src, expert_order, expert_next, xs, w_gate_up,
      b_gate_up.reshape(N_EXPERTS, 1, 2 * D_EXPERT), w_down, b_down.reshape(N_EXPERTS, 1, D_MODEL))


def _combine_kernel(unit_ref, x1_ref, meta_ref, ys_hbm, o_ref, buf, sem):
    i = pl.program_id(0)
    n = pl.num_programs(0)
    slot = i % 2
    tm = x1_ref.shape[0]

    @pl.when(i == 0)
    def _():
        _start_units(unit_ref, 0, TILE_UNITS, ys_hbm, buf.at[0], sem.at[0])

    @pl.when(i + 1 < n)
    def _():
        _start_units(unit_ref, (i + 1) * TILE_UNITS, TILE_UNITS, ys_hbm, buf.at[1 - slot],
                     sem.at[1 - slot])

    meta = meta_ref[...]
    slots = [meta[:, 2 * TOP_K + k:2 * TOP_K + k + 1].astype(jnp.int32) for k in range(TOP_K)]
    gates = [meta[:, TOP_K + k:TOP_K + k + 1] for k in range(TOP_K)]
    _wait_units(TILE_UNITS, ys_hbm, buf.at[slot], sem.at[slot])

    acc = x1_ref[...]
    for c in range(TILE_SLOTS // SLOT_CHUNK):
        first = c * SLOT_CHUNK
        col = lax.broadcasted_iota(jnp.int32, (tm, SLOT_CHUNK), 1) + first
        weight = jnp.zeros((tm, SLOT_CHUNK), F32)
        for k in range(TOP_K):
            weight = jnp.where(col == slots[k], gates[k], weight)
        acc = acc + _dot(weight.astype(BF16), buf[slot, first:first + SLOT_CHUNK, :])
    o_ref[...] = acc


def _combine(tile_units, x1, meta, ys):
    t = x1.shape[0]
    grid_spec = pltpu.PrefetchScalarGridSpec(
        num_scalar_prefetch=1,
        grid=(t // TM_OUT,),
        in_specs=[pl.BlockSpec((TM_OUT, D_MODEL), lambda i, us: (i, 0)),
                  pl.BlockSpec((TM_OUT, LANES), lambda i, us: (i, 0)),
                  pl.BlockSpec(memory_space=pl.ANY)],
        out_specs=pl.BlockSpec((TM_OUT, D_MODEL), lambda i, us: (i, 0)),
        scratch_shapes=[pltpu.VMEM((2, TILE_SLOTS, D_MODEL), BF16),
                        pltpu.SemaphoreType.DMA((2,))],
    )
    return pl.pallas_call(
        _combine_kernel,
        out_shape=jax.ShapeDtypeStruct((t, D_MODEL), F32),
        grid_spec=grid_spec,
        compiler_params=pltpu.CompilerParams(dimension_semantics=("arbitrary",),
                                             vmem_limit_bytes=VMEM_LIMIT),
        name="combine",
    )(tile_units, x1, meta, ys)


def _head_slabs(w):
    d = w.shape[0]
    w = w.reshape(d, N_HEADS, HEAD_DIM)
    return jnp.pad(w, ((0, 0), (0, 0), (0, SLAB - HEAD_DIM))).reshape(d, N_HEADS * SLAB)


def _block_diag(w):
    g = w.shape[0]
    eye = jnp.eye(g, dtype=w.dtype)
    return jnp.einsum('gij,gh->gihj', w, eye).reshape(g * REC_BLOCK, g * REC_BLOCK)


def _pad_lanes(v, n=LANES):
    return jnp.pad(v, (0, n - v.shape[0])).reshape(1, n)


def _excl_cumsum(a, axis):
    return jnp.cumsum(a, axis=axis) - a


def _pick(onehot, table):
    if table.ndim == 1:
        return jnp.sum(onehot * table[None, :], axis=1)
    return jnp.sum(onehot[:, :, None] * table[None, :, :], axis=1)


def _dispatch_tables(units, n_blocks):
    n_tiles = units.shape[0]
    i32 = jnp.int32
    tile_first = _excl_cumsum(units, 1)
    tile_end = tile_first + units
    run_first = _excl_cumsum(units, 0)
    run_end = run_first + units
    region_units = jnp.sum(units, axis=0)
    region_blocks = (region_units + BLOCK_UNITS - 1) // BLOCK_UNITS
    block_end = jnp.cumsum(region_blocks)
    region_first = (block_end - region_blocks) * BLOCK_UNITS
    n_used = block_end[-1:]
    experts = jnp.arange(N_EXPERTS, dtype=i32)
    tiles = jnp.arange(n_tiles, dtype=i32)

    blocks = jnp.arange(n_blocks, dtype=i32)
    block_expert = jnp.minimum(
        jnp.sum((block_end[None, :] <= blocks[:, None]).astype(i32), axis=1), N_EXPERTS - 1)

    g = jnp.arange(n_blocks * BLOCK_UNITS, dtype=i32)
    e = jnp.minimum(jnp.sum((block_end[None, :] * BLOCK_UNITS <= g[:, None]).astype(i32), axis=1),
                    N_EXPERTS - 1)
    e_hot = (e[:, None] == experts[None, :]).astype(i32)
    local = g - _pick(e_hot, region_first)
    valid = (local < _pick(e_hot, region_units)) & (g < n_used[0] * BLOCK_UNITS)
    ends = _pick(e_hot, run_end.T)
    tile = jnp.minimum(jnp.sum((ends <= local[:, None]).astype(i32), axis=1), n_tiles - 1)
    t_hot = (tile[:, None] == tiles[None, :]).astype(i32)
    src = (tile * TILE_UNITS + jnp.sum(_pick(t_hot, tile_first - run_first) * e_hot, axis=1) + local)
    unit_src = jnp.where(valid, src, 0).astype(i32)

    j = jnp.arange(TILE_UNITS, dtype=i32)
    ej = jnp.minimum(jnp.sum((tile_end[:, None, :] <= j[None, :, None]).astype(i32), axis=2),
                     N_EXPERTS - 1)
    ej_hot = (ej[:, :, None] == experts[None, None, :]).astype(i32)
    used = j[None, :] < jnp.sum(units, axis=1)[:, None]
    base = region_first[None, :] + run_first - tile_first
    dst = jnp.sum(ej_hot * base[:, None, :], axis=2) + j[None, :]
    tile_units = jnp.where(used, dst, 0).astype(i32).reshape(-1)
    owns = region_blocks > 0
    expert_order = _excl_cumsum(owns.astype(i32), 0)
    later = (experts[None, :] > experts[:, None]) & owns[None, :]
    expert_next = jnp.min(jnp.where(later, experts[None, :], N_EXPERTS), axis=1)
    expert_next = jnp.where(expert_next < N_EXPERTS, expert_next, -1).astype(i32)
    return (block_expert.astype(i32), n_used.astype(i32), unit_src, tile_units,
            expert_order.astype(i32), expert_next)


def _layer(x, ln1_w, w_in, conv_w, conv_b, w_rg_a, b_rg_a, w_rg_i, b_rg_i, rg_lambda, b_forget,
           q_norm_w, k_norm_w, rec_out_norm_w, attn_out_norm_w, w_out, ln2_w, w_router, b_router,
           w_gate_up, b_gate_up, w_down, b_down):
    batch, seq, _ = x.shape
    t = batch * seq
    x2d = x.reshape(t, D_MODEL)

    o_q = 2 * D_REC
    o_k = o_q + D_ATT
    o_v = o_k + D_ATT
    o_f = o_v + D_ATT
    w_all = jnp.concatenate(
        [w_in[:, :o_q], _head_slabs(w_in[:, o_q:o_k]), _head_slabs(w_in[:, o_k:o_v]),
         jnp.pad(w_in[:, o_f:], ((0, 0), (0, LANES - N_HEADS)))],
        axis=1).astype(BF16)
    w_vt = _head_slabs(w_in[:, o_v:o_f]).T.astype(BF16)
    scale = HEAD_DIM ** -0.5
    u, g, qp, kp, vp = _in_proj(x2d, ln1_w.reshape(1, D_MODEL), w_all, w_vt, _pad_lanes(b_forget),
                                _pad_lanes(q_norm_w * scale), _pad_lanes(k_norm_w), batch, seq)

    rec_n = _rglru(u.reshape(batch, seq, D_REC), g.reshape(batch, seq, D_REC), conv_w,
                   conv_b.reshape(1, D_REC), _block_diag(w_rg_a).astype(BF16), b_rg_a.reshape(1, D_REC),
                   _block_diag(w_rg_i).astype(BF16), b_rg_i.reshape(1, D_REC),
                   rg_lambda.reshape(1, D_REC), rec_out_norm_w.reshape(1, D_REC))

    att = _attention(qp, kp, vp)

    aw_slab = _head_slabs(attn_out_norm_w.reshape(1, D_ATT))
    w_att = jnp.pad(w_out[D_REC:].reshape(N_HEADS, HEAD_DIM, D_MODEL),
                    ((0, 0), (0, SLAB - HEAD_DIM), (0, 0))).reshape(N_HEADS * SLAB, D_MODEL)
    rw = jnp.pad(w_router, ((0, 0), (0, LANES - N_EXPERTS)))
    rw_hi = rw.astype(BF16)
    rw_lo = (rw - rw_hi.astype(F32)).astype(BF16)
    x1, xs, meta, units = _out_proj(
        x2d, rec_n.reshape(t, D_REC), att.reshape(t, N_HEADS * SLAB), aw_slab,
        w_out[:D_REC].astype(BF16), w_att.astype(BF16), ln2_w.reshape(1, D_MODEL),
        rw_hi, rw_lo, _pad_lanes(b_router))

    n_tiles = t // TM_OUT
    max_units = t * TOP_K // UNIT + n_tiles * N_EXPERTS
    n_blocks = -(-max_units // BLOCK_UNITS) + N_EXPERTS
    block_expert, n_used, unit_src, tile_units, expert_order, expert_next = _dispatch_tables(
        units[:, 0, :N_EXPERTS].astype(jnp.int32), n_blocks)

    ys = _experts(block_expert, n_used, unit_src, expert_order, expert_next, xs, w_gate_up, b_gate_up,
                  w_down, b_down)
    out = _combine(tile_units, x1, meta, ys)
    return out.reshape(batch, seq, D_MODEL)


def kernel(x, ln1_w, w_in, conv_w, conv_b, w_rg_a, b_rg_a, w_rg_i, b_rg_i, rg_lambda, b_forget,
           q_norm_w, k_norm_w, rec_out_norm_w, attn_out_norm_w, w_out, ln2_w, w_router, b_router,
           w_gate_up, b_gate_up, w_down, b_down):
    for l in range(ln1_w.shape[0]):
        x = _layer(x, ln1_w[l], w_in[l], conv_w[l], conv_b[l], w_rg_a[l], b_rg_a[l], w_rg_i[l],
                   b_rg_i[l], rg_lambda[l], b_forget[l], q_norm_w[l], k_norm_w[l], rec_out_norm_w[l],
                   attn_out_norm_w[l], w_out[l], ln2_w[l], w_router[l], b_router[l], w_gate_up[l],
                   b_gate_up[l], w_down[l], b_down[l])
    return x
```

```python
import functools

import jax
import jax.numpy as jnp
from jax import lax
from jax.experimental import pallas as pl
from jax.experimental.pallas import tpu as pltpu

D_MODEL = 1024
D_REC = 512
N_REC_BLOCKS = 8
REC_BLOCK = 64
CONV_WIDTH = 4
RGLRU_C = 8.0
D_ATT = 512
HEAD_DIM = 64
N_HEADS = 8
N_EXPERTS = 32
TOP_K = 4
D_EXPERT = 1024
SWIGLU_LIMIT = 7.0
SWIGLU_ALPHA = 1.702
EPS = 1e-6

LANES = 128
SLAB = LANES
VMEM_LIMIT = 56 * 1024 * 1024

F32 = jnp.float32
BF16 = jnp.bfloat16

O_U = 0
O_G = O_U + D_REC
O_Q = O_G + D_REC
O_K = O_Q + N_HEADS * SLAB
O_F = O_K + N_HEADS * SLAB
N_PROJ = O_F + LANES

L_C = HEAD_DIM
L_ONE = HEAD_DIM + 3

TM_IN = 512
TS_REC = 512
TQ = 512
TK = 512
HEADS_PER_STEP = 8
TM_OUT = 512
TM_E = 512

UNIT = 16
TILE_SLOTS = -(-(TM_OUT * TOP_K + N_EXPERTS * (UNIT - 1)) // LANES) * LANES
TILE_UNITS = TILE_SLOTS // UNIT
BLOCK_UNITS = TM_E // UNIT
SLOT_CHUNK = 512


def _split3(v):
    hi = v.astype(BF16).astype(F32)
    r = v - hi
    mid = r.astype(BF16).astype(F32)
    lo = (r - mid).astype(BF16).astype(F32)
    return hi, mid, lo


def _dot(a, b):
    return jnp.dot(a, b, preferred_element_type=F32)


def _sigmoid(x):
    return 0.5 * jnp.tanh(0.5 * x) + 0.5


def _dot_t(a, b):
    return lax.dot_general(a, b, (((0,), (0,)), ((), ())), preferred_element_type=F32)


def _in_proj_kernel(x_ref, ln_ref, w_ref, wvt_ref, bf_ref, qw_ref, kw_ref,
                    u_ref, g_ref, q_ref, k_ref, v_ref, carry_ref, *, tiles_per_seq):
    i = pl.program_id(0)
    tm = x_ref.shape[0]

    @pl.when(i % tiles_per_seq == 0)
    def _():
        carry_ref[...] = jnp.zeros_like(carry_ref)

    x = x_ref[...]
    ms = jnp.mean(x * x, axis=-1, keepdims=True)
    h = (x * lax.rsqrt(ms + EPS) * ln_ref[...]).astype(BF16)

    f = _dot(h, w_ref[:, O_F:O_F + LANES]) + bf_ref[...]
    logf = jnp.minimum(f, 0.0) - jnp.log1p(jnp.exp(-jnp.abs(f)))
    row = lax.broadcasted_iota(jnp.int32, (tm, tm), 0)
    col = lax.broadcasted_iota(jnp.int32, (tm, tm), 1)
    tri = (row >= col).astype(BF16)
    hi, mid, lo = _split3(logf)
    c = (_dot(tri, hi.astype(BF16)) + _dot(tri, mid.astype(BF16))
         + _dot(tri, lo.astype(BF16)) + carry_ref[...])
    carry_ref[...] = c[tm - 1:tm, :]

    lane = lax.broadcasted_iota(jnp.int32, (tm, SLAB), 1)
    q_all = _dot(h, w_ref[:, O_Q:O_Q + N_HEADS * SLAB])
    k_all = _dot(h, w_ref[:, O_K:O_K + N_HEADS * SLAB])
    qw = qw_ref[...]
    kw = kw_ref[...]
    g_row = lax.broadcasted_iota(jnp.int32, (N_HEADS * SLAB, LANES), 0)
    g_col = lax.broadcasted_iota(jnp.int32, (N_HEADS * SLAB, LANES), 1)
    group = ((g_row >= g_col * SLAB) & (g_row < (g_col + 1) * SLAB)).astype(BF16)
    q_inv = lax.rsqrt(_dot((q_all * q_all).astype(BF16), group) * (1.0 / HEAD_DIM) + EPS)
    k_inv = lax.rsqrt(_dot((k_all * k_all).astype(BF16), group) * (1.0 / HEAD_DIM) + EPS)
    for hd in range(N_HEADS):
        sl = slice(hd * SLAB, (hd + 1) * SLAB)
        qn = q_all[:, sl] * q_inv[:, hd:hd + 1] * qw
        kn = k_all[:, sl] * k_inv[:, hd:hd + 1] * kw
        c_hi, c_mid, c_lo = _split3(c[:, hd:hd + 1])
        qp = jnp.where(lane == L_C, c_hi, qn)
        qp = jnp.where(lane == L_C + 1, c_mid, qp)
        qp = jnp.where(lane == L_C + 2, c_lo, qp)
        qp = jnp.where((lane >= L_ONE) & (lane < L_ONE + 3), 1.0, qp)
        kp = jnp.where((lane >= L_C) & (lane < L_C + 3), 1.0, kn)
        kp = jnp.where(lane == L_ONE, -c_hi, kp)
        kp = jnp.where(lane == L_ONE + 1, -c_mid, kp)
        kp = jnp.where(lane == L_ONE + 2, -c_lo, kp)
        q_ref[0, hd] = qp.astype(BF16)
        k_ref[0, hd] = kp.astype(BF16)

    vt_all = lax.dot_general(wvt_ref[...], h, (((1,), (1,)), ((), ())), preferred_element_type=F32)
    feat = lax.broadcasted_iota(jnp.int32, (SLAB, tm), 0)
    for hd in range(N_HEADS):
        v_ref[0, hd] = jnp.where(feat == HEAD_DIM, 1.0,
                                 vt_all[hd * SLAB:(hd + 1) * SLAB, :]).astype(BF16)
    u_ref[...] = _dot(h, w_ref[:, O_U:O_U + D_REC])
    g_ref[...] = _dot(h, w_ref[:, O_G:O_G + D_REC])


def _in_proj(x2d, ln1_w, w_all, w_vt, bf_pad, qw_slab, kw_slab, batch, seq):
    t = x2d.shape[0]
    tiles_per_seq = seq // TM_IN
    slab_shape = jax.ShapeDtypeStruct((batch, N_HEADS, seq, SLAB), BF16)
    slab_spec = pl.BlockSpec((1, N_HEADS, TM_IN, SLAB),
                             lambda i: (i // tiles_per_seq, 0, i % tiles_per_seq, 0))
    slab_t_shape = jax.ShapeDtypeStruct((batch, N_HEADS, SLAB, seq), BF16)
    slab_t_spec = pl.BlockSpec((1, N_HEADS, SLAB, TM_IN),
                               lambda i: (i // tiles_per_seq, 0, 0, i % tiles_per_seq))
    row_spec = pl.BlockSpec((TM_IN, D_REC), lambda i: (i, 0))
    const = lambda shape: pl.BlockSpec(shape, lambda i: (0,) * len(shape))
    return pl.pallas_call(
        functools.partial(_in_proj_kernel, tiles_per_seq=tiles_per_seq),
        out_shape=(jax.ShapeDtypeStruct((t, D_REC), F32), jax.ShapeDtypeStruct((t, D_REC), F32),
                   slab_shape, slab_shape, slab_t_shape),
        grid=(t // TM_IN,),
        in_specs=[pl.BlockSpec((TM_IN, D_MODEL), lambda i: (i, 0)),
                  const((1, D_MODEL)), const((D_MODEL, N_PROJ)), const((N_HEADS * SLAB, D_MODEL)),
                  const((1, LANES)), const((1, SLAB)), const((1, SLAB))],
        out_specs=(row_spec, row_spec, slab_spec, slab_spec, slab_t_spec),
        scratch_shapes=[pltpu.VMEM((1, LANES), F32)],
        compiler_params=pltpu.CompilerParams(dimension_semantics=("arbitrary",),
                                             vmem_limit_bytes=VMEM_LIMIT),
        name="in_proj",
    )(x2d, ln1_w, w_all, w_vt, bf_pad, qw_slab, kw_slab)


R_CONV_B = CONV_WIDTH
R_B_A = CONV_WIDTH + 1
R_B_I = CONV_WIDTH + 2
R_LAMBDA = CONV_WIDTH + 3
R_NORM_W = CONV_WIDTH + 4
N_REC_ROWS = CONV_WIDTH + 5


def _rglru_kernel(u_ref, g_ref, vec_ref, wg_ref, o_ref, ext_ref, a_ref, b_ref, h_ref, hcar_ref):
    s = pl.program_id(1)
    ts = u_ref.shape[1]
    pad = 8
    row = lambda r: vec_ref[r:r + 1, :]

    @pl.when(s == 0)
    def _():
        ext_ref[0:pad, :] = jnp.zeros((pad, D_REC), F32)
        hcar_ref[...] = jnp.zeros_like(hcar_ref)

    @pl.when(s != 0)
    def _():
        ext_ref[0:pad, :] = ext_ref[ts:ts + pad, :]

    u = u_ref[0]
    ext_ref[pad:pad + ts, :] = u
    xc = row(R_CONV_B) + u * row(CONV_WIDTH - 1)
    for tap in range(CONV_WIDTH - 1):
        d = CONV_WIDTH - 1 - tap
        xc = xc + ext_ref[pad - d:pad - d + ts, :] * row(tap)

    gates = _dot(xc.astype(BF16), wg_ref[...])
    r = _sigmoid(gates[:, :D_REC] + row(R_B_A))
    ig = _sigmoid(gates[:, D_REC:] + row(R_B_I))
    nlam = -row(R_LAMBDA)
    softplus = jnp.maximum(nlam, 0.0) + jnp.log1p(jnp.exp(-jnp.abs(nlam)))
    log_a = -RGLRU_C * r * softplus
    a = jnp.exp(log_a)
    a_ref[...] = a
    b_ref[...] = jnp.sqrt(1.0 - jnp.exp(2.0 * log_a)) * (ig * xc)

    def step(t, h):
        h = a_ref[pl.ds(t, 1), :] * h + b_ref[pl.ds(t, 1), :]
        h_ref[pl.ds(t, 1), :] = h
        return h

    hcar_ref[...] = lax.fori_loop(0, ts, step, hcar_ref[...], unroll=8)

    rec = h_ref[...] * jax.nn.gelu(g_ref[0])
    ms = jnp.mean(rec * rec, axis=-1, keepdims=True)
    o_ref[0] = (rec * lax.rsqrt(ms + EPS) * row(R_NORM_W)).astype(BF16)


def _rglru(u, g, rec_vec, w_gates):
    batch, seq, _ = u.shape
    seq_spec = pl.BlockSpec((1, TS_REC, D_REC), lambda b, s: (b, s, 0))
    const = lambda shape: pl.BlockSpec(shape, lambda b, s: (0,) * len(shape))
    return pl.pallas_call(
        _rglru_kernel,
        out_shape=jax.ShapeDtypeStruct((batch, seq, D_REC), BF16),
        grid=(batch, seq // TS_REC),
        in_specs=[seq_spec, seq_spec, const((N_REC_ROWS, D_REC)), const((D_REC, 2 * D_REC))],
        out_specs=seq_spec,
        scratch_shapes=[pltpu.VMEM((TS_REC + 8, D_REC), F32), pltpu.VMEM((TS_REC, D_REC), F32),
                        pltpu.VMEM((TS_REC, D_REC), F32), pltpu.VMEM((TS_REC, D_REC), F32),
                        pltpu.VMEM((1, D_REC), F32)],
        compiler_params=pltpu.CompilerParams(dimension_semantics=("arbitrary", "arbitrary"),
                                             vmem_limit_bytes=VMEM_LIMIT),
        name="rglru",
    )(u, g, rec_vec, w_gates)


def _attn_kernel(q_ref, k_ref, v_ref, o_ref):
    qi = pl.program_id(2)
    hb = q_ref.shape[1]
    tq = q_ref.shape[2]
    qs = [q_ref[0, h] for h in range(hb)]

    def step(j, carry, masked):
        start = pl.multiple_of(j * TK, TK)
        scores = []
        for h in range(hb):
            kj = k_ref[0, h, pl.ds(start, TK), :]
            s = lax.dot_general(kj, qs[h], (((1,), (1,)), ((), ())), preferred_element_type=F32)
            if masked:
                key = lax.broadcasted_iota(jnp.int32, (TK, tq), 0)
                qry = lax.broadcasted_iota(jnp.int32, (TK, tq), 1)
                s = jnp.where(qry >= key, s, -jnp.inf)
            scores.append(s)
        m_news = [jnp.maximum(carry[h][0], jnp.max(scores[h], axis=0, keepdims=True))
                  for h in range(hb)]
        probs = [jnp.exp(scores[h] - m_news[h]).astype(BF16) for h in range(hb)]
        out = []
        for h in range(hb):
            m, acc = carry[h]
            vj = v_ref[0, h, :, pl.ds(start, TK)]
            out.append((m_news[h], jnp.exp(m - m_news[h]) * acc + _dot(vj, probs[h])))
        return tuple(out)

    init = tuple((jnp.full((1, tq), -jnp.inf, F32), jnp.zeros((SLAB, tq), F32)) for _ in range(hb))
    carry = lax.fori_loop(0, qi, lambda j, c: step(j, c, False), init)
    carry = step(qi, carry, True)
    feat = lax.broadcasted_iota(jnp.int32, (SLAB, tq), 0)
    for h in range(hb):
        acc = carry[h][1]
        out_t = jnp.where(feat < HEAD_DIM, acc / acc[HEAD_DIM:HEAD_DIM + 1, :], 0.0)
        o_ref[0, :, h * SLAB:(h + 1) * SLAB] = out_t.T


def _attention(qp, kp, vp):
    batch, _, seq, _ = qp.shape
    hb = HEADS_PER_STEP
    kv_spec = pl.BlockSpec((1, hb, seq, SLAB), lambda b, h, i: (b, h, 0, 0))
    vt_spec = pl.BlockSpec((1, hb, SLAB, seq), lambda b, h, i: (b, h, 0, 0))
    return pl.pallas_call(
        _attn_kernel,
        out_shape=jax.ShapeDtypeStruct((batch, seq, N_HEADS * SLAB), F32),
        grid=(batch, N_HEADS // hb, seq // TQ),
        in_specs=[pl.BlockSpec((1, hb, TQ, SLAB), lambda b, h, i: (b, h, i, 0)), kv_spec, vt_spec],
        out_specs=pl.BlockSpec((1, TQ, hb * SLAB), lambda b, h, i: (b, i, h)),
        compiler_params=pltpu.CompilerParams(
            dimension_semantics=("arbitrary", "arbitrary", "arbitrary"),
            vmem_limit_bytes=VMEM_LIMIT),
        name="attn",
    )(qp, kp, vp)


def _slot_onehot(slots, first, width):
    tm = slots[0].shape[0]
    col = lax.broadcasted_iota(jnp.int32, (tm, width), 1) + first
    hit = jnp.zeros((tm, width), F32)
    for slot in slots:
        hit = jnp.where(col == slot, 1.0, hit)
    return hit.astype(BF16)


def _out_proj_kernel(x_ref, rec_ref, att_ref, aw_ref, wr_ref, wa_ref, ln_ref, rwh_ref, rwl_ref, rb_ref,
                     x1_ref, xs_ref, meta_ref, units_ref):
    tm = x_ref.shape[0]

    att = att_ref[...]
    ms = jnp.sum(att * att, axis=-1, keepdims=True) * (1.0 / D_ATT)
    att_n = (att * lax.rsqrt(ms + EPS) * aw_ref[...]).astype(BF16)
    x1 = x_ref[...] + _dot(rec_ref[...], wr_ref[...]) + _dot(att_n, wa_ref[...])
    x1_ref[...] = x1

    ms2 = jnp.mean(x1 * x1, axis=-1, keepdims=True)
    h2 = x1 * lax.rsqrt(ms2 + EPS) * ln_ref[...]

    h_hi = h2.astype(BF16)
    h_lo = (h2 - h_hi.astype(F32)).astype(BF16)
    logits = (_dot(h_hi, rwh_ref[...]) + _dot(h_lo, rwh_ref[...]) + _dot(h_hi, rwl_ref[...])
              + rb_ref[...])

    lane = lax.broadcasted_iota(jnp.int32, (tm, LANES), 1)
    lane_f = lane.astype(F32)
    l = jnp.where(lane < N_EXPERTS, logits, -jnp.inf)

    vals, idxs, sels = [], [], []
    for _ in range(TOP_K):
        m = jnp.max(l, axis=-1, keepdims=True)
        idx = jnp.min(jnp.where(l == m, lane_f, float(LANES)), axis=-1, keepdims=True)
        sel = lane_f == idx
        vals.append(m)
        idxs.append(idx)
        sels.append(sel)
        l = jnp.where(sel, -jnp.inf, l)
    exps = [jnp.exp(v - vals[0]) for v in vals]
    denom = exps[0] + exps[1] + exps[2] + exps[3]

    any_sel = sels[0] | sels[1] | sels[2] | sels[3]
    onehot = jnp.where(any_sel, 1.0, 0.0)
    row = lax.broadcasted_iota(jnp.int32, (tm, tm), 0)
    col = lax.broadcasted_iota(jnp.int32, (tm, tm), 1)
    before = (row > col).astype(BF16)
    rank = _dot(before, onehot.astype(BF16))
    count = jnp.sum(onehot, axis=0, keepdims=True)
    units = jnp.floor((count + (UNIT - 1)) * (1.0 / UNIT))
    e_row = lax.broadcasted_iota(jnp.int32, (LANES, LANES), 0)
    e_col = lax.broadcasted_iota(jnp.int32, (LANES, LANES), 1)
    earlier = (e_row < e_col).astype(BF16)
    units8 = jnp.broadcast_to(units, (8, LANES))
    first_unit = _dot(units8.astype(BF16), earlier)[0:1, :]
    units_ref[0] = units8
    place = first_unit * float(UNIT) + rank

    meta = jnp.zeros((tm, LANES), F32)
    slots = []
    for k in range(TOP_K):
        slot = jnp.sum(jnp.where(sels[k], place, 0.0), axis=-1, keepdims=True)
        slots.append(slot.astype(jnp.int32))
        meta = jnp.where(lane == k, idxs[k], meta)
        meta = jnp.where(lane == TOP_K + k, exps[k] / denom, meta)
        meta = jnp.where(lane == 2 * TOP_K + k, slot, meta)
    meta_ref[...] = meta

    for c in range(TILE_SLOTS // SLOT_CHUNK):
        sel_t = _slot_onehot(slots, c * SLOT_CHUNK, SLOT_CHUNK)
        xs_ref[c * SLOT_CHUNK:(c + 1) * SLOT_CHUNK, :] = _dot_t(sel_t, h_hi).astype(BF16)


def _out_proj(x2d, rec_n, att, aw_slab, w_rec, w_att, ln2_w, rw_hi, rw_lo, rb_pad):
    t = x2d.shape[0]
    n_tiles = t // TM_OUT
    rows = lambda n: pl.BlockSpec((TM_OUT, n), lambda i: (i, 0))
    const = lambda shape: pl.BlockSpec(shape, lambda i: (0,) * len(shape))
    return pl.pallas_call(
        _out_proj_kernel,
        out_shape=(jax.ShapeDtypeStruct((t, D_MODEL), F32),
                   jax.ShapeDtypeStruct((n_tiles * TILE_SLOTS, D_MODEL), BF16),
                   jax.ShapeDtypeStruct((t, LANES), F32),
                   jax.ShapeDtypeStruct((n_tiles, 8, LANES), F32)),
        grid=(n_tiles,),
        in_specs=[rows(D_MODEL), rows(D_REC), rows(N_HEADS * SLAB), const((1, N_HEADS * SLAB)),
                  const((D_REC, D_MODEL)), const((N_HEADS * SLAB, D_MODEL)), const((1, D_MODEL)),
                  const((D_MODEL, LANES)), const((D_MODEL, LANES)), const((1, LANES))],
        out_specs=(rows(D_MODEL), pl.BlockSpec((TILE_SLOTS, D_MODEL), lambda i: (i, 0)), rows(LANES),
                   pl.BlockSpec((1, 8, LANES), lambda i: (i, 0, 0))),
        compiler_params=pltpu.CompilerParams(dimension_semantics=("arbitrary",),
                                             vmem_limit_bytes=VMEM_LIMIT),
        name="out_proj",
    )(x2d, rec_n, att, aw_slab, w_rec, w_att, ln2_w, rw_hi, rw_lo, rb_pad)


def _start_units(unit_ref, base, n_units, src_hbm, dst, sem):
    def body(j, carry):
        src_row = pl.multiple_of(unit_ref[base + j] * UNIT, UNIT)
        dst_row = pl.multiple_of(j * UNIT, UNIT)
        pltpu.make_async_copy(src_hbm.at[pl.ds(src_row, UNIT), :],
                              dst.at[pl.ds(dst_row, UNIT), :], sem).start()
        return carry
    lax.fori_loop(0, n_units, body, 0, unroll=8)


def _wait_units(n_units, src_hbm, dst, sem):
    pltpu.make_async_copy(src_hbm.at[pl.ds(0, n_units * UNIT), :], dst, sem).wait()


def _experts_kernel(be_ref, nblk_ref, unit_ref, order_ref, next_ref, xs_hbm, wgu_hbm, bgu_ref, wd_hbm,
                    bd_ref, ys_ref, xbuf, wgu_f32, wd_f32, wgu_bf, wd_bf, sem, wsem):
    b = pl.program_id(0)
    nblk = nblk_ref[0]
    slot = b % 2
    e = be_ref[b]

    def weight_copies(expert, wslot):
        return (pltpu.make_async_copy(wgu_hbm.at[expert], wgu_f32.at[wslot], wsem.at[wslot]),
                pltpu.make_async_copy(wd_hbm.at[expert], wd_f32.at[wslot], wsem.at[wslot]))

    @pl.when(b == 0)
    def _():
        _start_units(unit_ref, 0, BLOCK_UNITS, xs_hbm, xbuf.at[0], sem.at[0])
        for copy in weight_copies(e, 0):
            copy.start()

    @pl.when(b + 1 < nblk)
    def _():
        _start_units(unit_ref, (b + 1) * BLOCK_UNITS, BLOCK_UNITS, xs_hbm, xbuf.at[1 - slot],
                     sem.at[1 - slot])

    new_expert = jnp.logical_or(b == 0, e != be_ref[jnp.maximum(b - 1, 0)])

    @pl.when(jnp.logical_and(b < nblk, new_expert))
    def _():
        wslot = order_ref[e] % 2
        for copy in weight_copies(e, wslot):
            copy.wait()
        wgu_bf[...] = wgu_f32[wslot].astype(BF16)
        wd_bf[...] = wd_f32[wslot].astype(BF16)
        nxt = next_ref[e]

        @pl.when(nxt >= 0)
        def _():
            for copy in weight_copies(nxt, 1 - wslot):
                copy.start()

    @pl.when(b < nblk)
    def _():
        _wait_units(BLOCK_UNITS, xs_hbm, xbuf.at[slot], sem.at[slot])
        gu = _dot(xbuf[slot], wgu_bf[...]) + bgu_ref[0]
        g = jnp.minimum(gu[:, :D_EXPERT], SWIGLU_LIMIT)
        u = jnp.clip(gu[:, D_EXPERT:], -SWIGLU_LIMIT, SWIGLU_LIMIT)
        act = (u + 1.0) * (g * _sigmoid(SWIGLU_ALPHA * g))
        ys_ref[...] = (_dot(act.astype(BF16), wd_bf[...]) + bd_ref[0]).astype(BF16)

    @pl.when(b >= nblk)
    def _():
        ys_ref[...] = jnp.zeros_like(ys_ref)


def _experts(block_expert, n_blocks_used, unit_src, expert_order, expert_next, xs, w_gate_up,
             b_gate_up, w_down, b_down):
    n_blocks = block_expert.shape[0]
    bias_map = lambda b, be, nb, us, eo, en: (be[b], 0, 0)
    grid_spec = pltpu.PrefetchScalarGridSpec(
        num_scalar_prefetch=5,
        grid=(n_blocks,),
        in_specs=[pl.BlockSpec(memory_space=pl.ANY),
                  pl.BlockSpec(memory_space=pl.ANY),
                  pl.BlockSpec((1, 1, 2 * D_EXPERT), bias_map),
                  pl.BlockSpec(memory_space=pl.ANY),
                  pl.BlockSpec((1, 1, D_MODEL), bias_map)],
        out_specs=pl.BlockSpec((TM_E, D_MODEL), lambda b, be, nb, us, eo, en: (b, 0)),
        scratch_shapes=[pltpu.VMEM((2, TM_E, D_MODEL), BF16),
                        pltpu.VMEM((2, D_MODEL, 2 * D_EXPERT), F32),
                        pltpu.VMEM((2, D_EXPERT, D_MODEL), F32),
                        pltpu.VMEM((D_MODEL, 2 * D_EXPERT), BF16),
                        pltpu.VMEM((D_EXPERT, D_MODEL), BF16),
                        pltpu.SemaphoreType.DMA((2,)),
                        pltpu.SemaphoreType.DMA((2,))],
    )
    return pl.pallas_call(
        _experts_kernel,
        out_shape=jax.ShapeDtypeStruct((n_blocks * TM_E, D_MODEL), BF16),
        grid_spec=grid_spec,
        compiler_params=pltpu.CompilerParams(dimension_semantics=("arbitrary",),
                                             vmem_limit_bytes=VMEM_LIMIT),
        name="experts",
    )(block_expert, n_blocks_used, unit_src, expert_order, expert_next, xs, w_gate_up,
      b_gate_up.reshape(N_EXPERTS, 1, 2 * D_EXPERT), w_down, b_down.reshape(N_EXPERTS, 1, D_MODEL))


def _combine_kernel(unit_ref, x1_ref, meta_ref, ys_hbm, o_ref, buf, sem):
    i = pl.program_id(0)
    n = pl.num_programs(0)
    slot = i % 2
    tm = x1_ref.shape[0]

    @pl.when(i == 0)
    def _():
        _start_units(unit_ref, 0, TILE_UNITS, ys_hbm, buf.at[0], sem.at[0])

    @pl.when(i + 1 < n)
    def _():
        _start_units(unit_ref, (i + 1) * TILE_UNITS, TILE_UNITS, ys_hbm, buf.at[1 - slot],
                     sem.at[1 - slot])

    meta = meta_ref[...]
    slots = [meta[:, 2 * TOP_K + k:2 * TOP_K + k + 1].astype(jnp.int32) for k in range(TOP_K)]
    gates = [meta[:, TOP_K + k:TOP_K + k + 1] for k in range(TOP_K)]
    _wait_units(TILE_UNITS, ys_hbm, buf.at[slot], sem.at[slot])

    acc = x1_ref[...]
    for c in range(TILE_SLOTS // SLOT_CHUNK):
        first = c * SLOT_CHUNK
        col = lax.broadcasted_iota(jnp.int32, (tm, SLOT_CHUNK), 1) + first
        weight = jnp.zeros((tm, SLOT_CHUNK), F32)
        for k in range(TOP_K):
            weight = jnp.where(col == slots[k], gates[k], weight)
        acc = acc + _dot(weight.astype(BF16), buf[slot, first:first + SLOT_CHUNK, :])
    o_ref[...] = acc


def _combine(tile_units, x1, meta, ys):
    t = x1.shape[0]
    grid_spec = pltpu.PrefetchScalarGridSpec(
        num_scalar_prefetch=1,
        grid=(t // TM_OUT,),
        in_specs=[pl.BlockSpec((TM_OUT, D_MODEL), lambda i, us: (i, 0)),
                  pl.BlockSpec((TM_OUT, LANES), lambda i, us: (i, 0)),
                  pl.BlockSpec(memory_space=pl.ANY)],
        out_specs=pl.BlockSpec((TM_OUT, D_MODEL), lambda i, us: (i, 0)),
        scratch_shapes=[pltpu.VMEM((2, TILE_SLOTS, D_MODEL), BF16),
                        pltpu.SemaphoreType.DMA((2,))],
    )
    return pl.pallas_call(
        _combine_kernel,
        out_shape=jax.ShapeDtypeStruct((t, D_MODEL), F32),
        grid_spec=grid_spec,
        compiler_params=pltpu.CompilerParams(dimension_semantics=("arbitrary",),
                                             vmem_limit_bytes=VMEM_LIMIT),
        name="combine",
    )(tile_units, x1, meta, ys)


def _head_slabs(w):
    d = w.shape[0]
    w = w.reshape(d, -1, HEAD_DIM)
    return jnp.pad(w, ((0, 0), (0, 0), (0, SLAB - HEAD_DIM))).reshape(d, -1)


def _gate_matrix(w_a, w_i):
    g = w_a.shape[0]
    eye = jnp.eye(g, dtype=w_a.dtype)
    both = jnp.stack([w_a, w_i])
    return jnp.einsum('sgij,gh->gishj', both, eye).reshape(g * REC_BLOCK, 2 * g * REC_BLOCK)


def _pad_lanes(v, n=LANES):
    return jnp.pad(v, (0, n - v.shape[0])).reshape(1, n)


def _excl_cumsum(a, axis):
    return jnp.cumsum(a, axis=axis) - a


def _pick(onehot, table):
    if table.ndim == 1:
        return jnp.sum(onehot * table[None, :], axis=1)
    return jnp.sum(onehot[:, :, None] * table[None, :, :], axis=1)


def _dispatch_tables(units, n_blocks):
    n_tiles = units.shape[0]
    i32 = jnp.int32
    tile_first = _excl_cumsum(units, 1)
    tile_end = tile_first + units
    run_first = _excl_cumsum(units, 0)
    run_end = run_first + units
    region_units = jnp.sum(units, axis=0)
    region_blocks = (region_units + BLOCK_UNITS - 1) // BLOCK_UNITS
    block_end = jnp.cumsum(region_blocks)
    region_first = (block_end - region_blocks) * BLOCK_UNITS
    n_used = block_end[-1:]
    experts = jnp.arange(N_EXPERTS, dtype=i32)
    tiles = jnp.arange(n_tiles, dtype=i32)

    blocks = jnp.arange(n_blocks, dtype=i32)
    block_expert = jnp.minimum(
        jnp.sum((block_end[None, :] <= blocks[:, None]).astype(i32), axis=1), N_EXPERTS - 1)

    g = jnp.arange(n_blocks * BLOCK_UNITS, dtype=i32)
    e = jnp.minimum(jnp.sum((block_end[None, :] * BLOCK_UNITS <= g[:, None]).astype(i32), axis=1),
                    N_EXPERTS - 1)
    e_hot = (e[:, None] == experts[None, :]).astype(i32)
    local = g - _pick(e_hot, region_first)
    valid = (local < _pick(e_hot, region_units)) & (g < n_used[0] * BLOCK_UNITS)
    ends = _pick(e_hot, run_end.T)
    tile = jnp.minimum(jnp.sum((ends <= local[:, None]).astype(i32), axis=1), n_tiles - 1)
    t_hot = (tile[:, None] == tiles[None, :]).astype(i32)
    src = (tile * TILE_UNITS + jnp.sum(_pick(t_hot, tile_first - run_first) * e_hot, axis=1) + local)
    unit_src = jnp.where(valid, src, 0).astype(i32)

    j = jnp.arange(TILE_UNITS, dtype=i32)
    ej = jnp.minimum(jnp.sum((tile_end[:, None, :] <= j[None, :, None]).astype(i32), axis=2),
                     N_EXPERTS - 1)
    ej_hot = (ej[:, :, None] == experts[None, None, :]).astype(i32)
    used = j[None, :] < jnp.sum(units, axis=1)[:, None]
    base = region_first[None, :] + run_first - tile_first
    dst = jnp.sum(ej_hot * base[:, None, :], axis=2) + j[None, :]
    tile_units = jnp.where(used, dst, 0).astype(i32).reshape(-1)
    owns = region_blocks > 0
    expert_order = _excl_cumsum(owns.astype(i32), 0)
    later = (experts[None, :] > experts[:, None]) & owns[None, :]
    expert_next = jnp.min(jnp.where(later, experts[None, :], N_EXPERTS), axis=1)
    expert_next = jnp.where(expert_next < N_EXPERTS, expert_next, -1).astype(i32)
    return (block_expert.astype(i32), n_used.astype(i32), unit_src, tile_units,
            expert_order.astype(i32), expert_next)


def _layer(x, ln1_w, w_in, conv_w, conv_b, w_rg_a, b_rg_a, w_rg_i, b_rg_i, rg_lambda, b_forget,
           q_norm_w, k_norm_w, rec_out_norm_w, attn_out_norm_w, w_out, ln2_w, w_router, b_router,
           w_gate_up, b_gate_up, w_down, b_down):
    batch, seq, _ = x.shape
    t = batch * seq
    x2d = x.reshape(t, D_MODEL)

    o_q = 2 * D_REC
    o_k = o_q + D_ATT
    o_v = o_k + D_ATT
    o_f = o_v + D_ATT
    slabs = _head_slabs(w_in[:, o_q:o_f])
    w_all = jnp.concatenate(
        [w_in[:, :o_q], slabs[:, :2 * N_HEADS * SLAB],
         jnp.pad(w_in[:, o_f:], ((0, 0), (0, LANES - N_HEADS)))],
        axis=1).astype(BF16)
    w_vt = slabs[:, 2 * N_HEADS * SLAB:].T.astype(BF16)
    scale = HEAD_DIM ** -0.5
    u, g, qp, kp, vp = _in_proj(x2d, ln1_w.reshape(1, D_MODEL), w_all, w_vt, _pad_lanes(b_forget),
                                _pad_lanes(q_norm_w * scale), _pad_lanes(k_norm_w), batch, seq)

    rec_vec = jnp.concatenate(
        [conv_w, jnp.stack([conv_b, b_rg_a, b_rg_i, rg_lambda, rec_out_norm_w])])
    rec_n = _rglru(u.reshape(batch, seq, D_REC), g.reshape(batch, seq, D_REC), rec_vec,
                   _gate_matrix(w_rg_a, w_rg_i).astype(BF16))

    att = _attention(qp, kp, vp)

    aw_slab = _head_slabs(attn_out_norm_w.reshape(1, D_ATT))
    w_att = jnp.pad(w_out[D_REC:].reshape(N_HEADS, HEAD_DIM, D_MODEL),
                    ((0, 0), (0, SLAB - HEAD_DIM), (0, 0))).reshape(N_HEADS * SLAB, D_MODEL)
    rw = jnp.pad(w_router, ((0, 0), (0, LANES - N_EXPERTS)))
    rw_hi = rw.astype(BF16)
    rw_lo = (rw - rw_hi.astype(F32)).astype(BF16)
    x1, xs, meta, units = _out_proj(
        x2d, rec_n.reshape(t, D_REC), att.reshape(t, N_HEADS * SLAB), aw_slab,
        w_out[:D_REC].astype(BF16), w_att.astype(BF16), ln2_w.reshape(1, D_MODEL),
        rw_hi, rw_lo, _pad_lanes(b_router))

    n_tiles = t // TM_OUT
    max_units = t * TOP_K // UNIT + n_tiles * N_EXPERTS
    n_blocks = -(-max_units // BLOCK_UNITS) + N_EXPERTS
    block_expert, n_used, unit_src, tile_units, expert_order, expert_next = _dispatch_tables(
        units[:, 0, :N_EXPERTS].astype(jnp.int32), n_blocks)

    ys = _experts(block_expert, n_used, unit_src, expert_order, expert_next, xs, w_gate_up, b_gate_up,
                  w_down, b_down)
    out = _combine(tile_units, x1, meta, ys)
    return out.reshape(batch, seq, D_MODEL)


def kernel(x, ln1_w, w_in, conv_w, conv_b, w_rg_a, b_rg_a, w_rg_i, b_rg_i, rg_lambda, b_forget,
           q_norm_w, k_norm_w, rec_out_norm_w, attn_out_norm_w, w_out, ln2_w, w_router, b_router,
           w_gate_up, b_gate_up, w_down, b_down):
    for l in range(ln1_w.shape[0]):
        x = _layer(x, ln1_w[l], w_in[l], conv_w[l], conv_b[l], w_rg_a[l], b_rg_a[l], w_rg_i[l],
                   b_rg_i[l], rg_lambda[l], b_forget[l], q_norm_w[l], k_norm_w[l], rec_out_norm_w[l],
                   attn_out_norm_w[l], w_out[l], ln2_w[l], w_router[l], b_router[l], w_gate_up[l],
                   b_gate_up[l], w_down[l], b_down[l])
    return x
```

```python
import functools

import jax
import jax.numpy as jnp
from jax import lax
from jax.experimental import pallas as pl
from jax.experimental.pallas import tpu as pltpu

D_MODEL = 1024
D_REC = 512
N_REC_BLOCKS = 8
REC_BLOCK = 64
CONV_WIDTH = 4
RGLRU_C = 8.0
D_ATT = 512
HEAD_DIM = 64
N_HEADS = 8
N_EXPERTS = 32
TOP_K = 4
D_EXPERT = 1024
SWIGLU_LIMIT = 7.0
SWIGLU_ALPHA = 1.702
EPS = 1e-6

LANES = 128
SLAB = LANES
VMEM_LIMIT = 56 * 1024 * 1024

F32 = jnp.float32
BF16 = jnp.bfloat16

O_U = 0
O_G = O_U + D_REC
O_Q = O_G + D_REC
O_K = O_Q + N_HEADS * SLAB
O_F = O_K + N_HEADS * SLAB
N_PROJ = O_F + LANES

L_C = HEAD_DIM
L_ONE = HEAD_DIM + 3

TM_IN = 512
TS_REC = 512
TQ = 512
TK = 512
HEADS_PER_STEP = 8
TM_OUT = 512
TM_E = 512

UNIT = 16
TILE_SLOTS = -(-(TM_OUT * TOP_K + N_EXPERTS * (UNIT - 1)) // LANES) * LANES
TILE_UNITS = TILE_SLOTS // UNIT
BLOCK_UNITS = TM_E // UNIT
SLOT_CHUNK = 512


def _split3(v):
    hi = v.astype(BF16).astype(F32)
    r = v - hi
    mid = r.astype(BF16).astype(F32)
    lo = (r - mid).astype(BF16).astype(F32)
    return hi, mid, lo


def _dot(a, b):
    return jnp.dot(a, b, preferred_element_type=F32)


def _sigmoid(x):
    return 0.5 * jnp.tanh(0.5 * x) + 0.5


def _dot_t(a, b):
    return lax.dot_general(a, b, (((0,), (0,)), ((), ())), preferred_element_type=F32)


def _in_proj_kernel(x_ref, ln_ref, w_ref, wvt_ref, bf_ref, qw_ref, kw_ref,
                    u_ref, g_ref, q_ref, k_ref, v_ref, carry_ref, *, tiles_per_seq):
    i = pl.program_id(0)
    tm = x_ref.shape[0]

    @pl.when(i % tiles_per_seq == 0)
    def _():
        carry_ref[...] = jnp.zeros_like(carry_ref)

    x = x_ref[...]
    ms = jnp.mean(x * x, axis=-1, keepdims=True)
    h = (x * lax.rsqrt(ms + EPS) * ln_ref[...]).astype(BF16)

    f = _dot(h, w_ref[:, O_F:O_F + LANES]) + bf_ref[...]
    logf = jnp.minimum(f, 0.0) - jnp.log1p(jnp.exp(-jnp.abs(f)))
    row = lax.broadcasted_iota(jnp.int32, (tm, tm), 0)
    col = lax.broadcasted_iota(jnp.int32, (tm, tm), 1)
    tri = (row >= col).astype(BF16)
    hi, mid, lo = _split3(logf)
    c = (_dot(tri, hi.astype(BF16)) + _dot(tri, mid.astype(BF16))
         + _dot(tri, lo.astype(BF16)) + carry_ref[...])
    carry_ref[...] = c[tm - 1:tm, :]

    lane = lax.broadcasted_iota(jnp.int32, (tm, SLAB), 1)
    q_all = _dot(h, w_ref[:, O_Q:O_Q + N_HEADS * SLAB])
    k_all = _dot(h, w_ref[:, O_K:O_K + N_HEADS * SLAB])
    qw = qw_ref[...]
    kw = kw_ref[...]
    g_row = lax.broadcasted_iota(jnp.int32, (N_HEADS * SLAB, LANES), 0)
    g_col = lax.broadcasted_iota(jnp.int32, (N_HEADS * SLAB, LANES), 1)
    group = ((g_row >= g_col * SLAB) & (g_row < (g_col + 1) * SLAB)).astype(BF16)
    q_inv = lax.rsqrt(_dot((q_all * q_all).astype(BF16), group) * (1.0 / HEAD_DIM) + EPS)
    k_inv = lax.rsqrt(_dot((k_all * k_all).astype(BF16), group) * (1.0 / HEAD_DIM) + EPS)
    for hd in range(N_HEADS):
        sl = slice(hd * SLAB, (hd + 1) * SLAB)
        qn = q_all[:, sl] * q_inv[:, hd:hd + 1] * qw
        kn = k_all[:, sl] * k_inv[:, hd:hd + 1] * kw
        c_hi, c_mid, c_lo = _split3(c[:, hd:hd + 1])
        qp = jnp.where(lane == L_C, c_hi, qn)
        qp = jnp.where(lane == L_C + 1, c_mid, qp)
        qp = jnp.where(lane == L_C + 2, c_lo, qp)
        qp = jnp.where((lane >= L_ONE) & (lane < L_ONE + 3), 1.0, qp)
        kp = jnp.where((lane >= L_C) & (lane < L_C + 3), 1.0, kn)
        kp = jnp.where(lane == L_ONE, -c_hi, kp)
        kp = jnp.where(lane == L_ONE + 1, -c_mid, kp)
        kp = jnp.where(lane == L_ONE + 2, -c_lo, kp)
        q_ref[0, hd] = qp.astype(BF16)
        k_ref[0, hd] = kp.astype(BF16)

    vt_all = lax.dot_general(wvt_ref[...], h, (((1,), (1,)), ((), ())), preferred_element_type=F32)
    feat = lax.broadcasted_iota(jnp.int32, (SLAB, tm), 0)
    for hd in range(N_HEADS):
        v_ref[0, hd] = jnp.where(feat == HEAD_DIM, 1.0,
                                 vt_all[hd * SLAB:(hd + 1) * SLAB, :]).astype(BF16)
    u_ref[...] = _dot(h, w_ref[:, O_U:O_U + D_REC])
    g_ref[...] = _dot(h, w_ref[:, O_G:O_G + D_REC])


def _in_proj(x2d, ln1_w, w_all, w_vt, bf_pad, qw_slab, kw_slab, batch, seq):
    t = x2d.shape[0]
    tiles_per_seq = seq // TM_IN
    slab_shape = jax.ShapeDtypeStruct((batch, N_HEADS, seq, SLAB), BF16)
    slab_spec = pl.BlockSpec((1, N_HEADS, TM_IN, SLAB),
                             lambda i: (i // tiles_per_seq, 0, i % tiles_per_seq, 0))
    slab_t_shape = jax.ShapeDtypeStruct((batch, N_HEADS, SLAB, seq), BF16)
    slab_t_spec = pl.BlockSpec((1, N_HEADS, SLAB, TM_IN),
                               lambda i: (i // tiles_per_seq, 0, 0, i % tiles_per_seq))
    row_spec = pl.BlockSpec((TM_IN, D_REC), lambda i: (i, 0))
    const = lambda shape: pl.BlockSpec(shape, lambda i: (0,) * len(shape))
    return pl.pallas_call(
        functools.partial(_in_proj_kernel, tiles_per_seq=tiles_per_seq),
        out_shape=(jax.ShapeDtypeStruct((t, D_REC), F32), jax.ShapeDtypeStruct((t, D_REC), F32),
                   slab_shape, slab_shape, slab_t_shape),
        grid=(t // TM_IN,),
        in_specs=[pl.BlockSpec((TM_IN, D_MODEL), lambda i: (i, 0)),
                  const((1, D_MODEL)), const((D_MODEL, N_PROJ)), const((N_HEADS * SLAB, D_MODEL)),
                  const((1, LANES)), const((1, SLAB)), const((1, SLAB))],
        out_specs=(row_spec, row_spec, slab_spec, slab_spec, slab_t_spec),
        scratch_shapes=[pltpu.VMEM((1, LANES), F32)],
        compiler_params=pltpu.CompilerParams(dimension_semantics=("arbitrary",),
                                             vmem_limit_bytes=VMEM_LIMIT),
        name="in_proj",
    )(x2d, ln1_w, w_all, w_vt, bf_pad, qw_slab, kw_slab)


R_CONV_B = CONV_WIDTH
R_B_A = CONV_WIDTH + 1
R_B_I = CONV_WIDTH + 2
R_LAMBDA = CONV_WIDTH + 3
R_NORM_W = CONV_WIDTH + 4
N_REC_ROWS = CONV_WIDTH + 5


def _rglru_kernel(u_ref, g_ref, vec_ref, wg_ref, o_ref, ext_ref, a_ref, b_ref, h_ref, hcar_ref):
    s = pl.program_id(1)
    ts = u_ref.shape[1]
    pad = 8
    row = lambda r: vec_ref[r:r + 1, :]

    @pl.when(s == 0)
    def _():
        ext_ref[0:pad, :] = jnp.zeros((pad, D_REC), F32)
        hcar_ref[...] = jnp.zeros_like(hcar_ref)

    @pl.when(s != 0)
    def _():
        ext_ref[0:pad, :] = ext_ref[ts:ts + pad, :]

    u = u_ref[0]
    ext_ref[pad:pad + ts, :] = u
    xc = row(R_CONV_B) + u * row(CONV_WIDTH - 1)
    for tap in range(CONV_WIDTH - 1):
        d = CONV_WIDTH - 1 - tap
        xc = xc + ext_ref[pad - d:pad - d + ts, :] * row(tap)

    gates = _dot(xc.astype(BF16), wg_ref[...])
    r = _sigmoid(gates[:, :D_REC] + row(R_B_A))
    ig = _sigmoid(gates[:, D_REC:] + row(R_B_I))
    nlam = -row(R_LAMBDA)
    softplus = jnp.maximum(nlam, 0.0) + jnp.log1p(jnp.exp(-jnp.abs(nlam)))
    log_a = -RGLRU_C * r * softplus
    a = jnp.exp(log_a)
    a_ref[...] = a
    b_ref[...] = jnp.sqrt(1.0 - jnp.exp(2.0 * log_a)) * (ig * xc)

    def step(t, h):
        h = a_ref[pl.ds(t, 1), :] * h + b_ref[pl.ds(t, 1), :]
        h_ref[pl.ds(t, 1), :] = h
        return h

    hcar_ref[...] = lax.fori_loop(0, ts, step, hcar_ref[...], unroll=8)

    rec = h_ref[...] * jax.nn.gelu(g_ref[0])
    ms = jnp.mean(rec * rec, axis=-1, keepdims=True)
    o_ref[0] = (rec * lax.rsqrt(ms + EPS) * row(R_NORM_W)).astype(BF16)


def _rglru(u, g, rec_vec, w_gates):
    batch, seq, _ = u.shape
    seq_spec = pl.BlockSpec((1, TS_REC, D_REC), lambda b, s: (b, s, 0))
    const = lambda shape: pl.BlockSpec(shape, lambda b, s: (0,) * len(shape))
    return pl.pallas_call(
        _rglru_kernel,
        out_shape=jax.ShapeDtypeStruct((batch, seq, D_REC), BF16),
        grid=(batch, seq // TS_REC),
        in_specs=[seq_spec, seq_spec, const((N_REC_ROWS, D_REC)), const((D_REC, 2 * D_REC))],
        out_specs=seq_spec,
        scratch_shapes=[pltpu.VMEM((TS_REC + 8, D_REC), F32), pltpu.VMEM((TS_REC, D_REC), F32),
                        pltpu.VMEM((TS_REC, D_REC), F32), pltpu.VMEM((TS_REC, D_REC), F32),
                        pltpu.VMEM((1, D_REC), F32)],
        compiler_params=pltpu.CompilerParams(dimension_semantics=("arbitrary", "arbitrary"),
                                             vmem_limit_bytes=VMEM_LIMIT),
        name="rglru",
    )(u, g, rec_vec, w_gates)


def _attn_kernel(q_ref, k_ref, v_ref, o_ref):
    qi = pl.program_id(2)
    hb = q_ref.shape[1]
    tq = q_ref.shape[2]
    qs = [q_ref[0, h] for h in range(hb)]

    def step(j, carry, masked):
        start = pl.multiple_of(j * TK, TK)
        scores = []
        for h in range(hb):
            kj = k_ref[0, h, pl.ds(start, TK), :]
            s = lax.dot_general(kj, qs[h], (((1,), (1,)), ((), ())), preferred_element_type=F32)
            if masked:
                key = lax.broadcasted_iota(jnp.int32, (TK, tq), 0)
                qry = lax.broadcasted_iota(jnp.int32, (TK, tq), 1)
                s = jnp.where(qry >= key, s, -jnp.inf)
            scores.append(s)
        m_news = [jnp.maximum(carry[h][0], jnp.max(scores[h], axis=0, keepdims=True))
                  for h in range(hb)]
        probs = [jnp.exp(scores[h] - m_news[h]).astype(BF16) for h in range(hb)]
        out = []
        for h in range(hb):
            m, acc = carry[h]
            vj = v_ref[0, h, :, pl.ds(start, TK)]
            out.append((m_news[h], jnp.exp(m - m_news[h]) * acc + _dot(vj, probs[h])))
        return tuple(out)

    init = tuple((jnp.full((1, tq), -jnp.inf, F32), jnp.zeros((SLAB, tq), F32)) for _ in range(hb))
    carry = lax.fori_loop(0, qi, lambda j, c: step(j, c, False), init)
    carry = step(qi, carry, True)
    feat = lax.broadcasted_iota(jnp.int32, (SLAB, tq), 0)
    for h in range(hb):
        acc = carry[h][1]
        out_t = jnp.where(feat < HEAD_DIM, acc / acc[HEAD_DIM:HEAD_DIM + 1, :], 0.0)
        o_ref[0, :, h * SLAB:(h + 1) * SLAB] = out_t.T


def _attention(qp, kp, vp):
    batch, _, seq, _ = qp.shape
    hb = HEADS_PER_STEP
    kv_spec = pl.BlockSpec((1, hb, seq, SLAB), lambda b, h, i: (b, h, 0, 0))
    vt_spec = pl.BlockSpec((1, hb, SLAB, seq), lambda b, h, i: (b, h, 0, 0))
    return pl.pallas_call(
        _attn_kernel,
        out_shape=jax.ShapeDtypeStruct((batch, seq, N_HEADS * SLAB), F32),
        grid=(batch, N_HEADS // hb, seq // TQ),
        in_specs=[pl.BlockSpec((1, hb, TQ, SLAB), lambda b, h, i: (b, h, i, 0)), kv_spec, vt_spec],
        out_specs=pl.BlockSpec((1, TQ, hb * SLAB), lambda b, h, i: (b, i, h)),
        compiler_params=pltpu.CompilerParams(
            dimension_semantics=("arbitrary", "arbitrary", "arbitrary"),
            vmem_limit_bytes=VMEM_LIMIT),
        name="attn",
    )(qp, kp, vp)


def _slot_onehot(slots, first, width):
    tm = slots[0].shape[0]
    col = lax.broadcasted_iota(jnp.int32, (tm, width), 1) + first
    hit = jnp.zeros((tm, width), F32)
    for slot in slots:
        hit = jnp.where(col == slot, 1.0, hit)
    return hit.astype(BF16)


def _out_proj_kernel(x_ref, rec_ref, att_ref, aw_ref, wr_ref, wa_ref, ln_ref, rwh_ref, rwl_ref, rb_ref,
                     x1_ref, xs_ref, meta_ref, units_ref):
    tm = x_ref.shape[0]

    att = att_ref[...]
    ms = jnp.sum(att * att, axis=-1, keepdims=True) * (1.0 / D_ATT)
    att_n = (att * lax.rsqrt(ms + EPS) * aw_ref[...]).astype(BF16)
    x1 = x_ref[...] + _dot(rec_ref[...], wr_ref[...]) + _dot(att_n, wa_ref[...])
    x1_ref[...] = x1

    ms2 = jnp.mean(x1 * x1, axis=-1, keepdims=True)
    h2 = x1 * lax.rsqrt(ms2 + EPS) * ln_ref[...]

    h_hi = h2.astype(BF16)
    h_lo = (h2 - h_hi.astype(F32)).astype(BF16)
    logits = (_dot(h_hi, rwh_ref[...]) + _dot(h_lo, rwh_ref[...]) + _dot(h_hi, rwl_ref[...])
              + rb_ref[...])

    lane = lax.broadcasted_iota(jnp.int32, (tm, LANES), 1)
    lane_f = lane.astype(F32)
    l = jnp.where(lane < N_EXPERTS, logits, -jnp.inf)

    vals, idxs, sels = [], [], []
    for _ in range(TOP_K):
        m = jnp.max(l, axis=-1, keepdims=True)
        idx = jnp.min(jnp.where(l == m, lane_f, float(LANES)), axis=-1, keepdims=True)
        sel = lane_f == idx
        vals.append(m)
        idxs.append(idx)
        sels.append(sel)
        l = jnp.where(sel, -jnp.inf, l)
    exps = [jnp.exp(v - vals[0]) for v in vals]
    denom = exps[0] + exps[1] + exps[2] + exps[3]

    any_sel = sels[0] | sels[1] | sels[2] | sels[3]
    onehot = jnp.where(any_sel, 1.0, 0.0)
    row = lax.broadcasted_iota(jnp.int32, (tm, tm), 0)
    col = lax.broadcasted_iota(jnp.int32, (tm, tm), 1)
    before = (row > col).astype(BF16)
    rank = _dot(before, onehot.astype(BF16))
    count = jnp.sum(onehot, axis=0, keepdims=True)
    units = jnp.floor((count + (UNIT - 1)) * (1.0 / UNIT))
    e_row = lax.broadcasted_iota(jnp.int32, (LANES, LANES), 0)
    e_col = lax.broadcasted_iota(jnp.int32, (LANES, LANES), 1)
    earlier = (e_row < e_col).astype(BF16)
    units8 = jnp.broadcast_to(units, (8, LANES))
    first_unit = _dot(units8.astype(BF16), earlier)[0:1, :]
    units_ref[0] = units8
    place = first_unit * float(UNIT) + rank

    meta = jnp.zeros((tm, LANES), F32)
    slots = []
    for k in range(TOP_K):
        slot = jnp.sum(jnp.where(sels[k], place, 0.0), axis=-1, keepdims=True)
        slots.append(slot.astype(jnp.int32))
        meta = jnp.where(lane == k, idxs[k], meta)
        meta = jnp.where(lane == TOP_K + k, exps[k] / denom, meta)
        meta = jnp.where(lane == 2 * TOP_K + k, slot, meta)
    meta_ref[...] = meta

    for c in range(TILE_SLOTS // SLOT_CHUNK):
        sel_t = _slot_onehot(slots, c * SLOT_CHUNK, SLOT_CHUNK)
        xs_ref[c * SLOT_CHUNK:(c + 1) * SLOT_CHUNK, :] = _dot_t(sel_t, h_hi).astype(BF16)


def _out_proj(x2d, rec_n, att, aw_slab, w_rec, w_att, ln2_w, rw_hi, rw_lo, rb_pad):
    t = x2d.shape[0]
    n_tiles = t // TM_OUT
    rows = lambda n: pl.BlockSpec((TM_OUT, n), lambda i: (i, 0))
    const = lambda shape: pl.BlockSpec(shape, lambda i: (0,) * len(shape))
    return pl.pallas_call(
        _out_proj_kernel,
        out_shape=(jax.ShapeDtypeStruct((t, D_MODEL), F32),
                   jax.ShapeDtypeStruct((n_tiles * TILE_SLOTS, D_MODEL), BF16),
                   jax.ShapeDtypeStruct((t, LANES), F32),
                   jax.ShapeDtypeStruct((n_tiles, 8, LANES), F32)),
        grid=(n_tiles,),
        in_specs=[rows(D_MODEL), rows(D_REC), rows(N_HEADS * SLAB), const((1, N_HEADS * SLAB)),
                  const((D_REC, D_MODEL)), const((N_HEADS * SLAB, D_MODEL)), const((1, D_MODEL)),
                  const((D_MODEL, LANES)), const((D_MODEL, LANES)), const((1, LANES))],
        out_specs=(rows(D_MODEL), pl.BlockSpec((TILE_SLOTS, D_MODEL), lambda i: (i, 0)), rows(LANES),
                   pl.BlockSpec((1, 8, LANES), lambda i: (i, 0, 0))),
        compiler_params=pltpu.CompilerParams(dimension_semantics=("arbitrary",),
                                             vmem_limit_bytes=VMEM_LIMIT),
        name="out_proj",
    )(x2d, rec_n, att, aw_slab, w_rec, w_att, ln2_w, rw_hi, rw_lo, rb_pad)


def _start_units(unit_ref, base, n_units, src_hbm, dst, sem):
    def body(j, carry):
        src_row = pl.multiple_of(unit_ref[base + j] * UNIT, UNIT)
        dst_row = pl.multiple_of(j * UNIT, UNIT)
        pltpu.make_async_copy(src_hbm.at[pl.ds(src_row, UNIT), :],
                              dst.at[pl.ds(dst_row, UNIT), :], sem).start()
        return carry
    lax.fori_loop(0, n_units, body, 0, unroll=8)


def _wait_units(n_units, src_hbm, dst, sem):
    pltpu.make_async_copy(src_hbm.at[pl.ds(0, n_units * UNIT), :], dst, sem).wait()


def _experts_kernel(first_ref, count_ref, nblk_ref, unit_ref, order_ref, next_ref, owner_ref,
                    xs_hbm, wgu_hbm, bgu_ref, wd_hbm, bd_ref, ys_hbm,
                    xbuf, obuf, wgu_f32, wd_f32, wgu_bf, wd_bf, xsem, osem, wsem):
    e = pl.program_id(0)
    nblk = nblk_ref[0]
    first = first_ref[e]

    def weight_copies(expert, wslot):
        return (pltpu.make_async_copy(wgu_hbm.at[expert], wgu_f32.at[wslot], wsem.at[wslot]),
                pltpu.make_async_copy(wd_hbm.at[expert], wd_f32.at[wslot], wsem.at[wslot]))

    def out_copy(block, slot):
        rows = pl.ds(pl.multiple_of(block * TM_E, TM_E), TM_E)
        return pltpu.make_async_copy(obuf.at[slot], ys_hbm.at[rows, :], osem.at[slot])

    @pl.when(e == 0)
    def _():
        _start_units(unit_ref, 0, BLOCK_UNITS, xs_hbm, xbuf.at[0], xsem.at[0])
        for copy in weight_copies(owner_ref[0], 0):
            copy.start()

    @pl.when(count_ref[e] > 0)
    def _():
        wslot = order_ref[e] % 2
        for copy in weight_copies(e, wslot):
            copy.wait()
        wgu_bf[...] = wgu_f32[wslot].astype(BF16)
        wd_bf[...] = wd_f32[wslot].astype(BF16)
        nxt = next_ref[e]

        @pl.when(nxt >= 0)
        def _():
            for copy in weight_copies(nxt, 1 - wslot):
                copy.start()

        def block_step(k, carry):
            b = first + k
            slot = b % 2

            @pl.when(b + 1 < nblk)
            def _():
                _start_units(unit_ref, (b + 1) * BLOCK_UNITS, BLOCK_UNITS, xs_hbm,
                             xbuf.at[1 - slot], xsem.at[1 - slot])

            _wait_units(BLOCK_UNITS, xs_hbm, xbuf.at[slot], xsem.at[slot])
            gu = _dot(xbuf[slot], wgu_bf[...]) + bgu_ref[0]
            g = jnp.minimum(gu[:, :D_EXPERT], SWIGLU_LIMIT)
            u = jnp.clip(gu[:, D_EXPERT:], -SWIGLU_LIMIT, SWIGLU_LIMIT)
            act = (u + 1.0) * (g * _sigmoid(SWIGLU_ALPHA * g))
            y = (_dot(act.astype(BF16), wd_bf[...]) + bd_ref[0]).astype(BF16)

            @pl.when(b >= 2)
            def _():
                out_copy(b - 2, slot).wait()

            obuf[slot] = y
            out_copy(b, slot).start()
            return carry

        lax.fori_loop(0, count_ref[e], block_step, 0)

    @pl.when(e == pl.num_programs(0) - 1)
    def _():
        @pl.when(nblk >= 2)
        def _():
            out_copy(nblk - 2, nblk % 2).wait()

        out_copy(nblk - 1, (nblk - 1) % 2).wait()


def _experts(region_first, region_blocks, n_blocks_used, unit_src, expert_order, expert_next,
             first_owner, xs, w_gate_up, b_gate_up, w_down, b_down, n_blocks):
    bias_map = lambda e, *_: (e, 0, 0)
    grid_spec = pltpu.PrefetchScalarGridSpec(
        num_scalar_prefetch=7,
        grid=(N_EXPERTS,),
        in_specs=[pl.BlockSpec(memory_space=pl.ANY),
                  pl.BlockSpec(memory_space=pl.ANY),
                  pl.BlockSpec((1, 1, 2 * D_EXPERT), bias_map),
                  pl.BlockSpec(memory_space=pl.ANY),
                  pl.BlockSpec((1, 1, D_MODEL), bias_map)],
        out_specs=pl.BlockSpec(memory_space=pl.ANY),
        scratch_shapes=[pltpu.VMEM((2, TM_E, D_MODEL), BF16),
                        pltpu.VMEM((2, TM_E, D_MODEL), BF16),
                        pltpu.VMEM((2, D_MODEL, 2 * D_EXPERT), F32),
                        pltpu.VMEM((2, D_EXPERT, D_MODEL), F32),
                        pltpu.VMEM((D_MODEL, 2 * D_EXPERT), BF16),
                        pltpu.VMEM((D_EXPERT, D_MODEL), BF16),
                        pltpu.SemaphoreType.DMA((2,)),
                        pltpu.SemaphoreType.DMA((2,)),
                        pltpu.SemaphoreType.DMA((2,))],
    )
    return pl.pallas_call(
        _experts_kernel,
        out_shape=jax.ShapeDtypeStruct((n_blocks * TM_E, D_MODEL), BF16),
        grid_spec=grid_spec,
        compiler_params=pltpu.CompilerParams(dimension_semantics=("arbitrary",),
                                             vmem_limit_bytes=VMEM_LIMIT),
        name="experts",
    )(region_first, region_blocks, n_blocks_used, unit_src, expert_order, expert_next, first_owner,
      xs, w_gate_up, b_gate_up.reshape(N_EXPERTS, 1, 2 * D_EXPERT), w_down,
      b_down.reshape(N_EXPERTS, 1, D_MODEL))


def _combine_kernel(unit_ref, x1_ref, meta_ref, ys_hbm, o_ref, buf, sem):
    i = pl.program_id(0)
    n = pl.num_programs(0)
    slot = i % 2
    tm = x1_ref.shape[0]

    @pl.when(i == 0)
    def _():
        _start_units(unit_ref, 0, TILE_UNITS, ys_hbm, buf.at[0], sem.at[0])

    @pl.when(i + 1 < n)
    def _():
        _start_units(unit_ref, (i + 1) * TILE_UNITS, TILE_UNITS, ys_hbm, buf.at[1 - slot],
                     sem.at[1 - slot])

    meta = meta_ref[...]
    slots = [meta[:, 2 * TOP_K + k:2 * TOP_K + k + 1].astype(jnp.int32) for k in range(TOP_K)]
    gates = [meta[:, TOP_K + k:TOP_K + k + 1] for k in range(TOP_K)]
    _wait_units(TILE_UNITS, ys_hbm, buf.at[slot], sem.at[slot])

    acc = x1_ref[...]
    for c in range(TILE_SLOTS // SLOT_CHUNK):
        first = c * SLOT_CHUNK
        col = lax.broadcasted_iota(jnp.int32, (tm, SLOT_CHUNK), 1) + first
        weight = jnp.zeros((tm, SLOT_CHUNK), F32)
        for k in range(TOP_K):
            weight = jnp.where(col == slots[k], gates[k], weight)
        acc = acc + _dot(weight.astype(BF16), buf[slot, first:first + SLOT_CHUNK, :])
    o_ref[...] = acc


def _combine(tile_units, x1, meta, ys):
    t = x1.shape[0]
    grid_spec = pltpu.PrefetchScalarGridSpec(
        num_scalar_prefetch=1,
        grid=(t // TM_OUT,),
        in_specs=[pl.BlockSpec((TM_OUT, D_MODEL), lambda i, us: (i, 0)),
                  pl.BlockSpec((TM_OUT, LANES), lambda i, us: (i, 0)),
                  pl.BlockSpec(memory_space=pl.ANY)],
        out_specs=pl.BlockSpec((TM_OUT, D_MODEL), lambda i, us: (i, 0)),
        scratch_shapes=[pltpu.VMEM((2, TILE_SLOTS, D_MODEL), BF16),
                        pltpu.SemaphoreType.DMA((2,))],
    )
    return pl.pallas_call(
        _combine_kernel,
        out_shape=jax.ShapeDtypeStruct((t, D_MODEL), F32),
        grid_spec=grid_spec,
        compiler_params=pltpu.CompilerParams(dimension_semantics=("arbitrary",),
                                             vmem_limit_bytes=VMEM_LIMIT),
        name="combine",
    )(tile_units, x1, meta, ys)


def _head_slabs(w):
    d = w.shape[0]
    w = w.reshape(d, -1, HEAD_DIM)
    return jnp.pad(w, ((0, 0), (0, 0), (0, SLAB - HEAD_DIM))).reshape(d, -1)


def _gate_matrix(w_a, w_i):
    g = w_a.shape[0]
    eye = jnp.eye(g, dtype=w_a.dtype)
    both = jnp.stack([w_a, w_i])
    return jnp.einsum('sgij,gh->gishj', both, eye).reshape(g * REC_BLOCK, 2 * g * REC_BLOCK)


def _pad_lanes(v, n=LANES):
    return jnp.pad(v, (0, n - v.shape[0])).reshape(1, n)


def _excl_cumsum(a, axis):
    return jnp.cumsum(a, axis=axis) - a


def _pick(onehot, table):
    if table.ndim == 1:
        return jnp.sum(onehot * table[None, :], axis=1)
    return jnp.sum(onehot[:, :, None] * table[None, :, :], axis=1)


def _dispatch_tables(units, n_blocks):
    n_tiles = units.shape[0]
    i32 = jnp.int32
    tile_first = _excl_cumsum(units, 1)
    tile_end = tile_first + units
    run_first = _excl_cumsum(units, 0)
    run_end = run_first + units
    region_units = jnp.sum(units, axis=0)
    region_blocks = (region_units + BLOCK_UNITS - 1) // BLOCK_UNITS
    block_end = jnp.cumsum(region_blocks)
    region_first = (block_end - region_blocks) * BLOCK_UNITS
    n_used = block_end[-1:]
    experts = jnp.arange(N_EXPERTS, dtype=i32)
    tiles = jnp.arange(n_tiles, dtype=i32)

    g = jnp.arange(n_blocks * BLOCK_UNITS, dtype=i32)
    e = jnp.minimum(jnp.sum((block_end[None, :] * BLOCK_UNITS <= g[:, None]).astype(i32), axis=1),
                    N_EXPERTS - 1)
    e_hot = (e[:, None] == experts[None, :]).astype(i32)
    local = g - _pick(e_hot, region_first)
    valid = (local < _pick(e_hot, region_units)) & (g < n_used[0] * BLOCK_UNITS)
    ends = _pick(e_hot, run_end.T)
    tile = jnp.minimum(jnp.sum((ends <= local[:, None]).astype(i32), axis=1), n_tiles - 1)
    t_hot = (tile[:, None] == tiles[None, :]).astype(i32)
    src = (tile * TILE_UNITS + jnp.sum(_pick(t_hot, tile_first - run_first) * e_hot, axis=1) + local)
    unit_src = jnp.where(valid, src, 0).astype(i32)

    j = jnp.arange(TILE_UNITS, dtype=i32)
    ej = jnp.minimum(jnp.sum((tile_end[:, None, :] <= j[None, :, None]).astype(i32), axis=2),
                     N_EXPERTS - 1)
    ej_hot = (ej[:, :, None] == experts[None, None, :]).astype(i32)
    used = j[None, :] < jnp.sum(units, axis=1)[:, None]
    base = region_first[None, :] + run_first - tile_first
    dst = jnp.sum(ej_hot * base[:, None, :], axis=2) + j[None, :]
    tile_units = jnp.where(used, dst, 0).astype(i32).reshape(-1)
    owns = region_blocks > 0
    expert_order = _excl_cumsum(owns.astype(i32), 0)
    later = (experts[None, :] > experts[:, None]) & owns[None, :]
    expert_next = jnp.min(jnp.where(later, experts[None, :], N_EXPERTS), axis=1)
    expert_next = jnp.where(expert_next < N_EXPERTS, expert_next, -1).astype(i32)
    first_owner = jnp.min(jnp.where(owns, experts, N_EXPERTS - 1), keepdims=True).astype(i32)
    return ((block_end - region_blocks).astype(i32), region_blocks.astype(i32), n_used.astype(i32),
            unit_src, tile_units, expert_order.astype(i32), expert_next, first_owner)


def _layer(x, ln1_w, w_in, conv_w, conv_b, w_rg_a, b_rg_a, w_rg_i, b_rg_i, rg_lambda, b_forget,
           q_norm_w, k_norm_w, rec_out_norm_w, attn_out_norm_w, w_out, ln2_w, w_router, b_router,
           w_gate_up, b_gate_up, w_down, b_down):
    batch, seq, _ = x.shape
    t = batch * seq
    x2d = x.reshape(t, D_MODEL)

    o_q = 2 * D_REC
    o_k = o_q + D_ATT
    o_v = o_k + D_ATT
    o_f = o_v + D_ATT
    slabs = _head_slabs(w_in[:, o_q:o_f])
    w_all = jnp.concatenate(
        [w_in[:, :o_q], slabs[:, :2 * N_HEADS * SLAB],
         jnp.pad(w_in[:, o_f:], ((0, 0), (0, LANES - N_HEADS)))],
        axis=1).astype(BF16)
    w_vt = slabs[:, 2 * N_HEADS * SLAB:].T.astype(BF16)
    scale = HEAD_DIM ** -0.5
    u, g, qp, kp, vp = _in_proj(x2d, ln1_w.reshape(1, D_MODEL), w_all, w_vt, _pad_lanes(b_forget),
                                _pad_lanes(q_norm_w * scale), _pad_lanes(k_norm_w), batch, seq)

    rec_vec = jnp.concatenate(
        [conv_w, jnp.stack([conv_b, b_rg_a, b_rg_i, rg_lambda, rec_out_norm_w])])
    rec_n = _rglru(u.reshape(batch, seq, D_REC), g.reshape(batch, seq, D_REC), rec_vec,
                   _gate_matrix(w_rg_a, w_rg_i).astype(BF16))

    att = _attention(qp, kp, vp)

    aw_slab = _head_slabs(attn_out_norm_w.reshape(1, D_ATT))
    w_att = jnp.pad(w_out[D_REC:].reshape(N_HEADS, HEAD_DIM, D_MODEL),
                    ((0, 0), (0, SLAB - HEAD_DIM), (0, 0))).reshape(N_HEADS * SLAB, D_MODEL)
    rw = jnp.pad(w_router, ((0, 0), (0, LANES - N_EXPERTS)))
    rw_hi = rw.astype(BF16)
    rw_lo = (rw - rw_hi.astype(F32)).astype(BF16)
    x1, xs, meta, units = _out_proj(
        x2d, rec_n.reshape(t, D_REC), att.reshape(t, N_HEADS * SLAB), aw_slab,
        w_out[:D_REC].astype(BF16), w_att.astype(BF16), ln2_w.reshape(1, D_MODEL),
        rw_hi, rw_lo, _pad_lanes(b_router))

    n_tiles = t // TM_OUT
    max_units = t * TOP_K // UNIT + n_tiles * N_EXPERTS
    n_blocks = -(-max_units // BLOCK_UNITS) + N_EXPERTS
    (region_first, region_blocks, n_used, unit_src, tile_units, expert_order, expert_next,
     first_owner) = _dispatch_tables(units[:, 0, :N_EXPERTS].astype(jnp.int32), n_blocks)

    ys = _experts(region_first, region_blocks, n_used, unit_src, expert_order, expert_next,
                  first_owner, xs, w_gate_up, b_gate_up, w_down, b_down, n_blocks)
    out = _combine(tile_units, x1, meta, ys)
    return out.reshape(batch, seq, D_MODEL)


def kernel(x, ln1_w, w_in, conv_w, conv_b, w_rg_a, b_rg_a, w_rg_i, b_rg_i, rg_lambda, b_forget,
           q_norm_w, k_norm_w, rec_out_norm_w, attn_out_norm_w, w_out, ln2_w, w_router, b_router,
           w_gate_up, b_gate_up, w_down, b_down):
    for l in range(ln1_w.shape[0]):
        x = _layer(x, ln1_w[l], w_in[l], conv_w[l], conv_b[l], w_rg_a[l], b_rg_a[l], w_rg_i[l],
                   b_rg_i[l], rg_lambda[l], b_forget[l], q_norm_w[l], k_norm_w[l], rec_out_norm_w[l],
                   attn_out_norm_w[l], w_out[l], ln2_w[l], w_router[l], b_router[l], w_gate_up[l],
                   b_gate_up[l], w_down[l], b_down[l])
    return x
```

```python
import functools

import jax
import jax.numpy as jnp
from jax import lax
from jax.experimental import pallas as pl
from jax.experimental.pallas import tpu as pltpu

D_MODEL = 1024
D_REC = 512
N_REC_BLOCKS = 8
REC_BLOCK = 64
CONV_WIDTH = 4
RGLRU_C = 8.0
D_ATT = 512
HEAD_DIM = 64
N_HEADS = 8
N_EXPERTS = 32
TOP_K = 4
D_EXPERT = 1024
SWIGLU_LIMIT = 7.0
SWIGLU_ALPHA = 1.702
EPS = 1e-6

LANES = 128
SLAB = LANES
VMEM_LIMIT = 56 * 1024 * 1024

F32 = jnp.float32
BF16 = jnp.bfloat16

O_U = 0
O_G = O_U + D_REC
O_K = O_G + D_REC
N_PROJ = O_K + N_HEADS * SLAB

L_C = HEAD_DIM
L_ONE = HEAD_DIM + 3

TM_IN = 512
TS_REC = 512
TQ = 512
TK = 512
HEADS_PER_STEP = 8
TM_OUT = 512
TM_E = 512

UNIT = 16
TILE_SLOTS = -(-(TM_OUT * TOP_K + N_EXPERTS * (UNIT - 1)) // LANES) * LANES
TILE_UNITS = TILE_SLOTS // UNIT
BLOCK_UNITS = TM_E // UNIT
SLOT_CHUNK = 512


def _split3(v):
    hi = v.astype(BF16).astype(F32)
    r = v - hi
    mid = r.astype(BF16).astype(F32)
    lo = (r - mid).astype(BF16).astype(F32)
    return hi, mid, lo


def _dot(a, b):
    return jnp.dot(a, b, preferred_element_type=F32)


def _sigmoid(x):
    return 0.5 * jnp.tanh(0.5 * x) + 0.5


def _dot_t(a, b):
    return lax.dot_general(a, b, (((0,), (0,)), ((), ())), preferred_element_type=F32)


def _in_proj_kernel(x_ref, ln_ref, w_ref, wqt_ref, wvt_ref, wft_ref, bf_ref, qw_ref, kw_ref,
                    u_ref, g_ref, q_ref, k_ref, v_ref, carry_ref, *, tiles_per_seq):
    i = pl.program_id(0)
    tm = x_ref.shape[0]

    @pl.when(i % tiles_per_seq == 0)
    def _():
        carry_ref[...] = jnp.zeros_like(carry_ref)

    x = x_ref[...]
    ms = jnp.mean(x * x, axis=-1, keepdims=True)
    h = (x * lax.rsqrt(ms + EPS) * ln_ref[...]).astype(BF16)
    nt = (((1,), (1,)), ((), ()))

    f_t = lax.dot_general(wft_ref[...], h, nt, preferred_element_type=F32) + bf_ref[...]
    logf_t = jnp.minimum(f_t, 0.0) - jnp.log1p(jnp.exp(-jnp.abs(f_t)))
    row = lax.broadcasted_iota(jnp.int32, (tm, tm), 0)
    col = lax.broadcasted_iota(jnp.int32, (tm, tm), 1)
    upto = (row <= col).astype(BF16)
    hi, mid, lo = _split3(logf_t)
    c_t = (_dot(hi.astype(BF16), upto) + _dot(mid.astype(BF16), upto)
           + _dot(lo.astype(BF16), upto) + carry_ref[...])
    carry_ref[...] = c_t[:, tm - 1:tm]
    c = c_t.T

    qt_all = lax.dot_general(wqt_ref[...], h, nt, preferred_element_type=F32)
    spare = lax.broadcasted_iota(jnp.int32, (SLAB - HEAD_DIM, tm), 0)
    qw = qw_ref[...]
    for hd in range(N_HEADS):
        qt = qt_all[hd * HEAD_DIM:(hd + 1) * HEAD_DIM, :]
        inv = lax.rsqrt(jnp.sum(qt * qt, axis=0, keepdims=True) * (1.0 / HEAD_DIM) + EPS)
        c_hi, c_mid, c_lo = _split3(c_t[hd:hd + 1, :])
        extra = jnp.where(spare == L_C - HEAD_DIM, c_hi, 0.0)
        extra = jnp.where(spare == L_C + 1 - HEAD_DIM, c_mid, extra)
        extra = jnp.where(spare == L_C + 2 - HEAD_DIM, c_lo, extra)
        extra = jnp.where((spare >= L_ONE - HEAD_DIM) & (spare < L_ONE + 3 - HEAD_DIM), 1.0, extra)
        q_ref[0, hd] = jnp.concatenate([qt * inv * qw, extra], axis=0).astype(BF16)

    lane = lax.broadcasted_iota(jnp.int32, (tm, SLAB), 1)
    k_all = _dot(h, w_ref[:, O_K:O_K + N_HEADS * SLAB])
    kw = kw_ref[...]
    g_row = lax.broadcasted_iota(jnp.int32, (N_HEADS * SLAB, LANES), 0)
    g_col = lax.broadcasted_iota(jnp.int32, (N_HEADS * SLAB, LANES), 1)
    group = ((g_row >= g_col * SLAB) & (g_row < (g_col + 1) * SLAB)).astype(BF16)
    k_inv = lax.rsqrt(_dot((k_all * k_all).astype(BF16), group) * (1.0 / HEAD_DIM) + EPS)
    for hd in range(N_HEADS):
        sl = slice(hd * SLAB, (hd + 1) * SLAB)
        kn = k_all[:, sl] * k_inv[:, hd:hd + 1] * kw
        c_hi, c_mid, c_lo = _split3(c[:, hd:hd + 1])
        kp = jnp.where((lane >= L_C) & (lane < L_C + 3), 1.0, kn)
        kp = jnp.where(lane == L_ONE, -c_hi, kp)
        kp = jnp.where(lane == L_ONE + 1, -c_mid, kp)
        kp = jnp.where(lane == L_ONE + 2, -c_lo, kp)
        k_ref[0, hd] = kp.astype(BF16)

    vt_all = lax.dot_general(wvt_ref[...], h, (((1,), (1,)), ((), ())), preferred_element_type=F32)
    spare = lax.broadcasted_iota(jnp.int32, (SLAB - HEAD_DIM, tm), 0)
    ones_row = jnp.where(spare == 0, 1.0, 0.0)
    for hd in range(N_HEADS):
        v_ref[0, hd] = jnp.concatenate(
            [vt_all[hd * HEAD_DIM:(hd + 1) * HEAD_DIM, :], ones_row], axis=0).astype(BF16)
    u_ref[...] = _dot(h, w_ref[:, O_U:O_U + D_REC])
    g_ref[...] = _dot(h, w_ref[:, O_G:O_G + D_REC])


def _in_proj(x2d, ln1_w, w_all, w_qt, w_vt, w_ft, bf_col, qw_col, kw_slab, batch, seq):
    t = x2d.shape[0]
    tiles_per_seq = seq // TM_IN
    slab_shape = jax.ShapeDtypeStruct((batch, N_HEADS, seq, SLAB), BF16)
    slab_spec = pl.BlockSpec((1, N_HEADS, TM_IN, SLAB),
                             lambda i: (i // tiles_per_seq, 0, i % tiles_per_seq, 0))
    slab_t_shape = jax.ShapeDtypeStruct((batch, N_HEADS, SLAB, seq), BF16)
    slab_t_spec = pl.BlockSpec((1, N_HEADS, SLAB, TM_IN),
                               lambda i: (i // tiles_per_seq, 0, 0, i % tiles_per_seq))
    row_spec = pl.BlockSpec((TM_IN, D_REC), lambda i: (i, 0))
    const = lambda shape: pl.BlockSpec(shape, lambda i: (0,) * len(shape))
    return pl.pallas_call(
        functools.partial(_in_proj_kernel, tiles_per_seq=tiles_per_seq),
        out_shape=(jax.ShapeDtypeStruct((t, D_REC), F32), jax.ShapeDtypeStruct((t, D_REC), F32),
                   slab_t_shape, slab_shape, slab_t_shape),
        grid=(t // TM_IN,),
        in_specs=[pl.BlockSpec((TM_IN, D_MODEL), lambda i: (i, 0)),
                  const((1, D_MODEL)), const((D_MODEL, N_PROJ)), const((D_ATT, D_MODEL)),
                  const((D_ATT, D_MODEL)), const((LANES, D_MODEL)), const((LANES, 1)),
                  const((HEAD_DIM, 1)), const((1, SLAB))],
        out_specs=(row_spec, row_spec, slab_t_spec, slab_spec, slab_t_spec),
        scratch_shapes=[pltpu.VMEM((LANES, 1), F32)],
        compiler_params=pltpu.CompilerParams(dimension_semantics=("arbitrary",),
                                             vmem_limit_bytes=VMEM_LIMIT),
        name="in_proj",
    )(x2d, ln1_w, w_all, w_qt, w_vt, w_ft, bf_col, qw_col, kw_slab)


R_CONV_B = CONV_WIDTH
R_B_A = CONV_WIDTH + 1
R_B_I = CONV_WIDTH + 2
R_LAMBDA = CONV_WIDTH + 3
R_NORM_W = CONV_WIDTH + 4
N_REC_ROWS = CONV_WIDTH + 5


def _rglru_kernel(u_ref, g_ref, vec_ref, wg_ref, o_ref, ext_ref, a_ref, b_ref, h_ref, hcar_ref):
    s = pl.program_id(1)
    ts = u_ref.shape[1]
    pad = 8
    row = lambda r: vec_ref[r:r + 1, :]

    @pl.when(s == 0)
    def _():
        ext_ref[0:pad, :] = jnp.zeros((pad, D_REC), F32)
        hcar_ref[...] = jnp.zeros_like(hcar_ref)

    @pl.when(s != 0)
    def _():
        ext_ref[0:pad, :] = ext_ref[ts:ts + pad, :]

    u = u_ref[0]
    ext_ref[pad:pad + ts, :] = u
    xc = row(R_CONV_B) + u * row(CONV_WIDTH - 1)
    for tap in range(CONV_WIDTH - 1):
        d = CONV_WIDTH - 1 - tap
        xc = xc + ext_ref[pad - d:pad - d + ts, :] * row(tap)

    gates = _dot(xc.astype(BF16), wg_ref[...])
    r = _sigmoid(gates[:, :D_REC] + row(R_B_A))
    ig = _sigmoid(gates[:, D_REC:] + row(R_B_I))
    nlam = -row(R_LAMBDA)
    softplus = jnp.maximum(nlam, 0.0) + jnp.log1p(jnp.exp(-jnp.abs(nlam)))
    log_a = -RGLRU_C * r * softplus
    a = jnp.exp(log_a)
    a_ref[...] = a
    b_ref[...] = jnp.sqrt(1.0 - jnp.exp(2.0 * log_a)) * (ig * xc)

    def step(t, h):
        h = a_ref[pl.ds(t, 1), :] * h + b_ref[pl.ds(t, 1), :]
        h_ref[pl.ds(t, 1), :] = h
        return h

    hcar_ref[...] = lax.fori_loop(0, ts, step, hcar_ref[...], unroll=8)

    rec = h_ref[...] * jax.nn.gelu(g_ref[0])
    ms = jnp.mean(rec * rec, axis=-1, keepdims=True)
    o_ref[0] = (rec * lax.rsqrt(ms + EPS) * row(R_NORM_W)).astype(BF16)


def _rglru(u, g, rec_vec, w_gates):
    batch, seq, _ = u.shape
    seq_spec = pl.BlockSpec((1, TS_REC, D_REC), lambda b, s: (b, s, 0))
    const = lambda shape: pl.BlockSpec(shape, lambda b, s: (0,) * len(shape))
    return pl.pallas_call(
        _rglru_kernel,
        out_shape=jax.ShapeDtypeStruct((batch, seq, D_REC), BF16),
        grid=(batch, seq // TS_REC),
        in_specs=[seq_spec, seq_spec, const((N_REC_ROWS, D_REC)), const((D_REC, 2 * D_REC))],
        out_specs=seq_spec,
        scratch_shapes=[pltpu.VMEM((TS_REC + 8, D_REC), F32), pltpu.VMEM((TS_REC, D_REC), F32),
                        pltpu.VMEM((TS_REC, D_REC), F32), pltpu.VMEM((TS_REC, D_REC), F32),
                        pltpu.VMEM((1, D_REC), F32)],
        compiler_params=pltpu.CompilerParams(dimension_semantics=("arbitrary", "arbitrary"),
                                             vmem_limit_bytes=VMEM_LIMIT),
        name="rglru",
    )(u, g, rec_vec, w_gates)


def _attn_kernel(q_ref, k_ref, v_ref, o_ref):
    qi = pl.program_id(2)
    hb = q_ref.shape[1]
    tq = q_ref.shape[3]
    qs = [q_ref[0, h] for h in range(hb)]

    def step(j, carry, masked):
        start = pl.multiple_of(j * TK, TK)
        scores = []
        for h in range(hb):
            kj = k_ref[0, h, pl.ds(start, TK), :]
            s = _dot(kj, qs[h])
            if masked:
                key = lax.broadcasted_iota(jnp.int32, (TK, tq), 0)
                qry = lax.broadcasted_iota(jnp.int32, (TK, tq), 1)
                s = jnp.where(qry >= key, s, -jnp.inf)
            scores.append(s)
        m_news = [jnp.maximum(carry[h][0], jnp.max(scores[h], axis=0, keepdims=True))
                  for h in range(hb)]
        probs = [jnp.exp(scores[h] - m_news[h]).astype(BF16) for h in range(hb)]
        out = []
        for h in range(hb):
            m, acc = carry[h]
            vj = v_ref[0, h, :, pl.ds(start, TK)]
            out.append((m_news[h], jnp.exp(m - m_news[h]) * acc + _dot(vj, probs[h])))
        return tuple(out)

    init = tuple((jnp.full((1, tq), -jnp.inf, F32), jnp.zeros((SLAB, tq), F32)) for _ in range(hb))
    carry = lax.fori_loop(0, qi, lambda j, c: step(j, c, False), init)
    carry = step(qi, carry, True)
    feat = lax.broadcasted_iota(jnp.int32, (SLAB, tq), 0)
    for h in range(hb):
        acc = carry[h][1]
        out_t = jnp.where(feat < HEAD_DIM, acc / acc[HEAD_DIM:HEAD_DIM + 1, :], 0.0)
        o_ref[0, :, h * SLAB:(h + 1) * SLAB] = out_t.T


def _attention(qp, kp, vp):
    batch, _, seq, _ = kp.shape
    hb = HEADS_PER_STEP
    kv_spec = pl.BlockSpec((1, hb, seq, SLAB), lambda b, h, i: (b, h, 0, 0))
    vt_spec = pl.BlockSpec((1, hb, SLAB, seq), lambda b, h, i: (b, h, 0, 0))
    return pl.pallas_call(
        _attn_kernel,
        out_shape=jax.ShapeDtypeStruct((batch, seq, N_HEADS * SLAB), F32),
        grid=(batch, N_HEADS // hb, seq // TQ),
        in_specs=[pl.BlockSpec((1, hb, SLAB, TQ), lambda b, h, i: (b, h, 0, i)), kv_spec, vt_spec],
        out_specs=pl.BlockSpec((1, TQ, hb * SLAB), lambda b, h, i: (b, i, h)),
        compiler_params=pltpu.CompilerParams(
            dimension_semantics=("arbitrary", "arbitrary", "arbitrary"),
            vmem_limit_bytes=VMEM_LIMIT),
        name="attn",
    )(qp, kp, vp)


def _slot_onehot(slots, first, width):
    tm = slots[0].shape[0]
    col = lax.broadcasted_iota(jnp.int32, (tm, width), 1) + first
    hit = jnp.zeros((tm, width), F32)
    for slot in slots:
        hit = jnp.where(col == slot, 1.0, hit)
    return hit.astype(BF16)


def _out_proj_kernel(x_ref, rec_ref, att_ref, aw_ref, wr_ref, wa_ref, ln_ref, rwh_ref, rwl_ref, rb_ref,
                     x1_ref, xs_ref, meta_ref, units_ref):
    tm = x_ref.shape[0]

    att = att_ref[...]
    ms = jnp.sum(att * att, axis=-1, keepdims=True) * (1.0 / D_ATT)
    att_n = (att * lax.rsqrt(ms + EPS) * aw_ref[...]).astype(BF16)
    x1 = x_ref[...] + _dot(rec_ref[...], wr_ref[...]) + _dot(att_n, wa_ref[...])
    x1_ref[...] = x1

    ms2 = jnp.mean(x1 * x1, axis=-1, keepdims=True)
    h2 = x1 * lax.rsqrt(ms2 + EPS) * ln_ref[...]

    h_hi = h2.astype(BF16)
    h_lo = (h2 - h_hi.astype(F32)).astype(BF16)
    logits = (_dot(h_hi, rwh_ref[...]) + _dot(h_lo, rwh_ref[...]) + _dot(h_hi, rwl_ref[...])
              + rb_ref[...])

    lane = lax.broadcasted_iota(jnp.int32, (tm, LANES), 1)
    lane_f = lane.astype(F32)
    l = jnp.where(lane < N_EXPERTS, logits, -jnp.inf)

    vals, idxs, sels = [], [], []
    for _ in range(TOP_K):
        m = jnp.max(l, axis=-1, keepdims=True)
        idx = jnp.min(jnp.where(l == m, lane_f, float(LANES)), axis=-1, keepdims=True)
        sel = lane_f == idx
        vals.append(m)
        idxs.append(idx)
        sels.append(sel)
        l = jnp.where(sel, -jnp.inf, l)
    exps = [jnp.exp(v - vals[0]) for v in vals]
    denom = exps[0] + exps[1] + exps[2] + exps[3]

    any_sel = sels[0] | sels[1] | sels[2] | sels[3]
    onehot = jnp.where(any_sel, 1.0, 0.0)
    row = lax.broadcasted_iota(jnp.int32, (tm, tm), 0)
    col = lax.broadcasted_iota(jnp.int32, (tm, tm), 1)
    before = (row > col).astype(BF16)
    rank = _dot(before, onehot.astype(BF16))
    count = jnp.sum(onehot, axis=0, keepdims=True)
    units = jnp.floor((count + (UNIT - 1)) * (1.0 / UNIT))
    e_row = lax.broadcasted_iota(jnp.int32, (LANES, LANES), 0)
    e_col = lax.broadcasted_iota(jnp.int32, (LANES, LANES), 1)
    earlier = (e_row < e_col).astype(BF16)
    units8 = jnp.broadcast_to(units, (8, LANES))
    first_unit = _dot(units8.astype(BF16), earlier)[0:1, :]
    units_ref[0] = units8
    place = first_unit * float(UNIT) + rank

    meta = jnp.zeros((tm, LANES), F32)
    slots = []
    for k in range(TOP_K):
        slot = jnp.sum(jnp.where(sels[k], place, 0.0), axis=-1, keepdims=True)
        slots.append(slot.astype(jnp.int32))
        meta = jnp.where(lane == k, idxs[k], meta)
        meta = jnp.where(lane == TOP_K + k, exps[k] / denom, meta)
        meta = jnp.where(lane == 2 * TOP_K + k, slot, meta)
    meta_ref[...] = meta

    for c in range(TILE_SLOTS // SLOT_CHUNK):
        sel_t = _slot_onehot(slots, c * SLOT_CHUNK, SLOT_CHUNK)
        xs_ref[c * SLOT_CHUNK:(c + 1) * SLOT_CHUNK, :] = _dot_t(sel_t, h_hi).astype(BF16)


def _out_proj(x2d, rec_n, att, aw_slab, w_rec, w_att, ln2_w, rw_hi, rw_lo, rb_pad):
    t = x2d.shape[0]
    n_tiles = t // TM_OUT
    rows = lambda n: pl.BlockSpec((TM_OUT, n), lambda i: (i, 0))
    const = lambda shape: pl.BlockSpec(shape, lambda i: (0,) * len(shape))
    return pl.pallas_call(
        _out_proj_kernel,
        out_shape=(jax.ShapeDtypeStruct((t, D_MODEL), F32),
                   jax.ShapeDtypeStruct((n_tiles * TILE_SLOTS, D_MODEL), BF16),
                   jax.ShapeDtypeStruct((t, LANES), F32),
                   jax.ShapeDtypeStruct((n_tiles, 8, LANES), F32)),
        grid=(n_tiles,),
        in_specs=[rows(D_MODEL), rows(D_REC), rows(N_HEADS * SLAB), const((1, N_HEADS * SLAB)),
                  const((D_REC, D_MODEL)), const((N_HEADS * SLAB, D_MODEL)), const((1, D_MODEL)),
                  const((D_MODEL, LANES)), const((D_MODEL, LANES)), const((1, LANES))],
        out_specs=(rows(D_MODEL), pl.BlockSpec((TILE_SLOTS, D_MODEL), lambda i: (i, 0)), rows(LANES),
                   pl.BlockSpec((1, 8, LANES), lambda i: (i, 0, 0))),
        compiler_params=pltpu.CompilerParams(dimension_semantics=("arbitrary",),
                                             vmem_limit_bytes=VMEM_LIMIT),
        name="out_proj",
    )(x2d, rec_n, att, aw_slab, w_rec, w_att, ln2_w, rw_hi, rw_lo, rb_pad)


def _start_units(unit_ref, base, n_units, src_hbm, dst, sem):
    def body(j, carry):
        src_row = pl.multiple_of(unit_ref[base + j] * UNIT, UNIT)
        dst_row = pl.multiple_of(j * UNIT, UNIT)
        pltpu.make_async_copy(src_hbm.at[pl.ds(src_row, UNIT), :],
                              dst.at[pl.ds(dst_row, UNIT), :], sem).start()
        return carry
    lax.fori_loop(0, n_units, body, 0, unroll=8)


def _wait_units(n_units, src_hbm, dst, sem):
    pltpu.make_async_copy(src_hbm.at[pl.ds(0, n_units * UNIT), :], dst, sem).wait()


def _experts_kernel(be_ref, nblk_ref, unit_ref, order_ref, next_ref, xs_hbm, wgu_hbm, bgu_ref, wd_hbm,
                    bd_ref, ys_ref, xbuf, wgu_f32, wd_f32, wgu_bf, wd_bf, sem, wsem):
    b = pl.program_id(0)
    nblk = nblk_ref[0]
    slot = b % 2
    e = be_ref[b]

    def weight_copies(expert, wslot):
        return (pltpu.make_async_copy(wgu_hbm.at[expert], wgu_f32.at[wslot], wsem.at[wslot]),
                pltpu.make_async_copy(wd_hbm.at[expert], wd_f32.at[wslot], wsem.at[wslot]))

    @pl.when(b == 0)
    def _():
        _start_units(unit_ref, 0, BLOCK_UNITS, xs_hbm, xbuf.at[0], sem.at[0])
        for copy in weight_copies(e, 0):
            copy.start()

    @pl.when(b + 1 < nblk)
    def _():
        _start_units(unit_ref, (b + 1) * BLOCK_UNITS, BLOCK_UNITS, xs_hbm, xbuf.at[1 - slot],
                     sem.at[1 - slot])

    new_expert = jnp.logical_or(b == 0, e != be_ref[jnp.maximum(b - 1, 0)])

    @pl.when(jnp.logical_and(b < nblk, new_expert))
    def _():
        wslot = order_ref[e] % 2
        for copy in weight_copies(e, wslot):
            copy.wait()
        wgu_bf[...] = wgu_f32[wslot].astype(BF16)
        wd_bf[...] = wd_f32[wslot].astype(BF16)
        nxt = next_ref[e]

        @pl.when(nxt >= 0)
        def _():
            for copy in weight_copies(nxt, 1 - wslot):
                copy.start()

    @pl.when(b < nblk)
    def _():
        _wait_units(BLOCK_UNITS, xs_hbm, xbuf.at[slot], sem.at[slot])
        gu = _dot(xbuf[slot], wgu_bf[...]) + bgu_ref[0]
        g = jnp.minimum(gu[:, :D_EXPERT], SWIGLU_LIMIT)
        u = jnp.clip(gu[:, D_EXPERT:], -SWIGLU_LIMIT, SWIGLU_LIMIT)
        act = (u + 1.0) * (g * _sigmoid(SWIGLU_ALPHA * g))
        ys_ref[...] = (_dot(act.astype(BF16), wd_bf[...]) + bd_ref[0]).astype(BF16)

    @pl.when(b >= nblk)
    def _():
        ys_ref[...] = jnp.zeros_like(ys_ref)


def _experts(block_expert, n_blocks_used, unit_src, expert_order, expert_next, xs, w_gate_up,
             b_gate_up, w_down, b_down):
    n_blocks = block_expert.shape[0]
    bias_map = lambda b, be, nb, us, eo, en: (be[b], 0, 0)
    grid_spec = pltpu.PrefetchScalarGridSpec(
        num_scalar_prefetch=5,
        grid=(n_blocks,),
        in_specs=[pl.BlockSpec(memory_space=pl.ANY),
                  pl.BlockSpec(memory_space=pl.ANY),
                  pl.BlockSpec((1, 1, 2 * D_EXPERT), bias_map),
                  pl.BlockSpec(memory_space=pl.ANY),
                  pl.BlockSpec((1, 1, D_MODEL), bias_map)],
        out_specs=pl.BlockSpec((TM_E, D_MODEL), lambda b, be, nb, us, eo, en: (b, 0)),
        scratch_shapes=[pltpu.VMEM((2, TM_E, D_MODEL), BF16),
                        pltpu.VMEM((2, D_MODEL, 2 * D_EXPERT), F32),
                        pltpu.VMEM((2, D_EXPERT, D_MODEL), F32),
                        pltpu.VMEM((D_MODEL, 2 * D_EXPERT), BF16),
                        pltpu.VMEM((D_EXPERT, D_MODEL), BF16),
                        pltpu.SemaphoreType.DMA((2,)),
                        pltpu.SemaphoreType.DMA((2,))],
    )
    return pl.pallas_call(
        _experts_kernel,
        out_shape=jax.ShapeDtypeStruct((n_blocks * TM_E, D_MODEL), BF16),
        grid_spec=grid_spec,
        compiler_params=pltpu.CompilerParams(dimension_semantics=("arbitrary",),
                                             vmem_limit_bytes=VMEM_LIMIT),
        name="experts",
    )(block_expert, n_blocks_used, unit_src, expert_order, expert_next, xs, w_gate_up,
      b_gate_up.reshape(N_EXPERTS, 1, 2 * D_EXPERT), w_down, b_down.reshape(N_EXPERTS, 1, D_MODEL))


def _combine_kernel(unit_ref, x1_ref, meta_ref, ys_hbm, o_ref, buf, sem):
    i = pl.program_id(0)
    n = pl.num_programs(0)
    slot = i % 2
    tm = x1_ref.shape[0]

    @pl.when(i == 0)
    def _():
        _start_units(unit_ref, 0, TILE_UNITS, ys_hbm, buf.at[0], sem.at[0])

    @pl.when(i + 1 < n)
    def _():
        _start_units(unit_ref, (i + 1) * TILE_UNITS, TILE_UNITS, ys_hbm, buf.at[1 - slot],
                     sem.at[1 - slot])

    meta = meta_ref[...]
    slots = [meta[:, 2 * TOP_K + k:2 * TOP_K + k + 1].astype(jnp.int32) for k in range(TOP_K)]
    gates = [meta[:, TOP_K + k:TOP_K + k + 1] for k in range(TOP_K)]
    _wait_units(TILE_UNITS, ys_hbm, buf.at[slot], sem.at[slot])

    acc = x1_ref[...]
    for c in range(TILE_SLOTS // SLOT_CHUNK):
        first = c * SLOT_CHUNK
        col = lax.broadcasted_iota(jnp.int32, (tm, SLOT_CHUNK), 1) + first
        weight = jnp.zeros((tm, SLOT_CHUNK), F32)
        for k in range(TOP_K):
            weight = jnp.where(col == slots[k], gates[k], weight)
        acc = acc + _dot(weight.astype(BF16), buf[slot, first:first + SLOT_CHUNK, :])
    o_ref[...] = acc


def _combine(tile_units, x1, meta, ys):
    t = x1.shape[0]
    grid_spec = pltpu.PrefetchScalarGridSpec(
        num_scalar_prefetch=1,
        grid=(t // TM_OUT,),
        in_specs=[pl.BlockSpec((TM_OUT, D_MODEL), lambda i, us: (i, 0)),
                  pl.BlockSpec((TM_OUT, LANES), lambda i, us: (i, 0)),
                  pl.BlockSpec(memory_space=pl.ANY)],
        out_specs=pl.BlockSpec((TM_OUT, D_MODEL), lambda i, us: (i, 0)),
        scratch_shapes=[pltpu.VMEM((2, TILE_SLOTS, D_MODEL), BF16),
                        pltpu.SemaphoreType.DMA((2,))],
    )
    return pl.pallas_call(
        _combine_kernel,
        out_shape=jax.ShapeDtypeStruct((t, D_MODEL), F32),
        grid_spec=grid_spec,
        compiler_params=pltpu.CompilerParams(dimension_semantics=("arbitrary",),
                                             vmem_limit_bytes=VMEM_LIMIT),
        name="combine",
    )(tile_units, x1, meta, ys)


def _head_slabs(w):
    d = w.shape[0]
    w = w.reshape(d, -1, HEAD_DIM)
    return jnp.pad(w, ((0, 0), (0, 0), (0, SLAB - HEAD_DIM))).reshape(d, -1)


def _gate_matrix(w_a, w_i):
    g = w_a.shape[0]
    eye = jnp.eye(g, dtype=w_a.dtype)
    both = jnp.stack([w_a, w_i])
    return jnp.einsum('sgij,gh->gishj', both, eye).reshape(g * REC_BLOCK, 2 * g * REC_BLOCK)


def _pad_lanes(v, n=LANES):
    return jnp.pad(v, (0, n - v.shape[0])).reshape(1, n)


def _excl_cumsum(a, axis):
    return jnp.cumsum(a, axis=axis) - a


def _pick(onehot, table):
    if table.ndim == 1:
        return jnp.sum(onehot * table[None, :], axis=1)
    return jnp.sum(onehot[:, :, None] * table[None, :, :], axis=1)


def _dispatch_tables(units, n_blocks):
    n_tiles = units.shape[0]
    i32 = jnp.int32
    tile_first = _excl_cumsum(units, 1)
    tile_end = tile_first + units
    run_first = _excl_cumsum(units, 0)
    run_end = run_first + units
    region_units = jnp.sum(units, axis=0)
    region_blocks = (region_units + BLOCK_UNITS - 1) // BLOCK_UNITS
    block_end = jnp.cumsum(region_blocks)
    region_first = (block_end - region_blocks) * BLOCK_UNITS
    n_used = block_end[-1:]
    experts = jnp.arange(N_EXPERTS, dtype=i32)
    tiles = jnp.arange(n_tiles, dtype=i32)

    blocks = jnp.arange(n_blocks, dtype=i32)
    block_expert = jnp.minimum(
        jnp.sum((block_end[None, :] <= blocks[:, None]).astype(i32), axis=1), N_EXPERTS - 1)

    g = jnp.arange(n_blocks * BLOCK_UNITS, dtype=i32)
    e = jnp.minimum(jnp.sum((block_end[None, :] * BLOCK_UNITS <= g[:, None]).astype(i32), axis=1),
                    N_EXPERTS - 1)
    e_hot = (e[:, None] == experts[None, :]).astype(i32)
    local = g - _pick(e_hot, region_first)
    valid = (local < _pick(e_hot, region_units)) & (g < n_used[0] * BLOCK_UNITS)
    ends = _pick(e_hot, run_end.T)
    tile = jnp.minimum(jnp.sum((ends <= local[:, None]).astype(i32), axis=1), n_tiles - 1)
    t_hot = (tile[:, None] == tiles[None, :]).astype(i32)
    src = (tile * TILE_UNITS + jnp.sum(_pick(t_hot, tile_first - run_first) * e_hot, axis=1) + local)
    unit_src = jnp.where(valid, src, 0).astype(i32)

    j = jnp.arange(TILE_UNITS, dtype=i32)
    ej = jnp.minimum(jnp.sum((tile_end[:, None, :] <= j[None, :, None]).astype(i32), axis=2),
                     N_EXPERTS - 1)
    ej_hot = (ej[:, :, None] == experts[None, None, :]).astype(i32)
    used = j[None, :] < jnp.sum(units, axis=1)[:, None]
    base = region_first[None, :] + run_first - tile_first
    dst = jnp.sum(ej_hot * base[:, None, :], axis=2) + j[None, :]
    tile_units = jnp.where(used, dst, 0).astype(i32).reshape(-1)
    owns = region_blocks > 0
    expert_order = _excl_cumsum(owns.astype(i32), 0)
    later = (experts[None, :] > experts[:, None]) & owns[None, :]
    expert_next = jnp.min(jnp.where(later, experts[None, :], N_EXPERTS), axis=1)
    expert_next = jnp.where(expert_next < N_EXPERTS, expert_next, -1).astype(i32)
    return (block_expert.astype(i32), n_used.astype(i32), unit_src, tile_units,
            expert_order.astype(i32), expert_next)


def _layer(x, ln1_w, w_in, conv_w, conv_b, w_rg_a, b_rg_a, w_rg_i, b_rg_i, rg_lambda, b_forget,
           q_norm_w, k_norm_w, rec_out_norm_w, attn_out_norm_w, w_out, ln2_w, w_router, b_router,
           w_gate_up, b_gate_up, w_down, b_down):
    batch, seq, _ = x.shape
    t = batch * seq
    x2d = x.reshape(t, D_MODEL)

    o_q = 2 * D_REC
    o_k = o_q + D_ATT
    o_v = o_k + D_ATT
    o_f = o_v + D_ATT
    w_all = jnp.concatenate([w_in[:, :o_q], _head_slabs(w_in[:, o_k:o_v])], axis=1).astype(BF16)
    w_t = w_in.T.astype(BF16)
    w_ft = jnp.pad(w_t[o_f:], ((0, LANES - N_HEADS), (0, 0)))
    scale = HEAD_DIM ** -0.5
    u, g, qp, kp, vp = _in_proj(
        x2d, ln1_w.reshape(1, D_MODEL), w_all, w_t[o_q:o_k], w_t[o_v:o_f], w_ft,
        _pad_lanes(b_forget).reshape(LANES, 1), (q_norm_w * scale).reshape(HEAD_DIM, 1),
        _pad_lanes(k_norm_w), batch, seq)

    rec_vec = jnp.concatenate(
        [conv_w, jnp.stack([conv_b, b_rg_a, b_rg_i, rg_lambda, rec_out_norm_w])])
    rec_n = _rglru(u.reshape(batch, seq, D_REC), g.reshape(batch, seq, D_REC), rec_vec,
                   _gate_matrix(w_rg_a, w_rg_i).astype(BF16))

    att = _attention(qp, kp, vp)

    aw_slab = _head_slabs(attn_out_norm_w.reshape(1, D_ATT))
    w_att = jnp.pad(w_out[D_REC:].reshape(N_HEADS, HEAD_DIM, D_MODEL),
                    ((0, 0), (0, SLAB - HEAD_DIM), (0, 0))).reshape(N_HEADS * SLAB, D_MODEL)
    rw = jnp.pad(w_router, ((0, 0), (0, LANES - N_EXPERTS)))
    rw_hi = rw.astype(BF16)
    rw_lo = (rw - rw_hi.astype(F32)).astype(BF16)
    x1, xs, meta, units = _out_proj(
        x2d, rec_n.reshape(t, D_REC), att.reshape(t, N_HEADS * SLAB), aw_slab,
        w_out[:D_REC].astype(BF16), w_att.astype(BF16), ln2_w.reshape(1, D_MODEL),
        rw_hi, rw_lo, _pad_lanes(b_router))

    n_tiles = t // TM_OUT
    max_units = t * TOP_K // UNIT + n_tiles * N_EXPERTS
    n_blocks = -(-max_units // BLOCK_UNITS) + N_EXPERTS
    block_expert, n_used, unit_src, tile_units, expert_order, expert_next = _dispatch_tables(
        units[:, 0, :N_EXPERTS].astype(jnp.int32), n_blocks)

    ys = _experts(block_expert, n_used, unit_src, expert_order, expert_next, xs, w_gate_up, b_gate_up,
                  w_down, b_down)
    out = _combine(tile_units, x1, meta, ys)
    return out.reshape(batch, seq, D_MODEL)


def kernel(x, ln1_w, w_in, conv_w, conv_b, w_rg_a, b_rg_a, w_rg_i, b_rg_i, rg_lambda, b_forget,
           q_norm_w, k_norm_w, rec_out_norm_w, attn_out_norm_w, w_out, ln2_w, w_router, b_router,
           w_gate_up, b_gate_up, w_down, b_down):
    for l in range(ln1_w.shape[0]):
        x = _layer(x, ln1_w[l], w_in[l], conv_w[l], conv_b[l], w_rg_a[l], b_rg_a[l], w_rg_i[l],
                   b_rg_i[l], rg_lambda[l], b_forget[l], q_norm_w[l], k_norm_w[l], rec_out_norm_w[l],
                   attn_out_norm_w[l], w_out[l], ln2_w[l], w_router[l], b_router[l], w_gate_up[l],
                   b_gate_up[l], w_down[l], b_down[l])
    return x
```

```python
import functools

import jax
import jax.numpy as jnp
from jax import lax
from jax.experimental import pallas as pl
from jax.experimental.pallas import tpu as pltpu

D_MODEL = 1024
D_REC = 512
N_REC_BLOCKS = 8
REC_BLOCK = 64
CONV_WIDTH = 4
RGLRU_C = 8.0
D_ATT = 512
HEAD_DIM = 64
N_HEADS = 8
N_EXPERTS = 32
TOP_K = 4
D_EXPERT = 1024
SWIGLU_LIMIT = 7.0
SWIGLU_ALPHA = 1.702
EPS = 1e-6

LANES = 128
SLAB = LANES
VMEM_LIMIT = 56 * 1024 * 1024

F32 = jnp.float32
BF16 = jnp.bfloat16

O_U = 0
O_G = O_U + D_REC
N_PROJ = O_G + D_REC

L_C = HEAD_DIM
L_ONE = HEAD_DIM + 3

TM_IN = 512
TS_REC = 512
TQ = 512
TK = 512
HEADS_PER_STEP = 8
TM_OUT = 512
TM_E = 512

UNIT = 16
TILE_SLOTS = -(-(TM_OUT * TOP_K + N_EXPERTS * (UNIT - 1)) // LANES) * LANES
TILE_UNITS = TILE_SLOTS // UNIT
BLOCK_UNITS = TM_E // UNIT
SLOT_CHUNK = 512


def _split3(v):
    hi = v.astype(BF16).astype(F32)
    r = v - hi
    mid = r.astype(BF16).astype(F32)
    lo = (r - mid).astype(BF16).astype(F32)
    return hi, mid, lo


def _dot(a, b):
    return jnp.dot(a, b, preferred_element_type=F32)


def _sigmoid(x):
    return 0.5 * jnp.tanh(0.5 * x) + 0.5


def _dot_t(a, b):
    return lax.dot_general(a, b, (((0,), (0,)), ((), ())), preferred_element_type=F32)


def _in_proj_kernel(x_ref, ln_ref, w_ref, wqt_ref, wkt_ref, wvt_ref, wft_ref, bf_ref, qw_ref, kw_ref,
                    u_ref, g_ref, q_ref, k_ref, v_ref, carry_ref, *, tiles_per_seq):
    i = pl.program_id(0)
    tm = x_ref.shape[0]

    @pl.when(i % tiles_per_seq == 0)
    def _():
        carry_ref[...] = jnp.zeros_like(carry_ref)

    x = x_ref[...]
    ms = jnp.mean(x * x, axis=-1, keepdims=True)
    h = (x * lax.rsqrt(ms + EPS) * ln_ref[...]).astype(BF16)
    nt = (((1,), (1,)), ((), ()))

    f_t = lax.dot_general(wft_ref[...], h, nt, preferred_element_type=F32) + bf_ref[...]
    logf_t = jnp.minimum(f_t, 0.0) - jnp.log1p(jnp.exp(-jnp.abs(f_t)))
    row = lax.broadcasted_iota(jnp.int32, (tm, tm), 0)
    col = lax.broadcasted_iota(jnp.int32, (tm, tm), 1)
    upto = (row <= col).astype(BF16)
    hi, mid, lo = _split3(logf_t)
    c_t = (_dot(hi.astype(BF16), upto) + _dot(mid.astype(BF16), upto)
           + _dot(lo.astype(BF16), upto) + carry_ref[...])
    carry_ref[...] = c_t[:, tm - 1:tm]

    qt_all = lax.dot_general(wqt_ref[...], h, nt, preferred_element_type=F32)
    kt_all = lax.dot_general(wkt_ref[...], h, nt, preferred_element_type=F32)
    spare = lax.broadcasted_iota(jnp.int32, (SLAB - HEAD_DIM, tm), 0) + HEAD_DIM
    qw = qw_ref[...]
    kw = kw_ref[...]

    def normed(rows, weight):
        inv = lax.rsqrt(jnp.sum(rows * rows, axis=0, keepdims=True) * (1.0 / HEAD_DIM) + EPS)
        return rows * inv * weight

    for hd in range(N_HEADS):
        rows = slice(hd * HEAD_DIM, (hd + 1) * HEAD_DIM)
        c_hi, c_mid, c_lo = _split3(c_t[hd:hd + 1, :])
        q_extra = jnp.where(spare == L_C, c_hi, 0.0)
        q_extra = jnp.where(spare == L_C + 1, c_mid, q_extra)
        q_extra = jnp.where(spare == L_C + 2, c_lo, q_extra)
        q_extra = jnp.where((spare >= L_ONE) & (spare < L_ONE + 3), 1.0, q_extra)
        k_extra = jnp.where(spare < L_C + 3, 1.0, 0.0)
        k_extra = jnp.where(spare == L_ONE, -c_hi, k_extra)
        k_extra = jnp.where(spare == L_ONE + 1, -c_mid, k_extra)
        k_extra = jnp.where(spare == L_ONE + 2, -c_lo, k_extra)
        q_ref[0, hd] = jnp.concatenate([normed(qt_all[rows, :], qw), q_extra], axis=0).astype(BF16)
        k_ref[0, hd] = jnp.concatenate([normed(kt_all[rows, :], kw), k_extra], axis=0).T.astype(BF16)

    vt_all = lax.dot_general(wvt_ref[...], h, (((1,), (1,)), ((), ())), preferred_element_type=F32)
    spare = lax.broadcasted_iota(jnp.int32, (SLAB - HEAD_DIM, tm), 0)
    ones_row = jnp.where(spare == 0, 1.0, 0.0)
    for hd in range(N_HEADS):
        v_ref[0, hd] = jnp.concatenate(
            [vt_all[hd * HEAD_DIM:(hd + 1) * HEAD_DIM, :], ones_row], axis=0).astype(BF16)
    u_ref[...] = _dot(h, w_ref[:, O_U:O_U + D_REC])
    g_ref[...] = _dot(h, w_ref[:, O_G:O_G + D_REC])


def _in_proj(x2d, ln1_w, w_all, w_qt, w_kt, w_vt, w_ft, bf_col, qw_col, kw_col, batch, seq):
    t = x2d.shape[0]
    tiles_per_seq = seq // TM_IN
    slab_shape = jax.ShapeDtypeStruct((batch, N_HEADS, seq, SLAB), BF16)
    slab_spec = pl.BlockSpec((1, N_HEADS, TM_IN, SLAB),
                             lambda i: (i // tiles_per_seq, 0, i % tiles_per_seq, 0))
    slab_t_shape = jax.ShapeDtypeStruct((batch, N_HEADS, SLAB, seq), BF16)
    slab_t_spec = pl.BlockSpec((1, N_HEADS, SLAB, TM_IN),
                               lambda i: (i // tiles_per_seq, 0, 0, i % tiles_per_seq))
    row_spec = pl.BlockSpec((TM_IN, D_REC), lambda i: (i, 0))
    const = lambda shape: pl.BlockSpec(shape, lambda i: (0,) * len(shape))
    return pl.pallas_call(
        functools.partial(_in_proj_kernel, tiles_per_seq=tiles_per_seq),
        out_shape=(jax.ShapeDtypeStruct((t, D_REC), F32), jax.ShapeDtypeStruct((t, D_REC), F32),
                   slab_t_shape, slab_shape, slab_t_shape),
        grid=(t // TM_IN,),
        in_specs=[pl.BlockSpec((TM_IN, D_MODEL), lambda i: (i, 0)),
                  const((1, D_MODEL)), const((D_MODEL, N_PROJ)), const((D_ATT, D_MODEL)),
                  const((D_ATT, D_MODEL)), const((D_ATT, D_MODEL)), const((LANES, D_MODEL)),
                  const((LANES, 1)), const((HEAD_DIM, 1)), const((HEAD_DIM, 1))],
        out_specs=(row_spec, row_spec, slab_t_spec, slab_spec, slab_t_spec),
        scratch_shapes=[pltpu.VMEM((LANES, 1), F32)],
        compiler_params=pltpu.CompilerParams(dimension_semantics=("arbitrary",),
                                             vmem_limit_bytes=VMEM_LIMIT),
        name="in_proj",
    )(x2d, ln1_w, w_all, w_qt, w_kt, w_vt, w_ft, bf_col, qw_col, kw_col)


R_CONV_B = CONV_WIDTH
R_B_A = CONV_WIDTH + 1
R_B_I = CONV_WIDTH + 2
R_LAMBDA = CONV_WIDTH + 3
R_NORM_W = CONV_WIDTH + 4
N_REC_ROWS = CONV_WIDTH + 5


def _rglru_kernel(u_ref, g_ref, vec_ref, wg_ref, o_ref, ext_ref, a_ref, b_ref, h_ref, hcar_ref):
    s = pl.program_id(1)
    ts = u_ref.shape[1]
    pad = 8
    row = lambda r: vec_ref[r:r + 1, :]

    @pl.when(s == 0)
    def _():
        ext_ref[0:pad, :] = jnp.zeros((pad, D_REC), F32)
        hcar_ref[...] = jnp.zeros_like(hcar_ref)

    @pl.when(s != 0)
    def _():
        ext_ref[0:pad, :] = ext_ref[ts:ts + pad, :]

    u = u_ref[0]
    ext_ref[pad:pad + ts, :] = u
    xc = row(R_CONV_B) + u * row(CONV_WIDTH - 1)
    for tap in range(CONV_WIDTH - 1):
        d = CONV_WIDTH - 1 - tap
        xc = xc + ext_ref[pad - d:pad - d + ts, :] * row(tap)

    gates = _dot(xc.astype(BF16), wg_ref[...])
    r = _sigmoid(gates[:, :D_REC] + row(R_B_A))
    ig = _sigmoid(gates[:, D_REC:] + row(R_B_I))
    nlam = -row(R_LAMBDA)
    softplus = jnp.maximum(nlam, 0.0) + jnp.log1p(jnp.exp(-jnp.abs(nlam)))
    log_a = -RGLRU_C * r * softplus
    a = jnp.exp(log_a)
    a_ref[...] = a
    b_ref[...] = jnp.sqrt(1.0 - jnp.exp(2.0 * log_a)) * (ig * xc)

    def step(t, h):
        h = a_ref[pl.ds(t, 1), :] * h + b_ref[pl.ds(t, 1), :]
        h_ref[pl.ds(t, 1), :] = h
        return h

    hcar_ref[...] = lax.fori_loop(0, ts, step, hcar_ref[...], unroll=8)

    rec = h_ref[...] * jax.nn.gelu(g_ref[0])
    ms = jnp.mean(rec * rec, axis=-1, keepdims=True)
    o_ref[0] = (rec * lax.rsqrt(ms + EPS) * row(R_NORM_W)).astype(BF16)


def _rglru(u, g, rec_vec, w_gates):
    batch, seq, _ = u.shape
    seq_spec = pl.BlockSpec((1, TS_REC, D_REC), lambda b, s: (b, s, 0))
    const = lambda shape: pl.BlockSpec(shape, lambda b, s: (0,) * len(shape))
    return pl.pallas_call(
        _rglru_kernel,
        out_shape=jax.ShapeDtypeStruct((batch, seq, D_REC), BF16),
        grid=(batch, seq // TS_REC),
        in_specs=[seq_spec, seq_spec, const((N_REC_ROWS, D_REC)), const((D_REC, 2 * D_REC))],
        out_specs=seq_spec,
        scratch_shapes=[pltpu.VMEM((TS_REC + 8, D_REC), F32), pltpu.VMEM((TS_REC, D_REC), F32),
                        pltpu.VMEM((TS_REC, D_REC), F32), pltpu.VMEM((TS_REC, D_REC), F32),
                        pltpu.VMEM((1, D_REC), F32)],
        compiler_params=pltpu.CompilerParams(dimension_semantics=("arbitrary", "arbitrary"),
                                             vmem_limit_bytes=VMEM_LIMIT),
        name="rglru",
    )(u, g, rec_vec, w_gates)


def _attn_kernel(q_ref, k_ref, v_ref, o_ref):
    qi = pl.program_id(2)
    hb = q_ref.shape[1]
    tq = q_ref.shape[3]
    qs = [q_ref[0, h] for h in range(hb)]

    def step(j, carry, masked):
        start = pl.multiple_of(j * TK, TK)
        scores = []
        for h in range(hb):
            kj = k_ref[0, h, pl.ds(start, TK), :]
            s = _dot(kj, qs[h])
            if masked:
                key = lax.broadcasted_iota(jnp.int32, (TK, tq), 0)
                qry = lax.broadcasted_iota(jnp.int32, (TK, tq), 1)
                s = jnp.where(qry >= key, s, -jnp.inf)
            scores.append(s)
        m_news = [jnp.maximum(carry[h][0], jnp.max(scores[h], axis=0, keepdims=True))
                  for h in range(hb)]
        probs = [jnp.exp(scores[h] - m_news[h]).astype(BF16) for h in range(hb)]
        out = []
        for h in range(hb):
            m, acc = carry[h]
            vj = v_ref[0, h, :, pl.ds(start, TK)]
            out.append((m_news[h], jnp.exp(m - m_news[h]) * acc + _dot(vj, probs[h])))
        return tuple(out)

    init = tuple((jnp.full((1, tq), -jnp.inf, F32), jnp.zeros((SLAB, tq), F32)) for _ in range(hb))
    carry = lax.fori_loop(0, qi, lambda j, c: step(j, c, False), init)
    carry = step(qi, carry, True)
    outs = [carry[h][1][:HEAD_DIM, :] / carry[h][1][HEAD_DIM:HEAD_DIM + 1, :] for h in range(hb)]
    for pair in range(hb // 2):
        both = jnp.concatenate([outs[2 * pair], outs[2 * pair + 1]], axis=0)
        o_ref[0, :, pair * LANES:(pair + 1) * LANES] = both.T


def _attention(qp, kp, vp):
    batch, _, seq, _ = kp.shape
    hb = HEADS_PER_STEP
    kv_spec = pl.BlockSpec((1, hb, seq, SLAB), lambda b, h, i: (b, h, 0, 0))
    vt_spec = pl.BlockSpec((1, hb, SLAB, seq), lambda b, h, i: (b, h, 0, 0))
    return pl.pallas_call(
        _attn_kernel,
        out_shape=jax.ShapeDtypeStruct((batch, seq, D_ATT), F32),
        grid=(batch, N_HEADS // hb, seq // TQ),
        in_specs=[pl.BlockSpec((1, hb, SLAB, TQ), lambda b, h, i: (b, h, 0, i)), kv_spec, vt_spec],
        out_specs=pl.BlockSpec((1, TQ, hb * HEAD_DIM), lambda b, h, i: (b, i, h)),
        compiler_params=pltpu.CompilerParams(
            dimension_semantics=("arbitrary", "arbitrary", "arbitrary"),
            vmem_limit_bytes=VMEM_LIMIT),
        name="attn",
    )(qp, kp, vp)


def _slot_onehot(slots, first, width):
    tm = slots[0].shape[0]
    col = lax.broadcasted_iota(jnp.int32, (tm, width), 1) + first
    hit = jnp.zeros((tm, width), F32)
    for slot in slots:
        hit = jnp.where(col == slot, 1.0, hit)
    return hit.astype(BF16)


def _out_proj_kernel(x_ref, rec_ref, att_ref, aw_ref, wr_ref, wa_ref, ln_ref, rwh_ref, rwl_ref, rb_ref,
                     x1_ref, xs_ref, meta_ref, units_ref):
    tm = x_ref.shape[0]

    att = att_ref[...]
    ms = jnp.sum(att * att, axis=-1, keepdims=True) * (1.0 / D_ATT)
    att_n = (att * lax.rsqrt(ms + EPS) * aw_ref[...]).astype(BF16)
    x1 = x_ref[...] + _dot(rec_ref[...], wr_ref[...]) + _dot(att_n, wa_ref[...])
    x1_ref[...] = x1

    ms2 = jnp.mean(x1 * x1, axis=-1, keepdims=True)
    h2 = x1 * lax.rsqrt(ms2 + EPS) * ln_ref[...]

    h_hi = h2.astype(BF16)
    h_lo = (h2 - h_hi.astype(F32)).astype(BF16)
    logits = (_dot(h_hi, rwh_ref[...]) + _dot(h_lo, rwh_ref[...]) + _dot(h_hi, rwl_ref[...])
              + rb_ref[...])

    lane = lax.broadcasted_iota(jnp.int32, (tm, LANES), 1)
    lane_f = lane.astype(F32)
    l = jnp.where(lane < N_EXPERTS, logits, -jnp.inf)

    vals, idxs, sels = [], [], []
    for _ in range(TOP_K):
        m = jnp.max(l, axis=-1, keepdims=True)
        idx = jnp.min(jnp.where(l == m, lane_f, float(LANES)), axis=-1, keepdims=True)
        sel = lane_f == idx
        vals.append(m)
        idxs.append(idx)
        sels.append(sel)
        l = jnp.where(sel, -jnp.inf, l)
    exps = [jnp.exp(v - vals[0]) for v in vals]
    denom = exps[0] + exps[1] + exps[2] + exps[3]

    any_sel = sels[0] | sels[1] | sels[2] | sels[3]
    onehot = jnp.where(any_sel, 1.0, 0.0)
    row = lax.broadcasted_iota(jnp.int32, (tm, tm), 0)
    col = lax.broadcasted_iota(jnp.int32, (tm, tm), 1)
    before = (row > col).astype(BF16)
    rank = _dot(before, onehot.astype(BF16))
    count = jnp.sum(onehot, axis=0, keepdims=True)
    units = jnp.floor((count + (UNIT - 1)) * (1.0 / UNIT))
    e_row = lax.broadcasted_iota(jnp.int32, (LANES, LANES), 0)
    e_col = lax.broadcasted_iota(jnp.int32, (LANES, LANES), 1)
    earlier = (e_row < e_col).astype(BF16)
    units8 = jnp.broadcast_to(units, (8, LANES))
    first_unit = _dot(units8.astype(BF16), earlier)[0:1, :]
    units_ref[0] = units8
    place = first_unit * float(UNIT) + rank

    meta = jnp.zeros((tm, LANES), F32)
    slots = []
    for k in range(TOP_K):
        slot = jnp.sum(jnp.where(sels[k], place, 0.0), axis=-1, keepdims=True)
        slots.append(slot.astype(jnp.int32))
        meta = jnp.where(lane == k, idxs[k], meta)
        meta = jnp.where(lane == TOP_K + k, exps[k] / denom, meta)
        meta = jnp.where(lane == 2 * TOP_K + k, slot, meta)
    meta_ref[...] = meta

    for c in range(TILE_SLOTS // SLOT_CHUNK):
        sel_t = _slot_onehot(slots, c * SLOT_CHUNK, SLOT_CHUNK)
        xs_ref[c * SLOT_CHUNK:(c + 1) * SLOT_CHUNK, :] = _dot_t(sel_t, h_hi).astype(BF16)


def _out_proj(x2d, rec_n, att, aw_slab, w_rec, w_att, ln2_w, rw_hi, rw_lo, rb_pad):
    t = x2d.shape[0]
    n_tiles = t // TM_OUT
    rows = lambda n: pl.BlockSpec((TM_OUT, n), lambda i: (i, 0))
    const = lambda shape: pl.BlockSpec(shape, lambda i: (0,) * len(shape))
    return pl.pallas_call(
        _out_proj_kernel,
        out_shape=(jax.ShapeDtypeStruct((t, D_MODEL), F32),
                   jax.ShapeDtypeStruct((n_tiles * TILE_SLOTS, D_MODEL), BF16),
                   jax.ShapeDtypeStruct((t, LANES), F32),
                   jax.ShapeDtypeStruct((n_tiles, 8, LANES), F32)),
        grid=(n_tiles,),
        in_specs=[rows(D_MODEL), rows(D_REC), rows(D_ATT), const((1, D_ATT)),
                  const((D_REC, D_MODEL)), const((D_ATT, D_MODEL)), const((1, D_MODEL)),
                  const((D_MODEL, LANES)), const((D_MODEL, LANES)), const((1, LANES))],
        out_specs=(rows(D_MODEL), pl.BlockSpec((TILE_SLOTS, D_MODEL), lambda i: (i, 0)), rows(LANES),
                   pl.BlockSpec((1, 8, LANES), lambda i: (i, 0, 0))),
        compiler_params=pltpu.CompilerParams(dimension_semantics=("arbitrary",),
                                             vmem_limit_bytes=VMEM_LIMIT),
        name="out_proj",
    )(x2d, rec_n, att, aw_slab, w_rec, w_att, ln2_w, rw_hi, rw_lo, rb_pad)


def _start_units(unit_ref, base, n_units, src_hbm, dst, sem):
    def body(j, carry):
        src_row = pl.multiple_of(unit_ref[base + j] * UNIT, UNIT)
        dst_row = pl.multiple_of(j * UNIT, UNIT)
        pltpu.make_async_copy(src_hbm.at[pl.ds(src_row, UNIT), :],
                              dst.at[pl.ds(dst_row, UNIT), :], sem).start()
        return carry
    lax.fori_loop(0, n_units, body, 0, unroll=8)


def _wait_units(n_units, src_hbm, dst, sem):
    pltpu.make_async_copy(src_hbm.at[pl.ds(0, n_units * UNIT), :], dst, sem).wait()


def _experts_kernel(be_ref, nblk_ref, unit_ref, order_ref, next_ref, xs_hbm, wgu_hbm, bgu_ref, wd_hbm,
                    bd_ref, ys_ref, xbuf, wgu_f32, wd_f32, wgu_bf, wd_bf, sem, wsem):
    b = pl.program_id(0)
    nblk = nblk_ref[0]
    slot = b % 2
    e = be_ref[b]

    def weight_copies(expert, wslot):
        return (pltpu.make_async_copy(wgu_hbm.at[expert], wgu_f32.at[wslot], wsem.at[wslot]),
                pltpu.make_async_copy(wd_hbm.at[expert], wd_f32.at[wslot], wsem.at[wslot]))

    @pl.when(b == 0)
    def _():
        _start_units(unit_ref, 0, BLOCK_UNITS, xs_hbm, xbuf.at[0], sem.at[0])
        for copy in weight_copies(e, 0):
            copy.start()

    @pl.when(b + 1 < nblk)
    def _():
        _start_units(unit_ref, (b + 1) * BLOCK_UNITS, BLOCK_UNITS, xs_hbm, xbuf.at[1 - slot],
                     sem.at[1 - slot])

    new_expert = jnp.logical_or(b == 0, e != be_ref[jnp.maximum(b - 1, 0)])

    @pl.when(jnp.logical_and(b < nblk, new_expert))
    def _():
        wslot = order_ref[e] % 2
        for copy in weight_copies(e, wslot):
            copy.wait()
        wgu_bf[...] = wgu_f32[wslot].astype(BF16)
        wd_bf[...] = wd_f32[wslot].astype(BF16)
        nxt = next_ref[e]

        @pl.when(nxt >= 0)
        def _():
            for copy in weight_copies(nxt, 1 - wslot):
                copy.start()

    @pl.when(b < nblk)
    def _():
        _wait_units(BLOCK_UNITS, xs_hbm, xbuf.at[slot], sem.at[slot])
        gu = _dot(xbuf[slot], wgu_bf[...]) + bgu_ref[0]
        g = jnp.minimum(gu[:, :D_EXPERT], SWIGLU_LIMIT)
        u = jnp.clip(gu[:, D_EXPERT:], -SWIGLU_LIMIT, SWIGLU_LIMIT)
        act = (u + 1.0) * (g * _sigmoid(SWIGLU_ALPHA * g))
        ys_ref[...] = (_dot(act.astype(BF16), wd_bf[...]) + bd_ref[0]).astype(BF16)

    @pl.when(b >= nblk)
    def _():
        ys_ref[...] = jnp.zeros_like(ys_ref)


def _experts(block_expert, n_blocks_used, unit_src, expert_order, expert_next, xs, w_gate_up,
             b_gate_up, w_down, b_down):
    n_blocks = block_expert.shape[0]
    bias_map = lambda b, be, nb, us, eo, en: (be[b], 0, 0)
    grid_spec = pltpu.PrefetchScalarGridSpec(
        num_scalar_prefetch=5,
        grid=(n_blocks,),
        in_specs=[pl.BlockSpec(memory_space=pl.ANY),
                  pl.BlockSpec(memory_space=pl.ANY),
                  pl.BlockSpec((1, 1, 2 * D_EXPERT), bias_map),
                  pl.BlockSpec(memory_space=pl.ANY),
                  pl.BlockSpec((1, 1, D_MODEL), bias_map)],
        out_specs=pl.BlockSpec((TM_E, D_MODEL), lambda b, be, nb, us, eo, en: (b, 0)),
        scratch_shapes=[pltpu.VMEM((2, TM_E, D_MODEL), BF16),
                        pltpu.VMEM((2, D_MODEL, 2 * D_EXPERT), F32),
                        pltpu.VMEM((2, D_EXPERT, D_MODEL), F32),
                        pltpu.VMEM((D_MODEL, 2 * D_EXPERT), BF16),
                        pltpu.VMEM((D_EXPERT, D_MODEL), BF16),
                        pltpu.SemaphoreType.DMA((2,)),
                        pltpu.SemaphoreType.DMA((2,))],
    )
    return pl.pallas_call(
        _experts_kernel,
        out_shape=jax.ShapeDtypeStruct((n_blocks * TM_E, D_MODEL), BF16),
        grid_spec=grid_spec,
        compiler_params=pltpu.CompilerParams(dimension_semantics=("arbitrary",),
                                             vmem_limit_bytes=VMEM_LIMIT),
        name="experts",
    )(block_expert, n_blocks_used, unit_src, expert_order, expert_next, xs, w_gate_up,
      b_gate_up.reshape(N_EXPERTS, 1, 2 * D_EXPERT), w_down, b_down.reshape(N_EXPERTS, 1, D_MODEL))


def _combine_kernel(unit_ref, x1_ref, meta_ref, ys_hbm, o_ref, buf, sem):
    i = pl.program_id(0)
    n = pl.num_programs(0)
    slot = i % 2
    tm = x1_ref.shape[0]

    @pl.when(i == 0)
    def _():
        _start_units(unit_ref, 0, TILE_UNITS, ys_hbm, buf.at[0], sem.at[0])

    @pl.when(i + 1 < n)
    def _():
        _start_units(unit_ref, (i + 1) * TILE_UNITS, TILE_UNITS, ys_hbm, buf.at[1 - slot],
                     sem.at[1 - slot])

    meta = meta_ref[...]
    slots = [meta[:, 2 * TOP_K + k:2 * TOP_K + k + 1].astype(jnp.int32) for k in range(TOP_K)]
    gates = [meta[:, TOP_K + k:TOP_K + k + 1] for k in range(TOP_K)]
    _wait_units(TILE_UNITS, ys_hbm, buf.at[slot], sem.at[slot])

    acc = x1_ref[...]
    for c in range(TILE_SLOTS // SLOT_CHUNK):
        first = c * SLOT_CHUNK
        col = lax.broadcasted_iota(jnp.int32, (tm, SLOT_CHUNK), 1) + first
        weight = jnp.zeros((tm, SLOT_CHUNK), F32)
        for k in range(TOP_K):
            weight = jnp.where(col == slots[k], gates[k], weight)
        acc = acc + _dot(weight.astype(BF16), buf[slot, first:first + SLOT_CHUNK, :])
    o_ref[...] = acc


def _combine(tile_units, x1, meta, ys):
    t = x1.shape[0]
    grid_spec = pltpu.PrefetchScalarGridSpec(
        num_scalar_prefetch=1,
        grid=(t // TM_OUT,),
        in_specs=[pl.BlockSpec((TM_OUT, D_MODEL), lambda i, us: (i, 0)),
                  pl.BlockSpec((TM_OUT, LANES), lambda i, us: (i, 0)),
                  pl.BlockSpec(memory_space=pl.ANY)],
        out_specs=pl.BlockSpec((TM_OUT, D_MODEL), lambda i, us: (i, 0)),
        scratch_shapes=[pltpu.VMEM((2, TILE_SLOTS, D_MODEL), BF16),
                        pltpu.SemaphoreType.DMA((2,))],
    )
    return pl.pallas_call(
        _combine_kernel,
        out_shape=jax.ShapeDtypeStruct((t, D_MODEL), F32),
        grid_spec=grid_spec,
        compiler_params=pltpu.CompilerParams(dimension_semantics=("arbitrary",),
                                             vmem_limit_bytes=VMEM_LIMIT),
        name="combine",
    )(tile_units, x1, meta, ys)


def _gate_matrix(w_a, w_i):
    g = w_a.shape[0]
    eye = jnp.eye(g, dtype=w_a.dtype)
    both = jnp.stack([w_a, w_i])
    return jnp.einsum('sgij,gh->gishj', both, eye).reshape(g * REC_BLOCK, 2 * g * REC_BLOCK)


def _pad_lanes(v, n=LANES):
    return jnp.pad(v, (0, n - v.shape[0])).reshape(1, n)


def _excl_cumsum(a, axis):
    return jnp.cumsum(a, axis=axis) - a


def _pick(onehot, table):
    if table.ndim == 1:
        return jnp.sum(onehot * table[None, :], axis=1)
    return jnp.sum(onehot[:, :, None] * table[None, :, :], axis=1)


def _dispatch_tables(units, n_blocks):
    n_tiles = units.shape[0]
    i32 = jnp.int32
    tile_first = _excl_cumsum(units, 1)
    tile_end = tile_first + units
    run_first = _excl_cumsum(units, 0)
    run_end = run_first + units
    region_units = jnp.sum(units, axis=0)
    region_blocks = (region_units + BLOCK_UNITS - 1) // BLOCK_UNITS
    block_end = jnp.cumsum(region_blocks)
    region_first = (block_end - region_blocks) * BLOCK_UNITS
    n_used = block_end[-1:]
    experts = jnp.arange(N_EXPERTS, dtype=i32)
    tiles = jnp.arange(n_tiles, dtype=i32)

    blocks = jnp.arange(n_blocks, dtype=i32)
    block_expert = jnp.minimum(
        jnp.sum((block_end[None, :] <= blocks[:, None]).astype(i32), axis=1), N_EXPERTS - 1)

    g = jnp.arange(n_blocks * BLOCK_UNITS, dtype=i32)
    e = jnp.minimum(jnp.sum((block_end[None, :] * BLOCK_UNITS <= g[:, None]).astype(i32), axis=1),
                    N_EXPERTS - 1)
    e_hot = (e[:, None] == experts[None, :]).astype(i32)
    local = g - _pick(e_hot, region_first)
    valid = (local < _pick(e_hot, region_units)) & (g < n_used[0] * BLOCK_UNITS)
    ends = _pick(e_hot, run_end.T)
    tile = jnp.minimum(jnp.sum((ends <= local[:, None]).astype(i32), axis=1), n_tiles - 1)
    t_hot = (tile[:, None] == tiles[None, :]).astype(i32)
    src = (tile * TILE_UNITS + jnp.sum(_pick(t_hot, tile_first - run_first) * e_hot, axis=1) + local)
    unit_src = jnp.where(valid, src, 0).astype(i32)

    j = jnp.arange(TILE_UNITS, dtype=i32)
    ej = jnp.minimum(jnp.sum((tile_end[:, None, :] <= j[None, :, None]).astype(i32), axis=2),
                     N_EXPERTS - 1)
    ej_hot = (ej[:, :, None] == experts[None, None, :]).astype(i32)
    used = j[None, :] < jnp.sum(units, axis=1)[:, None]
    base = region_first[None, :] + run_first - tile_first
    dst = jnp.sum(ej_hot * base[:, None, :], axis=2) + j[None, :]
    tile_units = jnp.where(used, dst, 0).astype(i32).reshape(-1)
    owns = region_blocks > 0
    expert_order = _excl_cumsum(owns.astype(i32), 0)
    later = (experts[None, :] > experts[:, None]) & owns[None, :]
    expert_next = jnp.min(jnp.where(later, experts[None, :], N_EXPERTS), axis=1)
    expert_next = jnp.where(expert_next < N_EXPERTS, expert_next, -1).astype(i32)
    return (block_expert.astype(i32), n_used.astype(i32), unit_src, tile_units,
            expert_order.astype(i32), expert_next)


def _layer(x, ln1_w, w_in, conv_w, conv_b, w_rg_a, b_rg_a, w_rg_i, b_rg_i, rg_lambda, b_forget,
           q_norm_w, k_norm_w, rec_out_norm_w, attn_out_norm_w, w_out, ln2_w, w_router, b_router,
           w_gate_up, b_gate_up, w_down, b_down):
    batch, seq, _ = x.shape
    t = batch * seq
    x2d = x.reshape(t, D_MODEL)

    o_q = 2 * D_REC
    o_k = o_q + D_ATT
    o_v = o_k + D_ATT
    o_f = o_v + D_ATT
    w_t = w_in.T.astype(BF16)
    w_ft = jnp.pad(w_t[o_f:], ((0, LANES - N_HEADS), (0, 0)))
    scale = HEAD_DIM ** -0.5
    u, g, qp, kp, vp = _in_proj(
        x2d, ln1_w.reshape(1, D_MODEL), w_in[:, :o_q].astype(BF16), w_t[o_q:o_k], w_t[o_k:o_v],
        w_t[o_v:o_f], w_ft, _pad_lanes(b_forget).reshape(LANES, 1),
        (q_norm_w * scale).reshape(HEAD_DIM, 1), k_norm_w.reshape(HEAD_DIM, 1), batch, seq)

    rec_vec = jnp.concatenate(
        [conv_w, jnp.stack([conv_b, b_rg_a, b_rg_i, rg_lambda, rec_out_norm_w])])
    rec_n = _rglru(u.reshape(batch, seq, D_REC), g.reshape(batch, seq, D_REC), rec_vec,
                   _gate_matrix(w_rg_a, w_rg_i).astype(BF16))

    att = _attention(qp, kp, vp)

    rw = jnp.pad(w_router, ((0, 0), (0, LANES - N_EXPERTS)))
    rw_hi = rw.astype(BF16)
    rw_lo = (rw - rw_hi.astype(F32)).astype(BF16)
    x1, xs, meta, units = _out_proj(
        x2d, rec_n.reshape(t, D_REC), att.reshape(t, D_ATT), attn_out_norm_w.reshape(1, D_ATT),
        w_out[:D_REC].astype(BF16), w_out[D_REC:].astype(BF16), ln2_w.reshape(1, D_MODEL),
        rw_hi, rw_lo, _pad_lanes(b_router))

    n_tiles = t // TM_OUT
    max_units = t * TOP_K // UNIT + n_tiles * N_EXPERTS
    n_blocks = -(-max_units // BLOCK_UNITS) + N_EXPERTS
    block_expert, n_used, unit_src, tile_units, expert_order, expert_next = _dispatch_tables(
        units[:, 0, :N_EXPERTS].astype(jnp.int32), n_blocks)

    ys = _experts(block_expert, n_used, unit_src, expert_order, expert_next, xs, w_gate_up, b_gate_up,
                  w_down, b_down)
    out = _combine(tile_units, x1, meta, ys)
    return out.reshape(batch, seq, D_MODEL)


def kernel(x, ln1_w, w_in, conv_w, conv_b, w_rg_a, b_rg_a, w_rg_i, b_rg_i, rg_lambda, b_forget,
           q_norm_w, k_norm_w, rec_out_norm_w, attn_out_norm_w, w_out, ln2_w, w_router, b_router,
           w_gate_up, b_gate_up, w_down, b_down):
    for l in range(ln1_w.shape[0]):
        x = _layer(x, ln1_w[l], w_in[l], conv_w[l], conv_b[l], w_rg_a[l], b_rg_a[l], w_rg_i[l],
                   b_rg_i[l], rg_lambda[l], b_forget[l], q_norm_w[l], k_norm_w[l], rec_out_norm_w[l],
                   attn_out_norm_w[l], w_out[l], ln2_w[l], w_router[l], b_router[l], w_gate_up[l],
                   b_gate_up[l], w_down[l], b_down[l])
    return x
```

```python
import functools

import jax
import jax.numpy as jnp
from jax import lax
from jax.experimental import pallas as pl
from jax.experimental.pallas import tpu as pltpu

D_MODEL = 1024
D_REC = 512
N_REC_BLOCKS = 8
REC_BLOCK = 64
CONV_WIDTH = 4
RGLRU_C = 8.0
D_ATT = 512
HEAD_DIM = 64
N_HEADS = 8
N_EXPERTS = 32
TOP_K = 4
D_EXPERT = 1024
SWIGLU_LIMIT = 7.0
SWIGLU_ALPHA = 1.702
EPS = 1e-6

LANES = 128
SLAB = LANES
VMEM_LIMIT = 56 * 1024 * 1024

F32 = jnp.float32
BF16 = jnp.bfloat16

O_U = 0
O_G = O_U + D_REC
N_PROJ = O_G + D_REC

L_C = HEAD_DIM
L_ONE = HEAD_DIM + 3

TM_IN = 512
TS_REC = 512
TQ = 512
TK = 512
HEADS_PER_STEP = 8
TM_OUT = 512
TM_E = 512

UNIT = 16
TILE_SLOTS = -(-(TM_OUT * TOP_K + N_EXPERTS * (UNIT - 1)) // LANES) * LANES
TILE_UNITS = TILE_SLOTS // UNIT
BLOCK_UNITS = TM_E // UNIT
SLOT_CHUNK = 512


def _split3(v):
    hi = v.astype(BF16).astype(F32)
    r = v - hi
    mid = r.astype(BF16).astype(F32)
    lo = (r - mid).astype(BF16).astype(F32)
    return hi, mid, lo


def _dot(a, b):
    return jnp.dot(a, b, preferred_element_type=F32)


def _sigmoid(x):
    return 0.5 * jnp.tanh(0.5 * x) + 0.5


def _dot_t(a, b):
    return lax.dot_general(a, b, (((0,), (0,)), ((), ())), preferred_element_type=F32)


def _in_proj_kernel(x_ref, ln_ref, w_ref, wqt_ref, wkt_ref, wvt_ref, wft_ref, bf_ref, qw_ref, kw_ref,
                    u_ref, g_ref, q_ref, k_ref, v_ref, carry_ref, *, tiles_per_seq):
    i = pl.program_id(0)
    tm = x_ref.shape[0]

    @pl.when(i % tiles_per_seq == 0)
    def _():
        carry_ref[...] = jnp.zeros_like(carry_ref)

    x = x_ref[...]
    ms = jnp.mean(x * x, axis=-1, keepdims=True)
    h = (x * lax.rsqrt(ms + EPS) * ln_ref[...]).astype(BF16)
    nt = (((1,), (1,)), ((), ()))

    f_t = lax.dot_general(wft_ref[...], h, nt, preferred_element_type=F32) + bf_ref[...]
    logf_t = jnp.minimum(f_t, 0.0) - jnp.log1p(jnp.exp(-jnp.abs(f_t)))
    row = lax.broadcasted_iota(jnp.int32, (tm, tm), 0)
    col = lax.broadcasted_iota(jnp.int32, (tm, tm), 1)
    upto = (row <= col).astype(BF16)
    hi, mid, lo = _split3(logf_t)
    c_t = (_dot(hi.astype(BF16), upto) + _dot(mid.astype(BF16), upto)
           + _dot(lo.astype(BF16), upto) + carry_ref[...])
    carry_ref[...] = c_t[:, tm - 1:tm]

    qt_all = lax.dot_general(wqt_ref[...], h, nt, preferred_element_type=F32)
    kt_all = lax.dot_general(wkt_ref[...], h, nt, preferred_element_type=F32)
    spare = lax.broadcasted_iota(jnp.int32, (SLAB - HEAD_DIM, tm), 0) + HEAD_DIM
    qw = qw_ref[...]
    kw = kw_ref[...]

    def normed(rows, weight):
        inv = lax.rsqrt(jnp.sum(rows * rows, axis=0, keepdims=True) * (1.0 / HEAD_DIM) + EPS)
        return rows * inv * weight

    for hd in range(N_HEADS):
        rows = slice(hd * HEAD_DIM, (hd + 1) * HEAD_DIM)
        c_hi, c_mid, c_lo = _split3(c_t[hd:hd + 1, :])
        q_extra = jnp.where(spare == L_C, c_hi, 0.0)
        q_extra = jnp.where(spare == L_C + 1, c_mid, q_extra)
        q_extra = jnp.where(spare == L_C + 2, c_lo, q_extra)
        q_extra = jnp.where((spare >= L_ONE) & (spare < L_ONE + 3), 1.0, q_extra)
        k_extra = jnp.where(spare < L_C + 3, 1.0, 0.0)
        k_extra = jnp.where(spare == L_ONE, -c_hi, k_extra)
        k_extra = jnp.where(spare == L_ONE + 1, -c_mid, k_extra)
        k_extra = jnp.where(spare == L_ONE + 2, -c_lo, k_extra)
        q_ref[0, hd] = jnp.concatenate([normed(qt_all[rows, :], qw), q_extra], axis=0).T.astype(BF16)
        k_ref[0, hd] = jnp.concatenate([normed(kt_all[rows, :], kw), k_extra], axis=0).T.astype(BF16)

    vt_all = lax.dot_general(wvt_ref[...], h, (((1,), (1,)), ((), ())), preferred_element_type=F32)
    spare = lax.broadcasted_iota(jnp.int32, (SLAB - HEAD_DIM, tm), 0)
    ones_row = jnp.where(spare == 0, 1.0, 0.0)
    for hd in range(N_HEADS):
        v_ref[0, hd] = jnp.concatenate(
            [vt_all[hd * HEAD_DIM:(hd + 1) * HEAD_DIM, :], ones_row], axis=0).astype(BF16)
    u_ref[...] = _dot(h, w_ref[:, O_U:O_U + D_REC])
    g_ref[...] = _dot(h, w_ref[:, O_G:O_G + D_REC])


def _in_proj(x2d, ln1_w, w_all, w_qt, w_kt, w_vt, w_ft, bf_col, qw_col, kw_col, batch, seq):
    t = x2d.shape[0]
    tiles_per_seq = seq // TM_IN
    slab_shape = jax.ShapeDtypeStruct((batch, N_HEADS, seq, SLAB), BF16)
    slab_spec = pl.BlockSpec((1, N_HEADS, TM_IN, SLAB),
                             lambda i: (i // tiles_per_seq, 0, i % tiles_per_seq, 0))
    slab_t_shape = jax.ShapeDtypeStruct((batch, N_HEADS, SLAB, seq), BF16)
    slab_t_spec = pl.BlockSpec((1, N_HEADS, SLAB, TM_IN),
                               lambda i: (i // tiles_per_seq, 0, 0, i % tiles_per_seq))
    row_spec = pl.BlockSpec((TM_IN, D_REC), lambda i: (i, 0))
    const = lambda shape: pl.BlockSpec(shape, lambda i: (0,) * len(shape))
    return pl.pallas_call(
        functools.partial(_in_proj_kernel, tiles_per_seq=tiles_per_seq),
        out_shape=(jax.ShapeDtypeStruct((t, D_REC), F32), jax.ShapeDtypeStruct((t, D_REC), F32),
                   slab_shape, slab_shape, slab_t_shape),
        grid=(t // TM_IN,),
        in_specs=[pl.BlockSpec((TM_IN, D_MODEL), lambda i: (i, 0)),
                  const((1, D_MODEL)), const((D_MODEL, N_PROJ)), const((D_ATT, D_MODEL)),
                  const((D_ATT, D_MODEL)), const((D_ATT, D_MODEL)), const((LANES, D_MODEL)),
                  const((LANES, 1)), const((HEAD_DIM, 1)), const((HEAD_DIM, 1))],
        out_specs=(row_spec, row_spec, slab_spec, slab_spec, slab_t_spec),
        scratch_shapes=[pltpu.VMEM((LANES, 1), F32)],
        compiler_params=pltpu.CompilerParams(dimension_semantics=("arbitrary",),
                                             vmem_limit_bytes=VMEM_LIMIT),
        name="in_proj",
    )(x2d, ln1_w, w_all, w_qt, w_kt, w_vt, w_ft, bf_col, qw_col, kw_col)


R_CONV_B = CONV_WIDTH
R_B_A = CONV_WIDTH + 1
R_B_I = CONV_WIDTH + 2
R_LAMBDA = CONV_WIDTH + 3
R_NORM_W = CONV_WIDTH + 4
N_REC_ROWS = CONV_WIDTH + 5


def _rglru_kernel(u_ref, g_ref, vec_ref, wg_ref, o_ref, ext_ref, a_ref, b_ref, h_ref, hcar_ref):
    s = pl.program_id(1)
    ts = u_ref.shape[1]
    pad = 8
    row = lambda r: vec_ref[r:r + 1, :]

    @pl.when(s == 0)
    def _():
        ext_ref[0:pad, :] = jnp.zeros((pad, D_REC), F32)
        hcar_ref[...] = jnp.zeros_like(hcar_ref)

    @pl.when(s != 0)
    def _():
        ext_ref[0:pad, :] = ext_ref[ts:ts + pad, :]

    u = u_ref[0]
    ext_ref[pad:pad + ts, :] = u
    xc = row(R_CONV_B) + u * row(CONV_WIDTH - 1)
    for tap in range(CONV_WIDTH - 1):
        d = CONV_WIDTH - 1 - tap
        xc = xc + ext_ref[pad - d:pad - d + ts, :] * row(tap)

    gates = _dot(xc.astype(BF16), wg_ref[...])
    r = _sigmoid(gates[:, :D_REC] + row(R_B_A))
    ig = _sigmoid(gates[:, D_REC:] + row(R_B_I))
    nlam = -row(R_LAMBDA)
    softplus = jnp.maximum(nlam, 0.0) + jnp.log1p(jnp.exp(-jnp.abs(nlam)))
    log_a = -RGLRU_C * r * softplus
    a = jnp.exp(log_a)
    a_ref[...] = a
    b_ref[...] = jnp.sqrt(1.0 - jnp.exp(2.0 * log_a)) * (ig * xc)

    def step(t, h):
        h = a_ref[pl.ds(t, 1), :] * h + b_ref[pl.ds(t, 1), :]
        h_ref[pl.ds(t, 1), :] = h
        return h

    hcar_ref[...] = lax.fori_loop(0, ts, step, hcar_ref[...], unroll=8)

    rec = h_ref[...] * jax.nn.gelu(g_ref[0])
    ms = jnp.mean(rec * rec, axis=-1, keepdims=True)
    o_ref[0] = (rec * lax.rsqrt(ms + EPS) * row(R_NORM_W)).astype(BF16)


def _rglru(u, g, rec_vec, w_gates):
    batch, seq, _ = u.shape
    seq_spec = pl.BlockSpec((1, TS_REC, D_REC), lambda b, s: (b, s, 0))
    const = lambda shape: pl.BlockSpec(shape, lambda b, s: (0,) * len(shape))
    return pl.pallas_call(
        _rglru_kernel,
        out_shape=jax.ShapeDtypeStruct((batch, seq, D_REC), BF16),
        grid=(batch, seq // TS_REC),
        in_specs=[seq_spec, seq_spec, const((N_REC_ROWS, D_REC)), const((D_REC, 2 * D_REC))],
        out_specs=seq_spec,
        scratch_shapes=[pltpu.VMEM((TS_REC + 8, D_REC), F32), pltpu.VMEM((TS_REC, D_REC), F32),
                        pltpu.VMEM((TS_REC, D_REC), F32), pltpu.VMEM((TS_REC, D_REC), F32),
                        pltpu.VMEM((1, D_REC), F32)],
        compiler_params=pltpu.CompilerParams(dimension_semantics=("arbitrary", "arbitrary"),
                                             vmem_limit_bytes=VMEM_LIMIT),
        name="rglru",
    )(u, g, rec_vec, w_gates)


def _attn_kernel(q_ref, k_ref, v_ref, o_ref):
    qi = pl.program_id(2)
    hb = q_ref.shape[1]
    tq = q_ref.shape[2]
    qs = [q_ref[0, h] for h in range(hb)]

    def step(j, carry, masked):
        start = pl.multiple_of(j * TK, TK)
        scores = []
        for h in range(hb):
            kj = k_ref[0, h, pl.ds(start, TK), :]
            s = lax.dot_general(kj, qs[h], (((1,), (1,)), ((), ())), preferred_element_type=F32)
            if masked:
                key = lax.broadcasted_iota(jnp.int32, (TK, tq), 0)
                qry = lax.broadcasted_iota(jnp.int32, (TK, tq), 1)
                s = jnp.where(qry >= key, s, -jnp.inf)
            scores.append(s)
        m_news = [jnp.maximum(carry[h][0], jnp.max(scores[h], axis=0, keepdims=True))
                  for h in range(hb)]
        probs = [jnp.exp(scores[h] - m_news[h]).astype(BF16) for h in range(hb)]
        out = []
        for h in range(hb):
            m, acc = carry[h]
            vj = v_ref[0, h, :, pl.ds(start, TK)]
            out.append((m_news[h], jnp.exp(m - m_news[h]) * acc + _dot(vj, probs[h])))
        return tuple(out)

    init = tuple((jnp.full((1, tq), -jnp.inf, F32), jnp.zeros((SLAB, tq), F32)) for _ in range(hb))
    carry = lax.fori_loop(0, qi, lambda j, c: step(j, c, False), init)
    carry = step(qi, carry, True)
    outs = [carry[h][1][:HEAD_DIM, :] / carry[h][1][HEAD_DIM:HEAD_DIM + 1, :] for h in range(hb)]
    for pair in range(hb // 2):
        both = jnp.concatenate([outs[2 * pair], outs[2 * pair + 1]], axis=0)
        o_ref[0, :, pair * LANES:(pair + 1) * LANES] = both.T


def _attention(qp, kp, vp):
    batch, _, seq, _ = kp.shape
    hb = HEADS_PER_STEP
    kv_spec = pl.BlockSpec((1, hb, seq, SLAB), lambda b, h, i: (b, h, 0, 0))
    vt_spec = pl.BlockSpec((1, hb, SLAB, seq), lambda b, h, i: (b, h, 0, 0))
    return pl.pallas_call(
        _attn_kernel,
        out_shape=jax.ShapeDtypeStruct((batch, seq, D_ATT), F32),
        grid=(batch, N_HEADS // hb, seq // TQ),
        in_specs=[pl.BlockSpec((1, hb, TQ, SLAB), lambda b, h, i: (b, h, i, 0)), kv_spec, vt_spec],
        out_specs=pl.BlockSpec((1, TQ, hb * HEAD_DIM), lambda b, h, i: (b, i, h)),
        compiler_params=pltpu.CompilerParams(
            dimension_semantics=("arbitrary", "arbitrary", "arbitrary"),
            vmem_limit_bytes=VMEM_LIMIT),
        name="attn",
    )(qp, kp, vp)


def _slot_onehot(slots, first, width):
    tm = slots[0].shape[0]
    col = lax.broadcasted_iota(jnp.int32, (tm, width), 1) + first
    hit = jnp.zeros((tm, width), F32)
    for slot in slots:
        hit = jnp.where(col == slot, 1.0, hit)
    return hit.astype(BF16)


def _out_proj_kernel(x_ref, rec_ref, att_ref, aw_ref, wr_ref, wa_ref, ln_ref, rwh_ref, rwl_ref, rb_ref,
                     x1_ref, xs_ref, meta_ref, units_ref):
    tm = x_ref.shape[0]

    att = att_ref[...]
    ms = jnp.sum(att * att, axis=-1, keepdims=True) * (1.0 / D_ATT)
    att_n = (att * lax.rsqrt(ms + EPS) * aw_ref[...]).astype(BF16)
    x1 = x_ref[...] + _dot(rec_ref[...], wr_ref[...]) + _dot(att_n, wa_ref[...])
    x1_ref[...] = x1

    ms2 = jnp.mean(x1 * x1, axis=-1, keepdims=True)
    h2 = x1 * lax.rsqrt(ms2 + EPS) * ln_ref[...]

    h_hi = h2.astype(BF16)
    h_lo = (h2 - h_hi.astype(F32)).astype(BF16)
    nt = (((1,), (1,)), ((), ()))
    logits = (lax.dot_general(rwh_ref[...], h_hi, nt, preferred_element_type=F32)
              + lax.dot_general(rwh_ref[...], h_lo, nt, preferred_element_type=F32)
              + lax.dot_general(rwl_ref[...], h_hi, nt, preferred_element_type=F32) + rb_ref[...])
    expert = lax.broadcasted_iota(jnp.int32, (LANES, tm), 0)
    expert_f = expert.astype(F32)
    l = jnp.where(expert < N_EXPERTS, logits, -jnp.inf)

    vals, idxs, sels = [], [], []
    for _ in range(TOP_K):
        m = jnp.max(l, axis=0, keepdims=True)
        idx = jnp.min(jnp.where(l == m, expert_f, float(LANES)), axis=0, keepdims=True)
        sel = expert_f == idx
        vals.append(m)
        idxs.append(idx)
        sels.append(sel)
        l = jnp.where(sel, -jnp.inf, l)
    exps = [jnp.exp(v - vals[0]) for v in vals]
    denom = exps[0] + exps[1] + exps[2] + exps[3]

    onehot = jnp.zeros((LANES, tm), F32)
    for sel in sels:
        onehot = jnp.where(sel, 1.0, onehot)
    row = lax.broadcasted_iota(jnp.int32, (tm, tm), 0)
    col = lax.broadcasted_iota(jnp.int32, (tm, tm), 1)
    earlier_token = (row < col).astype(BF16)
    rank = _dot(onehot.astype(BF16), earlier_token)
    count = jnp.sum(onehot, axis=1, keepdims=True)
    units = jnp.floor((count + (UNIT - 1)) * (1.0 / UNIT))
    e_row = lax.broadcasted_iota(jnp.int32, (LANES, LANES), 0)
    e_col = lax.broadcasted_iota(jnp.int32, (LANES, LANES), 1)
    earlier_expert = (e_col < e_row).astype(BF16)
    units_wide = jnp.broadcast_to(units, (LANES, LANES))
    first_unit = _dot(earlier_expert, units_wide.astype(BF16))[:, 0:1]
    units_ref[0] = units_wide.T[0:8, :]
    place = first_unit * float(UNIT) + rank

    meta_t = jnp.zeros((LANES, tm), F32)
    for k in range(TOP_K):
        slot = jnp.sum(jnp.where(sels[k], place, 0.0), axis=0, keepdims=True)
        meta_t = jnp.where(expert == k, idxs[k], meta_t)
        meta_t = jnp.where(expert == TOP_K + k, exps[k] / denom, meta_t)
        meta_t = jnp.where(expert == 2 * TOP_K + k, slot, meta_t)
    meta = meta_t.T
    meta_ref[...] = meta
    slots = [meta[:, 2 * TOP_K + k:2 * TOP_K + k + 1].astype(jnp.int32) for k in range(TOP_K)]

    for c in range(TILE_SLOTS // SLOT_CHUNK):
        sel_t = _slot_onehot(slots, c * SLOT_CHUNK, SLOT_CHUNK)
        xs_ref[c * SLOT_CHUNK:(c + 1) * SLOT_CHUNK, :] = _dot_t(sel_t, h_hi).astype(BF16)


def _out_proj(x2d, rec_n, att, aw_slab, w_rec, w_att, ln2_w, rw_hi, rw_lo, rb_pad):
    t = x2d.shape[0]
    n_tiles = t // TM_OUT
    rows = lambda n: pl.BlockSpec((TM_OUT, n), lambda i: (i, 0))
    const = lambda shape: pl.BlockSpec(shape, lambda i: (0,) * len(shape))
    return pl.pallas_call(
        _out_proj_kernel,
        out_shape=(jax.ShapeDtypeStruct((t, D_MODEL), F32),
                   jax.ShapeDtypeStruct((n_tiles * TILE_SLOTS, D_MODEL), BF16),
                   jax.ShapeDtypeStruct((t, LANES), F32),
                   jax.ShapeDtypeStruct((n_tiles, 8, LANES), F32)),
        grid=(n_tiles,),
        in_specs=[rows(D_MODEL), rows(D_REC), rows(D_ATT), const((1, D_ATT)),
                  const((D_REC, D_MODEL)), const((D_ATT, D_MODEL)), const((1, D_MODEL)),
                  const((LANES, D_MODEL)), const((LANES, D_MODEL)), const((LANES, 1))],
        out_specs=(rows(D_MODEL), pl.BlockSpec((TILE_SLOTS, D_MODEL), lambda i: (i, 0)), rows(LANES),
                   pl.BlockSpec((1, 8, LANES), lambda i: (i, 0, 0))),
        compiler_params=pltpu.CompilerParams(dimension_semantics=("arbitrary",),
                                             vmem_limit_bytes=VMEM_LIMIT),
        name="out_proj",
    )(x2d, rec_n, att, aw_slab, w_rec, w_att, ln2_w, rw_hi, rw_lo, rb_pad)


def _start_units(unit_ref, base, n_units, src_hbm, dst, sem):
    def body(j, carry):
        src_row = pl.multiple_of(unit_ref[base + j] * UNIT, UNIT)
        dst_row = pl.multiple_of(j * UNIT, UNIT)
        pltpu.make_async_copy(src_hbm.at[pl.ds(src_row, UNIT), :],
                              dst.at[pl.ds(dst_row, UNIT), :], sem).start()
        return carry
    lax.fori_loop(0, n_units, body, 0, unroll=8)


def _wait_units(n_units, src_hbm, dst, sem):
    pltpu.make_async_copy(src_hbm.at[pl.ds(0, n_units * UNIT), :], dst, sem).wait()


def _experts_kernel(be_ref, nblk_ref, unit_ref, order_ref, next_ref, xs_hbm, wgu_hbm, bgu_ref, wd_hbm,
                    bd_ref, ys_ref, xbuf, wgu_f32, wd_f32, wgu_bf, wd_bf, sem, wsem):
    b = pl.program_id(0)
    nblk = nblk_ref[0]
    slot = b % 2
    e = be_ref[b]

    def weight_copies(expert, wslot):
        return (pltpu.make_async_copy(wgu_hbm.at[expert], wgu_f32.at[wslot], wsem.at[wslot]),
                pltpu.make_async_copy(wd_hbm.at[expert], wd_f32.at[wslot], wsem.at[wslot]))

    @pl.when(b == 0)
    def _():
        _start_units(unit_ref, 0, BLOCK_UNITS, xs_hbm, xbuf.at[0], sem.at[0])
        for copy in weight_copies(e, 0):
            copy.start()

    @pl.when(b + 1 < nblk)
    def _():
        _start_units(unit_ref, (b + 1) * BLOCK_UNITS, BLOCK_UNITS, xs_hbm, xbuf.at[1 - slot],
                     sem.at[1 - slot])

    new_expert = jnp.logical_or(b == 0, e != be_ref[jnp.maximum(b - 1, 0)])

    @pl.when(jnp.logical_and(b < nblk, new_expert))
    def _():
        wslot = order_ref[e] % 2
        for copy in weight_copies(e, wslot):
            copy.wait()
        wgu_bf[...] = wgu_f32[wslot].astype(BF16)
        wd_bf[...] = wd_f32[wslot].astype(BF16)
        nxt = next_ref[e]

        @pl.when(nxt >= 0)
        def _():
            for copy in weight_copies(nxt, 1 - wslot):
                copy.start()

    @pl.when(b < nblk)
    def _():
        _wait_units(BLOCK_UNITS, xs_hbm, xbuf.at[slot], sem.at[slot])
        gu = _dot(xbuf[slot], wgu_bf[...]) + bgu_ref[0]
        g = jnp.minimum(gu[:, :D_EXPERT], SWIGLU_LIMIT)
        u = jnp.clip(gu[:, D_EXPERT:], -SWIGLU_LIMIT, SWIGLU_LIMIT)
        act = (u + 1.0) * (g * _sigmoid(SWIGLU_ALPHA * g))
        ys_ref[...] = (_dot(act.astype(BF16), wd_bf[...]) + bd_ref[0]).astype(BF16)

    @pl.when(b >= nblk)
    def _():
        ys_ref[...] = jnp.zeros_like(ys_ref)


def _experts(block_expert, n_blocks_used, unit_src, expert_order, expert_next, xs, w_gate_up,
             b_gate_up, w_down, b_down):
    n_blocks = block_expert.shape[0]
    bias_map = lambda b, be, nb, us, eo, en: (be[b], 0, 0)
    grid_spec = pltpu.PrefetchScalarGridSpec(
        num_scalar_prefetch=5,
        grid=(n_blocks,),
        in_specs=[pl.BlockSpec(memory_space=pl.ANY),
                  pl.BlockSpec(memory_space=pl.ANY),
                  pl.BlockSpec((1, 1, 2 * D_EXPERT), bias_map),
                  pl.BlockSpec(memory_space=pl.ANY),
                  pl.BlockSpec((1, 1, D_MODEL), bias_map)],
        out_specs=pl.BlockSpec((TM_E, D_MODEL), lambda b, be, nb, us, eo, en: (b, 0)),
        scratch_shapes=[pltpu.VMEM((2, TM_E, D_MODEL), BF16),
                        pltpu.VMEM((2, D_MODEL, 2 * D_EXPERT), F32),
                        pltpu.VMEM((2, D_EXPERT, D_MODEL), F32),
                        pltpu.VMEM((D_MODEL, 2 * D_EXPERT), BF16),
                        pltpu.VMEM((D_EXPERT, D_MODEL), BF16),
                        pltpu.SemaphoreType.DMA((2,)),
                        pltpu.SemaphoreType.DMA((2,))],
    )
    return pl.pallas_call(
        _experts_kernel,
        out_shape=jax.ShapeDtypeStruct((n_blocks * TM_E, D_MODEL), BF16),
        grid_spec=grid_spec,
        compiler_params=pltpu.CompilerParams(dimension_semantics=("arbitrary",),
                                             vmem_limit_bytes=VMEM_LIMIT),
        name="experts",
    )(block_expert, n_blocks_used, unit_src, expert_order, expert_next, xs, w_gate_up,
      b_gate_up.reshape(N_EXPERTS, 1, 2 * D_EXPERT), w_down, b_down.reshape(N_EXPERTS, 1, D_MODEL))


def _combine_kernel(unit_ref, x1_ref, meta_ref, ys_hbm, o_ref, buf, sem):
    i = pl.program_id(0)
    n = pl.num_programs(0)
    slot = i % 2
    tm = x1_ref.shape[0]

    @pl.when(i == 0)
    def _():
        _start_units(unit_ref, 0, TILE_UNITS, ys_hbm, buf.at[0], sem.at[0])

    @pl.when(i + 1 < n)
    def _():
        _start_units(unit_ref, (i + 1) * TILE_UNITS, TILE_UNITS, ys_hbm, buf.at[1 - slot],
                     sem.at[1 - slot])

    meta = meta_ref[...]
    slots = [meta[:, 2 * TOP_K + k:2 * TOP_K + k + 1].astype(jnp.int32) for k in range(TOP_K)]
    gates = [meta[:, TOP_K + k:TOP_K + k + 1] for k in range(TOP_K)]
    _wait_units(TILE_UNITS, ys_hbm, buf.at[slot], sem.at[slot])

    acc = x1_ref[...]
    for c in range(TILE_SLOTS // SLOT_CHUNK):
        first = c * SLOT_CHUNK
        col = lax.broadcasted_iota(jnp.int32, (tm, SLOT_CHUNK), 1) + first
        weight = jnp.zeros((tm, SLOT_CHUNK), F32)
        for k in range(TOP_K):
            weight = jnp.where(col == slots[k], gates[k], weight)
        acc = acc + _dot(weight.astype(BF16), buf[slot, first:first + SLOT_CHUNK, :])
    o_ref[...] = acc


def _combine(tile_units, x1, meta, ys):
    t = x1.shape[0]
    grid_spec = pltpu.PrefetchScalarGridSpec(
        num_scalar_prefetch=1,
        grid=(t // TM_OUT,),
        in_specs=[pl.BlockSpec((TM_OUT, D_MODEL), lambda i, us: (i, 0)),
                  pl.BlockSpec((TM_OUT, LANES), lambda i, us: (i, 0)),
                  pl.BlockSpec(memory_space=pl.ANY)],
        out_specs=pl.BlockSpec((TM_OUT, D_MODEL), lambda i, us: (i, 0)),
        scratch_shapes=[pltpu.VMEM((2, TILE_SLOTS, D_MODEL), BF16),
                        pltpu.SemaphoreType.DMA((2,))],
    )
    return pl.pallas_call(
        _combine_kernel,
        out_shape=jax.ShapeDtypeStruct((t, D_MODEL), F32),
        grid_spec=grid_spec,
        compiler_params=pltpu.CompilerParams(dimension_semantics=("arbitrary",),
                                             vmem_limit_bytes=VMEM_LIMIT),
        name="combine",
    )(tile_units, x1, meta, ys)


def _gate_matrix(w_a, w_i):
    g = w_a.shape[0]
    eye = jnp.eye(g, dtype=w_a.dtype)
    both = jnp.stack([w_a, w_i])
    return jnp.einsum('sgij,gh->gishj', both, eye).reshape(g * REC_BLOCK, 2 * g * REC_BLOCK)


def _pad_lanes(v, n=LANES):
    return jnp.pad(v, (0, n - v.shape[0])).reshape(1, n)


def _excl_cumsum(a, axis):
    return jnp.cumsum(a, axis=axis) - a


def _pick(onehot, table):
    if table.ndim == 1:
        return jnp.sum(onehot * table[None, :], axis=1)
    return jnp.sum(onehot[:, :, None] * table[None, :, :], axis=1)


def _dispatch_tables(units, n_blocks):
    n_tiles = units.shape[0]
    i32 = jnp.int32
    tile_first = _excl_cumsum(units, 1)
    tile_end = tile_first + units
    run_first = _excl_cumsum(units, 0)
    run_end = run_first + units
    region_units = jnp.sum(units, axis=0)
    region_blocks = (region_units + BLOCK_UNITS - 1) // BLOCK_UNITS
    block_end = jnp.cumsum(region_blocks)
    region_first = (block_end - region_blocks) * BLOCK_UNITS
    n_used = block_end[-1:]
    experts = jnp.arange(N_EXPERTS, dtype=i32)
    tiles = jnp.arange(n_tiles, dtype=i32)

    blocks = jnp.arange(n_blocks, dtype=i32)
    block_expert = jnp.minimum(
        jnp.sum((block_end[None, :] <= blocks[:, None]).astype(i32), axis=1), N_EXPERTS - 1)

    g = jnp.arange(n_blocks * BLOCK_UNITS, dtype=i32)
    e = jnp.minimum(jnp.sum((block_end[None, :] * BLOCK_UNITS <= g[:, None]).astype(i32), axis=1),
                    N_EXPERTS - 1)
    e_hot = (e[:, None] == experts[None, :]).astype(i32)
    local = g - _pick(e_hot, region_first)
    valid = (local < _pick(e_hot, region_units)) & (g < n_used[0] * BLOCK_UNITS)
    ends = _pick(e_hot, run_end.T)
    tile = jnp.minimum(jnp.sum((ends <= local[:, None]).astype(i32), axis=1), n_tiles - 1)
    t_hot = (tile[:, None] == tiles[None, :]).astype(i32)
    src = (tile * TILE_UNITS + jnp.sum(_pick(t_hot, tile_first - run_first) * e_hot, axis=1) + local)
    unit_src = jnp.where(valid, src, 0).astype(i32)

    j = jnp.arange(TILE_UNITS, dtype=i32)
    ej = jnp.minimum(jnp.sum((tile_end[:, None, :] <= j[None, :, None]).astype(i32), axis=2),
                     N_EXPERTS - 1)
    ej_hot = (ej[:, :, None] == experts[None, None, :]).astype(i32)
    used = j[None, :] < jnp.sum(units, axis=1)[:, None]
    base = region_first[None, :] + run_first - tile_first
    dst = jnp.sum(ej_hot * base[:, None, :], axis=2) + j[None, :]
    tile_units = jnp.where(used, dst, 0).astype(i32).reshape(-1)
    owns = region_blocks > 0
    expert_order = _excl_cumsum(owns.astype(i32), 0)
    later = (experts[None, :] > experts[:, None]) & owns[None, :]
    expert_next = jnp.min(jnp.where(later, experts[None, :], N_EXPERTS), axis=1)
    expert_next = jnp.where(expert_next < N_EXPERTS, expert_next, -1).astype(i32)
    return (block_expert.astype(i32), n_used.astype(i32), unit_src, tile_units,
            expert_order.astype(i32), expert_next)


def _layer(x, ln1_w, w_in, conv_w, conv_b, w_rg_a, b_rg_a, w_rg_i, b_rg_i, rg_lambda, b_forget,
           q_norm_w, k_norm_w, rec_out_norm_w, attn_out_norm_w, w_out, ln2_w, w_router, b_router,
           w_gate_up, b_gate_up, w_down, b_down):
    batch, seq, _ = x.shape
    t = batch * seq
    x2d = x.reshape(t, D_MODEL)

    o_q = 2 * D_REC
    o_k = o_q + D_ATT
    o_v = o_k + D_ATT
    o_f = o_v + D_ATT
    w_t = w_in.T.astype(BF16)
    w_ft = jnp.pad(w_t[o_f:], ((0, LANES - N_HEADS), (0, 0)))
    scale = HEAD_DIM ** -0.5
    u, g, qp, kp, vp = _in_proj(
        x2d, ln1_w.reshape(1, D_MODEL), w_in[:, :o_q].astype(BF16), w_t[o_q:o_k], w_t[o_k:o_v],
        w_t[o_v:o_f], w_ft, _pad_lanes(b_forget).reshape(LANES, 1),
        (q_norm_w * scale).reshape(HEAD_DIM, 1), k_norm_w.reshape(HEAD_DIM, 1), batch, seq)

    rec_vec = jnp.concatenate(
        [conv_w, jnp.stack([conv_b, b_rg_a, b_rg_i, rg_lambda, rec_out_norm_w])])
    rec_n = _rglru(u.reshape(batch, seq, D_REC), g.reshape(batch, seq, D_REC), rec_vec,
                   _gate_matrix(w_rg_a, w_rg_i).astype(BF16))

    att = _attention(qp, kp, vp)

    rw = jnp.pad(w_router.T, ((0, LANES - N_EXPERTS), (0, 0)))
    rw_hi = rw.astype(BF16)
    rw_lo = (rw - rw_hi.astype(F32)).astype(BF16)
    x1, xs, meta, units = _out_proj(
        x2d, rec_n.reshape(t, D_REC), att.reshape(t, D_ATT), attn_out_norm_w.reshape(1, D_ATT),
        w_out[:D_REC].astype(BF16), w_out[D_REC:].astype(BF16), ln2_w.reshape(1, D_MODEL),
        rw_hi, rw_lo, _pad_lanes(b_router).reshape(LANES, 1))

    n_tiles = t // TM_OUT
    max_units = t * TOP_K // UNIT + n_tiles * N_EXPERTS
    n_blocks = -(-max_units // BLOCK_UNITS) + N_EXPERTS
    block_expert, n_used, unit_src, tile_units, expert_order, expert_next = _dispatch_tables(
        units[:, 0, :N_EXPERTS].astype(jnp.int32), n_blocks)

    ys = _experts(block_expert, n_used, unit_src, expert_order, expert_next, xs, w_gate_up, b_gate_up,
                  w_down, b_down)
    out = _combine(tile_units, x1, meta, ys)
    return out.reshape(batch, seq, D_MODEL)


def kernel(x, ln1_w, w_in, conv_w, conv_b, w_rg_a, b_rg_a, w_rg_i, b_rg_i, rg_lambda, b_forget,
           q_norm_w, k_norm_w, rec_out_norm_w, attn_out_norm_w, w_out, ln2_w, w_router, b_router,
           w_gate_up, b_gate_up, w_down, b_down):
    for l in range(ln1_w.shape[0]):
        x = _layer(x, ln1_w[l], w_in[l], conv_w[l], conv_b[l], w_rg_a[l], b_rg_a[l], w_rg_i[l],
                   b_rg_i[l], rg_lambda[l], b_forget[l], q_norm_w[l], k_norm_w[l], rec_out_norm_w[l],
                   attn_out_norm_w[l], w_out[l], ln2_w[l], w_router[l], b_router[l], w_gate_up[l],
                   b_gate_up[l], w_down[l], b_down[l])
    return x
```

```python
import functools

import jax
import jax.numpy as jnp
from jax import lax
from jax.experimental import pallas as pl
from jax.experimental.pallas import tpu as pltpu

D_MODEL = 1024
D_REC = 512
REC_BLOCK = 64
CONV_WIDTH = 4
RGLRU_C = 8.0
D_ATT = 512
HEAD_DIM = 64
N_HEADS = 8
N_EXPERTS = 32
TOP_K = 4
D_EXPERT = 1024
SWIGLU_LIMIT = 7.0
SWIGLU_ALPHA = 1.702
EPS = 1e-6

LANES = 128
SLAB = LANES
VMEM_LIMIT = 56 * 1024 * 1024

F32 = jnp.float32
BF16 = jnp.bfloat16

O_U = 0
O_G = O_U + D_REC
N_PROJ = O_G + D_REC

L_C = HEAD_DIM
L_ONE = HEAD_DIM + 3

TM_IN = 512
TS_REC = 512
TQ = 512
TK = 512
HEADS_PER_STEP = 8
TM_OUT = 512
TM_E = 512

UNIT = 16
TILE_SLOTS = -(-(TM_OUT * TOP_K + N_EXPERTS * (UNIT - 1)) // LANES) * LANES
TILE_UNITS = TILE_SLOTS // UNIT
BLOCK_UNITS = TM_E // UNIT
SLOT_CHUNK = 512


def _split3(v):
    hi = v.astype(BF16).astype(F32)
    r = v - hi
    mid = r.astype(BF16).astype(F32)
    lo = (r - mid).astype(BF16).astype(F32)
    return hi, mid, lo


def _dot(a, b):
    return jnp.dot(a, b, preferred_element_type=F32)


def _sigmoid(x):
    return 0.5 * jnp.tanh(0.5 * x) + 0.5


def _in_proj_kernel(x_ref, ln_ref, w_ref, wqt_ref, wkt_ref, wvt_ref, wft_ref, bf_ref, qw_ref, kw_ref,
                    u_ref, g_ref, q_ref, k_ref, v_ref, carry_ref, *, tiles_per_seq):
    i = pl.program_id(0)
    tm = x_ref.shape[0]

    @pl.when(i % tiles_per_seq == 0)
    def _():
        carry_ref[...] = jnp.zeros_like(carry_ref)

    x = x_ref[...]
    ms = jnp.mean(x * x, axis=-1, keepdims=True)
    h = (x * lax.rsqrt(ms + EPS) * ln_ref[...]).astype(BF16)
    nt = (((1,), (1,)), ((), ()))

    f_t = lax.dot_general(wft_ref[...], h, nt, preferred_element_type=F32) + bf_ref[...]
    logf_t = jnp.minimum(f_t, 0.0) - jnp.log1p(jnp.exp(-jnp.abs(f_t)))
    row = lax.broadcasted_iota(jnp.int32, (tm, tm), 0)
    col = lax.broadcasted_iota(jnp.int32, (tm, tm), 1)
    upto = (row <= col).astype(BF16)
    hi, mid, lo = _split3(logf_t)
    c_t = (_dot(hi.astype(BF16), upto) + _dot(mid.astype(BF16), upto)
           + _dot(lo.astype(BF16), upto) + carry_ref[...])
    carry_ref[...] = c_t[:, tm - 1:tm]

    qt_all = lax.dot_general(wqt_ref[...], h, nt, preferred_element_type=F32)
    kt_all = lax.dot_general(wkt_ref[...], h, nt, preferred_element_type=F32)
    spare = lax.broadcasted_iota(jnp.int32, (SLAB - HEAD_DIM, tm), 0) + HEAD_DIM
    qw = qw_ref[...]
    kw = kw_ref[...]

    def normed(rows, weight):
        inv = lax.rsqrt(jnp.sum(rows * rows, axis=0, keepdims=True) * (1.0 / HEAD_DIM) + EPS)
        return rows * inv * weight

    for hd in range(N_HEADS):
        rows = slice(hd * HEAD_DIM, (hd + 1) * HEAD_DIM)
        c_hi, c_mid, c_lo = _split3(c_t[hd:hd + 1, :])
        q_extra = jnp.where(spare == L_C, c_hi, 0.0)
        q_extra = jnp.where(spare == L_C + 1, c_mid, q_extra)
        q_extra = jnp.where(spare == L_C + 2, c_lo, q_extra)
        q_extra = jnp.where((spare >= L_ONE) & (spare < L_ONE + 3), 1.0, q_extra)
        k_extra = jnp.where(spare < L_C + 3, 1.0, 0.0)
        k_extra = jnp.where(spare == L_ONE, -c_hi, k_extra)
        k_extra = jnp.where(spare == L_ONE + 1, -c_mid, k_extra)
        k_extra = jnp.where(spare == L_ONE + 2, -c_lo, k_extra)
        q_ref[0, hd] = jnp.concatenate([normed(qt_all[rows, :], qw), q_extra], axis=0).T.astype(BF16)
        k_ref[0, hd] = jnp.concatenate([normed(kt_all[rows, :], kw), k_extra], axis=0).T.astype(BF16)

    vt_all = lax.dot_general(wvt_ref[...], h, (((1,), (1,)), ((), ())), preferred_element_type=F32)
    spare = lax.broadcasted_iota(jnp.int32, (SLAB - HEAD_DIM, tm), 0)
    ones_row = jnp.where(spare == 0, 1.0, 0.0)
    for hd in range(N_HEADS):
        v_ref[0, hd] = jnp.concatenate(
            [vt_all[hd * HEAD_DIM:(hd + 1) * HEAD_DIM, :], ones_row], axis=0).astype(BF16)
    u_ref[...] = _dot(h, w_ref[:, O_U:O_U + D_REC])
    g_ref[...] = _dot(h, w_ref[:, O_G:O_G + D_REC])


def _in_proj(x2d, ln1_w, w_all, w_qt, w_kt, w_vt, w_ft, bf_col, qw_col, kw_col, batch, seq):
    t = x2d.shape[0]
    tiles_per_seq = seq // TM_IN
    slab_shape = jax.ShapeDtypeStruct((batch, N_HEADS, seq, SLAB), BF16)
    slab_spec = pl.BlockSpec((1, N_HEADS, TM_IN, SLAB),
                             lambda i: (i // tiles_per_seq, 0, i % tiles_per_seq, 0))
    slab_t_shape = jax.ShapeDtypeStruct((batch, N_HEADS, SLAB, seq), BF16)
    slab_t_spec = pl.BlockSpec((1, N_HEADS, SLAB, TM_IN),
                               lambda i: (i // tiles_per_seq, 0, 0, i % tiles_per_seq))
    row_spec = pl.BlockSpec((TM_IN, D_REC), lambda i: (i, 0))
    const = lambda shape: pl.BlockSpec(shape, lambda i: (0,) * len(shape))
    return pl.pallas_call(
        functools.partial(_in_proj_kernel, tiles_per_seq=tiles_per_seq),
        out_shape=(jax.ShapeDtypeStruct((t, D_REC), F32), jax.ShapeDtypeStruct((t, D_REC), F32),
                   slab_shape, slab_shape, slab_t_shape),
        grid=(t // TM_IN,),
        in_specs=[pl.BlockSpec((TM_IN, D_MODEL), lambda i: (i, 0)),
                  const((1, D_MODEL)), const((D_MODEL, N_PROJ)), const((D_ATT, D_MODEL)),
                  const((D_ATT, D_MODEL)), const((D_ATT, D_MODEL)), const((LANES, D_MODEL)),
                  const((LANES, 1)), const((HEAD_DIM, 1)), const((HEAD_DIM, 1))],
        out_specs=(row_spec, row_spec, slab_spec, slab_spec, slab_t_spec),
        scratch_shapes=[pltpu.VMEM((LANES, 1), F32)],
        compiler_params=pltpu.CompilerParams(dimension_semantics=("arbitrary",),
                                             vmem_limit_bytes=VMEM_LIMIT),
        name="in_proj",
    )(x2d, ln1_w, w_all, w_qt, w_kt, w_vt, w_ft, bf_col, qw_col, kw_col)


R_CONV_B = CONV_WIDTH
R_B_A = CONV_WIDTH + 1
R_B_I = CONV_WIDTH + 2
R_LAMBDA = CONV_WIDTH + 3
R_NORM_W = CONV_WIDTH + 4
N_REC_ROWS = CONV_WIDTH + 5


def _rglru_kernel(u_ref, g_ref, vec_ref, wg_ref, o_ref, ext_ref, a_ref, b_ref, h_ref, hcar_ref):
    s = pl.program_id(1)
    ts = u_ref.shape[1]
    pad = 8
    row = lambda r: vec_ref[r:r + 1, :]

    @pl.when(s == 0)
    def _():
        ext_ref[0:pad, :] = jnp.zeros((pad, D_REC), F32)
        hcar_ref[...] = jnp.zeros_like(hcar_ref)

    @pl.when(s != 0)
    def _():
        ext_ref[0:pad, :] = ext_ref[ts:ts + pad, :]

    u = u_ref[0]
    ext_ref[pad:pad + ts, :] = u
    xc = row(R_CONV_B) + u * row(CONV_WIDTH - 1)
    for tap in range(CONV_WIDTH - 1):
        d = CONV_WIDTH - 1 - tap
        xc = xc + ext_ref[pad - d:pad - d + ts, :] * row(tap)

    gates = _dot(xc.astype(BF16), wg_ref[...])
    r = _sigmoid(gates[:, :D_REC] + row(R_B_A))
    ig = _sigmoid(gates[:, D_REC:] + row(R_B_I))
    nlam = -row(R_LAMBDA)
    softplus = jnp.maximum(nlam, 0.0) + jnp.log1p(jnp.exp(-jnp.abs(nlam)))
    log_a = -RGLRU_C * r * softplus
    a = jnp.exp(log_a)
    a_ref[...] = a
    b_ref[...] = jnp.sqrt(1.0 - jnp.exp(2.0 * log_a)) * (ig * xc)

    def step(t, h):
        h = a_ref[pl.ds(t, 1), :] * h + b_ref[pl.ds(t, 1), :]
        h_ref[pl.ds(t, 1), :] = h
        return h

    hcar_ref[...] = lax.fori_loop(0, ts, step, hcar_ref[...], unroll=8)

    rec = h_ref[...] * jax.nn.gelu(g_ref[0])
    ms = jnp.mean(rec * rec, axis=-1, keepdims=True)
    o_ref[0] = (rec * lax.rsqrt(ms + EPS) * row(R_NORM_W)).astype(BF16)


def _rglru(u, g, rec_vec, w_gates):
    batch, seq, _ = u.shape
    seq_spec = pl.BlockSpec((1, TS_REC, D_REC), lambda b, s: (b, s, 0))
    const = lambda shape: pl.BlockSpec(shape, lambda b, s: (0,) * len(shape))
    return pl.pallas_call(
        _rglru_kernel,
        out_shape=jax.ShapeDtypeStruct((batch, seq, D_REC), BF16),
        grid=(batch, seq // TS_REC),
        in_specs=[seq_spec, seq_spec, const((N_REC_ROWS, D_REC)), const((D_REC, 2 * D_REC))],
        out_specs=seq_spec,
        scratch_shapes=[pltpu.VMEM((TS_REC + 8, D_REC), F32), pltpu.VMEM((TS_REC, D_REC), F32),
                        pltpu.VMEM((TS_REC, D_REC), F32), pltpu.VMEM((TS_REC, D_REC), F32),
                        pltpu.VMEM((1, D_REC), F32)],
        compiler_params=pltpu.CompilerParams(dimension_semantics=("arbitrary", "arbitrary"),
                                             vmem_limit_bytes=VMEM_LIMIT),
        name="rglru",
    )(u, g, rec_vec, w_gates)


def _attn_kernel(q_ref, k_ref, v_ref, o_ref):
    qi = pl.program_id(2)
    hb = q_ref.shape[1]
    tq = q_ref.shape[2]
    qs = [q_ref[0, h] for h in range(hb)]

    def step(j, carry, masked):
        start = pl.multiple_of(j * TK, TK)
        scores = []
        for h in range(hb):
            kj = k_ref[0, h, pl.ds(start, TK), :]
            s = lax.dot_general(kj, qs[h], (((1,), (1,)), ((), ())), preferred_element_type=F32)
            if masked:
                key = lax.broadcasted_iota(jnp.int32, (TK, tq), 0)
                qry = lax.broadcasted_iota(jnp.int32, (TK, tq), 1)
                s = jnp.where(qry >= key, s, -jnp.inf)
            scores.append(s)
        m_news = [jnp.maximum(carry[h][0], jnp.max(scores[h], axis=0, keepdims=True))
                  for h in range(hb)]
        probs = [jnp.exp(scores[h] - m_news[h]).astype(BF16) for h in range(hb)]
        out = []
        for h in range(hb):
            m, acc = carry[h]
            vj = v_ref[0, h, :, pl.ds(start, TK)]
            out.append((m_news[h], jnp.exp(m - m_news[h]) * acc + _dot(vj, probs[h])))
        return tuple(out)

    init = tuple((jnp.full((1, tq), -jnp.inf, F32), jnp.zeros((SLAB, tq), F32)) for _ in range(hb))
    carry = lax.fori_loop(0, qi, lambda j, c: step(j, c, False), init)
    carry = step(qi, carry, True)
    outs = [carry[h][1][:HEAD_DIM, :] / carry[h][1][HEAD_DIM:HEAD_DIM + 1, :] for h in range(hb)]
    for pair in range(hb // 2):
        both = jnp.concatenate([outs[2 * pair], outs[2 * pair + 1]], axis=0)
        o_ref[0, :, pair * LANES:(pair + 1) * LANES] = both.T


def _attention(qp, kp, vp):
    batch, _, seq, _ = kp.shape
    hb = HEADS_PER_STEP
    kv_spec = pl.BlockSpec((1, hb, seq, SLAB), lambda b, h, i: (b, h, 0, 0))
    vt_spec = pl.BlockSpec((1, hb, SLAB, seq), lambda b, h, i: (b, h, 0, 0))
    return pl.pallas_call(
        _attn_kernel,
        out_shape=jax.ShapeDtypeStruct((batch, seq, D_ATT), F32),
        grid=(batch, N_HEADS // hb, seq // TQ),
        in_specs=[pl.BlockSpec((1, hb, TQ, SLAB), lambda b, h, i: (b, h, i, 0)), kv_spec, vt_spec],
        out_specs=pl.BlockSpec((1, TQ, hb * HEAD_DIM), lambda b, h, i: (b, i, h)),
        compiler_params=pltpu.CompilerParams(
            dimension_semantics=("arbitrary", "arbitrary", "arbitrary"),
            vmem_limit_bytes=VMEM_LIMIT),
        name="attn",
    )(qp, kp, vp)


def _out_proj_kernel(x_ref, rec_ref, att_ref, aw_ref, wr_ref, wa_ref, ln_ref, rwh_ref, rwl_ref, rb_ref,
                     x1_ref, xs_ref, meta_ref, units_ref):
    tm = x_ref.shape[0]

    att = att_ref[...]
    ms = jnp.sum(att * att, axis=-1, keepdims=True) * (1.0 / D_ATT)
    att_n = (att * lax.rsqrt(ms + EPS) * aw_ref[...]).astype(BF16)
    x1 = x_ref[...] + _dot(rec_ref[...], wr_ref[...]) + _dot(att_n, wa_ref[...])
    x1_ref[...] = x1

    ms2 = jnp.mean(x1 * x1, axis=-1, keepdims=True)
    h2 = x1 * lax.rsqrt(ms2 + EPS) * ln_ref[...]

    h_hi = h2.astype(BF16)
    h_lo = (h2 - h_hi.astype(F32)).astype(BF16)
    nt = (((1,), (1,)), ((), ()))
    logits = (lax.dot_general(rwh_ref[...], h_hi, nt, preferred_element_type=F32)
              + lax.dot_general(rwh_ref[...], h_lo, nt, preferred_element_type=F32)
              + lax.dot_general(rwl_ref[...], h_hi, nt, preferred_element_type=F32) + rb_ref[...])
    expert = lax.broadcasted_iota(jnp.int32, (LANES, tm), 0)
    expert_f = expert.astype(F32)
    l = jnp.where(expert < N_EXPERTS, logits, -jnp.inf)

    vals, idxs, sels = [], [], []
    for _ in range(TOP_K):
        m = jnp.max(l, axis=0, keepdims=True)
        idx = jnp.min(jnp.where(l == m, expert_f, float(LANES)), axis=0, keepdims=True)
        sel = expert_f == idx
        vals.append(m)
        idxs.append(idx)
        sels.append(sel)
        l = jnp.where(sel, -jnp.inf, l)
    exps = [jnp.exp(v - vals[0]) for v in vals]
    denom = exps[0] + exps[1] + exps[2] + exps[3]

    onehot = jnp.zeros((LANES, tm), F32)
    for sel in sels:
        onehot = jnp.where(sel, 1.0, onehot)
    row = lax.broadcasted_iota(jnp.int32, (tm, tm), 0)
    col = lax.broadcasted_iota(jnp.int32, (tm, tm), 1)
    earlier_token = (row < col).astype(BF16)
    rank = _dot(onehot.astype(BF16), earlier_token)
    count = jnp.sum(onehot, axis=1, keepdims=True)
    units = jnp.floor((count + (UNIT - 1)) * (1.0 / UNIT))
    e_row = lax.broadcasted_iota(jnp.int32, (LANES, LANES), 0)
    e_col = lax.broadcasted_iota(jnp.int32, (LANES, LANES), 1)
    earlier_expert = (e_col < e_row).astype(BF16)
    units_wide = jnp.broadcast_to(units, (LANES, LANES))
    first_unit = _dot(earlier_expert, units_wide.astype(BF16))[:, 0:1]
    units_ref[0] = units_wide.T[0:8, :]
    place = first_unit * float(UNIT) + rank

    meta_t = jnp.zeros((LANES, tm), F32)
    for k in range(TOP_K):
        slot = jnp.sum(jnp.where(sels[k], place, 0.0), axis=0, keepdims=True)
        meta_t = jnp.where(expert == k, idxs[k], meta_t)
        meta_t = jnp.where(expert == TOP_K + k, exps[k] / denom, meta_t)
        meta_t = jnp.where(expert == 2 * TOP_K + k, slot, meta_t)
    meta_ref[...] = meta_t.T

    slot_rows = [meta_t[2 * TOP_K + k:2 * TOP_K + k + 1, :].astype(jnp.int32) for k in range(TOP_K)]
    for c in range(TILE_SLOTS // SLOT_CHUNK):
        slot_id = lax.broadcasted_iota(jnp.int32, (SLOT_CHUNK, tm), 0) + c * SLOT_CHUNK
        hit = jnp.zeros((SLOT_CHUNK, tm), F32)
        for slot_row in slot_rows:
            hit = jnp.where(slot_id == slot_row, 1.0, hit)
        xs_ref[c * SLOT_CHUNK:(c + 1) * SLOT_CHUNK, :] = _dot(hit.astype(BF16), h_hi).astype(BF16)


def _out_proj(x2d, rec_n, att, aw_slab, w_rec, w_att, ln2_w, rw_hi, rw_lo, rb_pad):
    t = x2d.shape[0]
    n_tiles = t // TM_OUT
    rows = lambda n: pl.BlockSpec((TM_OUT, n), lambda i: (i, 0))
    const = lambda shape: pl.BlockSpec(shape, lambda i: (0,) * len(shape))
    return pl.pallas_call(
        _out_proj_kernel,
        out_shape=(jax.ShapeDtypeStruct((t, D_MODEL), F32),
                   jax.ShapeDtypeStruct((n_tiles * TILE_SLOTS, D_MODEL), BF16),
                   jax.ShapeDtypeStruct((t, LANES), F32),
                   jax.ShapeDtypeStruct((n_tiles, 8, LANES), F32)),
        grid=(n_tiles,),
        in_specs=[rows(D_MODEL), rows(D_REC), rows(D_ATT), const((1, D_ATT)),
                  const((D_REC, D_MODEL)), const((D_ATT, D_MODEL)), const((1, D_MODEL)),
                  const((LANES, D_MODEL)), const((LANES, D_MODEL)), const((LANES, 1))],
        out_specs=(rows(D_MODEL), pl.BlockSpec((TILE_SLOTS, D_MODEL), lambda i: (i, 0)), rows(LANES),
                   pl.BlockSpec((1, 8, LANES), lambda i: (i, 0, 0))),
        compiler_params=pltpu.CompilerParams(dimension_semantics=("arbitrary",),
                                             vmem_limit_bytes=VMEM_LIMIT),
        name="out_proj",
    )(x2d, rec_n, att, aw_slab, w_rec, w_att, ln2_w, rw_hi, rw_lo, rb_pad)


def _start_units(unit_ref, base, n_units, src_hbm, dst, sem):
    def body(j, carry):
        src_row = pl.multiple_of(unit_ref[base + j] * UNIT, UNIT)
        dst_row = pl.multiple_of(j * UNIT, UNIT)
        pltpu.make_async_copy(src_hbm.at[pl.ds(src_row, UNIT), :],
                              dst.at[pl.ds(dst_row, UNIT), :], sem).start()
        return carry
    lax.fori_loop(0, n_units, body, 0, unroll=8)


def _wait_units(n_units, src_hbm, dst, sem):
    pltpu.make_async_copy(src_hbm.at[pl.ds(0, n_units * UNIT), :], dst, sem).wait()


def _experts_kernel(be_ref, nblk_ref, unit_ref, order_ref, next_ref, xs_hbm, wgu_hbm, bgu_ref, wd_hbm,
                    bd_ref, ys_ref, xbuf, wgu_f32, wd_f32, wgu_bf, wd_bf, sem, wsem):
    b = pl.program_id(0)
    nblk = nblk_ref[0]
    slot = b % 2
    e = be_ref[b]

    def weight_copies(expert, wslot):
        return (pltpu.make_async_copy(wgu_hbm.at[expert], wgu_f32.at[wslot], wsem.at[wslot]),
                pltpu.make_async_copy(wd_hbm.at[expert], wd_f32.at[wslot], wsem.at[wslot]))

    @pl.when(b == 0)
    def _():
        _start_units(unit_ref, 0, BLOCK_UNITS, xs_hbm, xbuf.at[0], sem.at[0])
        for copy in weight_copies(e, 0):
            copy.start()

    @pl.when(b + 1 < nblk)
    def _():
        _start_units(unit_ref, (b + 1) * BLOCK_UNITS, BLOCK_UNITS, xs_hbm, xbuf.at[1 - slot],
                     sem.at[1 - slot])

    new_expert = jnp.logical_or(b == 0, e != be_ref[jnp.maximum(b - 1, 0)])

    @pl.when(jnp.logical_and(b < nblk, new_expert))
    def _():
        wslot = order_ref[e] % 2
        for copy in weight_copies(e, wslot):
            copy.wait()
        wgu_bf[...] = wgu_f32[wslot].astype(BF16)
        wd_bf[...] = wd_f32[wslot].astype(BF16)
        nxt = next_ref[e]

        @pl.when(nxt >= 0)
        def _():
            for copy in weight_copies(nxt, 1 - wslot):
                copy.start()

    @pl.when(b < nblk)
    def _():
        _wait_units(BLOCK_UNITS, xs_hbm, xbuf.at[slot], sem.at[slot])
        gu = _dot(xbuf[slot], wgu_bf[...]) + bgu_ref[0]
        g = jnp.minimum(gu[:, :D_EXPERT], SWIGLU_LIMIT)
        u = jnp.clip(gu[:, D_EXPERT:], -SWIGLU_LIMIT, SWIGLU_LIMIT)
        act = (u + 1.0) * (g * _sigmoid(SWIGLU_ALPHA * g))
        ys_ref[...] = (_dot(act.astype(BF16), wd_bf[...]) + bd_ref[0]).astype(BF16)

    @pl.when(b >= nblk)
    def _():
        ys_ref[...] = jnp.zeros_like(ys_ref)


def _experts(block_expert, n_blocks_used, unit_src, expert_order, expert_next, xs, w_gate_up,
             b_gate_up, w_down, b_down):
    n_blocks = block_expert.shape[0]
    bias_map = lambda b, be, nb, us, eo, en: (be[b], 0, 0)
    grid_spec = pltpu.PrefetchScalarGridSpec(
        num_scalar_prefetch=5,
        grid=(n_blocks,),
        in_specs=[pl.BlockSpec(memory_space=pl.ANY),
                  pl.BlockSpec(memory_space=pl.ANY),
                  pl.BlockSpec((1, 1, 2 * D_EXPERT), bias_map),
                  pl.BlockSpec(memory_space=pl.ANY),
                  pl.BlockSpec((1, 1, D_MODEL), bias_map)],
        out_specs=pl.BlockSpec((TM_E, D_MODEL), lambda b, be, nb, us, eo, en: (b, 0)),
        scratch_shapes=[pltpu.VMEM((2, TM_E, D_MODEL), BF16),
                        pltpu.VMEM((2, D_MODEL, 2 * D_EXPERT), F32),
                        pltpu.VMEM((2, D_EXPERT, D_MODEL), F32),
                        pltpu.VMEM((D_MODEL, 2 * D_EXPERT), BF16),
                        pltpu.VMEM((D_EXPERT, D_MODEL), BF16),
                        pltpu.SemaphoreType.DMA((2,)),
                        pltpu.SemaphoreType.DMA((2,))],
    )
    return pl.pallas_call(
        _experts_kernel,
        out_shape=jax.ShapeDtypeStruct((n_blocks * TM_E, D_MODEL), BF16),
        grid_spec=grid_spec,
        compiler_params=pltpu.CompilerParams(dimension_semantics=("arbitrary",),
                                             vmem_limit_bytes=VMEM_LIMIT),
        name="experts",
    )(block_expert, n_blocks_used, unit_src, expert_order, expert_next, xs, w_gate_up,
      b_gate_up.reshape(N_EXPERTS, 1, 2 * D_EXPERT), w_down, b_down.reshape(N_EXPERTS, 1, D_MODEL))


def _combine_kernel(unit_ref, x1_ref, meta_ref, ys_hbm, o_ref, buf, sem):
    i = pl.program_id(0)
    n = pl.num_programs(0)
    slot = i % 2
    tm = x1_ref.shape[0]

    @pl.when(i == 0)
    def _():
        _start_units(unit_ref, 0, TILE_UNITS, ys_hbm, buf.at[0], sem.at[0])

    @pl.when(i + 1 < n)
    def _():
        _start_units(unit_ref, (i + 1) * TILE_UNITS, TILE_UNITS, ys_hbm, buf.at[1 - slot],
                     sem.at[1 - slot])

    meta = meta_ref[...]
    slots = [meta[:, 2 * TOP_K + k:2 * TOP_K + k + 1].astype(jnp.int32) for k in range(TOP_K)]
    gates = [meta[:, TOP_K + k:TOP_K + k + 1] for k in range(TOP_K)]
    _wait_units(TILE_UNITS, ys_hbm, buf.at[slot], sem.at[slot])

    acc = x1_ref[...]
    for c in range(TILE_SLOTS // SLOT_CHUNK):
        first = c * SLOT_CHUNK
        col = lax.broadcasted_iota(jnp.int32, (tm, SLOT_CHUNK), 1) + first
        weight = jnp.zeros((tm, SLOT_CHUNK), F32)
        for k in range(TOP_K):
            weight = jnp.where(col == slots[k], gates[k], weight)
        acc = acc + _dot(weight.astype(BF16), buf[slot, first:first + SLOT_CHUNK, :])
    o_ref[...] = acc


def _combine(tile_units, x1, meta, ys):
    t = x1.shape[0]
    grid_spec = pltpu.PrefetchScalarGridSpec(
        num_scalar_prefetch=1,
        grid=(t // TM_OUT,),
        in_specs=[pl.BlockSpec((TM_OUT, D_MODEL), lambda i, us: (i, 0)),
                  pl.BlockSpec((TM_OUT, LANES), lambda i, us: (i, 0)),
                  pl.BlockSpec(memory_space=pl.ANY)],
        out_specs=pl.BlockSpec((TM_OUT, D_MODEL), lambda i, us: (i, 0)),
        scratch_shapes=[pltpu.VMEM((2, TILE_SLOTS, D_MODEL), BF16),
                        pltpu.SemaphoreType.DMA((2,))],
    )
    return pl.pallas_call(
        _combine_kernel,
        out_shape=jax.ShapeDtypeStruct((t, D_MODEL), F32),
        grid_spec=grid_spec,
        compiler_params=pltpu.CompilerParams(dimension_semantics=("arbitrary",),
                                             vmem_limit_bytes=VMEM_LIMIT),
        name="combine",
    )(tile_units, x1, meta, ys)


def _gate_matrix(w_a, w_i):
    g = w_a.shape[0]
    eye = jnp.eye(g, dtype=w_a.dtype)
    both = jnp.stack([w_a, w_i])
    return jnp.einsum('sgij,gh->gishj', both, eye).reshape(g * REC_BLOCK, 2 * g * REC_BLOCK)


def _pad_lanes(v, n=LANES):
    return jnp.pad(v, (0, n - v.shape[0])).reshape(1, n)


def _excl_cumsum(a, axis):
    return jnp.cumsum(a, axis=axis) - a


def _pick(onehot, table):
    if table.ndim == 1:
        return jnp.sum(onehot * table[None, :], axis=1)
    return jnp.sum(onehot[:, :, None] * table[None, :, :], axis=1)


def _dispatch_tables(units, n_blocks):
    n_tiles = units.shape[0]
    i32 = jnp.int32
    tile_first = _excl_cumsum(units, 1)
    tile_end = tile_first + units
    run_first = _excl_cumsum(units, 0)
    run_end = run_first + units
    region_units = jnp.sum(units, axis=0)
    region_blocks = (region_units + BLOCK_UNITS - 1) // BLOCK_UNITS
    block_end = jnp.cumsum(region_blocks)
    region_first = (block_end - region_blocks) * BLOCK_UNITS
    n_used = block_end[-1:]
    experts = jnp.arange(N_EXPERTS, dtype=i32)
    tiles = jnp.arange(n_tiles, dtype=i32)

    blocks = jnp.arange(n_blocks, dtype=i32)
    block_expert = jnp.minimum(
        jnp.sum((block_end[None, :] <= blocks[:, None]).astype(i32), axis=1), N_EXPERTS - 1)

    g = jnp.arange(n_blocks * BLOCK_UNITS, dtype=i32)
    e = jnp.minimum(jnp.sum((block_end[None, :] * BLOCK_UNITS <= g[:, None]).astype(i32), axis=1),
                    N_EXPERTS - 1)
    e_hot = (e[:, None] == experts[None, :]).astype(i32)
    local = g - _pick(e_hot, region_first)
    valid = (local < _pick(e_hot, region_units)) & (g < n_used[0] * BLOCK_UNITS)
    ends = _pick(e_hot, run_end.T)
    tile = jnp.minimum(jnp.sum((ends <= local[:, None]).astype(i32), axis=1), n_tiles - 1)
    t_hot = (tile[:, None] == tiles[None, :]).astype(i32)
    src = (tile * TILE_UNITS + jnp.sum(_pick(t_hot, tile_first - run_first) * e_hot, axis=1) + local)
    unit_src = jnp.where(valid, src, 0).astype(i32)

    j = jnp.arange(TILE_UNITS, dtype=i32)
    ej = jnp.minimum(jnp.sum((tile_end[:, None, :] <= j[None, :, None]).astype(i32), axis=2),
                     N_EXPERTS - 1)
    ej_hot = (ej[:, :, None] == experts[None, None, :]).astype(i32)
    used = j[None, :] < jnp.sum(units, axis=1)[:, None]
    base = region_first[None, :] + run_first - tile_first
    dst = jnp.sum(ej_hot * base[:, None, :], axis=2) + j[None, :]
    tile_units = jnp.where(used, dst, 0).astype(i32).reshape(-1)
    owns = region_blocks > 0
    expert_order = _excl_cumsum(owns.astype(i32), 0)
    later = (experts[None, :] > experts[:, None]) & owns[None, :]
    expert_next = jnp.min(jnp.where(later, experts[None, :], N_EXPERTS), axis=1)
    expert_next = jnp.where(expert_next < N_EXPERTS, expert_next, -1).astype(i32)
    return (block_expert.astype(i32), n_used.astype(i32), unit_src, tile_units,
            expert_order.astype(i32), expert_next)


def _layer(x, ln1_w, w_in, conv_w, conv_b, w_rg_a, b_rg_a, w_rg_i, b_rg_i, rg_lambda, b_forget,
           q_norm_w, k_norm_w, rec_out_norm_w, attn_out_norm_w, w_out, ln2_w, w_router, b_router,
           w_gate_up, b_gate_up, w_down, b_down):
    batch, seq, _ = x.shape
    t = batch * seq
    x2d = x.reshape(t, D_MODEL)

    o_q = 2 * D_REC
    o_k = o_q + D_ATT
    o_v = o_k + D_ATT
    o_f = o_v + D_ATT
    w_t = w_in.T.astype(BF16)
    w_ft = jnp.pad(w_t[o_f:], ((0, LANES - N_HEADS), (0, 0)))
    scale = HEAD_DIM ** -0.5
    u, g, qp, kp, vp = _in_proj(
        x2d, ln1_w.reshape(1, D_MODEL), w_in[:, :o_q].astype(BF16), w_t[o_q:o_k], w_t[o_k:o_v],
        w_t[o_v:o_f], w_ft, _pad_lanes(b_forget).reshape(LANES, 1),
        (q_norm_w * scale).reshape(HEAD_DIM, 1), k_norm_w.reshape(HEAD_DIM, 1), batch, seq)

    rec_vec = jnp.concatenate(
        [conv_w, jnp.stack([conv_b, b_rg_a, b_rg_i, rg_lambda, rec_out_norm_w])])
    rec_n = _rglru(u.reshape(batch, seq, D_REC), g.reshape(batch, seq, D_REC), rec_vec,
                   _gate_matrix(w_rg_a, w_rg_i).astype(BF16))

    att = _attention(qp, kp, vp)

    rw = jnp.pad(w_router.T, ((0, LANES - N_EXPERTS), (0, 0)))
    rw_hi = rw.astype(BF16)
    rw_lo = (rw - rw_hi.astype(F32)).astype(BF16)
    x1, xs, meta, units = _out_proj(
        x2d, rec_n.reshape(t, D_REC), att.reshape(t, D_ATT), attn_out_norm_w.reshape(1, D_ATT),
        w_out[:D_REC].astype(BF16), w_out[D_REC:].astype(BF16), ln2_w.reshape(1, D_MODEL),
        rw_hi, rw_lo, _pad_lanes(b_router).reshape(LANES, 1))

    n_tiles = t // TM_OUT
    max_units = t * TOP_K // UNIT + n_tiles * N_EXPERTS
    n_blocks = -(-max_units // BLOCK_UNITS) + N_EXPERTS
    block_expert, n_used, unit_src, tile_units, expert_order, expert_next = _dispatch_tables(
        units[:, 0, :N_EXPERTS].astype(jnp.int32), n_blocks)

    ys = _experts(block_expert, n_used, unit_src, expert_order, expert_next, xs, w_gate_up, b_gate_up,
                  w_down, b_down)
    out = _combine(tile_units, x1, meta, ys)
    return out.reshape(batch, seq, D_MODEL)


def kernel(x, ln1_w, w_in, conv_w, conv_b, w_rg_a, b_rg_a, w_rg_i, b_rg_i, rg_lambda, b_forget,
           q_norm_w, k_norm_w, rec_out_norm_w, attn_out_norm_w, w_out, ln2_w, w_router, b_router,
           w_gate_up, b_gate_up, w_down, b_down):
    for l in range(ln1_w.shape[0]):
        x = _layer(x, ln1_w[l], w_in[l], conv_w[l], conv_b[l], w_rg_a[l], b_rg_a[l], w_rg_i[l],
                   b_rg_i[l], rg_lambda[l], b_forget[l], q_norm_w[l], k_norm_w[l], rec_out_norm_w[l],
                   attn_out_norm_w[l], w_out[l], ln2_w[l], w_router[l], b_router[l], w_gate_up[l],
                   b_gate_up[l], w_down[l], b_down[l])
    return x
```

```python
import functools

import jax
import jax.numpy as jnp
from jax import lax
from jax.experimental import pallas as pl
from jax.experimental.pallas import tpu as pltpu

D_MODEL = 1024
D_REC = 512
REC_BLOCK = 64
CONV_WIDTH = 4
RGLRU_C = 8.0
D_ATT = 512
HEAD_DIM = 64
N_HEADS = 8
N_EXPERTS = 32
TOP_K = 4
D_EXPERT = 1024
SWIGLU_LIMIT = 7.0
SWIGLU_ALPHA = 1.702
EPS = 1e-6

LANES = 128
SLAB = LANES
V_ROWS = HEAD_DIM + 16
VMEM_LIMIT = 56 * 1024 * 1024

F32 = jnp.float32
BF16 = jnp.bfloat16

O_U = 0
O_G = O_U + D_REC
N_PROJ = O_G + D_REC

L_C = HEAD_DIM
L_ONE = HEAD_DIM + 3

TM_IN = 512
TS_REC = 512
TQ = 512
TK = 512
HEADS_PER_STEP = 8
TM_OUT = 512
TM_E = 512

UNIT = 16
TILE_SLOTS = -(-(TM_OUT * TOP_K + N_EXPERTS * (UNIT - 1)) // LANES) * LANES
TILE_UNITS = TILE_SLOTS // UNIT
BLOCK_UNITS = TM_E // UNIT
SLOT_CHUNK = 512


def _split3(v):
    hi = v.astype(BF16).astype(F32)
    r = v - hi
    mid = r.astype(BF16).astype(F32)
    lo = (r - mid).astype(BF16).astype(F32)
    return hi, mid, lo


def _dot(a, b):
    return jnp.dot(a, b, preferred_element_type=F32)


def _sigmoid(x):
    return 0.5 * jnp.tanh(0.5 * x) + 0.5


def _in_proj_kernel(x_ref, ln_ref, w_ref, wqt_ref, wkt_ref, wvt_ref, wft_ref, bf_ref, qw_ref, kw_ref,
                    u_ref, g_ref, q_ref, k_ref, v_ref, carry_ref, *, tiles_per_seq):
    i = pl.program_id(0)
    tm = x_ref.shape[0]

    @pl.when(i % tiles_per_seq == 0)
    def _():
        carry_ref[...] = jnp.zeros_like(carry_ref)

    x = x_ref[...]
    ms = jnp.mean(x * x, axis=-1, keepdims=True)
    h = (x * lax.rsqrt(ms + EPS) * ln_ref[...]).astype(BF16)
    nt = (((1,), (1,)), ((), ()))

    f_t = lax.dot_general(wft_ref[...], h, nt, preferred_element_type=F32) + bf_ref[...]
    logf_t = jnp.minimum(f_t, 0.0) - jnp.log1p(jnp.exp(-jnp.abs(f_t)))
    row = lax.broadcasted_iota(jnp.int32, (tm, tm), 0)
    col = lax.broadcasted_iota(jnp.int32, (tm, tm), 1)
    upto = (row <= col).astype(BF16)
    hi, mid, lo = _split3(logf_t)
    c_t = (_dot(hi.astype(BF16), upto) + _dot(mid.astype(BF16), upto)
           + _dot(lo.astype(BF16), upto) + carry_ref[...])
    carry_ref[...] = c_t[:, tm - 1:tm]

    qt_all = lax.dot_general(wqt_ref[...], h, nt, preferred_element_type=F32)
    kt_all = lax.dot_general(wkt_ref[...], h, nt, preferred_element_type=F32)
    spare = lax.broadcasted_iota(jnp.int32, (SLAB - HEAD_DIM, tm), 0) + HEAD_DIM
    qw = qw_ref[...]
    kw = kw_ref[...]

    def normed(rows, weight):
        inv = lax.rsqrt(jnp.sum(rows * rows, axis=0, keepdims=True) * (1.0 / HEAD_DIM) + EPS)
        return rows * inv * weight

    for hd in range(N_HEADS):
        rows = slice(hd * HEAD_DIM, (hd + 1) * HEAD_DIM)
        c_hi, c_mid, c_lo = _split3(c_t[hd:hd + 1, :])
        q_extra = jnp.where(spare == L_C, c_hi, 0.0)
        q_extra = jnp.where(spare == L_C + 1, c_mid, q_extra)
        q_extra = jnp.where(spare == L_C + 2, c_lo, q_extra)
        q_extra = jnp.where((spare >= L_ONE) & (spare < L_ONE + 3), 1.0, q_extra)
        k_extra = jnp.where(spare < L_C + 3, 1.0, 0.0)
        k_extra = jnp.where(spare == L_ONE, -c_hi, k_extra)
        k_extra = jnp.where(spare == L_ONE + 1, -c_mid, k_extra)
        k_extra = jnp.where(spare == L_ONE + 2, -c_lo, k_extra)
        q_ref[0, hd] = jnp.concatenate([normed(qt_all[rows, :], qw), q_extra], axis=0).T.astype(BF16)
        k_ref[0, hd] = jnp.concatenate([normed(kt_all[rows, :], kw), k_extra], axis=0).T.astype(BF16)

    vt_all = lax.dot_general(wvt_ref[...], h, (((1,), (1,)), ((), ())), preferred_element_type=F32)
    spare = lax.broadcasted_iota(jnp.int32, (V_ROWS - HEAD_DIM, tm), 0)
    ones_row = jnp.where(spare == 0, 1.0, 0.0)
    for hd in range(N_HEADS):
        v_ref[0, hd] = jnp.concatenate(
            [vt_all[hd * HEAD_DIM:(hd + 1) * HEAD_DIM, :], ones_row], axis=0).astype(BF16)
    u_ref[...] = _dot(h, w_ref[:, O_U:O_U + D_REC])
    g_ref[...] = _dot(h, w_ref[:, O_G:O_G + D_REC])


def _in_proj(x2d, ln1_w, w_all, w_qt, w_kt, w_vt, w_ft, bf_col, qw_col, kw_col, batch, seq):
    t = x2d.shape[0]
    tiles_per_seq = seq // TM_IN
    slab_shape = jax.ShapeDtypeStruct((batch, N_HEADS, seq, SLAB), BF16)
    slab_spec = pl.BlockSpec((1, N_HEADS, TM_IN, SLAB),
                             lambda i: (i // tiles_per_seq, 0, i % tiles_per_seq, 0))
    slab_t_shape = jax.ShapeDtypeStruct((batch, N_HEADS, V_ROWS, seq), BF16)
    slab_t_spec = pl.BlockSpec((1, N_HEADS, V_ROWS, TM_IN),
                               lambda i: (i // tiles_per_seq, 0, 0, i % tiles_per_seq))
    row_spec = pl.BlockSpec((TM_IN, D_REC), lambda i: (i, 0))
    const = lambda shape: pl.BlockSpec(shape, lambda i: (0,) * len(shape))
    return pl.pallas_call(
        functools.partial(_in_proj_kernel, tiles_per_seq=tiles_per_seq),
        out_shape=(jax.ShapeDtypeStruct((t, D_REC), F32), jax.ShapeDtypeStruct((t, D_REC), F32),
                   slab_shape, slab_shape, slab_t_shape),
        grid=(t // TM_IN,),
        in_specs=[pl.BlockSpec((TM_IN, D_MODEL), lambda i: (i, 0)),
                  const((1, D_MODEL)), const((D_MODEL, N_PROJ)), const((D_ATT, D_MODEL)),
                  const((D_ATT, D_MODEL)), const((D_ATT, D_MODEL)), const((LANES, D_MODEL)),
                  const((LANES, 1)), const((HEAD_DIM, 1)), const((HEAD_DIM, 1))],
        out_specs=(row_spec, row_spec, slab_spec, slab_spec, slab_t_spec),
        scratch_shapes=[pltpu.VMEM((LANES, 1), F32)],
        compiler_params=pltpu.CompilerParams(dimension_semantics=("arbitrary",),
                                             vmem_limit_bytes=VMEM_LIMIT),
        name="in_proj",
    )(x2d, ln1_w, w_all, w_qt, w_kt, w_vt, w_ft, bf_col, qw_col, kw_col)


R_CONV_B = CONV_WIDTH
R_B_A = CONV_WIDTH + 1
R_B_I = CONV_WIDTH + 2
R_LAMBDA = CONV_WIDTH + 3
R_NORM_W = CONV_WIDTH + 4
N_REC_ROWS = CONV_WIDTH + 5


def _rglru_kernel(u_ref, g_ref, vec_ref, wg_ref, o_ref, ext_ref, a_ref, b_ref, h_ref, hcar_ref):
    s = pl.program_id(1)
    ts = u_ref.shape[1]
    pad = 8
    row = lambda r: vec_ref[r:r + 1, :]

    @pl.when(s == 0)
    def _():
        ext_ref[0:pad, :] = jnp.zeros((pad, D_REC), F32)
        hcar_ref[...] = jnp.zeros_like(hcar_ref)

    @pl.when(s != 0)
    def _():
        ext_ref[0:pad, :] = ext_ref[ts:ts + pad, :]

    u = u_ref[0]
    ext_ref[pad:pad + ts, :] = u
    xc = row(R_CONV_B) + u * row(CONV_WIDTH - 1)
    for tap in range(CONV_WIDTH - 1):
        d = CONV_WIDTH - 1 - tap
        xc = xc + ext_ref[pad - d:pad - d + ts, :] * row(tap)

    gates = _dot(xc.astype(BF16), wg_ref[...])
    r = _sigmoid(gates[:, :D_REC] + row(R_B_A))
    ig = _sigmoid(gates[:, D_REC:] + row(R_B_I))
    nlam = -row(R_LAMBDA)
    softplus = jnp.maximum(nlam, 0.0) + jnp.log1p(jnp.exp(-jnp.abs(nlam)))
    log_a = -RGLRU_C * r * softplus
    a = jnp.exp(log_a)
    a_ref[...] = a
    b_ref[...] = jnp.sqrt(1.0 - jnp.exp(2.0 * log_a)) * (ig * xc)

    def step(t, h):
        h = a_ref[pl.ds(t, 1), :] * h + b_ref[pl.ds(t, 1), :]
        h_ref[pl.ds(t, 1), :] = h
        return h

    hcar_ref[...] = lax.fori_loop(0, ts, step, hcar_ref[...], unroll=8)

    rec = h_ref[...] * jax.nn.gelu(g_ref[0])
    ms = jnp.mean(rec * rec, axis=-1, keepdims=True)
    o_ref[0] = (rec * lax.rsqrt(ms + EPS) * row(R_NORM_W)).astype(BF16)


def _rglru(u, g, rec_vec, w_gates):
    batch, seq, _ = u.shape
    seq_spec = pl.BlockSpec((1, TS_REC, D_REC), lambda b, s: (b, s, 0))
    const = lambda shape: pl.BlockSpec(shape, lambda b, s: (0,) * len(shape))
    return pl.pallas_call(
        _rglru_kernel,
        out_shape=jax.ShapeDtypeStruct((batch, seq, D_REC), BF16),
        grid=(batch, seq // TS_REC),
        in_specs=[seq_spec, seq_spec, const((N_REC_ROWS, D_REC)), const((D_REC, 2 * D_REC))],
        out_specs=seq_spec,
        scratch_shapes=[pltpu.VMEM((TS_REC + 8, D_REC), F32), pltpu.VMEM((TS_REC, D_REC), F32),
                        pltpu.VMEM((TS_REC, D_REC), F32), pltpu.VMEM((TS_REC, D_REC), F32),
                        pltpu.VMEM((1, D_REC), F32)],
        compiler_params=pltpu.CompilerParams(dimension_semantics=("arbitrary", "arbitrary"),
                                             vmem_limit_bytes=VMEM_LIMIT),
        name="rglru",
    )(u, g, rec_vec, w_gates)


def _attn_kernel(q_ref, k_ref, v_ref, o_ref):
    qi = pl.program_id(2)
    hb = q_ref.shape[1]
    tq = q_ref.shape[2]
    qs = [q_ref[0, h] for h in range(hb)]

    def step(j, carry, key_lo=0, n_keys=TK, q_lo=0, masked=False):
        start = pl.multiple_of(j * TK + key_lo, n_keys)
        nq = tq - q_lo
        scores = []
        for h in range(hb):
            kj = k_ref[0, h, pl.ds(start, n_keys), :]
            qh = qs[h] if q_lo == 0 else q_ref[0, h, q_lo:, :]
            s = lax.dot_general(kj, qh, (((1,), (1,)), ((), ())), preferred_element_type=F32)
            if masked:
                key = lax.broadcasted_iota(jnp.int32, (n_keys, nq), 0) + key_lo
                qry = lax.broadcasted_iota(jnp.int32, (n_keys, nq), 1) + q_lo
                s = jnp.where(qry >= key, s, -jnp.inf)
            scores.append(s)
        m_olds = [carry[h][0][:, q_lo:] for h in range(hb)]
        m_news = [jnp.maximum(m_olds[h], jnp.max(scores[h], axis=0, keepdims=True))
                  for h in range(hb)]
        probs = [jnp.exp(scores[h] - m_news[h]).astype(BF16) for h in range(hb)]
        out = []
        for h in range(hb):
            m, acc = carry[h]
            vj = v_ref[0, h, :, pl.ds(start, n_keys)]
            acc_new = jnp.exp(m_olds[h] - m_news[h]) * acc[:, q_lo:] + _dot(vj, probs[h])
            if q_lo == 0:
                out.append((m_news[h], acc_new))
            else:
                out.append((jnp.concatenate([m[:, :q_lo], m_news[h]], axis=1),
                            jnp.concatenate([acc[:, :q_lo], acc_new], axis=1)))
        return tuple(out)

    init = tuple((jnp.full((1, tq), -jnp.inf, F32), jnp.zeros((V_ROWS, tq), F32)) for _ in range(hb))
    carry = lax.fori_loop(0, qi, step, init)
    half = TK // 2
    carry = step(qi, carry, key_lo=0, n_keys=half, masked=True)
    carry = step(qi, carry, key_lo=half, n_keys=half, q_lo=half, masked=True)
    outs = [carry[h][1][:HEAD_DIM, :] / carry[h][1][HEAD_DIM:HEAD_DIM + 1, :] for h in range(hb)]
    for pair in range(hb // 2):
        both = jnp.concatenate([outs[2 * pair], outs[2 * pair + 1]], axis=0)
        o_ref[0, :, pair * LANES:(pair + 1) * LANES] = both.T


def _attention(qp, kp, vp):
    batch, _, seq, _ = kp.shape
    hb = HEADS_PER_STEP
    assert TQ == TK
    kv_spec = pl.BlockSpec((1, hb, seq, SLAB), lambda b, h, i: (b, h, 0, 0))
    vt_spec = pl.BlockSpec((1, hb, V_ROWS, seq), lambda b, h, i: (b, h, 0, 0))
    return pl.pallas_call(
        _attn_kernel,
        out_shape=jax.ShapeDtypeStruct((batch, seq, D_ATT), F32),
        grid=(batch, N_HEADS // hb, seq // TQ),
        in_specs=[pl.BlockSpec((1, hb, TQ, SLAB), lambda b, h, i: (b, h, i, 0)), kv_spec, vt_spec],
        out_specs=pl.BlockSpec((1, TQ, hb * HEAD_DIM), lambda b, h, i: (b, i, h)),
        compiler_params=pltpu.CompilerParams(
            dimension_semantics=("arbitrary", "arbitrary", "arbitrary"),
            vmem_limit_bytes=VMEM_LIMIT),
        name="attn",
    )(qp, kp, vp)


def _out_proj_kernel(x_ref, rec_ref, att_ref, aw_ref, wr_ref, wa_ref, ln_ref, rwh_ref, rwl_ref, rb_ref,
                     x1_ref, xs_ref, meta_ref, units_ref):
    tm = x_ref.shape[0]

    att = att_ref[...]
    ms = jnp.sum(att * att, axis=-1, keepdims=True) * (1.0 / D_ATT)
    att_n = (att * lax.rsqrt(ms + EPS) * aw_ref[...]).astype(BF16)
    x1 = x_ref[...] + _dot(rec_ref[...], wr_ref[...]) + _dot(att_n, wa_ref[...])
    x1_ref[...] = x1

    ms2 = jnp.mean(x1 * x1, axis=-1, keepdims=True)
    h2 = x1 * lax.rsqrt(ms2 + EPS) * ln_ref[...]

    h_hi = h2.astype(BF16)
    h_lo = (h2 - h_hi.astype(F32)).astype(BF16)
    nt = (((1,), (1,)), ((), ()))
    logits = (lax.dot_general(rwh_ref[...], h_hi, nt, preferred_element_type=F32)
              + lax.dot_general(rwh_ref[...], h_lo, nt, preferred_element_type=F32)
              + lax.dot_general(rwl_ref[...], h_hi, nt, preferred_element_type=F32) + rb_ref[...])
    expert = lax.broadcasted_iota(jnp.int32, (LANES, tm), 0)
    expert_f = expert.astype(F32)
    l = jnp.where(expert < N_EXPERTS, logits, -jnp.inf)

    vals, idxs, sels = [], [], []
    for _ in range(TOP_K):
        m = jnp.max(l, axis=0, keepdims=True)
        idx = jnp.min(jnp.where(l == m, expert_f, float(LANES)), axis=0, keepdims=True)
        sel = expert_f == idx
        vals.append(m)
        idxs.append(idx)
        sels.append(sel)
        l = jnp.where(sel, -jnp.inf, l)
    exps = [jnp.exp(v - vals[0]) for v in vals]
    denom = exps[0] + exps[1] + exps[2] + exps[3]

    onehot = jnp.zeros((LANES, tm), F32)
    for sel in sels:
        onehot = jnp.where(sel, 1.0, onehot)
    row = lax.broadcasted_iota(jnp.int32, (tm, tm), 0)
    col = lax.broadcasted_iota(jnp.int32, (tm, tm), 1)
    earlier_token = (row < col).astype(BF16)
    rank = _dot(onehot.astype(BF16), earlier_token)
    count = jnp.sum(onehot, axis=1, keepdims=True)
    units = jnp.floor((count + (UNIT - 1)) * (1.0 / UNIT))
    e_row = lax.broadcasted_iota(jnp.int32, (LANES, LANES), 0)
    e_col = lax.broadcasted_iota(jnp.int32, (LANES, LANES), 1)
    earlier_expert = (e_col < e_row).astype(BF16)
    units_wide = jnp.broadcast_to(units, (LANES, LANES))
    first_unit = _dot(earlier_expert, units_wide.astype(BF16))[:, 0:1]
    units_ref[0] = units_wide.T[0:8, :]
    place = first_unit * float(UNIT) + rank

    meta_t = jnp.zeros((LANES, tm), F32)
    for k in range(TOP_K):
        slot = jnp.sum(jnp.where(sels[k], place, 0.0), axis=0, keepdims=True)
        meta_t = jnp.where(expert == k, idxs[k], meta_t)
        meta_t = jnp.where(expert == TOP_K + k, exps[k] / denom, meta_t)
        meta_t = jnp.where(expert == 2 * TOP_K + k, slot, meta_t)
    meta_ref[...] = meta_t.T

    slot_rows = [meta_t[2 * TOP_K + k:2 * TOP_K + k + 1, :].astype(jnp.int32) for k in range(TOP_K)]
    for c in range(TILE_SLOTS // SLOT_CHUNK):
        slot_id = lax.broadcasted_iota(jnp.int32, (SLOT_CHUNK, tm), 0) + c * SLOT_CHUNK
        hit = jnp.zeros((SLOT_CHUNK, tm), F32)
        for slot_row in slot_rows:
            hit = jnp.where(slot_id == slot_row, 1.0, hit)
        xs_ref[c * SLOT_CHUNK:(c + 1) * SLOT_CHUNK, :] = _dot(hit.astype(BF16), h_hi).astype(BF16)


def _out_proj(x2d, rec_n, att, aw_slab, w_rec, w_att, ln2_w, rw_hi, rw_lo, rb_pad):
    t = x2d.shape[0]
    n_tiles = t // TM_OUT
    rows = lambda n: pl.BlockSpec((TM_OUT, n), lambda i: (i, 0))
    const = lambda shape: pl.BlockSpec(shape, lambda i: (0,) * len(shape))
    return pl.pallas_call(
        _out_proj_kernel,
        out_shape=(jax.ShapeDtypeStruct((t, D_MODEL), F32),
                   jax.ShapeDtypeStruct((n_tiles * TILE_SLOTS, D_MODEL), BF16),
                   jax.ShapeDtypeStruct((t, LANES), F32),
                   jax.ShapeDtypeStruct((n_tiles, 8, LANES), F32)),
        grid=(n_tiles,),
        in_specs=[rows(D_MODEL), rows(D_REC), rows(D_ATT), const((1, D_ATT)),
                  const((D_REC, D_MODEL)), const((D_ATT, D_MODEL)), const((1, D_MODEL)),
                  const((LANES, D_MODEL)), const((LANES, D_MODEL)), const((LANES, 1))],
        out_specs=(rows(D_MODEL), pl.BlockSpec((TILE_SLOTS, D_MODEL), lambda i: (i, 0)), rows(LANES),
                   pl.BlockSpec((1, 8, LANES), lambda i: (i, 0, 0))),
        compiler_params=pltpu.CompilerParams(dimension_semantics=("arbitrary",),
                                             vmem_limit_bytes=VMEM_LIMIT),
        name="out_proj",
    )(x2d, rec_n, att, aw_slab, w_rec, w_att, ln2_w, rw_hi, rw_lo, rb_pad)


def _start_units(unit_ref, base, n_units, src_hbm, dst, sem):
    def body(j, carry):
        src_row = pl.multiple_of(unit_ref[base + j] * UNIT, UNIT)
        dst_row = pl.multiple_of(j * UNIT, UNIT)
        pltpu.make_async_copy(src_hbm.at[pl.ds(src_row, UNIT), :],
                              dst.at[pl.ds(dst_row, UNIT), :], sem).start()
        return carry
    lax.fori_loop(0, n_units, body, 0, unroll=8)


def _wait_units(n_units, src_hbm, dst, sem):
    pltpu.make_async_copy(src_hbm.at[pl.ds(0, n_units * UNIT), :], dst, sem).wait()


def _experts_kernel(be_ref, nblk_ref, unit_ref, order_ref, next_ref, xs_hbm, wgu_hbm, bgu_ref, wd_hbm,
                    bd_ref, ys_ref, xbuf, wgu_f32, wd_f32, wgu_bf, wd_bf, sem, wsem):
    b = pl.program_id(0)
    nblk = nblk_ref[0]
    slot = b % 2
    e = be_ref[b]

    def weight_copies(expert, wslot):
        return (pltpu.make_async_copy(wgu_hbm.at[expert], wgu_f32.at[wslot], wsem.at[wslot]),
                pltpu.make_async_copy(wd_hbm.at[expert], wd_f32.at[wslot], wsem.at[wslot]))

    @pl.when(b == 0)
    def _():
        _start_units(unit_ref, 0, BLOCK_UNITS, xs_hbm, xbuf.at[0], sem.at[0])
        for copy in weight_copies(e, 0):
            copy.start()

    @pl.when(b + 1 < nblk)
    def _():
        _start_units(unit_ref, (b + 1) * BLOCK_UNITS, BLOCK_UNITS, xs_hbm, xbuf.at[1 - slot],
                     sem.at[1 - slot])

    new_expert = jnp.logical_or(b == 0, e != be_ref[jnp.maximum(b - 1, 0)])

    @pl.when(jnp.logical_and(b < nblk, new_expert))
    def _():
        wslot = order_ref[e] % 2
        for copy in weight_copies(e, wslot):
            copy.wait()
        wgu_bf[...] = wgu_f32[wslot].astype(BF16)
        wd_bf[...] = wd_f32[wslot].astype(BF16)
        nxt = next_ref[e]

        @pl.when(nxt >= 0)
        def _():
            for copy in weight_copies(nxt, 1 - wslot):
                copy.start()

    @pl.when(b < nblk)
    def _():
        _wait_units(BLOCK_UNITS, xs_hbm, xbuf.at[slot], sem.at[slot])
        gu = _dot(xbuf[slot], wgu_bf[...]) + bgu_ref[0]
        g = jnp.minimum(gu[:, :D_EXPERT], SWIGLU_LIMIT)
        u = jnp.clip(gu[:, D_EXPERT:], -SWIGLU_LIMIT, SWIGLU_LIMIT)
        act = (u + 1.0) * (g * _sigmoid(SWIGLU_ALPHA * g))
        ys_ref[...] = (_dot(act.astype(BF16), wd_bf[...]) + bd_ref[0]).astype(BF16)

    @pl.when(b >= nblk)
    def _():
        ys_ref[...] = jnp.zeros_like(ys_ref)


def _experts(block_expert, n_blocks_used, unit_src, expert_order, expert_next, xs, w_gate_up,
             b_gate_up, w_down, b_down):
    n_blocks = block_expert.shape[0]
    bias_map = lambda b, be, nb, us, eo, en: (be[b], 0, 0)
    grid_spec = pltpu.PrefetchScalarGridSpec(
        num_scalar_prefetch=5,
        grid=(n_blocks,),
        in_specs=[pl.BlockSpec(memory_space=pl.ANY),
                  pl.BlockSpec(memory_space=pl.ANY),
                  pl.BlockSpec((1, 1, 2 * D_EXPERT), bias_map),
                  pl.BlockSpec(memory_space=pl.ANY),
                  pl.BlockSpec((1, 1, D_MODEL), bias_map)],
        out_specs=pl.BlockSpec((TM_E, D_MODEL), lambda b, be, nb, us, eo, en: (b, 0)),
        scratch_shapes=[pltpu.VMEM((2, TM_E, D_MODEL), BF16),
                        pltpu.VMEM((2, D_MODEL, 2 * D_EXPERT), F32),
                        pltpu.VMEM((2, D_EXPERT, D_MODEL), F32),
                        pltpu.VMEM((D_MODEL, 2 * D_EXPERT), BF16),
                        pltpu.VMEM((D_EXPERT, D_MODEL), BF16),
                        pltpu.SemaphoreType.DMA((2,)),
                        pltpu.SemaphoreType.DMA((2,))],
    )
    return pl.pallas_call(
        _experts_kernel,
        out_shape=jax.ShapeDtypeStruct((n_blocks * TM_E, D_MODEL), BF16),
        grid_spec=grid_spec,
        compiler_params=pltpu.CompilerParams(dimension_semantics=("arbitrary",),
                                             vmem_limit_bytes=VMEM_LIMIT),
        name="experts",
    )(block_expert, n_blocks_used, unit_src, expert_order, expert_next, xs, w_gate_up,
      b_gate_up.reshape(N_EXPERTS, 1, 2 * D_EXPERT), w_down, b_down.reshape(N_EXPERTS, 1, D_MODEL))


def _combine_kernel(unit_ref, x1_ref, meta_ref, ys_hbm, o_ref, buf, sem):
    i = pl.program_id(0)
    n = pl.num_programs(0)
    slot = i % 2
    tm = x1_ref.shape[0]

    @pl.when(i == 0)
    def _():
        _start_units(unit_ref, 0, TILE_UNITS, ys_hbm, buf.at[0], sem.at[0])

    @pl.when(i + 1 < n)
    def _():
        _start_units(unit_ref, (i + 1) * TILE_UNITS, TILE_UNITS, ys_hbm, buf.at[1 - slot],
                     sem.at[1 - slot])

    meta = meta_ref[...]
    slots = [meta[:, 2 * TOP_K + k:2 * TOP_K + k + 1].astype(jnp.int32) for k in range(TOP_K)]
    gates = [meta[:, TOP_K + k:TOP_K + k + 1] for k in range(TOP_K)]
    _wait_units(TILE_UNITS, ys_hbm, buf.at[slot], sem.at[slot])

    acc = x1_ref[...]
    for c in range(TILE_SLOTS // SLOT_CHUNK):
        first = c * SLOT_CHUNK
        col = lax.broadcasted_iota(jnp.int32, (tm, SLOT_CHUNK), 1) + first
        weight = jnp.zeros((tm, SLOT_CHUNK), F32)
        for k in range(TOP_K):
            weight = jnp.where(col == slots[k], gates[k], weight)
        acc = acc + _dot(weight.astype(BF16), buf[slot, first:first + SLOT_CHUNK, :])
    o_ref[...] = acc


def _combine(tile_units, x1, meta, ys):
    t = x1.shape[0]
    grid_spec = pltpu.PrefetchScalarGridSpec(
        num_scalar_prefetch=1,
        grid=(t // TM_OUT,),
        in_specs=[pl.BlockSpec((TM_OUT, D_MODEL), lambda i, us: (i, 0)),
                  pl.BlockSpec((TM_OUT, LANES), lambda i, us: (i, 0)),
                  pl.BlockSpec(memory_space=pl.ANY)],
        out_specs=pl.BlockSpec((TM_OUT, D_MODEL), lambda i, us: (i, 0)),
        scratch_shapes=[pltpu.VMEM((2, TILE_SLOTS, D_MODEL), BF16),
                        pltpu.SemaphoreType.DMA((2,))],
    )
    return pl.pallas_call(
        _combine_kernel,
        out_shape=jax.ShapeDtypeStruct((t, D_MODEL), F32),
        grid_spec=grid_spec,
        compiler_params=pltpu.CompilerParams(dimension_semantics=("arbitrary",),
                                             vmem_limit_bytes=VMEM_LIMIT),
        name="combine",
    )(tile_units, x1, meta, ys)


def _gate_matrix(w_a, w_i):
    g = w_a.shape[0]
    eye = jnp.eye(g, dtype=w_a.dtype)
    both = jnp.stack([w_a, w_i])
    return jnp.einsum('sgij,gh->gishj', both, eye).reshape(g * REC_BLOCK, 2 * g * REC_BLOCK)


def _pad_lanes(v, n=LANES):
    return jnp.pad(v, (0, n - v.shape[0])).reshape(1, n)


def _excl_cumsum(a, axis):
    return jnp.cumsum(a, axis=axis) - a


def _pick(onehot, table):
    if table.ndim == 1:
        return jnp.sum(onehot * table[None, :], axis=1)
    return jnp.sum(onehot[:, :, None] * table[None, :, :], axis=1)


def _dispatch_tables(units, n_blocks):
    n_tiles = units.shape[0]
    i32 = jnp.int32
    tile_first = _excl_cumsum(units, 1)
    tile_end = tile_first + units
    run_first = _excl_cumsum(units, 0)
    run_end = run_first + units
    region_units = jnp.sum(units, axis=0)
    region_blocks = (region_units + BLOCK_UNITS - 1) // BLOCK_UNITS
    block_end = jnp.cumsum(region_blocks)
    region_first = (block_end - region_blocks) * BLOCK_UNITS
    n_used = block_end[-1:]
    experts = jnp.arange(N_EXPERTS, dtype=i32)
    tiles = jnp.arange(n_tiles, dtype=i32)

    blocks = jnp.arange(n_blocks, dtype=i32)
    block_expert = jnp.minimum(
        jnp.sum((block_end[None, :] <= blocks[:, None]).astype(i32), axis=1), N_EXPERTS - 1)

    g = jnp.arange(n_blocks * BLOCK_UNITS, dtype=i32)
    e = jnp.minimum(jnp.sum((block_end[None, :] * BLOCK_UNITS <= g[:, None]).astype(i32), axis=1),
                    N_EXPERTS - 1)
    e_hot = (e[:, None] == experts[None, :]).astype(i32)
    local = g - _pick(e_hot, region_first)
    valid = (local < _pick(e_hot, region_units)) & (g < n_used[0] * BLOCK_UNITS)
    ends = _pick(e_hot, run_end.T)
    tile = jnp.minimum(jnp.sum((ends <= local[:, None]).astype(i32), axis=1), n_tiles - 1)
    t_hot = (tile[:, None] == tiles[None, :]).astype(i32)
    src = (tile * TILE_UNITS + jnp.sum(_pick(t_hot, tile_first - run_first) * e_hot, axis=1) + local)
    unit_src = jnp.where(valid, src, 0).astype(i32)

    j = jnp.arange(TILE_UNITS, dtype=i32)
    ej = jnp.minimum(jnp.sum((tile_end[:, None, :] <= j[None, :, None]).astype(i32), axis=2),
                     N_EXPERTS - 1)
    ej_hot = (ej[:, :, None] == experts[None, None, :]).astype(i32)
    used = j[None, :] < jnp.sum(units, axis=1)[:, None]
    base = region_first[None, :] + run_first - tile_first
    dst = jnp.sum(ej_hot * base[:, None, :], axis=2) + j[None, :]
    tile_units = jnp.where(used, dst, 0).astype(i32).reshape(-1)
    owns = region_blocks > 0
    expert_order = _excl_cumsum(owns.astype(i32), 0)
    later = (experts[None, :] > experts[:, None]) & owns[None, :]
    expert_next = jnp.min(jnp.where(later, experts[None, :], N_EXPERTS), axis=1)
    expert_next = jnp.where(expert_next < N_EXPERTS, expert_next, -1).astype(i32)
    return (block_expert.astype(i32), n_used.astype(i32), unit_src, tile_units,
            expert_order.astype(i32), expert_next)


def _layer(x, ln1_w, w_in, conv_w, conv_b, w_rg_a, b_rg_a, w_rg_i, b_rg_i, rg_lambda, b_forget,
           q_norm_w, k_norm_w, rec_out_norm_w, attn_out_norm_w, w_out, ln2_w, w_router, b_router,
           w_gate_up, b_gate_up, w_down, b_down):
    batch, seq, _ = x.shape
    t = batch * seq
    x2d = x.reshape(t, D_MODEL)

    o_q = 2 * D_REC
    o_k = o_q + D_ATT
    o_v = o_k + D_ATT
    o_f = o_v + D_ATT
    w_t = w_in.T.astype(BF16)
    w_ft = jnp.pad(w_t[o_f:], ((0, LANES - N_HEADS), (0, 0)))
    scale = HEAD_DIM ** -0.5
    u, g, qp, kp, vp = _in_proj(
        x2d, ln1_w.reshape(1, D_MODEL), w_in[:, :o_q].astype(BF16), w_t[o_q:o_k], w_t[o_k:o_v],
        w_t[o_v:o_f], w_ft, _pad_lanes(b_forget).reshape(LANES, 1),
        (q_norm_w * scale).reshape(HEAD_DIM, 1), k_norm_w.reshape(HEAD_DIM, 1), batch, seq)

    rec_vec = jnp.concatenate(
        [conv_w, jnp.stack([conv_b, b_rg_a, b_rg_i, rg_lambda, rec_out_norm_w])])
    rec_n = _rglru(u.reshape(batch, seq, D_REC), g.reshape(batch, seq, D_REC), rec_vec,
                   _gate_matrix(w_rg_a, w_rg_i).astype(BF16))

    att = _attention(qp, kp, vp)

    rw = jnp.pad(w_router.T, ((0, LANES - N_EXPERTS), (0, 0)))
    rw_hi = rw.astype(BF16)
    rw_lo = (rw - rw_hi.astype(F32)).astype(BF16)
    x1, xs, meta, units = _out_proj(
        x2d, rec_n.reshape(t, D_REC), att.reshape(t, D_ATT), attn_out_norm_w.reshape(1, D_ATT),
        w_out[:D_REC].astype(BF16), w_out[D_REC:].astype(BF16), ln2_w.reshape(1, D_MODEL),
        rw_hi, rw_lo, _pad_lanes(b_router).reshape(LANES, 1))

    n_tiles = t // TM_OUT
    max_units = t * TOP_K // UNIT + n_tiles * N_EXPERTS
    n_blocks = -(-max_units // BLOCK_UNITS) + N_EXPERTS
    block_expert, n_used, unit_src, tile_units, expert_order, expert_next = _dispatch_tables(
        units[:, 0, :N_EXPERTS].astype(jnp.int32), n_blocks)

    ys = _experts(block_expert, n_used, unit_src, expert_order, expert_next, xs, w_gate_up, b_gate_up,
                  w_down, b_down)
    out = _combine(tile_units, x1, meta, ys)
    return out.reshape(batch, seq, D_MODEL)


def kernel(x, ln1_w, w_in, conv_w, conv_b, w_rg_a, b_rg_a, w_rg_i, b_rg_i, rg_lambda, b_forget,
           q_norm_w, k_norm_w, rec_out_norm_w, attn_out_norm_w, w_out, ln2_w, w_router, b_router,
           w_gate_up, b_gate_up, w_down, b_down):
    for l in range(ln1_w.shape[0]):
        x = _layer(x, ln1_w[l], w_in[l], conv_w[l], conv_b[l], w_rg_a[l], b_rg_a[l], w_rg_i[l],
                   b_rg_i[l], rg_lambda[l], b_forget[l], q_norm_w[l], k_norm_w[l], rec_out_norm_w[l],
                   attn_out_norm_w[l], w_out[l], ln2_w[l], w_router[l], b_router[l], w_gate_up[l],
                   b_gate_up[l], w_down[l], b_down[l])
    return x
```

```python
import functools

import jax
import jax.numpy as jnp
from jax import lax
from jax.experimental import pallas as pl
from jax.experimental.pallas import tpu as pltpu

D_MODEL = 1024
D_REC = 512
REC_BLOCK = 64
CONV_WIDTH = 4
RGLRU_C = 8.0
D_ATT = 512
HEAD_DIM = 64
N_HEADS = 8
N_EXPERTS = 32
TOP_K = 4
D_EXPERT = 1024
SWIGLU_LIMIT = 7.0
SWIGLU_ALPHA = 1.702
EPS = 1e-6

LANES = 128
SLAB = LANES
V_ROWS = HEAD_DIM + 16
VMEM_LIMIT = 56 * 1024 * 1024

F32 = jnp.float32
BF16 = jnp.bfloat16

O_U = 0
O_G = O_U + D_REC
N_PROJ = O_G + D_REC

L_C = HEAD_DIM
L_ONE = HEAD_DIM + 3

TM_IN = 512
TS_REC = 512
TQ = 512
TK = 512
HEADS_PER_STEP = 8
TM_OUT = 512
TM_E = 512

UNIT = 16
TILE_SLOTS = -(-(TM_OUT * TOP_K + N_EXPERTS * (UNIT - 1)) // LANES) * LANES
TILE_UNITS = TILE_SLOTS // UNIT
BLOCK_UNITS = TM_E // UNIT
SLOT_CHUNK = 512


def _split3(v):
    hi = v.astype(BF16).astype(F32)
    r = v - hi
    mid = r.astype(BF16).astype(F32)
    lo = (r - mid).astype(BF16).astype(F32)
    return hi, mid, lo


def _dot(a, b):
    return jnp.dot(a, b, preferred_element_type=F32)


def _sigmoid(x):
    return 0.5 * jnp.tanh(0.5 * x) + 0.5


def _in_proj_kernel(x_ref, ln_ref, w_ref, wqt_ref, wkt_ref, wvt_ref, wft_ref, bf_ref, qw_ref, kw_ref,
                    u_ref, g_ref, q_ref, k_ref, v_ref, carry_ref, *, tiles_per_seq):
    i = pl.program_id(0)
    tm = x_ref.shape[0]

    @pl.when(i % tiles_per_seq == 0)
    def _():
        carry_ref[...] = jnp.zeros_like(carry_ref)

    x = x_ref[...]
    ms = jnp.mean(x * x, axis=-1, keepdims=True)
    h = (x * lax.rsqrt(ms + EPS) * ln_ref[...]).astype(BF16)
    nt = (((1,), (1,)), ((), ()))

    f_t = lax.dot_general(wft_ref[...], h, nt, preferred_element_type=F32) + bf_ref[...]
    logf_t = jnp.minimum(f_t, 0.0) - jnp.log1p(jnp.exp(-jnp.abs(f_t)))
    row = lax.broadcasted_iota(jnp.int32, (tm, tm), 0)
    col = lax.broadcasted_iota(jnp.int32, (tm, tm), 1)
    upto = (row <= col).astype(BF16)
    hi, mid, lo = _split3(logf_t)
    c_t = (_dot(hi.astype(BF16), upto) + _dot(mid.astype(BF16), upto)
           + _dot(lo.astype(BF16), upto) + carry_ref[...])
    carry_ref[...] = c_t[:, tm - 1:tm]

    qt_all = lax.dot_general(wqt_ref[...], h, nt, preferred_element_type=F32)
    kt_all = lax.dot_general(wkt_ref[...], h, nt, preferred_element_type=F32)
    spare = lax.broadcasted_iota(jnp.int32, (SLAB - HEAD_DIM, tm), 0) + HEAD_DIM
    qw = qw_ref[...]
    kw = kw_ref[...]

    def normed(rows, weight):
        inv = lax.rsqrt(jnp.sum(rows * rows, axis=0, keepdims=True) * (1.0 / HEAD_DIM) + EPS)
        return rows * inv * weight

    for hd in range(N_HEADS):
        rows = slice(hd * HEAD_DIM, (hd + 1) * HEAD_DIM)
        c_hi, c_mid, c_lo = _split3(c_t[hd:hd + 1, :])
        q_extra = jnp.where(spare == L_C, c_hi, 0.0)
        q_extra = jnp.where(spare == L_C + 1, c_mid, q_extra)
        q_extra = jnp.where(spare == L_C + 2, c_lo, q_extra)
        q_extra = jnp.where((spare >= L_ONE) & (spare < L_ONE + 3), 1.0, q_extra)
        k_extra = jnp.where(spare < L_C + 3, 1.0, 0.0)
        k_extra = jnp.where(spare == L_ONE, -c_hi, k_extra)
        k_extra = jnp.where(spare == L_ONE + 1, -c_mid, k_extra)
        k_extra = jnp.where(spare == L_ONE + 2, -c_lo, k_extra)
        q_ref[0, hd] = jnp.concatenate([normed(qt_all[rows, :], qw), q_extra], axis=0).T.astype(BF16)
        k_ref[0, hd] = jnp.concatenate([normed(kt_all[rows, :], kw), k_extra], axis=0).T.astype(BF16)

    vt_all = lax.dot_general(wvt_ref[...], h, (((1,), (1,)), ((), ())), preferred_element_type=F32)
    spare = lax.broadcasted_iota(jnp.int32, (V_ROWS - HEAD_DIM, tm), 0)
    ones_row = jnp.where(spare == 0, 1.0, 0.0)
    for hd in range(N_HEADS):
        v_ref[0, hd] = jnp.concatenate(
            [vt_all[hd * HEAD_DIM:(hd + 1) * HEAD_DIM, :], ones_row], axis=0).astype(BF16)
    u_ref[...] = _dot(h, w_ref[:, O_U:O_U + D_REC])
    g_ref[...] = _dot(h, w_ref[:, O_G:O_G + D_REC])


def _in_proj(x2d, ln1_w, w_all, w_qt, w_kt, w_vt, w_ft, bf_col, qw_col, kw_col, batch, seq):
    t = x2d.shape[0]
    tiles_per_seq = seq // TM_IN
    slab_shape = jax.ShapeDtypeStruct((batch, N_HEADS, seq, SLAB), BF16)
    slab_spec = pl.BlockSpec((1, N_HEADS, TM_IN, SLAB),
                             lambda i: (i // tiles_per_seq, 0, i % tiles_per_seq, 0))
    slab_t_shape = jax.ShapeDtypeStruct((batch, N_HEADS, V_ROWS, seq), BF16)
    slab_t_spec = pl.BlockSpec((1, N_HEADS, V_ROWS, TM_IN),
                               lambda i: (i // tiles_per_seq, 0, 0, i % tiles_per_seq))
    row_spec = pl.BlockSpec((TM_IN, D_REC), lambda i: (i, 0))
    const = lambda shape: pl.BlockSpec(shape, lambda i: (0,) * len(shape))
    return pl.pallas_call(
        functools.partial(_in_proj_kernel, tiles_per_seq=tiles_per_seq),
        out_shape=(jax.ShapeDtypeStruct((t, D_REC), F32), jax.ShapeDtypeStruct((t, D_REC), F32),
                   slab_shape, slab_shape, slab_t_shape),
        grid=(t // TM_IN,),
        in_specs=[pl.BlockSpec((TM_IN, D_MODEL), lambda i: (i, 0)),
                  const((1, D_MODEL)), const((D_MODEL, N_PROJ)), const((D_ATT, D_MODEL)),
                  const((D_ATT, D_MODEL)), const((D_ATT, D_MODEL)), const((LANES, D_MODEL)),
                  const((LANES, 1)), const((HEAD_DIM, 1)), const((HEAD_DIM, 1))],
        out_specs=(row_spec, row_spec, slab_spec, slab_spec, slab_t_spec),
        scratch_shapes=[pltpu.VMEM((LANES, 1), F32)],
        compiler_params=pltpu.CompilerParams(dimension_semantics=("arbitrary",),
                                             vmem_limit_bytes=VMEM_LIMIT),
        name="in_proj",
    )(x2d, ln1_w, w_all, w_qt, w_kt, w_vt, w_ft, bf_col, qw_col, kw_col)


R_CONV_B = CONV_WIDTH
R_B_A = CONV_WIDTH + 1
R_B_I = CONV_WIDTH + 2
R_LAMBDA = CONV_WIDTH + 3
R_NORM_W = CONV_WIDTH + 4
N_REC_ROWS = CONV_WIDTH + 5


def _rglru_kernel(u_ref, g_ref, vec_ref, wg_ref, o_ref, ext_ref, a_ref, b_ref, h_ref, hcar_ref):
    s = pl.program_id(1)
    ts = u_ref.shape[1]
    pad = 8
    row = lambda r: vec_ref[r:r + 1, :]

    @pl.when(s == 0)
    def _():
        ext_ref[0:pad, :] = jnp.zeros((pad, D_REC), F32)
        hcar_ref[...] = jnp.zeros_like(hcar_ref)

    @pl.when(s != 0)
    def _():
        ext_ref[0:pad, :] = ext_ref[ts:ts + pad, :]

    u = u_ref[0]
    ext_ref[pad:pad + ts, :] = u
    xc = row(R_CONV_B) + u * row(CONV_WIDTH - 1)
    for tap in range(CONV_WIDTH - 1):
        d = CONV_WIDTH - 1 - tap
        xc = xc + ext_ref[pad - d:pad - d + ts, :] * row(tap)

    gates = _dot(xc.astype(BF16), wg_ref[...])
    r = _sigmoid(gates[:, :D_REC] + row(R_B_A))
    ig = _sigmoid(gates[:, D_REC:] + row(R_B_I))
    nlam = -row(R_LAMBDA)
    softplus = jnp.maximum(nlam, 0.0) + jnp.log1p(jnp.exp(-jnp.abs(nlam)))
    log_a = -RGLRU_C * r * softplus
    a = jnp.exp(log_a)
    a_ref[...] = a
    b_ref[...] = jnp.sqrt(1.0 - jnp.exp(2.0 * log_a)) * (ig * xc)

    def step(t, h):
        h = a_ref[pl.ds(t, 1), :] * h + b_ref[pl.ds(t, 1), :]
        h_ref[pl.ds(t, 1), :] = h
        return h

    hcar_ref[...] = lax.fori_loop(0, ts, step, hcar_ref[...], unroll=8)

    rec = h_ref[...] * jax.nn.gelu(g_ref[0])
    ms = jnp.mean(rec * rec, axis=-1, keepdims=True)
    o_ref[0] = (rec * lax.rsqrt(ms + EPS) * row(R_NORM_W)).astype(BF16)


def _rglru(u, g, rec_vec, w_gates):
    batch, seq, _ = u.shape
    seq_spec = pl.BlockSpec((1, TS_REC, D_REC), lambda b, s: (b, s, 0))
    const = lambda shape: pl.BlockSpec(shape, lambda b, s: (0,) * len(shape))
    return pl.pallas_call(
        _rglru_kernel,
        out_shape=jax.ShapeDtypeStruct((batch, seq, D_REC), BF16),
        grid=(batch, seq // TS_REC),
        in_specs=[seq_spec, seq_spec, const((N_REC_ROWS, D_REC)), const((D_REC, 2 * D_REC))],
        out_specs=seq_spec,
        scratch_shapes=[pltpu.VMEM((TS_REC + 8, D_REC), F32), pltpu.VMEM((TS_REC, D_REC), F32),
                        pltpu.VMEM((TS_REC, D_REC), F32), pltpu.VMEM((TS_REC, D_REC), F32),
                        pltpu.VMEM((1, D_REC), F32)],
        compiler_params=pltpu.CompilerParams(dimension_semantics=("arbitrary", "arbitrary"),
                                             vmem_limit_bytes=VMEM_LIMIT),
        name="rglru",
    )(u, g, rec_vec, w_gates)


def _attn_kernel(q_ref, k_ref, v_ref, o_ref):
    qi = pl.program_id(2)
    hb = q_ref.shape[1]
    tq = q_ref.shape[2]
    qs = [q_ref[0, h] for h in range(hb)]

    def step(j, carry, masked):
        start = pl.multiple_of(j * TK, TK)
        scores = []
        for h in range(hb):
            kj = k_ref[0, h, pl.ds(start, TK), :]
            s = lax.dot_general(kj, qs[h], (((1,), (1,)), ((), ())), preferred_element_type=F32)
            if masked:
                key = lax.broadcasted_iota(jnp.int32, (TK, tq), 0)
                qry = lax.broadcasted_iota(jnp.int32, (TK, tq), 1)
                s = jnp.where(qry >= key, s, -jnp.inf)
            scores.append(s)
        m_news = [jnp.maximum(carry[h][0], jnp.max(scores[h], axis=0, keepdims=True))
                  for h in range(hb)]
        probs = [jnp.exp(scores[h] - m_news[h]).astype(BF16) for h in range(hb)]
        out = []
        for h in range(hb):
            m, acc = carry[h]
            vj = v_ref[0, h, :, pl.ds(start, TK)]
            out.append((m_news[h], jnp.exp(m - m_news[h]) * acc + _dot(vj, probs[h])))
        return tuple(out)

    init = tuple((jnp.full((1, tq), -jnp.inf, F32), jnp.zeros((V_ROWS, tq), F32)) for _ in range(hb))
    carry = lax.fori_loop(0, qi, lambda j, c: step(j, c, False), init)
    carry = step(qi, carry, True)
    outs = [carry[h][1][:HEAD_DIM, :] / carry[h][1][HEAD_DIM:HEAD_DIM + 1, :] for h in range(hb)]
    for pair in range(hb // 2):
        both = jnp.concatenate([outs[2 * pair], outs[2 * pair + 1]], axis=0)
        o_ref[0, :, pair * LANES:(pair + 1) * LANES] = both.T


def _attention(qp, kp, vp):
    batch, _, seq, _ = kp.shape
    hb = HEADS_PER_STEP
    kv_spec = pl.BlockSpec((1, hb, seq, SLAB), lambda b, h, i: (b, h, 0, 0))
    vt_spec = pl.BlockSpec((1, hb, V_ROWS, seq), lambda b, h, i: (b, h, 0, 0))
    return pl.pallas_call(
        _attn_kernel,
        out_shape=jax.ShapeDtypeStruct((batch, seq, D_ATT), F32),
        grid=(batch, N_HEADS // hb, seq // TQ),
        in_specs=[pl.BlockSpec((1, hb, TQ, SLAB), lambda b, h, i: (b, h, i, 0)), kv_spec, vt_spec],
        out_specs=pl.BlockSpec((1, TQ, hb * HEAD_DIM), lambda b, h, i: (b, i, h)),
        compiler_params=pltpu.CompilerParams(
            dimension_semantics=("arbitrary", "arbitrary", "arbitrary"),
            vmem_limit_bytes=VMEM_LIMIT),
        name="attn",
    )(qp, kp, vp)


def _out_proj_kernel(x_ref, rec_ref, att_ref, aw_ref, wr_ref, wa_ref, ln_ref, rwh_ref, rwl_ref, rb_ref,
                     x1_ref, xs_ref, meta_ref, units_ref):
    tm = x_ref.shape[0]

    att = att_ref[...]
    ms = jnp.sum(att * att, axis=-1, keepdims=True) * (1.0 / D_ATT)
    att_n = (att * lax.rsqrt(ms + EPS) * aw_ref[...]).astype(BF16)
    x1 = x_ref[...] + _dot(rec_ref[...], wr_ref[...]) + _dot(att_n, wa_ref[...])
    x1_ref[...] = x1

    ms2 = jnp.mean(x1 * x1, axis=-1, keepdims=True)
    h2 = x1 * lax.rsqrt(ms2 + EPS) * ln_ref[...]

    h_hi = h2.astype(BF16)
    h_lo = (h2 - h_hi.astype(F32)).astype(BF16)
    nt = (((1,), (1,)), ((), ()))
    logits = (lax.dot_general(rwh_ref[...], h_hi, nt, preferred_element_type=F32)
              + lax.dot_general(rwh_ref[...], h_lo, nt, preferred_element_type=F32)
              + lax.dot_general(rwl_ref[...], h_hi, nt, preferred_element_type=F32) + rb_ref[...])
    expert = lax.broadcasted_iota(jnp.int32, (LANES, tm), 0)
    expert_f = expert.astype(F32)
    l = jnp.where(expert < N_EXPERTS, logits, -jnp.inf)

    vals, idxs, sels = [], [], []
    for _ in range(TOP_K):
        m = jnp.max(l, axis=0, keepdims=True)
        idx = jnp.min(jnp.where(l == m, expert_f, float(LANES)), axis=0, keepdims=True)
        sel = expert_f == idx
        vals.append(m)
        idxs.append(idx)
        sels.append(sel)
        l = jnp.where(sel, -jnp.inf, l)
    exps = [jnp.exp(v - vals[0]) for v in vals]
    denom = exps[0] + exps[1] + exps[2] + exps[3]

    onehot = jnp.zeros((LANES, tm), F32)
    for sel in sels:
        onehot = jnp.where(sel, 1.0, onehot)
    row = lax.broadcasted_iota(jnp.int32, (tm, tm), 0)
    col = lax.broadcasted_iota(jnp.int32, (tm, tm), 1)
    earlier_token = (row < col).astype(BF16)
    rank = _dot(onehot.astype(BF16), earlier_token)
    count = jnp.sum(onehot, axis=1, keepdims=True)
    units = jnp.floor((count + (UNIT - 1)) * (1.0 / UNIT))
    e_row = lax.broadcasted_iota(jnp.int32, (LANES, LANES), 0)
    e_col = lax.broadcasted_iota(jnp.int32, (LANES, LANES), 1)
    earlier_expert = (e_col < e_row).astype(BF16)
    units_wide = jnp.broadcast_to(units, (LANES, LANES))
    first_unit = _dot(earlier_expert, units_wide.astype(BF16))[:, 0:1]
    units_ref[0] = units_wide.T[0:8, :]
    place = first_unit * float(UNIT) + rank

    meta_t = jnp.zeros((LANES, tm), F32)
    for k in range(TOP_K):
        slot = jnp.sum(jnp.where(sels[k], place, 0.0), axis=0, keepdims=True)
        meta_t = jnp.where(expert == k, idxs[k], meta_t)
        meta_t = jnp.where(expert == TOP_K + k, exps[k] / denom, meta_t)
        meta_t = jnp.where(expert == 2 * TOP_K + k, slot, meta_t)
    meta_ref[...] = meta_t.T

    slot_rows = [meta_t[2 * TOP_K + k:2 * TOP_K + k + 1, :].astype(jnp.int32) for k in range(TOP_K)]
    for c in range(TILE_SLOTS // SLOT_CHUNK):
        slot_id = lax.broadcasted_iota(jnp.int32, (SLOT_CHUNK, tm), 0) + c * SLOT_CHUNK
        hit = jnp.zeros((SLOT_CHUNK, tm), F32)
        for slot_row in slot_rows:
            hit = jnp.where(slot_id == slot_row, 1.0, hit)
        xs_ref[c * SLOT_CHUNK:(c + 1) * SLOT_CHUNK, :] = _dot(hit.astype(BF16), h_hi).astype(BF16)


def _out_proj(x2d, rec_n, att, aw_slab, w_rec, w_att, ln2_w, rw_hi, rw_lo, rb_pad):
    t = x2d.shape[0]
    n_tiles = t // TM_OUT
    rows = lambda n: pl.BlockSpec((TM_OUT, n), lambda i: (i, 0))
    const = lambda shape: pl.BlockSpec(shape, lambda i: (0,) * len(shape))
    return pl.pallas_call(
        _out_proj_kernel,
        out_shape=(jax.ShapeDtypeStruct((t, D_MODEL), F32),
                   jax.ShapeDtypeStruct((n_tiles * TILE_SLOTS, D_MODEL), BF16),
                   jax.ShapeDtypeStruct((t, LANES), F32),
                   jax.ShapeDtypeStruct((n_tiles, 8, LANES), F32)),
        grid=(n_tiles,),
        in_specs=[rows(D_MODEL), rows(D_REC), rows(D_ATT), const((1, D_ATT)),
                  const((D_REC, D_MODEL)), const((D_ATT, D_MODEL)), const((1, D_MODEL)),
                  const((LANES, D_MODEL)), const((LANES, D_MODEL)), const((LANES, 1))],
        out_specs=(rows(D_MODEL), pl.BlockSpec((TILE_SLOTS, D_MODEL), lambda i: (i, 0)), rows(LANES),
                   pl.BlockSpec((1, 8, LANES), lambda i: (i, 0, 0))),
        compiler_params=pltpu.CompilerParams(dimension_semantics=("arbitrary",),
                                             vmem_limit_bytes=VMEM_LIMIT),
        name="out_proj",
    )(x2d, rec_n, att, aw_slab, w_rec, w_att, ln2_w, rw_hi, rw_lo, rb_pad)


def _start_units(unit_ref, base, n_units, src_hbm, dst, sem):
    def body(j, carry):
        src_row = pl.multiple_of(unit_ref[base + j] * UNIT, UNIT)
        dst_row = pl.multiple_of(j * UNIT, UNIT)
        pltpu.make_async_copy(src_hbm.at[pl.ds(src_row, UNIT), :],
                              dst.at[pl.ds(dst_row, UNIT), :], sem).start()
        return carry
    lax.fori_loop(0, n_units, body, 0, unroll=8)


def _wait_units(n_units, src_hbm, dst, sem):
    pltpu.make_async_copy(src_hbm.at[pl.ds(0, n_units * UNIT), :], dst, sem).wait()


def _experts_kernel(be_ref, nblk_ref, unit_ref, order_ref, next_ref, xs_hbm, wgu_hbm, bgu_ref, wd_hbm,
                    bd_ref, ys_ref, xbuf, wgu_f32, wd_f32, wgu_bf, wd_bf, sem, wsem):
    b = pl.program_id(0)
    nblk = nblk_ref[0]
    slot = b % 2
    e = be_ref[b]

    def weight_copies(expert, wslot):
        return (pltpu.make_async_copy(wgu_hbm.at[expert], wgu_f32.at[wslot], wsem.at[wslot]),
                pltpu.make_async_copy(wd_hbm.at[expert], wd_f32.at[wslot], wsem.at[wslot]))

    @pl.when(b == 0)
    def _():
        _start_units(unit_ref, 0, BLOCK_UNITS, xs_hbm, xbuf.at[0], sem.at[0])
        for copy in weight_copies(e, 0):
            copy.start()

    @pl.when(b + 1 < nblk)
    def _():
        _start_units(unit_ref, (b + 1) * BLOCK_UNITS, BLOCK_UNITS, xs_hbm, xbuf.at[1 - slot],
                     sem.at[1 - slot])

    new_expert = jnp.logical_or(b == 0, e != be_ref[jnp.maximum(b - 1, 0)])

    @pl.when(jnp.logical_and(b < nblk, new_expert))
    def _():
        wslot = order_ref[e] % 2
        for copy in weight_copies(e, wslot):
            copy.wait()
        wgu_bf[...] = wgu_f32[wslot].astype(BF16)
        wd_bf[...] = wd_f32[wslot].astype(BF16)
        nxt = next_ref[e]

        @pl.when(nxt >= 0)
        def _():
            for copy in weight_copies(nxt, 1 - wslot):
                copy.start()

    @pl.when(b < nblk)
    def _():
        _wait_units(BLOCK_UNITS, xs_hbm, xbuf.at[slot], sem.at[slot])
        gu = _dot(xbuf[slot], wgu_bf[...]) + bgu_ref[0]
        g = jnp.minimum(gu[:, :D_EXPERT], SWIGLU_LIMIT)
        u = jnp.clip(gu[:, D_EXPERT:], -SWIGLU_LIMIT, SWIGLU_LIMIT)
        act = (u + 1.0) * (g * _sigmoid(SWIGLU_ALPHA * g))
        ys_ref[...] = (_dot(act.astype(BF16), wd_bf[...]) + bd_ref[0]).astype(BF16)

    @pl.when(b >= nblk)
    def _():
        ys_ref[...] = jnp.zeros_like(ys_ref)


def _experts(block_expert, n_blocks_used, unit_src, expert_order, expert_next, xs, w_gate_up,
             b_gate_up, w_down, b_down):
    n_blocks = block_expert.shape[0]
    bias_map = lambda b, be, nb, us, eo, en: (be[b], 0, 0)
    grid_spec = pltpu.PrefetchScalarGridSpec(
        num_scalar_prefetch=5,
        grid=(n_blocks,),
        in_specs=[pl.BlockSpec(memory_space=pl.ANY),
                  pl.BlockSpec(memory_space=pl.ANY),
                  pl.BlockSpec((1, 1, 2 * D_EXPERT), bias_map),
                  pl.BlockSpec(memory_space=pl.ANY),
                  pl.BlockSpec((1, 1, D_MODEL), bias_map)],
        out_specs=pl.BlockSpec((TM_E, D_MODEL), lambda b, be, nb, us, eo, en: (b, 0)),
        scratch_shapes=[pltpu.VMEM((2, TM_E, D_MODEL), BF16),
                        pltpu.VMEM((2, D_MODEL, 2 * D_EXPERT), F32),
                        pltpu.VMEM((2, D_EXPERT, D_MODEL), F32),
                        pltpu.VMEM((D_MODEL, 2 * D_EXPERT), BF16),
                        pltpu.VMEM((D_EXPERT, D_MODEL), BF16),
                        pltpu.SemaphoreType.DMA((2,)),
                        pltpu.SemaphoreType.DMA((2,))],
    )
    return pl.pallas_call(
        _experts_kernel,
        out_shape=jax.ShapeDtypeStruct((n_blocks * TM_E, D_MODEL), BF16),
        grid_spec=grid_spec,
        compiler_params=pltpu.CompilerParams(dimension_semantics=("arbitrary",),
                                             vmem_limit_bytes=VMEM_LIMIT),
        name="experts",
    )(block_expert, n_blocks_used, unit_src, expert_order, expert_next, xs, w_gate_up,
      b_gate_up.reshape(N_EXPERTS, 1, 2 * D_EXPERT), w_down, b_down.reshape(N_EXPERTS, 1, D_MODEL))


def _combine_kernel(unit_ref, x1_ref, meta_ref, ys_hbm, o_ref, buf, sem):
    i = pl.program_id(0)
    n = pl.num_programs(0)
    slot = i % 2
    tm = x1_ref.shape[0]

    @pl.when(i == 0)
    def _():
        _start_units(unit_ref, 0, TILE_UNITS, ys_hbm, buf.at[0], sem.at[0])

    @pl.when(i + 1 < n)
    def _():
        _start_units(unit_ref, (i + 1) * TILE_UNITS, TILE_UNITS, ys_hbm, buf.at[1 - slot],
                     sem.at[1 - slot])

    meta = meta_ref[...]
    slots = [meta[:, 2 * TOP_K + k:2 * TOP_K + k + 1].astype(jnp.int32) for k in range(TOP_K)]
    gates = [meta[:, TOP_K + k:TOP_K + k + 1] for k in range(TOP_K)]
    _wait_units(TILE_UNITS, ys_hbm, buf.at[slot], sem.at[slot])

    acc = x1_ref[...]
    for c in range(TILE_SLOTS // SLOT_CHUNK):
        first = c * SLOT_CHUNK
        col = lax.broadcasted_iota(jnp.int32, (tm, SLOT_CHUNK), 1) + first
        weight = jnp.zeros((tm, SLOT_CHUNK), F32)
        for k in range(TOP_K):
            weight = jnp.where(col == slots[k], gates[k], weight)
        acc = acc + _dot(weight.astype(BF16), buf[slot, first:first + SLOT_CHUNK, :])
    o_ref[...] = acc


def _combine(tile_units, x1, meta, ys):
    t = x1.shape[0]
    grid_spec = pltpu.PrefetchScalarGridSpec(
        num_scalar_prefetch=1,
        grid=(t // TM_OUT,),
        in_specs=[pl.BlockSpec((TM_OUT, D_MODEL), lambda i, us: (i, 0)),
                  pl.BlockSpec((TM_OUT, LANES), lambda i, us: (i, 0)),
                  pl.BlockSpec(memory_space=pl.ANY)],
        out_specs=pl.BlockSpec((TM_OUT, D_MODEL), lambda i, us: (i, 0)),
        scratch_shapes=[pltpu.VMEM((2, TILE_SLOTS, D_MODEL), BF16),
                        pltpu.SemaphoreType.DMA((2,))],
    )
    return pl.pallas_call(
        _combine_kernel,
        out_shape=jax.ShapeDtypeStruct((t, D_MODEL), F32),
        grid_spec=grid_spec,
        compiler_params=pltpu.CompilerParams(dimension_semantics=("arbitrary",),
                                             vmem_limit_bytes=VMEM_LIMIT),
        name="combine",
    )(tile_units, x1, meta, ys)


def _gate_matrix(w_a, w_i):
    g = w_a.shape[0]
    eye = jnp.eye(g, dtype=w_a.dtype)
    both = jnp.stack([w_a, w_i])
    return jnp.einsum('sgij,gh->gishj', both, eye).reshape(g * REC_BLOCK, 2 * g * REC_BLOCK)


def _pad_lanes(v, n=LANES):
    return jnp.pad(v, (0, n - v.shape[0])).reshape(1, n)


def _excl_cumsum(a, axis):
    return jnp.cumsum(a, axis=axis) - a


def _pick(onehot, table):
    if table.ndim == 1:
        return jnp.sum(onehot * table[None, :], axis=1)
    return jnp.sum(onehot[:, :, None] * table[None, :, :], axis=1)


def _dispatch_tables(units, n_blocks):
    n_tiles = units.shape[0]
    i32 = jnp.int32
    tile_first = _excl_cumsum(units, 1)
    tile_end = tile_first + units
    run_first = _excl_cumsum(units, 0)
    run_end = run_first + units
    region_units = jnp.sum(units, axis=0)
    region_blocks = (region_units + BLOCK_UNITS - 1) // BLOCK_UNITS
    block_end = jnp.cumsum(region_blocks)
    region_first = (block_end - region_blocks) * BLOCK_UNITS
    n_used = block_end[-1:]
    experts = jnp.arange(N_EXPERTS, dtype=i32)
    tiles = jnp.arange(n_tiles, dtype=i32)

    blocks = jnp.arange(n_blocks, dtype=i32)
    block_expert = jnp.minimum(
        jnp.sum((block_end[None, :] <= blocks[:, None]).astype(i32), axis=1), N_EXPERTS - 1)

    g = jnp.arange(n_blocks * BLOCK_UNITS, dtype=i32)
    e = jnp.minimum(jnp.sum((block_end[None, :] * BLOCK_UNITS <= g[:, None]).astype(i32), axis=1),
                    N_EXPERTS - 1)
    e_hot = (e[:, None] == experts[None, :]).astype(i32)
    local = g - _pick(e_hot, region_first)
    valid = (local < _pick(e_hot, region_units)) & (g < n_used[0] * BLOCK_UNITS)
    ends = _pick(e_hot, run_end.T)
    tile = jnp.minimum(jnp.sum((ends <= local[:, None]).astype(i32), axis=1), n_tiles - 1)
    t_hot = (tile[:, None] == tiles[None, :]).astype(i32)
    src = (tile * TILE_UNITS + jnp.sum(_pick(t_hot, tile_first - run_first) * e_hot, axis=1) + local)
    unit_src = jnp.where(valid, src, 0).astype(i32)

    j = jnp.arange(TILE_UNITS, dtype=i32)
    ej = jnp.minimum(jnp.sum((tile_end[:, None, :] <= j[None, :, None]).astype(i32), axis=2),
                     N_EXPERTS - 1)
    ej_hot = (ej[:, :, None] == experts[None, None, :]).astype(i32)
    used = j[None, :] < jnp.sum(units, axis=1)[:, None]
    base = region_first[None, :] + run_first - tile_first
    dst = jnp.sum(ej_hot * base[:, None, :], axis=2) + j[None, :]
    tile_units = jnp.where(used, dst, 0).astype(i32).reshape(-1)
    owns = region_blocks > 0
    expert_order = _excl_cumsum(owns.astype(i32), 0)
    later = (experts[None, :] > experts[:, None]) & owns[None, :]
    expert_next = jnp.min(jnp.where(later, experts[None, :], N_EXPERTS), axis=1)
    expert_next = jnp.where(expert_next < N_EXPERTS, expert_next, -1).astype(i32)
    return (block_expert.astype(i32), n_used.astype(i32), unit_src, tile_units,
            expert_order.astype(i32), expert_next)


def _layer(x, ln1_w, w_in, conv_w, conv_b, w_rg_a, b_rg_a, w_rg_i, b_rg_i, rg_lambda, b_forget,
           q_norm_w, k_norm_w, rec_out_norm_w, attn_out_norm_w, w_out, ln2_w, w_router, b_router,
           w_gate_up, b_gate_up, w_down, b_down):
    batch, seq, _ = x.shape
    t = batch * seq
    x2d = x.reshape(t, D_MODEL)

    o_q = 2 * D_REC
    o_k = o_q + D_ATT
    o_v = o_k + D_ATT
    o_f = o_v + D_ATT
    w_t = w_in.T.astype(BF16)
    w_ft = jnp.pad(w_t[o_f:], ((0, LANES - N_HEADS), (0, 0)))
    scale = HEAD_DIM ** -0.5
    u, g, qp, kp, vp = _in_proj(
        x2d, ln1_w.reshape(1, D_MODEL), w_in[:, :o_q].astype(BF16), w_t[o_q:o_k], w_t[o_k:o_v],
        w_t[o_v:o_f], w_ft, _pad_lanes(b_forget).reshape(LANES, 1),
        (q_norm_w * scale).reshape(HEAD_DIM, 1), k_norm_w.reshape(HEAD_DIM, 1), batch, seq)

    rec_vec = jnp.concatenate(
        [conv_w, jnp.stack([conv_b, b_rg_a, b_rg_i, rg_lambda, rec_out_norm_w])])
    rec_n = _rglru(u.reshape(batch, seq, D_REC), g.reshape(batch, seq, D_REC), rec_vec,
                   _gate_matrix(w_rg_a, w_rg_i).astype(BF16))

    att = _attention(qp, kp, vp)

    rw = jnp.pad(w_router.T, ((0, LANES - N_EXPERTS), (0, 0)))
    rw_hi = rw.astype(BF16)
    rw_lo = (rw - rw_hi.astype(F32)).astype(BF16)
    x1, xs, meta, units = _out_proj(
        x2d, rec_n.reshape(t, D_REC), att.reshape(t, D_ATT), attn_out_norm_w.reshape(1, D_ATT),
        w_out[:D_REC].astype(BF16), w_out[D_REC:].astype(BF16), ln2_w.reshape(1, D_MODEL),
        rw_hi, rw_lo, _pad_lanes(b_router).reshape(LANES, 1))

    n_tiles = t // TM_OUT
    max_units = t * TOP_K // UNIT + n_tiles * N_EXPERTS
    n_blocks = -(-max_units // BLOCK_UNITS) + N_EXPERTS
    block_expert, n_used, unit_src, tile_units, expert_order, expert_next = _dispatch_tables(
        units[:, 0, :N_EXPERTS].astype(jnp.int32), n_blocks)

    ys = _experts(block_expert, n_used, unit_src, expert_order, expert_next, xs, w_gate_up, b_gate_up,
                  w_down, b_down)
    out = _combine(tile_units, x1, meta, ys)
    return out.reshape(batch, seq, D_MODEL)


def kernel(x, ln1_w, w_in, conv_w, conv_b, w_rg_a, b_rg_a, w_rg_i, b_rg_i, rg_lambda, b_forget,
           q_norm_w, k_norm_w, rec_out_norm_w, attn_out_norm_w, w_out, ln2_w, w_router, b_router,
           w_gate_up, b_gate_up, w_down, b_down):
    for l in range(ln1_w.shape[0]):
        x = _layer(x, ln1_w[l], w_in[l], conv_w[l], conv_b[l], w_rg_a[l], b_rg_a[l], w_rg_i[l],
                   b_rg_i[l], rg_lambda[l], b_forget[l], q_norm_w[l], k_norm_w[l], rec_out_norm_w[l],
                   attn_out_norm_w[l], w_out[l], ln2_w[l], w_router[l], b_router[l], w_gate_up[l],
                   b_gate_up[l], w_down[l], b_down[l])
    return x
```

```python
import functools

import jax
import jax.numpy as jnp
from jax import lax
from jax.experimental import pallas as pl
from jax.experimental.pallas import tpu as pltpu

D_MODEL = 1024
D_REC = 512
REC_BLOCK = 64
CONV_WIDTH = 4
RGLRU_C = 8.0
D_ATT = 512
HEAD_DIM = 64
N_HEADS = 8
N_EXPERTS = 32
TOP_K = 4
D_EXPERT = 1024
SWIGLU_LIMIT = 7.0
SWIGLU_ALPHA = 1.702
EPS = 1e-6

LANES = 128
SLAB = LANES
VMEM_LIMIT = 56 * 1024 * 1024

F32 = jnp.float32
BF16 = jnp.bfloat16

O_U = 0
O_G = O_U + D_REC
N_PROJ = O_G + D_REC

L_C = HEAD_DIM
L_ONE = HEAD_DIM + 3

TM_IN = 512
TS_REC = 512
TQ = 512
TK = 512
HEADS_PER_STEP = 8
TM_OUT = 512
TM_E = 512

UNIT = 16
TILE_SLOTS = -(-(TM_OUT * TOP_K + N_EXPERTS * (UNIT - 1)) // LANES) * LANES
TILE_UNITS = TILE_SLOTS // UNIT
BLOCK_UNITS = TM_E // UNIT
SLOT_CHUNK = 512


def _split3(v):
    hi = v.astype(BF16).astype(F32)
    r = v - hi
    mid = r.astype(BF16).astype(F32)
    lo = (r - mid).astype(BF16).astype(F32)
    return hi, mid, lo


def _dot(a, b):
    return jnp.dot(a, b, preferred_element_type=F32)


def _sigmoid(x):
    return 0.5 * jnp.tanh(0.5 * x) + 0.5


def _in_proj_kernel(x_ref, ln_ref, w_ref, wqt_ref, wkt_ref, wvt_ref, wft_ref, bf_ref, qw_ref, kw_ref,
                    u_ref, g_ref, q_ref, k_ref, v_ref, carry_ref, *, tiles_per_seq):
    i = pl.program_id(0)
    tm = x_ref.shape[0]

    @pl.when(i % tiles_per_seq == 0)
    def _():
        carry_ref[...] = jnp.zeros_like(carry_ref)

    x = x_ref[...]
    ms = jnp.mean(x * x, axis=-1, keepdims=True)
    h = (x * lax.rsqrt(ms + EPS) * ln_ref[...]).astype(BF16)
    nt = (((1,), (1,)), ((), ()))

    f_t = lax.dot_general(wft_ref[...], h, nt, preferred_element_type=F32) + bf_ref[...]
    logf_t = jnp.minimum(f_t, 0.0) - jnp.log1p(jnp.exp(-jnp.abs(f_t)))
    row = lax.broadcasted_iota(jnp.int32, (tm, tm), 0)
    col = lax.broadcasted_iota(jnp.int32, (tm, tm), 1)
    upto = (row <= col).astype(BF16)
    hi, mid, lo = _split3(logf_t)
    c_t = (_dot(hi.astype(BF16), upto) + _dot(mid.astype(BF16), upto)
           + _dot(lo.astype(BF16), upto) + carry_ref[...])
    carry_ref[...] = c_t[:, tm - 1:tm]

    qt_all = lax.dot_general(wqt_ref[...], h, nt, preferred_element_type=F32)
    kt_all = lax.dot_general(wkt_ref[...], h, nt, preferred_element_type=F32)
    spare = lax.broadcasted_iota(jnp.int32, (SLAB - HEAD_DIM, tm), 0) + HEAD_DIM
    qw = qw_ref[...]
    kw = kw_ref[...]

    def normed(rows, weight):
        inv = lax.rsqrt(jnp.sum(rows * rows, axis=0, keepdims=True) * (1.0 / HEAD_DIM) + EPS)
        return rows * inv * weight

    for hd in range(N_HEADS):
        rows = slice(hd * HEAD_DIM, (hd + 1) * HEAD_DIM)
        c_hi, c_mid, c_lo = _split3(c_t[hd:hd + 1, :])
        q_extra = jnp.where(spare == L_C, c_hi, 0.0)
        q_extra = jnp.where(spare == L_C + 1, c_mid, q_extra)
        q_extra = jnp.where(spare == L_C + 2, c_lo, q_extra)
        q_extra = jnp.where((spare >= L_ONE) & (spare < L_ONE + 3), 1.0, q_extra)
        k_extra = jnp.where(spare < L_C + 3, 1.0, 0.0)
        k_extra = jnp.where(spare == L_ONE, -c_hi, k_extra)
        k_extra = jnp.where(spare == L_ONE + 1, -c_mid, k_extra)
        k_extra = jnp.where(spare == L_ONE + 2, -c_lo, k_extra)
        q_ref[0, hd] = jnp.concatenate([normed(qt_all[rows, :], qw), q_extra], axis=0).T.astype(BF16)
        k_ref[0, hd] = jnp.concatenate([normed(kt_all[rows, :], kw), k_extra], axis=0).T.astype(BF16)

    vt_all = lax.dot_general(wvt_ref[...], h, (((1,), (1,)), ((), ())), preferred_element_type=F32)
    spare = lax.broadcasted_iota(jnp.int32, (SLAB - HEAD_DIM, tm), 0)
    ones_row = jnp.where(spare == 0, 1.0, 0.0)
    for hd in range(N_HEADS):
        v_ref[0, hd] = jnp.concatenate(
            [vt_all[hd * HEAD_DIM:(hd + 1) * HEAD_DIM, :], ones_row], axis=0).astype(BF16)
    u_ref[...] = lax.dot_general(h, w_ref[O_U:O_U + D_REC, :], nt, preferred_element_type=F32)
    g_ref[...] = lax.dot_general(h, w_ref[O_G:O_G + D_REC, :], nt, preferred_element_type=F32)


def _in_proj(x2d, ln1_w, w_t, w_ft, bf_col, qw_col, kw_col, batch, seq):
    t = x2d.shape[0]
    rows = lambda first, n: pl.BlockSpec((n, D_MODEL), lambda i: (first // n, 0))
    o_q = N_PROJ
    o_k = o_q + D_ATT
    o_v = o_k + D_ATT
    tiles_per_seq = seq // TM_IN
    slab_shape = jax.ShapeDtypeStruct((batch, N_HEADS, seq, SLAB), BF16)
    slab_spec = pl.BlockSpec((1, N_HEADS, TM_IN, SLAB),
                             lambda i: (i // tiles_per_seq, 0, i % tiles_per_seq, 0))
    slab_t_shape = jax.ShapeDtypeStruct((batch, N_HEADS, SLAB, seq), BF16)
    slab_t_spec = pl.BlockSpec((1, N_HEADS, SLAB, TM_IN),
                               lambda i: (i // tiles_per_seq, 0, 0, i % tiles_per_seq))
    row_spec = pl.BlockSpec((TM_IN, D_REC), lambda i: (i, 0))
    const = lambda shape: pl.BlockSpec(shape, lambda i: (0,) * len(shape))
    return pl.pallas_call(
        functools.partial(_in_proj_kernel, tiles_per_seq=tiles_per_seq),
        out_shape=(jax.ShapeDtypeStruct((t, D_REC), F32), jax.ShapeDtypeStruct((t, D_REC), F32),
                   slab_shape, slab_shape, slab_t_shape),
        grid=(t // TM_IN,),
        in_specs=[pl.BlockSpec((TM_IN, D_MODEL), lambda i: (i, 0)),
                  const((1, D_MODEL)), rows(0, N_PROJ), rows(o_q, D_ATT),
                  rows(o_k, D_ATT), rows(o_v, D_ATT), const((LANES, D_MODEL)),
                  const((LANES, 1)), const((HEAD_DIM, 1)), const((HEAD_DIM, 1))],
        out_specs=(row_spec, row_spec, slab_spec, slab_spec, slab_t_spec),
        scratch_shapes=[pltpu.VMEM((LANES, 1), F32)],
        compiler_params=pltpu.CompilerParams(dimension_semantics=("arbitrary",),
                                             vmem_limit_bytes=VMEM_LIMIT),
        name="in_proj",
    )(x2d, ln1_w, w_t, w_t, w_t, w_t, w_ft, bf_col, qw_col, kw_col)


R_CONV_B = CONV_WIDTH
R_B_A = CONV_WIDTH + 1
R_B_I = CONV_WIDTH + 2
R_LAMBDA = CONV_WIDTH + 3
R_NORM_W = CONV_WIDTH + 4
N_REC_ROWS = CONV_WIDTH + 5


def _rglru_kernel(u_ref, g_ref, vec_ref, wg_ref, o_ref, ext_ref, a_ref, b_ref, h_ref, hcar_ref):
    s = pl.program_id(1)
    ts = u_ref.shape[1]
    pad = 8
    row = lambda r: vec_ref[r:r + 1, :]

    @pl.when(s == 0)
    def _():
        ext_ref[0:pad, :] = jnp.zeros((pad, D_REC), F32)
        hcar_ref[...] = jnp.zeros_like(hcar_ref)

    @pl.when(s != 0)
    def _():
        ext_ref[0:pad, :] = ext_ref[ts:ts + pad, :]

    u = u_ref[0]
    ext_ref[pad:pad + ts, :] = u
    xc = row(R_CONV_B) + u * row(CONV_WIDTH - 1)
    for tap in range(CONV_WIDTH - 1):
        d = CONV_WIDTH - 1 - tap
        xc = xc + ext_ref[pad - d:pad - d + ts, :] * row(tap)

    gates = _dot(xc.astype(BF16), wg_ref[...])
    r = _sigmoid(gates[:, :D_REC] + row(R_B_A))
    ig = _sigmoid(gates[:, D_REC:] + row(R_B_I))
    nlam = -row(R_LAMBDA)
    softplus = jnp.maximum(nlam, 0.0) + jnp.log1p(jnp.exp(-jnp.abs(nlam)))
    log_a = -RGLRU_C * r * softplus
    a = jnp.exp(log_a)
    a_ref[...] = a
    b_ref[...] = jnp.sqrt(1.0 - jnp.exp(2.0 * log_a)) * (ig * xc)

    def step(t, h):
        h = a_ref[pl.ds(t, 1), :] * h + b_ref[pl.ds(t, 1), :]
        h_ref[pl.ds(t, 1), :] = h
        return h

    hcar_ref[...] = lax.fori_loop(0, ts, step, hcar_ref[...], unroll=8)

    rec = h_ref[...] * jax.nn.gelu(g_ref[0])
    ms = jnp.mean(rec * rec, axis=-1, keepdims=True)
    o_ref[0] = (rec * lax.rsqrt(ms + EPS) * row(R_NORM_W)).astype(BF16)


def _rglru(u, g, rec_vec, w_gates):
    batch, seq, _ = u.shape
    seq_spec = pl.BlockSpec((1, TS_REC, D_REC), lambda b, s: (b, s, 0))
    const = lambda shape: pl.BlockSpec(shape, lambda b, s: (0,) * len(shape))
    return pl.pallas_call(
        _rglru_kernel,
        out_shape=jax.ShapeDtypeStruct((batch, seq, D_REC), BF16),
        grid=(batch, seq // TS_REC),
        in_specs=[seq_spec, seq_spec, const((N_REC_ROWS, D_REC)), const((D_REC, 2 * D_REC))],
        out_specs=seq_spec,
        scratch_shapes=[pltpu.VMEM((TS_REC + 8, D_REC), F32), pltpu.VMEM((TS_REC, D_REC), F32),
                        pltpu.VMEM((TS_REC, D_REC), F32), pltpu.VMEM((TS_REC, D_REC), F32),
                        pltpu.VMEM((1, D_REC), F32)],
        compiler_params=pltpu.CompilerParams(dimension_semantics=("arbitrary", "arbitrary"),
                                             vmem_limit_bytes=VMEM_LIMIT),
        name="rglru",
    )(u, g, rec_vec, w_gates)


def _attn_kernel(q_ref, k_ref, v_ref, o_ref):
    qi = pl.program_id(2)
    hb = q_ref.shape[1]
    tq = q_ref.shape[2]
    qs = [q_ref[0, h] for h in range(hb)]

    def step(j, carry, masked):
        start = pl.multiple_of(j * TK, TK)
        scores = []
        for h in range(hb):
            kj = k_ref[0, h, pl.ds(start, TK), :]
            s = lax.dot_general(kj, qs[h], (((1,), (1,)), ((), ())), preferred_element_type=F32)
            if masked:
                key = lax.broadcasted_iota(jnp.int32, (TK, tq), 0)
                qry = lax.broadcasted_iota(jnp.int32, (TK, tq), 1)
                s = jnp.where(qry >= key, s, -jnp.inf)
            scores.append(s)
        m_news = [jnp.maximum(carry[h][0], jnp.max(scores[h], axis=0, keepdims=True))
                  for h in range(hb)]
        probs = [jnp.exp(scores[h] - m_news[h]).astype(BF16) for h in range(hb)]
        out = []
        for h in range(hb):
            m, acc = carry[h]
            vj = v_ref[0, h, :, pl.ds(start, TK)]
            out.append((m_news[h], jnp.exp(m - m_news[h]) * acc + _dot(vj, probs[h])))
        return tuple(out)

    init = tuple((jnp.full((1, tq), -jnp.inf, F32), jnp.zeros((SLAB, tq), F32)) for _ in range(hb))
    carry = lax.fori_loop(0, qi, lambda j, c: step(j, c, False), init)
    carry = step(qi, carry, True)
    outs = [carry[h][1][:HEAD_DIM, :] / carry[h][1][HEAD_DIM:HEAD_DIM + 1, :] for h in range(hb)]
    for pair in range(hb // 2):
        both = jnp.concatenate([outs[2 * pair], outs[2 * pair + 1]], axis=0)
        o_ref[0, :, pair * LANES:(pair + 1) * LANES] = both.T


def _attention(qp, kp, vp):
    batch, _, seq, _ = kp.shape
    hb = HEADS_PER_STEP
    kv_spec = pl.BlockSpec((1, hb, seq, SLAB), lambda b, h, i: (b, h, 0, 0))
    vt_spec = pl.BlockSpec((1, hb, SLAB, seq), lambda b, h, i: (b, h, 0, 0))
    return pl.pallas_call(
        _attn_kernel,
        out_shape=jax.ShapeDtypeStruct((batch, seq, D_ATT), F32),
        grid=(batch, N_HEADS // hb, seq // TQ),
        in_specs=[pl.BlockSpec((1, hb, TQ, SLAB), lambda b, h, i: (b, h, i, 0)), kv_spec, vt_spec],
        out_specs=pl.BlockSpec((1, TQ, hb * HEAD_DIM), lambda b, h, i: (b, i, h)),
        compiler_params=pltpu.CompilerParams(
            dimension_semantics=("arbitrary", "arbitrary", "arbitrary"),
            vmem_limit_bytes=VMEM_LIMIT),
        name="attn",
    )(qp, kp, vp)


def _out_proj_kernel(x_ref, rec_ref, att_ref, aw_ref, wr_ref, wa_ref, ln_ref, rwh_ref, rwl_ref, rb_ref,
                     x1_ref, xs_ref, meta_ref, units_ref):
    tm = x_ref.shape[0]

    att = att_ref[...]
    ms = jnp.sum(att * att, axis=-1, keepdims=True) * (1.0 / D_ATT)
    att_n = (att * lax.rsqrt(ms + EPS) * aw_ref[...]).astype(BF16)
    x1 = x_ref[...] + _dot(rec_ref[...], wr_ref[...]) + _dot(att_n, wa_ref[...])
    x1_ref[...] = x1

    ms2 = jnp.mean(x1 * x1, axis=-1, keepdims=True)
    h2 = x1 * lax.rsqrt(ms2 + EPS) * ln_ref[...]

    h_hi = h2.astype(BF16)
    h_lo = (h2 - h_hi.astype(F32)).astype(BF16)
    nt = (((1,), (1,)), ((), ()))
    logits = (lax.dot_general(rwh_ref[...], h_hi, nt, preferred_element_type=F32)
              + lax.dot_general(rwh_ref[...], h_lo, nt, preferred_element_type=F32)
              + lax.dot_general(rwl_ref[...], h_hi, nt, preferred_element_type=F32) + rb_ref[...])
    expert = lax.broadcasted_iota(jnp.int32, (LANES, tm), 0)
    expert_f = expert.astype(F32)
    l = jnp.where(expert < N_EXPERTS, logits, -jnp.inf)

    vals, idxs, sels = [], [], []
    for _ in range(TOP_K):
        m = jnp.max(l, axis=0, keepdims=True)
        idx = jnp.min(jnp.where(l == m, expert_f, float(LANES)), axis=0, keepdims=True)
        sel = expert_f == idx
        vals.append(m)
        idxs.append(idx)
        sels.append(sel)
        l = jnp.where(sel, -jnp.inf, l)
    exps = [jnp.exp(v - vals[0]) for v in vals]
    denom = exps[0] + exps[1] + exps[2] + exps[3]

    onehot = jnp.zeros((LANES, tm), F32)
    for sel in sels:
        onehot = jnp.where(sel, 1.0, onehot)
    row = lax.broadcasted_iota(jnp.int32, (tm, tm), 0)
    col = lax.broadcasted_iota(jnp.int32, (tm, tm), 1)
    earlier_token = (row < col).astype(BF16)
    rank = _dot(onehot.astype(BF16), earlier_token)
    count = jnp.sum(onehot, axis=1, keepdims=True)
    units = jnp.floor((count + (UNIT - 1)) * (1.0 / UNIT))
    e_row = lax.broadcasted_iota(jnp.int32, (LANES, LANES), 0)
    e_col = lax.broadcasted_iota(jnp.int32, (LANES, LANES), 1)
    earlier_expert = (e_col < e_row).astype(BF16)
    units_wide = jnp.broadcast_to(units, (LANES, LANES))
    first_unit = _dot(earlier_expert, units_wide.astype(BF16))[:, 0:1]
    units_ref[0] = units_wide.T[0:8, :]
    place = first_unit * float(UNIT) + rank

    meta_t = jnp.zeros((LANES, tm), F32)
    for k in range(TOP_K):
        slot = jnp.sum(jnp.where(sels[k], place, 0.0), axis=0, keepdims=True)
        meta_t = jnp.where(expert == k, idxs[k], meta_t)
        meta_t = jnp.where(expert == TOP_K + k, exps[k] / denom, meta_t)
        meta_t = jnp.where(expert == 2 * TOP_K + k, slot, meta_t)
    meta_ref[...] = meta_t.T

    slot_rows = [meta_t[2 * TOP_K + k:2 * TOP_K + k + 1, :].astype(jnp.int32) for k in range(TOP_K)]
    for c in range(TILE_SLOTS // SLOT_CHUNK):
        slot_id = lax.broadcasted_iota(jnp.int32, (SLOT_CHUNK, tm), 0) + c * SLOT_CHUNK
        hit = jnp.zeros((SLOT_CHUNK, tm), F32)
        for slot_row in slot_rows:
            hit = jnp.where(slot_id == slot_row, 1.0, hit)
        xs_ref[c * SLOT_CHUNK:(c + 1) * SLOT_CHUNK, :] = _dot(hit.astype(BF16), h_hi).astype(BF16)


def _out_proj(x2d, rec_n, att, aw_slab, w_rec, w_att, ln2_w, rw_hi, rw_lo, rb_pad):
    t = x2d.shape[0]
    n_tiles = t // TM_OUT
    rows = lambda n: pl.BlockSpec((TM_OUT, n), lambda i: (i, 0))
    const = lambda shape: pl.BlockSpec(shape, lambda i: (0,) * len(shape))
    return pl.pallas_call(
        _out_proj_kernel,
        out_shape=(jax.ShapeDtypeStruct((t, D_MODEL), F32),
                   jax.ShapeDtypeStruct((n_tiles * TILE_SLOTS, D_MODEL), BF16),
                   jax.ShapeDtypeStruct((t, LANES), F32),
                   jax.ShapeDtypeStruct((n_tiles, 8, LANES), F32)),
        grid=(n_tiles,),
        in_specs=[rows(D_MODEL), rows(D_REC), rows(D_ATT), const((1, D_ATT)),
                  const((D_REC, D_MODEL)), const((D_ATT, D_MODEL)), const((1, D_MODEL)),
                  const((LANES, D_MODEL)), const((LANES, D_MODEL)), const((LANES, 1))],
        out_specs=(rows(D_MODEL), pl.BlockSpec((TILE_SLOTS, D_MODEL), lambda i: (i, 0)), rows(LANES),
                   pl.BlockSpec((1, 8, LANES), lambda i: (i, 0, 0))),
        compiler_params=pltpu.CompilerParams(dimension_semantics=("arbitrary",),
                                             vmem_limit_bytes=VMEM_LIMIT),
        name="out_proj",
    )(x2d, rec_n, att, aw_slab, w_rec, w_att, ln2_w, rw_hi, rw_lo, rb_pad)


def _start_units(unit_ref, base, n_units, src_hbm, dst, sem):
    def body(j, carry):
        src_row = pl.multiple_of(unit_ref[base + j] * UNIT, UNIT)
        dst_row = pl.multiple_of(j * UNIT, UNIT)
        pltpu.make_async_copy(src_hbm.at[pl.ds(src_row, UNIT), :],
                              dst.at[pl.ds(dst_row, UNIT), :], sem).start()
        return carry
    lax.fori_loop(0, n_units, body, 0, unroll=8)


def _wait_units(n_units, src_hbm, dst, sem):
    pltpu.make_async_copy(src_hbm.at[pl.ds(0, n_units * UNIT), :], dst, sem).wait()


def _experts_kernel(be_ref, nblk_ref, unit_ref, order_ref, next_ref, xs_hbm, wgu_hbm, bgu_ref, wd_hbm,
                    bd_ref, ys_ref, xbuf, wgu_f32, wd_f32, wgu_bf, wd_bf, sem, wsem):
    b = pl.program_id(0)
    nblk = nblk_ref[0]
    slot = b % 2
    e = be_ref[b]

    def weight_copies(expert, wslot):
        return (pltpu.make_async_copy(wgu_hbm.at[expert], wgu_f32.at[wslot], wsem.at[wslot]),
                pltpu.make_async_copy(wd_hbm.at[expert], wd_f32.at[wslot], wsem.at[wslot]))

    @pl.when(b == 0)
    def _():
        _start_units(unit_ref, 0, BLOCK_UNITS, xs_hbm, xbuf.at[0], sem.at[0])
        for copy in weight_copies(e, 0):
            copy.start()

    @pl.when(b + 1 < nblk)
    def _():
        _start_units(unit_ref, (b + 1) * BLOCK_UNITS, BLOCK_UNITS, xs_hbm, xbuf.at[1 - slot],
                     sem.at[1 - slot])

    new_expert = jnp.logical_or(b == 0, e != be_ref[jnp.maximum(b - 1, 0)])

    @pl.when(jnp.logical_and(b < nblk, new_expert))
    def _():
        wslot = order_ref[e] % 2
        for copy in weight_copies(e, wslot):
            copy.wait()
        wgu_bf[...] = wgu_f32[wslot].astype(BF16)
        wd_bf[...] = wd_f32[wslot].astype(BF16)
        nxt = next_ref[e]

        @pl.when(nxt >= 0)
        def _():
            for copy in weight_copies(nxt, 1 - wslot):
                copy.start()

    @pl.when(b < nblk)
    def _():
        _wait_units(BLOCK_UNITS, xs_hbm, xbuf.at[slot], sem.at[slot])
        gu = _dot(xbuf[slot], wgu_bf[...]) + bgu_ref[0]
        g = jnp.minimum(gu[:, :D_EXPERT], SWIGLU_LIMIT)
        u = jnp.clip(gu[:, D_EXPERT:], -SWIGLU_LIMIT, SWIGLU_LIMIT)
        act = (u + 1.0) * (g * _sigmoid(SWIGLU_ALPHA * g))
        ys_ref[...] = (_dot(act.astype(BF16), wd_bf[...]) + bd_ref[0]).astype(BF16)

    @pl.when(b >= nblk)
    def _():
        ys_ref[...] = jnp.zeros_like(ys_ref)


def _experts(block_expert, n_blocks_used, unit_src, expert_order, expert_next, xs, w_gate_up,
             b_gate_up, w_down, b_down):
    n_blocks = block_expert.shape[0]
    bias_map = lambda b, be, nb, us, eo, en: (be[b], 0, 0)
    grid_spec = pltpu.PrefetchScalarGridSpec(
        num_scalar_prefetch=5,
        grid=(n_blocks,),
        in_specs=[pl.BlockSpec(memory_space=pl.ANY),
                  pl.BlockSpec(memory_space=pl.ANY),
                  pl.BlockSpec((1, 1, 2 * D_EXPERT), bias_map),
                  pl.BlockSpec(memory_space=pl.ANY),
                  pl.BlockSpec((1, 1, D_MODEL), bias_map)],
        out_specs=pl.BlockSpec((TM_E, D_MODEL), lambda b, be, nb, us, eo, en: (b, 0)),
        scratch_shapes=[pltpu.VMEM((2, TM_E, D_MODEL), BF16),
                        pltpu.VMEM((2, D_MODEL, 2 * D_EXPERT), F32),
                        pltpu.VMEM((2, D_EXPERT, D_MODEL), F32),
                        pltpu.VMEM((D_MODEL, 2 * D_EXPERT), BF16),
                        pltpu.VMEM((D_EXPERT, D_MODEL), BF16),
                        pltpu.SemaphoreType.DMA((2,)),
                        pltpu.SemaphoreType.DMA((2,))],
    )
    return pl.pallas_call(
        _experts_kernel,
        out_shape=jax.ShapeDtypeStruct((n_blocks * TM_E, D_MODEL), BF16),
        grid_spec=grid_spec,
        compiler_params=pltpu.CompilerParams(dimension_semantics=("arbitrary",),
                                             vmem_limit_bytes=VMEM_LIMIT),
        name="experts",
    )(block_expert, n_blocks_used, unit_src, expert_order, expert_next, xs, w_gate_up,
      b_gate_up.reshape(N_EXPERTS, 1, 2 * D_EXPERT), w_down, b_down.reshape(N_EXPERTS, 1, D_MODEL))


def _combine_kernel(unit_ref, x1_ref, meta_ref, ys_hbm, o_ref, buf, sem):
    i = pl.program_id(0)
    n = pl.num_programs(0)
    slot = i % 2
    tm = x1_ref.shape[0]

    @pl.when(i == 0)
    def _():
        _start_units(unit_ref, 0, TILE_UNITS, ys_hbm, buf.at[0], sem.at[0])

    @pl.when(i + 1 < n)
    def _():
        _start_units(unit_ref, (i + 1) * TILE_UNITS, TILE_UNITS, ys_hbm, buf.at[1 - slot],
                     sem.at[1 - slot])

    meta = meta_ref[...]
    slots = [meta[:, 2 * TOP_K + k:2 * TOP_K + k + 1].astype(jnp.int32) for k in range(TOP_K)]
    gates = [meta[:, TOP_K + k:TOP_K + k + 1] for k in range(TOP_K)]
    _wait_units(TILE_UNITS, ys_hbm, buf.at[slot], sem.at[slot])

    acc = x1_ref[...]
    for c in range(TILE_SLOTS // SLOT_CHUNK):
        first = c * SLOT_CHUNK
        col = lax.broadcasted_iota(jnp.int32, (tm, SLOT_CHUNK), 1) + first
        weight = jnp.zeros((tm, SLOT_CHUNK), F32)
        for k in range(TOP_K):
            weight = jnp.where(col == slots[k], gates[k], weight)
        acc = acc + _dot(weight.astype(BF16), buf[slot, first:first + SLOT_CHUNK, :])
    o_ref[...] = acc


def _combine(tile_units, x1, meta, ys):
    t = x1.shape[0]
    grid_spec = pltpu.PrefetchScalarGridSpec(
        num_scalar_prefetch=1,
        grid=(t // TM_OUT,),
        in_specs=[pl.BlockSpec((TM_OUT, D_MODEL), lambda i, us: (i, 0)),
                  pl.BlockSpec((TM_OUT, LANES), lambda i, us: (i, 0)),
                  pl.BlockSpec(memory_space=pl.ANY)],
        out_specs=pl.BlockSpec((TM_OUT, D_MODEL), lambda i, us: (i, 0)),
        scratch_shapes=[pltpu.VMEM((2, TILE_SLOTS, D_MODEL), BF16),
                        pltpu.SemaphoreType.DMA((2,))],
    )
    return pl.pallas_call(
        _combine_kernel,
        out_shape=jax.ShapeDtypeStruct((t, D_MODEL), F32),
        grid_spec=grid_spec,
        compiler_params=pltpu.CompilerParams(dimension_semantics=("arbitrary",),
                                             vmem_limit_bytes=VMEM_LIMIT),
        name="combine",
    )(tile_units, x1, meta, ys)


def _gate_matrix(w_a, w_i):
    g = w_a.shape[0]
    eye = jnp.eye(g, dtype=w_a.dtype)
    both = jnp.stack([w_a, w_i])
    return jnp.einsum('sgij,gh->gishj', both, eye).reshape(g * REC_BLOCK, 2 * g * REC_BLOCK)


def _pad_lanes(v, n=LANES):
    return jnp.pad(v, (0, n - v.shape[0])).reshape(1, n)


def _excl_cumsum(a, axis):
    return jnp.cumsum(a, axis=axis) - a


def _pick(onehot, table):
    if table.ndim == 1:
        return jnp.sum(onehot * table[None, :], axis=1)
    return jnp.sum(onehot[:, :, None] * table[None, :, :], axis=1)


def _dispatch_tables(units, n_blocks):
    n_tiles = units.shape[0]
    i32 = jnp.int32
    tile_first = _excl_cumsum(units, 1)
    tile_end = tile_first + units
    run_first = _excl_cumsum(units, 0)
    run_end = run_first + units
    region_units = jnp.sum(units, axis=0)
    region_blocks = (region_units + BLOCK_UNITS - 1) // BLOCK_UNITS
    block_end = jnp.cumsum(region_blocks)
    region_first = (block_end - region_blocks) * BLOCK_UNITS
    n_used = block_end[-1:]
    experts = jnp.arange(N_EXPERTS, dtype=i32)
    tiles = jnp.arange(n_tiles, dtype=i32)

    blocks = jnp.arange(n_blocks, dtype=i32)
    block_expert = jnp.minimum(
        jnp.sum((block_end[None, :] <= blocks[:, None]).astype(i32), axis=1), N_EXPERTS - 1)

    g = jnp.arange(n_blocks * BLOCK_UNITS, dtype=i32)
    e = jnp.minimum(jnp.sum((block_end[None, :] * BLOCK_UNITS <= g[:, None]).astype(i32), axis=1),
                    N_EXPERTS - 1)
    e_hot = (e[:, None] == experts[None, :]).astype(i32)
    local = g - _pick(e_hot, region_first)
    valid = (local < _pick(e_hot, region_units)) & (g < n_used[0] * BLOCK_UNITS)
    ends = _pick(e_hot, run_end.T)
    tile = jnp.minimum(jnp.sum((ends <= local[:, None]).astype(i32), axis=1), n_tiles - 1)
    t_hot = (tile[:, None] == tiles[None, :]).astype(i32)
    src = (tile * TILE_UNITS + jnp.sum(_pick(t_hot, tile_first - run_first) * e_hot, axis=1) + local)
    unit_src = jnp.where(valid, src, 0).astype(i32)

    j = jnp.arange(TILE_UNITS, dtype=i32)
    ej = jnp.minimum(jnp.sum((tile_end[:, None, :] <= j[None, :, None]).astype(i32), axis=2),
                     N_EXPERTS - 1)
    ej_hot = (ej[:, :, None] == experts[None, None, :]).astype(i32)
    used = j[None, :] < jnp.sum(units, axis=1)[:, None]
    base = region_first[None, :] + run_first - tile_first
    dst = jnp.sum(ej_hot * base[:, None, :], axis=2) + j[None, :]
    tile_units = jnp.where(used, dst, 0).astype(i32).reshape(-1)
    owns = region_blocks > 0
    expert_order = _excl_cumsum(owns.astype(i32), 0)
    later = (experts[None, :] > experts[:, None]) & owns[None, :]
    expert_next = jnp.min(jnp.where(later, experts[None, :], N_EXPERTS), axis=1)
    expert_next = jnp.where(expert_next < N_EXPERTS, expert_next, -1).astype(i32)
    return (block_expert.astype(i32), n_used.astype(i32), unit_src, tile_units,
            expert_order.astype(i32), expert_next)


def _layer(x, ln1_w, w_in, conv_w, conv_b, w_rg_a, b_rg_a, w_rg_i, b_rg_i, rg_lambda, b_forget,
           q_norm_w, k_norm_w, rec_out_norm_w, attn_out_norm_w, w_out, ln2_w, w_router, b_router,
           w_gate_up, b_gate_up, w_down, b_down):
    batch, seq, _ = x.shape
    t = batch * seq
    x2d = x.reshape(t, D_MODEL)

    o_f = N_PROJ + 3 * D_ATT
    w_t = w_in.T.astype(BF16)
    w_ft = jnp.pad(w_t[o_f:], ((0, LANES - N_HEADS), (0, 0)))
    scale = HEAD_DIM ** -0.5
    u, g, qp, kp, vp = _in_proj(
        x2d, ln1_w.reshape(1, D_MODEL), w_t, w_ft, _pad_lanes(b_forget).reshape(LANES, 1),
        (q_norm_w * scale).reshape(HEAD_DIM, 1), k_norm_w.reshape(HEAD_DIM, 1), batch, seq)

    rec_vec = jnp.concatenate(
        [conv_w, jnp.stack([conv_b, b_rg_a, b_rg_i, rg_lambda, rec_out_norm_w])])
    rec_n = _rglru(u.reshape(batch, seq, D_REC), g.reshape(batch, seq, D_REC), rec_vec,
                   _gate_matrix(w_rg_a, w_rg_i).astype(BF16))

    att = _attention(qp, kp, vp)

    rw = jnp.pad(w_router.T, ((0, LANES - N_EXPERTS), (0, 0)))
    rw_hi = rw.astype(BF16)
    rw_lo = (rw - rw_hi.astype(F32)).astype(BF16)
    x1, xs, meta, units = _out_proj(
        x2d, rec_n.reshape(t, D_REC), att.reshape(t, D_ATT), attn_out_norm_w.reshape(1, D_ATT),
        w_out[:D_REC].astype(BF16), w_out[D_REC:].astype(BF16), ln2_w.reshape(1, D_MODEL),
        rw_hi, rw_lo, _pad_lanes(b_router).reshape(LANES, 1))

    n_tiles = t // TM_OUT
    max_units = t * TOP_K // UNIT + n_tiles * N_EXPERTS
    n_blocks = -(-max_units // BLOCK_UNITS) + N_EXPERTS
    block_expert, n_used, unit_src, tile_units, expert_order, expert_next = _dispatch_tables(
        units[:, 0, :N_EXPERTS].astype(jnp.int32), n_blocks)

    ys = _experts(block_expert, n_used, unit_src, expert_order, expert_next, xs, w_gate_up, b_gate_up,
                  w_down, b_down)
    out = _combine(tile_units, x1, meta, ys)
    return out.reshape(batch, seq, D_MODEL)


def kernel(x, ln1_w, w_in, conv_w, conv_b, w_rg_a, b_rg_a, w_rg_i, b_rg_i, rg_lambda, b_forget,
           q_norm_w, k_norm_w, rec_out_norm_w, attn_out_norm_w, w_out, ln2_w, w_router, b_router,
           w_gate_up, b_gate_up, w_down, b_down):
    for l in range(ln1_w.shape[0]):
        x = _layer(x, ln1_w[l], w_in[l], conv_w[l], conv_b[l], w_rg_a[l], b_rg_a[l], w_rg_i[l],
                   b_rg_i[l], rg_lambda[l], b_forget[l], q_norm_w[l], k_norm_w[l], rec_out_norm_w[l],
                   attn_out_norm_w[l], w_out[l], ln2_w[l], w_router[l], b_router[l], w_gate_up[l],
                   b_gate_up[l], w_down[l], b_down[l])
    return x
```

```python
import functools

import jax
import jax.numpy as jnp
from jax import lax
from jax.experimental import pallas as pl
from jax.experimental.pallas import tpu as pltpu

D_MODEL = 1024
D_REC = 512
REC_BLOCK = 64
CONV_WIDTH = 4
RGLRU_C = 8.0
D_ATT = 512
HEAD_DIM = 64
N_HEADS = 8
N_EXPERTS = 32
TOP_K = 4
D_EXPERT = 1024
SWIGLU_LIMIT = 7.0
SWIGLU_ALPHA = 1.702
EPS = 1e-6

LANES = 128
SLAB = LANES
VMEM_LIMIT = 56 * 1024 * 1024

F32 = jnp.float32
BF16 = jnp.bfloat16

O_U = 0
O_G = O_U + D_REC
N_PROJ = O_G + D_REC

L_C = HEAD_DIM
L_ONE = HEAD_DIM + 3

TM_IN = 512
TS_REC = 512
SCAN_WAYS = 8
TQ = 512
TK = 512
HEADS_PER_STEP = 8
TM_OUT = 512
TM_E = 512

UNIT = 16
TILE_SLOTS = -(-(TM_OUT * TOP_K + N_EXPERTS * (UNIT - 1)) // LANES) * LANES
TILE_UNITS = TILE_SLOTS // UNIT
BLOCK_UNITS = TM_E // UNIT
SLOT_CHUNK = 512


def _split3(v):
    hi = v.astype(BF16).astype(F32)
    r = v - hi
    mid = r.astype(BF16).astype(F32)
    lo = (r - mid).astype(BF16).astype(F32)
    return hi, mid, lo


def _dot(a, b):
    return jnp.dot(a, b, preferred_element_type=F32)


def _sigmoid(x):
    return 0.5 * jnp.tanh(0.5 * x) + 0.5


def _in_proj_kernel(x_ref, ln_ref, w_ref, wqt_ref, wkt_ref, wvt_ref, wft_ref, bf_ref, qw_ref, kw_ref,
                    u_ref, g_ref, q_ref, k_ref, v_ref, carry_ref, *, tiles_per_seq):
    i = pl.program_id(0)
    tm = x_ref.shape[0]

    @pl.when(i % tiles_per_seq == 0)
    def _():
        carry_ref[...] = jnp.zeros_like(carry_ref)

    x = x_ref[...]
    ms = jnp.mean(x * x, axis=-1, keepdims=True)
    h = (x * lax.rsqrt(ms + EPS) * ln_ref[...]).astype(BF16)
    nt = (((1,), (1,)), ((), ()))

    f_t = lax.dot_general(wft_ref[...], h, nt, preferred_element_type=F32) + bf_ref[...]
    logf_t = jnp.minimum(f_t, 0.0) - jnp.log1p(jnp.exp(-jnp.abs(f_t)))
    row = lax.broadcasted_iota(jnp.int32, (tm, tm), 0)
    col = lax.broadcasted_iota(jnp.int32, (tm, tm), 1)
    upto = (row <= col).astype(BF16)
    hi, mid, lo = _split3(logf_t)
    c_t = (_dot(hi.astype(BF16), upto) + _dot(mid.astype(BF16), upto)
           + _dot(lo.astype(BF16), upto) + carry_ref[...])
    carry_ref[...] = c_t[:, tm - 1:tm]

    qt_all = lax.dot_general(wqt_ref[...], h, nt, preferred_element_type=F32)
    kt_all = lax.dot_general(wkt_ref[...], h, nt, preferred_element_type=F32)
    spare = lax.broadcasted_iota(jnp.int32, (SLAB - HEAD_DIM, tm), 0) + HEAD_DIM
    qw = qw_ref[...]
    kw = kw_ref[...]

    def normed(rows, weight):
        inv = lax.rsqrt(jnp.sum(rows * rows, axis=0, keepdims=True) * (1.0 / HEAD_DIM) + EPS)
        return rows * inv * weight

    for hd in range(N_HEADS):
        rows = slice(hd * HEAD_DIM, (hd + 1) * HEAD_DIM)
        c_hi, c_mid, c_lo = _split3(c_t[hd:hd + 1, :])
        q_extra = jnp.where(spare == L_C, c_hi, 0.0)
        q_extra = jnp.where(spare == L_C + 1, c_mid, q_extra)
        q_extra = jnp.where(spare == L_C + 2, c_lo, q_extra)
        q_extra = jnp.where((spare >= L_ONE) & (spare < L_ONE + 3), 1.0, q_extra)
        k_extra = jnp.where(spare < L_C + 3, 1.0, 0.0)
        k_extra = jnp.where(spare == L_ONE, -c_hi, k_extra)
        k_extra = jnp.where(spare == L_ONE + 1, -c_mid, k_extra)
        k_extra = jnp.where(spare == L_ONE + 2, -c_lo, k_extra)
        q_ref[0, hd] = jnp.concatenate([normed(qt_all[rows, :], qw), q_extra], axis=0).T.astype(BF16)
        k_ref[0, hd] = jnp.concatenate([normed(kt_all[rows, :], kw), k_extra], axis=0).T.astype(BF16)

    vt_all = lax.dot_general(wvt_ref[...], h, (((1,), (1,)), ((), ())), preferred_element_type=F32)
    spare = lax.broadcasted_iota(jnp.int32, (SLAB - HEAD_DIM, tm), 0)
    ones_row = jnp.where(spare == 0, 1.0, 0.0)
    for hd in range(N_HEADS):
        v_ref[0, hd] = jnp.concatenate(
            [vt_all[hd * HEAD_DIM:(hd + 1) * HEAD_DIM, :], ones_row], axis=0).astype(BF16)
    u_ref[...] = lax.dot_general(h, w_ref[O_U:O_U + D_REC, :], nt, preferred_element_type=F32)
    g_ref[...] = lax.dot_general(h, w_ref[O_G:O_G + D_REC, :], nt, preferred_element_type=F32)


def _in_proj(x2d, ln1_w, w_t, w_ft, bf_col, qw_col, kw_col, batch, seq):
    t = x2d.shape[0]
    rows = lambda first, n: pl.BlockSpec((n, D_MODEL), lambda i: (first // n, 0))
    o_q = N_PROJ
    o_k = o_q + D_ATT
    o_v = o_k + D_ATT
    tiles_per_seq = seq // TM_IN
    slab_shape = jax.ShapeDtypeStruct((batch, N_HEADS, seq, SLAB), BF16)
    slab_spec = pl.BlockSpec((1, N_HEADS, TM_IN, SLAB),
                             lambda i: (i // tiles_per_seq, 0, i % tiles_per_seq, 0))
    slab_t_shape = jax.ShapeDtypeStruct((batch, N_HEADS, SLAB, seq), BF16)
    slab_t_spec = pl.BlockSpec((1, N_HEADS, SLAB, TM_IN),
                               lambda i: (i // tiles_per_seq, 0, 0, i % tiles_per_seq))
    row_spec = pl.BlockSpec((TM_IN, D_REC), lambda i: (i, 0))
    const = lambda shape: pl.BlockSpec(shape, lambda i: (0,) * len(shape))
    return pl.pallas_call(
        functools.partial(_in_proj_kernel, tiles_per_seq=tiles_per_seq),
        out_shape=(jax.ShapeDtypeStruct((t, D_REC), F32), jax.ShapeDtypeStruct((t, D_REC), F32),
                   slab_shape, slab_shape, slab_t_shape),
        grid=(t // TM_IN,),
        in_specs=[pl.BlockSpec((TM_IN, D_MODEL), lambda i: (i, 0)),
                  const((1, D_MODEL)), rows(0, N_PROJ), rows(o_q, D_ATT),
                  rows(o_k, D_ATT), rows(o_v, D_ATT), const((LANES, D_MODEL)),
                  const((LANES, 1)), const((HEAD_DIM, 1)), const((HEAD_DIM, 1))],
        out_specs=(row_spec, row_spec, slab_spec, slab_spec, slab_t_spec),
        scratch_shapes=[pltpu.VMEM((LANES, 1), F32)],
        compiler_params=pltpu.CompilerParams(dimension_semantics=("arbitrary",),
                                             vmem_limit_bytes=VMEM_LIMIT),
        name="in_proj",
    )(x2d, ln1_w, w_t, w_t, w_t, w_t, w_ft, bf_col, qw_col, kw_col)


R_CONV_B = CONV_WIDTH
R_B_A = CONV_WIDTH + 1
R_B_I = CONV_WIDTH + 2
R_LAMBDA = CONV_WIDTH + 3
R_NORM_W = CONV_WIDTH + 4
N_REC_ROWS = CONV_WIDTH + 5


def _rglru_kernel(u_ref, g_ref, vec_ref, wg_ref, o_ref, ext_ref, a_ref, b_ref, h_ref, hcar_ref):
    s = pl.program_id(1)
    ts = u_ref.shape[1]
    pad = 8
    row = lambda r: vec_ref[r:r + 1, :]

    @pl.when(s == 0)
    def _():
        ext_ref[0:pad, :] = jnp.zeros((pad, D_REC), F32)
        hcar_ref[...] = jnp.zeros_like(hcar_ref)

    @pl.when(s != 0)
    def _():
        ext_ref[0:pad, :] = ext_ref[ts:ts + pad, :]

    u = u_ref[0]
    ext_ref[pad:pad + ts, :] = u
    xc = row(R_CONV_B) + u * row(CONV_WIDTH - 1)
    for tap in range(CONV_WIDTH - 1):
        d = CONV_WIDTH - 1 - tap
        xc = xc + ext_ref[pad - d:pad - d + ts, :] * row(tap)

    gates = _dot(xc.astype(BF16), wg_ref[...])
    r = _sigmoid(gates[:, :D_REC] + row(R_B_A))
    ig = _sigmoid(gates[:, D_REC:] + row(R_B_I))
    nlam = -row(R_LAMBDA)
    softplus = jnp.maximum(nlam, 0.0) + jnp.log1p(jnp.exp(-jnp.abs(nlam)))
    log_a = -RGLRU_C * r * softplus
    a = jnp.exp(log_a)
    b = jnp.sqrt(1.0 - jnp.exp(2.0 * log_a)) * (ig * xc)
    sub = ts // SCAN_WAYS
    side_by_side = lambda v: jnp.swapaxes(v.reshape(SCAN_WAYS, sub, D_REC), 0, 1)
    a_ref[...] = side_by_side(a)
    b_ref[...] = side_by_side(b)
    h = jnp.zeros((SCAN_WAYS, D_REC), F32)
    p = jnp.ones((SCAN_WAYS, D_REC), F32)
    for r in range(sub):
        a_r = a_ref[r]
        h = a_r * h + b_ref[r]
        p = a_r * p
        h_ref[r] = h
        a_ref[r] = p
    entering = hcar_ref[...]
    enter = []
    for w in range(SCAN_WAYS):
        enter.append(entering)
        entering = h[w:w + 1, :] + p[w:w + 1, :] * entering
    hcar_ref[...] = entering
    h_all = h_ref[...] + a_ref[...] * jnp.concatenate(enter, axis=0)[None]
    h_all = jnp.swapaxes(h_all, 0, 1).reshape(ts, D_REC)

    rec = h_all * jax.nn.gelu(g_ref[0])
    ms = jnp.mean(rec * rec, axis=-1, keepdims=True)
    o_ref[0] = (rec * lax.rsqrt(ms + EPS) * row(R_NORM_W)).astype(BF16)


def _rglru(u, g, rec_vec, w_gates):
    batch, seq, _ = u.shape
    seq_spec = pl.BlockSpec((1, TS_REC, D_REC), lambda b, s: (b, s, 0))
    const = lambda shape: pl.BlockSpec(shape, lambda b, s: (0,) * len(shape))
    return pl.pallas_call(
        _rglru_kernel,
        out_shape=jax.ShapeDtypeStruct((batch, seq, D_REC), BF16),
        grid=(batch, seq // TS_REC),
        in_specs=[seq_spec, seq_spec, const((N_REC_ROWS, D_REC)), const((D_REC, 2 * D_REC))],
        out_specs=seq_spec,
        scratch_shapes=[pltpu.VMEM((TS_REC + 8, D_REC), F32)]
        + [pltpu.VMEM((TS_REC // SCAN_WAYS, SCAN_WAYS, D_REC), F32)] * 3
        + [pltpu.VMEM((1, D_REC), F32)],
        compiler_params=pltpu.CompilerParams(dimension_semantics=("arbitrary", "arbitrary"),
                                             vmem_limit_bytes=VMEM_LIMIT),
        name="rglru",
    )(u, g, rec_vec, w_gates)


def _attn_kernel(q_ref, k_ref, v_ref, o_ref):
    qi = pl.program_id(2)
    hb = q_ref.shape[1]
    tq = q_ref.shape[2]
    qs = [q_ref[0, h] for h in range(hb)]

    def step(j, carry, masked):
        start = pl.multiple_of(j * TK, TK)
        scores = []
        for h in range(hb):
            kj = k_ref[0, h, pl.ds(start, TK), :]
            s = lax.dot_general(kj, qs[h], (((1,), (1,)), ((), ())), preferred_element_type=F32)
            if masked:
                key = lax.broadcasted_iota(jnp.int32, (TK, tq), 0)
                qry = lax.broadcasted_iota(jnp.int32, (TK, tq), 1)
                s = jnp.where(qry >= key, s, -jnp.inf)
            scores.append(s)
        m_news = [jnp.maximum(carry[h][0], jnp.max(scores[h], axis=0, keepdims=True))
                  for h in range(hb)]
        probs = [jnp.exp(scores[h] - m_news[h]).astype(BF16) for h in range(hb)]
        out = []
        for h in range(hb):
            m, acc = carry[h]
            vj = v_ref[0, h, :, pl.ds(start, TK)]
            out.append((m_news[h], jnp.exp(m - m_news[h]) * acc + _dot(vj, probs[h])))
        return tuple(out)

    init = tuple((jnp.full((1, tq), -jnp.inf, F32), jnp.zeros((SLAB, tq), F32)) for _ in range(hb))
    carry = lax.fori_loop(0, qi, lambda j, c: step(j, c, False), init)
    carry = step(qi, carry, True)
    outs = [carry[h][1][:HEAD_DIM, :] / carry[h][1][HEAD_DIM:HEAD_DIM + 1, :] for h in range(hb)]
    for pair in range(hb // 2):
        both = jnp.concatenate([outs[2 * pair], outs[2 * pair + 1]], axis=0)
        o_ref[0, :, pair * LANES:(pair + 1) * LANES] = both.T


def _attention(qp, kp, vp):
    batch, _, seq, _ = kp.shape
    hb = HEADS_PER_STEP
    kv_spec = pl.BlockSpec((1, hb, seq, SLAB), lambda b, h, i: (b, h, 0, 0))
    vt_spec = pl.BlockSpec((1, hb, SLAB, seq), lambda b, h, i: (b, h, 0, 0))
    return pl.pallas_call(
        _attn_kernel,
        out_shape=jax.ShapeDtypeStruct((batch, seq, D_ATT), F32),
        grid=(batch, N_HEADS // hb, seq // TQ),
        in_specs=[pl.BlockSpec((1, hb, TQ, SLAB), lambda b, h, i: (b, h, i, 0)), kv_spec, vt_spec],
        out_specs=pl.BlockSpec((1, TQ, hb * HEAD_DIM), lambda b, h, i: (b, i, h)),
        compiler_params=pltpu.CompilerParams(
            dimension_semantics=("arbitrary", "arbitrary", "arbitrary"),
            vmem_limit_bytes=VMEM_LIMIT),
        name="attn",
    )(qp, kp, vp)


def _out_proj_kernel(x_ref, rec_ref, att_ref, aw_ref, wr_ref, wa_ref, ln_ref, rwh_ref, rwl_ref, rb_ref,
                     x1_ref, xs_ref, meta_ref, units_ref):
    tm = x_ref.shape[0]

    att = att_ref[...]
    ms = jnp.sum(att * att, axis=-1, keepdims=True) * (1.0 / D_ATT)
    att_n = (att * lax.rsqrt(ms + EPS) * aw_ref[...]).astype(BF16)
    x1 = x_ref[...] + _dot(rec_ref[...], wr_ref[...]) + _dot(att_n, wa_ref[...])
    x1_ref[...] = x1

    ms2 = jnp.mean(x1 * x1, axis=-1, keepdims=True)
    h2 = x1 * lax.rsqrt(ms2 + EPS) * ln_ref[...]

    h_hi = h2.astype(BF16)
    h_lo = (h2 - h_hi.astype(F32)).astype(BF16)
    nt = (((1,), (1,)), ((), ()))
    logits = (lax.dot_general(rwh_ref[...], h_hi, nt, preferred_element_type=F32)
              + lax.dot_general(rwh_ref[...], h_lo, nt, preferred_element_type=F32)
              + lax.dot_general(rwl_ref[...], h_hi, nt, preferred_element_type=F32) + rb_ref[...])
    expert = lax.broadcasted_iota(jnp.int32, (LANES, tm), 0)
    expert_f = expert.astype(F32)
    l = jnp.where(expert < N_EXPERTS, logits, -jnp.inf)

    vals, idxs, sels = [], [], []
    for _ in range(TOP_K):
        m = jnp.max(l, axis=0, keepdims=True)
        idx = jnp.min(jnp.where(l == m, expert_f, float(LANES)), axis=0, keepdims=True)
        sel = expert_f == idx
        vals.append(m)
        idxs.append(idx)
        sels.append(sel)
        l = jnp.where(sel, -jnp.inf, l)
    exps = [jnp.exp(v - vals[0]) for v in vals]
    denom = exps[0] + exps[1] + exps[2] + exps[3]

    onehot = jnp.zeros((LANES, tm), F32)
    for sel in sels:
        onehot = jnp.where(sel, 1.0, onehot)
    row = lax.broadcasted_iota(jnp.int32, (tm, tm), 0)
    col = lax.broadcasted_iota(jnp.int32, (tm, tm), 1)
    earlier_token = (row < col).astype(BF16)
    rank = _dot(onehot.astype(BF16), earlier_token)
    count = jnp.sum(onehot, axis=1, keepdims=True)
    units = jnp.floor((count + (UNIT - 1)) * (1.0 / UNIT))
    e_row = lax.broadcasted_iota(jnp.int32, (LANES, LANES), 0)
    e_col = lax.broadcasted_iota(jnp.int32, (LANES, LANES), 1)
    earlier_expert = (e_col < e_row).astype(BF16)
    units_wide = jnp.broadcast_to(units, (LANES, LANES))
    first_unit = _dot(earlier_expert, units_wide.astype(BF16))[:, 0:1]
    units_ref[0] = units_wide.T[0:8, :]
    place = first_unit * float(UNIT) + rank

    meta_t = jnp.zeros((LANES, tm), F32)
    for k in range(TOP_K):
        slot = jnp.sum(jnp.where(sels[k], place, 0.0), axis=0, keepdims=True)
        meta_t = jnp.where(expert == k, idxs[k], meta_t)
        meta_t = jnp.where(expert == TOP_K + k, exps[k] / denom, meta_t)
        meta_t = jnp.where(expert == 2 * TOP_K + k, slot, meta_t)
    meta_ref[...] = meta_t.T

    slot_rows = [meta_t[2 * TOP_K + k:2 * TOP_K + k + 1, :].astype(jnp.int32) for k in range(TOP_K)]
    for c in range(TILE_SLOTS // SLOT_CHUNK):
        slot_id = lax.broadcasted_iota(jnp.int32, (SLOT_CHUNK, tm), 0) + c * SLOT_CHUNK
        hit = jnp.zeros((SLOT_CHUNK, tm), F32)
        for slot_row in slot_rows:
            hit = jnp.where(slot_id == slot_row, 1.0, hit)
        xs_ref[c * SLOT_CHUNK:(c + 1) * SLOT_CHUNK, :] = _dot(hit.astype(BF16), h_hi).astype(BF16)


def _out_proj(x2d, rec_n, att, aw_slab, w_rec, w_att, ln2_w, rw_hi, rw_lo, rb_pad):
    t = x2d.shape[0]
    n_tiles = t // TM_OUT
    rows = lambda n: pl.BlockSpec((TM_OUT, n), lambda i: (i, 0))
    const = lambda shape: pl.BlockSpec(shape, lambda i: (0,) * len(shape))
    return pl.pallas_call(
        _out_proj_kernel,
        out_shape=(jax.ShapeDtypeStruct((t, D_MODEL), F32),
                   jax.ShapeDtypeStruct((n_tiles * TILE_SLOTS, D_MODEL), BF16),
                   jax.ShapeDtypeStruct((t, LANES), F32),
                   jax.ShapeDtypeStruct((n_tiles, 8, LANES), F32)),
        grid=(n_tiles,),
        in_specs=[rows(D_MODEL), rows(D_REC), rows(D_ATT), const((1, D_ATT)),
                  const((D_REC, D_MODEL)), const((D_ATT, D_MODEL)), const((1, D_MODEL)),
                  const((LANES, D_MODEL)), const((LANES, D_MODEL)), const((LANES, 1))],
        out_specs=(rows(D_MODEL), pl.BlockSpec((TILE_SLOTS, D_MODEL), lambda i: (i, 0)), rows(LANES),
                   pl.BlockSpec((1, 8, LANES), lambda i: (i, 0, 0))),
        compiler_params=pltpu.CompilerParams(dimension_semantics=("arbitrary",),
                                             vmem_limit_bytes=VMEM_LIMIT),
        name="out_proj",
    )(x2d, rec_n, att, aw_slab, w_rec, w_att, ln2_w, rw_hi, rw_lo, rb_pad)


def _start_units(unit_ref, base, n_units, src_hbm, dst, sem):
    def body(j, carry):
        src_row = pl.multiple_of(unit_ref[base + j] * UNIT, UNIT)
        dst_row = pl.multiple_of(j * UNIT, UNIT)
        pltpu.make_async_copy(src_hbm.at[pl.ds(src_row, UNIT), :],
                              dst.at[pl.ds(dst_row, UNIT), :], sem).start()
        return carry
    lax.fori_loop(0, n_units, body, 0, unroll=8)


def _wait_units(n_units, src_hbm, dst, sem):
    pltpu.make_async_copy(src_hbm.at[pl.ds(0, n_units * UNIT), :], dst, sem).wait()


def _experts_kernel(be_ref, nblk_ref, unit_ref, order_ref, next_ref, xs_hbm, wgu_hbm, bgu_ref, wd_hbm,
                    bd_ref, ys_ref, xbuf, wgu_f32, wd_f32, wgu_bf, wd_bf, sem, wsem):
    b = pl.program_id(0)
    nblk = nblk_ref[0]
    slot = b % 2
    e = be_ref[b]

    def weight_copies(expert, wslot):
        return (pltpu.make_async_copy(wgu_hbm.at[expert], wgu_f32.at[wslot], wsem.at[wslot]),
                pltpu.make_async_copy(wd_hbm.at[expert], wd_f32.at[wslot], wsem.at[wslot]))

    @pl.when(b == 0)
    def _():
        _start_units(unit_ref, 0, BLOCK_UNITS, xs_hbm, xbuf.at[0], sem.at[0])
        for copy in weight_copies(e, 0):
            copy.start()

    @pl.when(b + 1 < nblk)
    def _():
        _start_units(unit_ref, (b + 1) * BLOCK_UNITS, BLOCK_UNITS, xs_hbm, xbuf.at[1 - slot],
                     sem.at[1 - slot])

    new_expert = jnp.logical_or(b == 0, e != be_ref[jnp.maximum(b - 1, 0)])

    @pl.when(jnp.logical_and(b < nblk, new_expert))
    def _():
        wslot = order_ref[e] % 2
        for copy in weight_copies(e, wslot):
            copy.wait()
        wgu_bf[...] = wgu_f32[wslot].astype(BF16)
        wd_bf[...] = wd_f32[wslot].astype(BF16)
        nxt = next_ref[e]

        @pl.when(nxt >= 0)
        def _():
            for copy in weight_copies(nxt, 1 - wslot):
                copy.start()

    @pl.when(b < nblk)
    def _():
        _wait_units(BLOCK_UNITS, xs_hbm, xbuf.at[slot], sem.at[slot])
        gu = _dot(xbuf[slot], wgu_bf[...]) + bgu_ref[0]
        g = jnp.minimum(gu[:, :D_EXPERT], SWIGLU_LIMIT)
        u = jnp.clip(gu[:, D_EXPERT:], -SWIGLU_LIMIT, SWIGLU_LIMIT)
        act = (u + 1.0) * (g * _sigmoid(SWIGLU_ALPHA * g))
        ys_ref[...] = (_dot(act.astype(BF16), wd_bf[...]) + bd_ref[0]).astype(BF16)

    @pl.when(b >= nblk)
    def _():
        ys_ref[...] = jnp.zeros_like(ys_ref)


def _experts(block_expert, n_blocks_used, unit_src, expert_order, expert_next, xs, w_gate_up,
             b_gate_up, w_down, b_down):
    n_blocks = block_expert.shape[0]
    bias_map = lambda b, be, nb, us, eo, en: (be[b], 0, 0)
    grid_spec = pltpu.PrefetchScalarGridSpec(
        num_scalar_prefetch=5,
        grid=(n_blocks,),
        in_specs=[pl.BlockSpec(memory_space=pl.ANY),
                  pl.BlockSpec(memory_space=pl.ANY),
                  pl.BlockSpec((1, 1, 2 * D_EXPERT), bias_map),
                  pl.BlockSpec(memory_space=pl.ANY),
                  pl.BlockSpec((1, 1, D_MODEL), bias_map)],
        out_specs=pl.BlockSpec((TM_E, D_MODEL), lambda b, be, nb, us, eo, en: (b, 0)),
        scratch_shapes=[pltpu.VMEM((2, TM_E, D_MODEL), BF16),
                        pltpu.VMEM((2, D_MODEL, 2 * D_EXPERT), F32),
                        pltpu.VMEM((2, D_EXPERT, D_MODEL), F32),
                        pltpu.VMEM((D_MODEL, 2 * D_EXPERT), BF16),
                        pltpu.VMEM((D_EXPERT, D_MODEL), BF16),
                        pltpu.SemaphoreType.DMA((2,)),
                        pltpu.SemaphoreType.DMA((2,))],
    )
    return pl.pallas_call(
        _experts_kernel,
        out_shape=jax.ShapeDtypeStruct((n_blocks * TM_E, D_MODEL), BF16),
        grid_spec=grid_spec,
        compiler_params=pltpu.CompilerParams(dimension_semantics=("arbitrary",),
                                             vmem_limit_bytes=VMEM_LIMIT),
        name="experts",
    )(block_expert, n_blocks_used, unit_src, expert_order, expert_next, xs, w_gate_up,
      b_gate_up.reshape(N_EXPERTS, 1, 2 * D_EXPERT), w_down, b_down.reshape(N_EXPERTS, 1, D_MODEL))


def _combine_kernel(unit_ref, x1_ref, meta_ref, ys_hbm, o_ref, buf, sem):
    i = pl.program_id(0)
    n = pl.num_programs(0)
    slot = i % 2
    tm = x1_ref.shape[0]

    @pl.when(i == 0)
    def _():
        _start_units(unit_ref, 0, TILE_UNITS, ys_hbm, buf.at[0], sem.at[0])

    @pl.when(i + 1 < n)
    def _():
        _start_units(unit_ref, (i + 1) * TILE_UNITS, TILE_UNITS, ys_hbm, buf.at[1 - slot],
                     sem.at[1 - slot])

    meta = meta_ref[...]
    slots = [meta[:, 2 * TOP_K + k:2 * TOP_K + k + 1].astype(jnp.int32) for k in range(TOP_K)]
    gates = [meta[:, TOP_K + k:TOP_K + k + 1] for k in range(TOP_K)]
    _wait_units(TILE_UNITS, ys_hbm, buf.at[slot], sem.at[slot])

    acc = x1_ref[...]
    for c in range(TILE_SLOTS // SLOT_CHUNK):
        first = c * SLOT_CHUNK
        col = lax.broadcasted_iota(jnp.int32, (tm, SLOT_CHUNK), 1) + first
        weight = jnp.zeros((tm, SLOT_CHUNK), F32)
        for k in range(TOP_K):
            weight = jnp.where(col == slots[k], gates[k], weight)
        acc = acc + _dot(weight.astype(BF16), buf[slot, first:first + SLOT_CHUNK, :])
    o_ref[...] = acc


def _combine(tile_units, x1, meta, ys):
    t = x1.shape[0]
    grid_spec = pltpu.PrefetchScalarGridSpec(
        num_scalar_prefetch=1,
        grid=(t // TM_OUT,),
        in_specs=[pl.BlockSpec((TM_OUT, D_MODEL), lambda i, us: (i, 0)),
                  pl.BlockSpec((TM_OUT, LANES), lambda i, us: (i, 0)),
                  pl.BlockSpec(memory_space=pl.ANY)],
        out_specs=pl.BlockSpec((TM_OUT, D_MODEL), lambda i, us: (i, 0)),
        scratch_shapes=[pltpu.VMEM((2, TILE_SLOTS, D_MODEL), BF16),
                        pltpu.SemaphoreType.DMA((2,))],
    )
    return pl.pallas_call(
        _combine_kernel,
        out_shape=jax.ShapeDtypeStruct((t, D_MODEL), F32),
        grid_spec=grid_spec,
        compiler_params=pltpu.CompilerParams(dimension_semantics=("arbitrary",),
                                             vmem_limit_bytes=VMEM_LIMIT),
        name="combine",
    )(tile_units, x1, meta, ys)


def _gate_matrix(w_a, w_i):
    g = w_a.shape[0]
    eye = jnp.eye(g, dtype=w_a.dtype)
    both = jnp.stack([w_a, w_i])
    return jnp.einsum('sgij,gh->gishj', both, eye).reshape(g * REC_BLOCK, 2 * g * REC_BLOCK)


def _pad_lanes(v, n=LANES):
    return jnp.pad(v, (0, n - v.shape[0])).reshape(1, n)


def _excl_cumsum(a, axis):
    return jnp.cumsum(a, axis=axis) - a


def _pick(onehot, table):
    if table.ndim == 1:
        return jnp.sum(onehot * table[None, :], axis=1)
    return jnp.sum(onehot[:, :, None] * table[None, :, :], axis=1)


def _dispatch_tables(units, n_blocks):
    n_tiles = units.shape[0]
    i32 = jnp.int32
    tile_first = _excl_cumsum(units, 1)
    tile_end = tile_first + units
    run_first = _excl_cumsum(units, 0)
    run_end = run_first + units
    region_units = jnp.sum(units, axis=0)
    region_blocks = (region_units + BLOCK_UNITS - 1) // BLOCK_UNITS
    block_end = jnp.cumsum(region_blocks)
    region_first = (block_end - region_blocks) * BLOCK_UNITS
    n_used = block_end[-1:]
    experts = jnp.arange(N_EXPERTS, dtype=i32)
    tiles = jnp.arange(n_tiles, dtype=i32)

    blocks = jnp.arange(n_blocks, dtype=i32)
    block_expert = jnp.minimum(
        jnp.sum((block_end[None, :] <= blocks[:, None]).astype(i32), axis=1), N_EXPERTS - 1)

    g = jnp.arange(n_blocks * BLOCK_UNITS, dtype=i32)
    e = jnp.minimum(jnp.sum((block_end[None, :] * BLOCK_UNITS <= g[:, None]).astype(i32), axis=1),
                    N_EXPERTS - 1)
    e_hot = (e[:, None] == experts[None, :]).astype(i32)
    local = g - _pick(e_hot, region_first)
    valid = (local < _pick(e_hot, region_units)) & (g < n_used[0] * BLOCK_UNITS)
    ends = _pick(e_hot, run_end.T)
    tile = jnp.minimum(jnp.sum((ends <= local[:, None]).astype(i32), axis=1), n_tiles - 1)
    t_hot = (tile[:, None] == tiles[None, :]).astype(i32)
    src = (tile * TILE_UNITS + jnp.sum(_pick(t_hot, tile_first - run_first) * e_hot, axis=1) + local)
    unit_src = jnp.where(valid, src, 0).astype(i32)

    j = jnp.arange(TILE_UNITS, dtype=i32)
    ej = jnp.minimum(jnp.sum((tile_end[:, None, :] <= j[None, :, None]).astype(i32), axis=2),
                     N_EXPERTS - 1)
    ej_hot = (ej[:, :, None] == experts[None, None, :]).astype(i32)
    used = j[None, :] < jnp.sum(units, axis=1)[:, None]
    base = region_first[None, :] + run_first - tile_first
    dst = jnp.sum(ej_hot * base[:, None, :], axis=2) + j[None, :]
    tile_units = jnp.where(used, dst, 0).astype(i32).reshape(-1)
    owns = region_blocks > 0
    expert_order = _excl_cumsum(owns.astype(i32), 0)
    later = (experts[None, :] > experts[:, None]) & owns[None, :]
    expert_next = jnp.min(jnp.where(later, experts[None, :], N_EXPERTS), axis=1)
    expert_next = jnp.where(expert_next < N_EXPERTS, expert_next, -1).astype(i32)
    return (block_expert.astype(i32), n_used.astype(i32), unit_src, tile_units,
            expert_order.astype(i32), expert_next)


def _layer(x, ln1_w, w_in, conv_w, conv_b, w_rg_a, b_rg_a, w_rg_i, b_rg_i, rg_lambda, b_forget,
           q_norm_w, k_norm_w, rec_out_norm_w, attn_out_norm_w, w_out, ln2_w, w_router, b_router,
           w_gate_up, b_gate_up, w_down, b_down):
    batch, seq, _ = x.shape
    t = batch * seq
    x2d = x.reshape(t, D_MODEL)

    o_f = N_PROJ + 3 * D_ATT
    w_t = w_in.T.astype(BF16)
    w_ft = jnp.pad(w_t[o_f:], ((0, LANES - N_HEADS), (0, 0)))
    scale = HEAD_DIM ** -0.5
    u, g, qp, kp, vp = _in_proj(
        x2d, ln1_w.reshape(1, D_MODEL), w_t, w_ft, _pad_lanes(b_forget).reshape(LANES, 1),
        (q_norm_w * scale).reshape(HEAD_DIM, 1), k_norm_w.reshape(HEAD_DIM, 1), batch, seq)

    rec_vec = jnp.concatenate(
        [conv_w, jnp.stack([conv_b, b_rg_a, b_rg_i, rg_lambda, rec_out_norm_w])])
    rec_n = _rglru(u.reshape(batch, seq, D_REC), g.reshape(batch, seq, D_REC), rec_vec,
                   _gate_matrix(w_rg_a, w_rg_i).astype(BF16))

    att = _attention(qp, kp, vp)

    rw = jnp.pad(w_router.T, ((0, LANES - N_EXPERTS), (0, 0)))
    rw_hi = rw.astype(BF16)
    rw_lo = (rw - rw_hi.astype(F32)).astype(BF16)
    x1, xs, meta, units = _out_proj(
        x2d, rec_n.reshape(t, D_REC), att.reshape(t, D_ATT), attn_out_norm_w.reshape(1, D_ATT),
        w_out[:D_REC].astype(BF16), w_out[D_REC:].astype(BF16), ln2_w.reshape(1, D_MODEL),
        rw_hi, rw_lo, _pad_lanes(b_router).reshape(LANES, 1))

    n_tiles = t // TM_OUT
    max_units = t * TOP_K // UNIT + n_tiles * N_EXPERTS
    n_blocks = -(-max_units // BLOCK_UNITS) + N_EXPERTS
    block_expert, n_used, unit_src, tile_units, expert_order, expert_next = _dispatch_tables(
        units[:, 0, :N_EXPERTS].astype(jnp.int32), n_blocks)

    ys = _experts(block_expert, n_used, unit_src, expert_order, expert_next, xs, w_gate_up, b_gate_up,
                  w_down, b_down)
    out = _combine(tile_units, x1, meta, ys)
    return out.reshape(batch, seq, D_MODEL)


def kernel(x, ln1_w, w_in, conv_w, conv_b, w_rg_a, b_rg_a, w_rg_i, b_rg_i, rg_lambda, b_forget,
           q_norm_w, k_norm_w, rec_out_norm_w, attn_out_norm_w, w_out, ln2_w, w_router, b_router,
           w_gate_up, b_gate_up, w_down, b_down):
    for l in range(ln1_w.shape[0]):
        x = _layer(x, ln1_w[l], w_in[l], conv_w[l], conv_b[l], w_rg_a[l], b_rg_a[l], w_rg_i[l],
                   b_rg_i[l], rg_lambda[l], b_forget[l], q_norm_w[l], k_norm_w[l], rec_out_norm_w[l],
                   attn_out_norm_w[l], w_out[l], ln2_w[l], w_router[l], b_router[l], w_gate_up[l],
                   b_gate_up[l], w_down[l], b_down[l])
    return x
```

```python
import functools

import jax
import jax.numpy as jnp
from jax import lax
from jax.experimental import pallas as pl
from jax.experimental.pallas import tpu as pltpu

D_MODEL = 1024
D_REC = 512
REC_BLOCK = 64
CONV_WIDTH = 4
RGLRU_C = 8.0
D_ATT = 512
HEAD_DIM = 64
N_HEADS = 8
N_EXPERTS = 32
TOP_K = 4
D_EXPERT = 1024
SWIGLU_LIMIT = 7.0
SWIGLU_ALPHA = 1.702
EPS = 1e-6

LANES = 128
SLAB = LANES
VMEM_LIMIT = 56 * 1024 * 1024

F32 = jnp.float32
BF16 = jnp.bfloat16

O_U = 0
O_G = O_U + D_REC
N_PROJ = O_G + D_REC

L_C = HEAD_DIM
L_ONE = HEAD_DIM + 3

TM_IN = 512
TS_REC = 512
SCAN_WAYS = 8
TQ = 512
TK = 512
HEADS_PER_STEP = 8
TM_OUT = 512
TM_E = 512

UNIT = 16
TILE_SLOTS = -(-(TM_OUT * TOP_K + N_EXPERTS * (UNIT - 1)) // LANES) * LANES
TILE_UNITS = TILE_SLOTS // UNIT
BLOCK_UNITS = TM_E // UNIT
SLOT_CHUNK = 512


def _split3(v):
    hi = v.astype(BF16).astype(F32)
    r = v - hi
    mid = r.astype(BF16).astype(F32)
    lo = (r - mid).astype(BF16).astype(F32)
    return hi, mid, lo


def _dot(a, b):
    return jnp.dot(a, b, preferred_element_type=F32)


def _sigmoid(x):
    return 0.5 * jnp.tanh(0.5 * x) + 0.5


def _in_proj_kernel(x_ref, ln_ref, w_ref, wqt_ref, wkt_ref, wvt_ref, wft_ref, bf_ref, qw_ref, kw_ref,
                    u_ref, g_ref, q_ref, k_ref, v_ref, carry_ref, *, tiles_per_seq):
    i = pl.program_id(0)
    tm = x_ref.shape[0]

    @pl.when(i % tiles_per_seq == 0)
    def _():
        carry_ref[...] = jnp.zeros_like(carry_ref)

    x = x_ref[...]
    ms = jnp.mean(x * x, axis=-1, keepdims=True)
    h = (x * lax.rsqrt(ms + EPS) * ln_ref[...]).astype(BF16)
    nt = (((1,), (1,)), ((), ()))

    f_t = lax.dot_general(wft_ref[...], h, nt, preferred_element_type=F32) + bf_ref[...]
    logf_t = jnp.minimum(f_t, 0.0) - jnp.log1p(jnp.exp(-jnp.abs(f_t)))
    row = lax.broadcasted_iota(jnp.int32, (tm, tm), 0)
    col = lax.broadcasted_iota(jnp.int32, (tm, tm), 1)
    upto = (row <= col).astype(BF16)
    hi, mid, lo = _split3(logf_t)
    c_t = (_dot(hi.astype(BF16), upto) + _dot(mid.astype(BF16), upto)
           + _dot(lo.astype(BF16), upto) + carry_ref[...])
    carry_ref[...] = c_t[:, tm - 1:tm]

    qt_all = lax.dot_general(wqt_ref[...], h, nt, preferred_element_type=F32)
    kt_all = lax.dot_general(wkt_ref[...], h, nt, preferred_element_type=F32)
    spare = lax.broadcasted_iota(jnp.int32, (SLAB - HEAD_DIM, tm), 0) + HEAD_DIM
    qw = qw_ref[...]
    kw = kw_ref[...]

    def normed(rows, weight):
        inv = lax.rsqrt(jnp.sum(rows * rows, axis=0, keepdims=True) * (1.0 / HEAD_DIM) + EPS)
        return rows * inv * weight

    for hd in range(N_HEADS):
        rows = slice(hd * HEAD_DIM, (hd + 1) * HEAD_DIM)
        c_hi, c_mid, c_lo = _split3(c_t[hd:hd + 1, :])
        q_extra = jnp.where(spare == L_C, c_hi, 0.0)
        q_extra = jnp.where(spare == L_C + 1, c_mid, q_extra)
        q_extra = jnp.where(spare == L_C + 2, c_lo, q_extra)
        q_extra = jnp.where((spare >= L_ONE) & (spare < L_ONE + 3), 1.0, q_extra)
        k_extra = jnp.where(spare < L_C + 3, 1.0, 0.0)
        k_extra = jnp.where(spare == L_ONE, -c_hi, k_extra)
        k_extra = jnp.where(spare == L_ONE + 1, -c_mid, k_extra)
        k_extra = jnp.where(spare == L_ONE + 2, -c_lo, k_extra)
        q_ref[0, hd] = jnp.concatenate([normed(qt_all[rows, :], qw), q_extra], axis=0).T.astype(BF16)
        k_ref[0, hd] = jnp.concatenate([normed(kt_all[rows, :], kw), k_extra], axis=0).T.astype(BF16)

    vt_all = lax.dot_general(wvt_ref[...], h, (((1,), (1,)), ((), ())), preferred_element_type=F32)
    spare = lax.broadcasted_iota(jnp.int32, (SLAB - HEAD_DIM, tm), 0)
    ones_row = jnp.where(spare == 0, 1.0, 0.0)
    for hd in range(N_HEADS):
        v_ref[0, hd] = jnp.concatenate(
            [vt_all[hd * HEAD_DIM:(hd + 1) * HEAD_DIM, :], ones_row], axis=0).astype(BF16)
    u_ref[...] = lax.dot_general(h, w_ref[O_U:O_U + D_REC, :], nt, preferred_element_type=F32)
    g_ref[...] = lax.dot_general(h, w_ref[O_G:O_G + D_REC, :], nt, preferred_element_type=F32)


def _in_proj(x2d, ln1_w, w_t, w_ft, bf_col, qw_col, kw_col, batch, seq):
    t = x2d.shape[0]
    rows = lambda first, n: pl.BlockSpec((n, D_MODEL), lambda i: (first // n, 0))
    o_q = N_PROJ
    o_k = o_q + D_ATT
    o_v = o_k + D_ATT
    tiles_per_seq = seq // TM_IN
    slab_shape = jax.ShapeDtypeStruct((batch, N_HEADS, seq, SLAB), BF16)
    slab_spec = pl.BlockSpec((1, N_HEADS, TM_IN, SLAB),
                             lambda i: (i // tiles_per_seq, 0, i % tiles_per_seq, 0))
    slab_t_shape = jax.ShapeDtypeStruct((batch, N_HEADS, SLAB, seq), BF16)
    slab_t_spec = pl.BlockSpec((1, N_HEADS, SLAB, TM_IN),
                               lambda i: (i // tiles_per_seq, 0, 0, i % tiles_per_seq))
    row_spec = pl.BlockSpec((TM_IN, D_REC), lambda i: (i, 0))
    const = lambda shape: pl.BlockSpec(shape, lambda i: (0,) * len(shape))
    return pl.pallas_call(
        functools.partial(_in_proj_kernel, tiles_per_seq=tiles_per_seq),
        out_shape=(jax.ShapeDtypeStruct((t, D_REC), F32), jax.ShapeDtypeStruct((t, D_REC), F32),
                   slab_shape, slab_shape, slab_t_shape),
        grid=(t // TM_IN,),
        in_specs=[pl.BlockSpec((TM_IN, D_MODEL), lambda i: (i, 0)),
                  const((1, D_MODEL)), rows(0, N_PROJ), rows(o_q, D_ATT),
                  rows(o_k, D_ATT), rows(o_v, D_ATT), const((LANES, D_MODEL)),
                  const((LANES, 1)), const((HEAD_DIM, 1)), const((HEAD_DIM, 1))],
        out_specs=(row_spec, row_spec, slab_spec, slab_spec, slab_t_spec),
        scratch_shapes=[pltpu.VMEM((LANES, 1), F32)],
        compiler_params=pltpu.CompilerParams(dimension_semantics=("arbitrary",),
                                             vmem_limit_bytes=VMEM_LIMIT),
        name="in_proj",
    )(x2d, ln1_w, w_t, w_t, w_t, w_t, w_ft, bf_col, qw_col, kw_col)


R_CONV_B = CONV_WIDTH
R_B_A = CONV_WIDTH + 1
R_B_I = CONV_WIDTH + 2
R_LAMBDA = CONV_WIDTH + 3
R_NORM_W = CONV_WIDTH + 4
N_REC_ROWS = CONV_WIDTH + 5


def _rglru_kernel(u_ref, g_ref, vec_ref, wg_ref, o_ref, ext_ref, a_ref, b_ref, h_ref, hcar_ref):
    s = pl.program_id(1)
    ts = u_ref.shape[1]
    pad = 8
    row = lambda r: vec_ref[r:r + 1, :]

    @pl.when(s == 0)
    def _():
        ext_ref[0:pad, :] = jnp.zeros((pad, D_REC), F32)
        hcar_ref[...] = jnp.zeros_like(hcar_ref)

    @pl.when(s != 0)
    def _():
        ext_ref[0:pad, :] = ext_ref[ts:ts + pad, :]

    u = u_ref[0]
    ext_ref[pad:pad + ts, :] = u
    xc = row(R_CONV_B) + u * row(CONV_WIDTH - 1)
    for tap in range(CONV_WIDTH - 1):
        d = CONV_WIDTH - 1 - tap
        xc = xc + ext_ref[pad - d:pad - d + ts, :] * row(tap)

    gates = _dot(xc.astype(BF16), wg_ref[...])
    r = _sigmoid(gates[:, :D_REC] + row(R_B_A))
    ig = _sigmoid(gates[:, D_REC:] + row(R_B_I))
    nlam = -row(R_LAMBDA)
    softplus = jnp.maximum(nlam, 0.0) + jnp.log1p(jnp.exp(-jnp.abs(nlam)))
    log_a = -RGLRU_C * r * softplus
    a = jnp.exp(log_a)
    b = jnp.sqrt(1.0 - jnp.exp(2.0 * log_a)) * (ig * xc)
    sub = ts // SCAN_WAYS
    side_by_side = lambda v: jnp.swapaxes(v.reshape(SCAN_WAYS, sub, D_REC), 0, 1)
    a_ref[...] = side_by_side(a)
    b_ref[...] = side_by_side(b)
    h = jnp.zeros((SCAN_WAYS, D_REC), F32)
    p = jnp.ones((SCAN_WAYS, D_REC), F32)
    for r in range(sub):
        a_r = a_ref[r]
        h = a_r * h + b_ref[r]
        p = a_r * p
        h_ref[r] = h
        a_ref[r] = p
    entering = hcar_ref[...]
    enter = []
    for w in range(SCAN_WAYS):
        enter.append(entering)
        entering = h[w:w + 1, :] + p[w:w + 1, :] * entering
    hcar_ref[...] = entering
    h_all = h_ref[...] + a_ref[...] * jnp.concatenate(enter, axis=0)[None]
    h_all = jnp.swapaxes(h_all, 0, 1).reshape(ts, D_REC)

    rec = h_all * jax.nn.gelu(g_ref[0])
    ms = jnp.mean(rec * rec, axis=-1, keepdims=True)
    o_ref[0] = (rec * lax.rsqrt(ms + EPS) * row(R_NORM_W)).astype(BF16)


def _rglru(u, g, rec_vec, w_gates):
    batch, seq, _ = u.shape
    seq_spec = pl.BlockSpec((1, TS_REC, D_REC), lambda b, s: (b, s, 0))
    const = lambda shape: pl.BlockSpec(shape, lambda b, s: (0,) * len(shape))
    return pl.pallas_call(
        _rglru_kernel,
        out_shape=jax.ShapeDtypeStruct((batch, seq, D_REC), BF16),
        grid=(batch, seq // TS_REC),
        in_specs=[seq_spec, seq_spec, const((N_REC_ROWS, D_REC)), const((D_REC, 2 * D_REC))],
        out_specs=seq_spec,
        scratch_shapes=[pltpu.VMEM((TS_REC + 8, D_REC), F32)]
        + [pltpu.VMEM((TS_REC // SCAN_WAYS, SCAN_WAYS, D_REC), F32)] * 3
        + [pltpu.VMEM((1, D_REC), F32)],
        compiler_params=pltpu.CompilerParams(dimension_semantics=("arbitrary", "arbitrary"),
                                             vmem_limit_bytes=VMEM_LIMIT),
        name="rglru",
    )(u, g, rec_vec, w_gates)


def _attn_kernel(q_ref, k_ref, v_ref, o_ref):
    qi = pl.program_id(2)
    hb = q_ref.shape[1]
    tq = q_ref.shape[2]
    qs = [q_ref[0, h] for h in range(hb)]

    def step(j, carry, masked):
        start = pl.multiple_of(j * TK, TK)
        scores = []
        for h in range(hb):
            kj = k_ref[0, h, pl.ds(start, TK), :]
            s = lax.dot_general(kj, qs[h], (((1,), (1,)), ((), ())), preferred_element_type=F32)
            if masked:
                key = lax.broadcasted_iota(jnp.int32, (TK, tq), 0)
                qry = lax.broadcasted_iota(jnp.int32, (TK, tq), 1)
                s = jnp.where(qry >= key, s, -jnp.inf)
            scores.append(s)
        m_news = [jnp.maximum(carry[h][0], jnp.max(scores[h], axis=0, keepdims=True))
                  for h in range(hb)]
        probs = [jnp.exp(scores[h] - m_news[h]).astype(BF16) for h in range(hb)]
        out = []
        for h in range(hb):
            m, acc = carry[h]
            vj = v_ref[0, h, :, pl.ds(start, TK)]
            out.append((m_news[h], jnp.exp(m - m_news[h]) * acc + _dot(vj, probs[h])))
        return tuple(out)

    init = tuple((jnp.full((1, tq), -jnp.inf, F32), jnp.zeros((SLAB, tq), F32)) for _ in range(hb))
    carry = lax.fori_loop(0, qi, lambda j, c: step(j, c, False), init)
    carry = step(qi, carry, True)
    outs = [carry[h][1][:HEAD_DIM, :] / carry[h][1][HEAD_DIM:HEAD_DIM + 1, :] for h in range(hb)]
    for pair in range(hb // 2):
        both = jnp.concatenate([outs[2 * pair], outs[2 * pair + 1]], axis=0)
        o_ref[0, :, pair * LANES:(pair + 1) * LANES] = both.T


def _attention(qp, kp, vp):
    batch, _, seq, _ = kp.shape
    hb = HEADS_PER_STEP
    kv_spec = pl.BlockSpec((1, hb, seq, SLAB), lambda b, h, i: (b, h, 0, 0))
    vt_spec = pl.BlockSpec((1, hb, SLAB, seq), lambda b, h, i: (b, h, 0, 0))
    return pl.pallas_call(
        _attn_kernel,
        out_shape=jax.ShapeDtypeStruct((batch, seq, D_ATT), F32),
        grid=(batch, N_HEADS // hb, seq // TQ),
        in_specs=[pl.BlockSpec((1, hb, TQ, SLAB), lambda b, h, i: (b, h, i, 0)), kv_spec, vt_spec],
        out_specs=pl.BlockSpec((1, TQ, hb * HEAD_DIM), lambda b, h, i: (b, i, h)),
        compiler_params=pltpu.CompilerParams(
            dimension_semantics=("arbitrary", "arbitrary", "arbitrary"),
            vmem_limit_bytes=VMEM_LIMIT),
        name="attn",
    )(qp, kp, vp)


def _out_proj_kernel(x_ref, rec_ref, att_ref, aw_ref, wr_ref, wa_ref, ln_ref, rwh_ref, rwl_ref, rb_ref,
                     x1_ref, xs_ref, meta_ref, units_ref):
    tm = x_ref.shape[0]

    att = att_ref[...]
    ms = jnp.sum(att * att, axis=-1, keepdims=True) * (1.0 / D_ATT)
    att_n = (att * lax.rsqrt(ms + EPS) * aw_ref[...]).astype(BF16)
    x1 = x_ref[...] + _dot(rec_ref[...], wr_ref[...]) + _dot(att_n, wa_ref[...])
    x1_ref[...] = x1

    ms2 = jnp.mean(x1 * x1, axis=-1, keepdims=True)
    h2 = x1 * lax.rsqrt(ms2 + EPS) * ln_ref[...]

    h_hi = h2.astype(BF16)
    h_lo = (h2 - h_hi.astype(F32)).astype(BF16)
    nt = (((1,), (1,)), ((), ()))
    logits = (lax.dot_general(rwh_ref[...], h_hi, nt, preferred_element_type=F32)
              + lax.dot_general(rwh_ref[...], h_lo, nt, preferred_element_type=F32)
              + lax.dot_general(rwl_ref[...], h_hi, nt, preferred_element_type=F32) + rb_ref[...])
    expert = lax.broadcasted_iota(jnp.int32, (LANES, tm), 0)
    expert_f = expert.astype(F32)
    l = jnp.where(expert < N_EXPERTS, logits, -jnp.inf)

    vals, idxs, sels = [], [], []
    for _ in range(TOP_K):
        m = jnp.max(l, axis=0, keepdims=True)
        idx = jnp.min(jnp.where(l == m, expert_f, float(LANES)), axis=0, keepdims=True)
        sel = expert_f == idx
        vals.append(m)
        idxs.append(idx)
        sels.append(sel)
        l = jnp.where(sel, -jnp.inf, l)
    exps = [jnp.exp(v - vals[0]) for v in vals]
    denom = exps[0] + exps[1] + exps[2] + exps[3]

    onehot = jnp.zeros((LANES, tm), F32)
    for sel in sels:
        onehot = jnp.where(sel, 1.0, onehot)
    row = lax.broadcasted_iota(jnp.int32, (tm, tm), 0)
    col = lax.broadcasted_iota(jnp.int32, (tm, tm), 1)
    earlier_token = (row < col).astype(BF16)
    rank = _dot(onehot.astype(BF16), earlier_token)
    count = jnp.sum(onehot, axis=1, keepdims=True)
    units = jnp.floor((count + (UNIT - 1)) * (1.0 / UNIT))
    e_row = lax.broadcasted_iota(jnp.int32, (LANES, LANES), 0)
    e_col = lax.broadcasted_iota(jnp.int32, (LANES, LANES), 1)
    earlier_expert = (e_col < e_row).astype(BF16)
    units_wide = jnp.broadcast_to(units, (LANES, LANES))
    first_unit = _dot(earlier_expert, units_wide.astype(BF16))[:, 0:1]
    units_ref[0] = units_wide.T[0:8, :]
    place = first_unit * float(UNIT) + rank

    meta_t = jnp.zeros((LANES, tm), F32)
    for k in range(TOP_K):
        slot = jnp.sum(jnp.where(sels[k], place, 0.0), axis=0, keepdims=True)
        meta_t = jnp.where(expert == k, idxs[k], meta_t)
        meta_t = jnp.where(expert == TOP_K + k, exps[k] / denom, meta_t)
        meta_t = jnp.where(expert == 2 * TOP_K + k, slot, meta_t)
    meta_ref[...] = meta_t.T

    slot_rows = [meta_t[2 * TOP_K + k:2 * TOP_K + k + 1, :].astype(jnp.int32) for k in range(TOP_K)]
    for c in range(TILE_SLOTS // SLOT_CHUNK):
        slot_id = lax.broadcasted_iota(jnp.int32, (SLOT_CHUNK, tm), 0) + c * SLOT_CHUNK
        hit = jnp.zeros((SLOT_CHUNK, tm), F32)
        for slot_row in slot_rows:
            hit = jnp.where(slot_id == slot_row, 1.0, hit)
        xs_ref[c * SLOT_CHUNK:(c + 1) * SLOT_CHUNK, :] = _dot(hit.astype(BF16), h_hi).astype(BF16)


def _out_proj(x2d, rec_n, att, aw_slab, w_out, ln2_w, rw_hi, rw_lo, rb_pad):
    assert D_REC == D_ATT
    t = x2d.shape[0]
    n_tiles = t // TM_OUT
    rows = lambda n: pl.BlockSpec((TM_OUT, n), lambda i: (i, 0))
    const = lambda shape: pl.BlockSpec(shape, lambda i: (0,) * len(shape))
    return pl.pallas_call(
        _out_proj_kernel,
        out_shape=(jax.ShapeDtypeStruct((t, D_MODEL), F32),
                   jax.ShapeDtypeStruct((n_tiles * TILE_SLOTS, D_MODEL), BF16),
                   jax.ShapeDtypeStruct((t, LANES), F32),
                   jax.ShapeDtypeStruct((n_tiles, 8, LANES), F32)),
        grid=(n_tiles,),
        in_specs=[rows(D_MODEL), rows(D_REC), rows(D_ATT), const((1, D_ATT)),
                  pl.BlockSpec((D_REC, D_MODEL), lambda i: (0, 0)),
                  pl.BlockSpec((D_ATT, D_MODEL), lambda i: (1, 0)), const((1, D_MODEL)),
                  const((LANES, D_MODEL)), const((LANES, D_MODEL)), const((LANES, 1))],
        out_specs=(rows(D_MODEL), pl.BlockSpec((TILE_SLOTS, D_MODEL), lambda i: (i, 0)), rows(LANES),
                   pl.BlockSpec((1, 8, LANES), lambda i: (i, 0, 0))),
        compiler_params=pltpu.CompilerParams(dimension_semantics=("arbitrary",),
                                             vmem_limit_bytes=VMEM_LIMIT),
        name="out_proj",
    )(x2d, rec_n, att, aw_slab, w_out, w_out, ln2_w, rw_hi, rw_lo, rb_pad)


def _start_units(unit_ref, base, n_units, src_hbm, dst, sem):
    def body(j, carry):
        src_row = pl.multiple_of(unit_ref[base + j] * UNIT, UNIT)
        dst_row = pl.multiple_of(j * UNIT, UNIT)
        pltpu.make_async_copy(src_hbm.at[pl.ds(src_row, UNIT), :],
                              dst.at[pl.ds(dst_row, UNIT), :], sem).start()
        return carry
    lax.fori_loop(0, n_units, body, 0, unroll=8)


def _wait_units(n_units, src_hbm, dst, sem):
    pltpu.make_async_copy(src_hbm.at[pl.ds(0, n_units * UNIT), :], dst, sem).wait()


def _experts_kernel(be_ref, nblk_ref, unit_ref, order_ref, next_ref, xs_hbm, wgu_hbm, bgu_ref, wd_hbm,
                    bd_ref, ys_ref, xbuf, wgu_f32, wd_f32, wgu_bf, wd_bf, sem, wsem):
    b = pl.program_id(0)
    nblk = nblk_ref[0]
    slot = b % 2
    e = be_ref[b]

    def weight_copies(expert, wslot):
        return (pltpu.make_async_copy(wgu_hbm.at[expert], wgu_f32.at[wslot], wsem.at[wslot]),
                pltpu.make_async_copy(wd_hbm.at[expert], wd_f32.at[wslot], wsem.at[wslot]))

    @pl.when(b == 0)
    def _():
        _start_units(unit_ref, 0, BLOCK_UNITS, xs_hbm, xbuf.at[0], sem.at[0])
        for copy in weight_copies(e, 0):
            copy.start()

    @pl.when(b + 1 < nblk)
    def _():
        _start_units(unit_ref, (b + 1) * BLOCK_UNITS, BLOCK_UNITS, xs_hbm, xbuf.at[1 - slot],
                     sem.at[1 - slot])

    new_expert = jnp.logical_or(b == 0, e != be_ref[jnp.maximum(b - 1, 0)])

    @pl.when(jnp.logical_and(b < nblk, new_expert))
    def _():
        wslot = order_ref[e] % 2
        for copy in weight_copies(e, wslot):
            copy.wait()
        wgu_bf[...] = wgu_f32[wslot].astype(BF16)
        wd_bf[...] = wd_f32[wslot].astype(BF16)
        nxt = next_ref[e]

        @pl.when(nxt >= 0)
        def _():
            for copy in weight_copies(nxt, 1 - wslot):
                copy.start()

    @pl.when(b < nblk)
    def _():
        _wait_units(BLOCK_UNITS, xs_hbm, xbuf.at[slot], sem.at[slot])
        gu = _dot(xbuf[slot], wgu_bf[...]) + bgu_ref[0]
        g = jnp.minimum(gu[:, :D_EXPERT], SWIGLU_LIMIT)
        u = jnp.clip(gu[:, D_EXPERT:], -SWIGLU_LIMIT, SWIGLU_LIMIT)
        act = (u + 1.0) * (g * _sigmoid(SWIGLU_ALPHA * g))
        ys_ref[...] = (_dot(act.astype(BF16), wd_bf[...]) + bd_ref[0]).astype(BF16)

    @pl.when(b >= nblk)
    def _():
        ys_ref[...] = jnp.zeros_like(ys_ref)


def _experts(block_expert, n_blocks_used, unit_src, expert_order, expert_next, xs, w_gate_up,
             b_gate_up, w_down, b_down):
    n_blocks = block_expert.shape[0]
    bias_map = lambda b, be, nb, us, eo, en: (be[b], 0, 0)
    grid_spec = pltpu.PrefetchScalarGridSpec(
        num_scalar_prefetch=5,
        grid=(n_blocks,),
        in_specs=[pl.BlockSpec(memory_space=pl.ANY),
                  pl.BlockSpec(memory_space=pl.ANY),
                  pl.BlockSpec((1, 1, 2 * D_EXPERT), bias_map),
                  pl.BlockSpec(memory_space=pl.ANY),
                  pl.BlockSpec((1, 1, D_MODEL), bias_map)],
        out_specs=pl.BlockSpec((TM_E, D_MODEL), lambda b, be, nb, us, eo, en: (b, 0)),
        scratch_shapes=[pltpu.VMEM((2, TM_E, D_MODEL), BF16),
                        pltpu.VMEM((2, D_MODEL, 2 * D_EXPERT), F32),
                        pltpu.VMEM((2, D_EXPERT, D_MODEL), F32),
                        pltpu.VMEM((D_MODEL, 2 * D_EXPERT), BF16),
                        pltpu.VMEM((D_EXPERT, D_MODEL), BF16),
                        pltpu.SemaphoreType.DMA((2,)),
                        pltpu.SemaphoreType.DMA((2,))],
    )
    return pl.pallas_call(
        _experts_kernel,
        out_shape=jax.ShapeDtypeStruct((n_blocks * TM_E, D_MODEL), BF16),
        grid_spec=grid_spec,
        compiler_params=pltpu.CompilerParams(dimension_semantics=("arbitrary",),
                                             vmem_limit_bytes=VMEM_LIMIT),
        name="experts",
    )(block_expert, n_blocks_used, unit_src, expert_order, expert_next, xs, w_gate_up,
      b_gate_up.reshape(N_EXPERTS, 1, 2 * D_EXPERT), w_down, b_down.reshape(N_EXPERTS, 1, D_MODEL))


def _combine_kernel(unit_ref, x1_ref, meta_ref, ys_hbm, o_ref, buf, sem):
    i = pl.program_id(0)
    n = pl.num_programs(0)
    slot = i % 2
    tm = x1_ref.shape[0]

    @pl.when(i == 0)
    def _():
        _start_units(unit_ref, 0, TILE_UNITS, ys_hbm, buf.at[0], sem.at[0])

    @pl.when(i + 1 < n)
    def _():
        _start_units(unit_ref, (i + 1) * TILE_UNITS, TILE_UNITS, ys_hbm, buf.at[1 - slot],
                     sem.at[1 - slot])

    meta = meta_ref[...]
    slots = [meta[:, 2 * TOP_K + k:2 * TOP_K + k + 1].astype(jnp.int32) for k in range(TOP_K)]
    gates = [meta[:, TOP_K + k:TOP_K + k + 1] for k in range(TOP_K)]
    _wait_units(TILE_UNITS, ys_hbm, buf.at[slot], sem.at[slot])

    acc = x1_ref[...]
    for c in range(TILE_SLOTS // SLOT_CHUNK):
        first = c * SLOT_CHUNK
        col = lax.broadcasted_iota(jnp.int32, (tm, SLOT_CHUNK), 1) + first
        weight = jnp.zeros((tm, SLOT_CHUNK), F32)
        for k in range(TOP_K):
            weight = jnp.where(col == slots[k], gates[k], weight)
        acc = acc + _dot(weight.astype(BF16), buf[slot, first:first + SLOT_CHUNK, :])
    o_ref[...] = acc


def _combine(tile_units, x1, meta, ys):
    t = x1.shape[0]
    grid_spec = pltpu.PrefetchScalarGridSpec(
        num_scalar_prefetch=1,
        grid=(t // TM_OUT,),
        in_specs=[pl.BlockSpec((TM_OUT, D_MODEL), lambda i, us: (i, 0)),
                  pl.BlockSpec((TM_OUT, LANES), lambda i, us: (i, 0)),
                  pl.BlockSpec(memory_space=pl.ANY)],
        out_specs=pl.BlockSpec((TM_OUT, D_MODEL), lambda i, us: (i, 0)),
        scratch_shapes=[pltpu.VMEM((2, TILE_SLOTS, D_MODEL), BF16),
                        pltpu.SemaphoreType.DMA((2,))],
    )
    return pl.pallas_call(
        _combine_kernel,
        out_shape=jax.ShapeDtypeStruct((t, D_MODEL), F32),
        grid_spec=grid_spec,
        compiler_params=pltpu.CompilerParams(dimension_semantics=("arbitrary",),
                                             vmem_limit_bytes=VMEM_LIMIT),
        name="combine",
    )(tile_units, x1, meta, ys)


def _gate_matrix(w_a, w_i):
    g = w_a.shape[0]
    eye = jnp.eye(g, dtype=w_a.dtype)
    both = jnp.stack([w_a, w_i])
    return jnp.einsum('sgij,gh->gishj', both, eye).reshape(g * REC_BLOCK, 2 * g * REC_BLOCK)


def _pad_lanes(v, n=LANES):
    return jnp.pad(v, (0, n - v.shape[0])).reshape(1, n)


def _excl_cumsum(a, axis):
    return jnp.cumsum(a, axis=axis) - a


def _pick(onehot, table):
    if table.ndim == 1:
        return jnp.sum(onehot * table[None, :], axis=1)
    return jnp.sum(onehot[:, :, None] * table[None, :, :], axis=1)


def _dispatch_tables(units, n_blocks):
    n_tiles = units.shape[0]
    i32 = jnp.int32
    tile_first = _excl_cumsum(units, 1)
    tile_end = tile_first + units
    run_first = _excl_cumsum(units, 0)
    run_end = run_first + units
    region_units = jnp.sum(units, axis=0)
    region_blocks = (region_units + BLOCK_UNITS - 1) // BLOCK_UNITS
    block_end = jnp.cumsum(region_blocks)
    region_first = (block_end - region_blocks) * BLOCK_UNITS
    n_used = block_end[-1:]
    experts = jnp.arange(N_EXPERTS, dtype=i32)
    tiles = jnp.arange(n_tiles, dtype=i32)

    blocks = jnp.arange(n_blocks, dtype=i32)
    block_expert = jnp.minimum(
        jnp.sum((block_end[None, :] <= blocks[:, None]).astype(i32), axis=1), N_EXPERTS - 1)

    g = jnp.arange(n_blocks * BLOCK_UNITS, dtype=i32)
    e = jnp.minimum(jnp.sum((block_end[None, :] * BLOCK_UNITS <= g[:, None]).astype(i32), axis=1),
                    N_EXPERTS - 1)
    e_hot = (e[:, None] == experts[None, :]).astype(i32)
    local = g - _pick(e_hot, region_first)
    valid = (local < _pick(e_hot, region_units)) & (g < n_used[0] * BLOCK_UNITS)
    ends = _pick(e_hot, run_end.T)
    tile = jnp.minimum(jnp.sum((ends <= local[:, None]).astype(i32), axis=1), n_tiles - 1)
    t_hot = (tile[:, None] == tiles[None, :]).astype(i32)
    src = (tile * TILE_UNITS + jnp.sum(_pick(t_hot, tile_first - run_first) * e_hot, axis=1) + local)
    unit_src = jnp.where(valid, src, 0).astype(i32)

    j = jnp.arange(TILE_UNITS, dtype=i32)
    ej = jnp.minimum(jnp.sum((tile_end[:, None, :] <= j[None, :, None]).astype(i32), axis=2),
                     N_EXPERTS - 1)
    ej_hot = (ej[:, :, None] == experts[None, None, :]).astype(i32)
    used = j[None, :] < jnp.sum(units, axis=1)[:, None]
    base = region_first[None, :] + run_first - tile_first
    dst = jnp.sum(ej_hot * base[:, None, :], axis=2) + j[None, :]
    tile_units = jnp.where(used, dst, 0).astype(i32).reshape(-1)
    owns = region_blocks > 0
    expert_order = _excl_cumsum(owns.astype(i32), 0)
    later = (experts[None, :] > experts[:, None]) & owns[None, :]
    expert_next = jnp.min(jnp.where(later, experts[None, :], N_EXPERTS), axis=1)
    expert_next = jnp.where(expert_next < N_EXPERTS, expert_next, -1).astype(i32)
    return (block_expert.astype(i32), n_used.astype(i32), unit_src, tile_units,
            expert_order.astype(i32), expert_next)


def _layer(x, ln1_w, w_in, conv_w, conv_b, w_rg_a, b_rg_a, w_rg_i, b_rg_i, rg_lambda, b_forget,
           q_norm_w, k_norm_w, rec_out_norm_w, attn_out_norm_w, w_out, ln2_w, w_router, b_router,
           w_gate_up, b_gate_up, w_down, b_down):
    batch, seq, _ = x.shape
    t = batch * seq
    x2d = x.reshape(t, D_MODEL)

    o_f = N_PROJ + 3 * D_ATT
    w_t = w_in.T.astype(BF16)
    w_ft = jnp.pad(w_t[o_f:], ((0, LANES - N_HEADS), (0, 0)))
    scale = HEAD_DIM ** -0.5
    u, g, qp, kp, vp = _in_proj(
        x2d, ln1_w.reshape(1, D_MODEL), w_t, w_ft, _pad_lanes(b_forget).reshape(LANES, 1),
        (q_norm_w * scale).reshape(HEAD_DIM, 1), k_norm_w.reshape(HEAD_DIM, 1), batch, seq)

    rec_vec = jnp.concatenate(
        [conv_w, jnp.stack([conv_b, b_rg_a, b_rg_i, rg_lambda, rec_out_norm_w])])
    rec_n = _rglru(u.reshape(batch, seq, D_REC), g.reshape(batch, seq, D_REC), rec_vec,
                   _gate_matrix(w_rg_a, w_rg_i).astype(BF16))

    att = _attention(qp, kp, vp)

    rw = jnp.pad(w_router.T, ((0, LANES - N_EXPERTS), (0, 0)))
    rw_hi = rw.astype(BF16)
    rw_lo = (rw - rw_hi.astype(F32)).astype(BF16)
    x1, xs, meta, units = _out_proj(
        x2d, rec_n.reshape(t, D_REC), att.reshape(t, D_ATT), attn_out_norm_w.reshape(1, D_ATT),
        w_out.astype(BF16), ln2_w.reshape(1, D_MODEL),
        rw_hi, rw_lo, _pad_lanes(b_router).reshape(LANES, 1))

    n_tiles = t // TM_OUT
    max_units = t * TOP_K // UNIT + n_tiles * N_EXPERTS
    n_blocks = -(-max_units // BLOCK_UNITS) + N_EXPERTS
    block_expert, n_used, unit_src, tile_units, expert_order, expert_next = _dispatch_tables(
        units[:, 0, :N_EXPERTS].astype(jnp.int32), n_blocks)

    ys = _experts(block_expert, n_used, unit_src, expert_order, expert_next, xs, w_gate_up, b_gate_up,
                  w_down, b_down)
    out = _combine(tile_units, x1, meta, ys)
    return out.reshape(batch, seq, D_MODEL)


def kernel(x, ln1_w, w_in, conv_w, conv_b, w_rg_a, b_rg_a, w_rg_i, b_rg_i, rg_lambda, b_forget,
           q_norm_w, k_norm_w, rec_out_norm_w, attn_out_norm_w, w_out, ln2_w, w_router, b_router,
           w_gate_up, b_gate_up, w_down, b_down):
    for l in range(ln1_w.shape[0]):
        x = _layer(x, ln1_w[l], w_in[l], conv_w[l], conv_b[l], w_rg_a[l], b_rg_a[l], w_rg_i[l],
                   b_rg_i[l], rg_lambda[l], b_forget[l], q_norm_w[l], k_norm_w[l], rec_out_norm_w[l],
                   attn_out_norm_w[l], w_out[l], ln2_w[l], w_router[l], b_router[l], w_gate_up[l],
                   b_gate_up[l], w_down[l], b_down[l])
    return x
```

```python
import functools

import jax
import jax.numpy as jnp
from jax import lax
from jax.experimental import pallas as pl
from jax.experimental.pallas import tpu as pltpu

D_MODEL = 1024
D_REC = 512
REC_BLOCK = 64
CONV_WIDTH = 4
RGLRU_C = 8.0
D_ATT = 512
HEAD_DIM = 64
N_HEADS = 8
N_EXPERTS = 32
TOP_K = 4
D_EXPERT = 1024
SWIGLU_LIMIT = 7.0
SWIGLU_ALPHA = 1.702
EPS = 1e-6

LANES = 128
SLAB = LANES
VMEM_LIMIT = 56 * 1024 * 1024

F32 = jnp.float32
BF16 = jnp.bfloat16

O_U = 0
O_G = O_U + D_REC
N_PROJ = O_G + D_REC

L_C = HEAD_DIM
L_ONE = HEAD_DIM + 3

TM_IN = 512
TS_REC = 512
SCAN_WAYS = 8
TQ = 512
TK = 512
HEADS_PER_STEP = 8
TM_OUT = 512
TM_E = 512

UNIT = 16
TILE_SLOTS = -(-(TM_OUT * TOP_K + N_EXPERTS * (UNIT - 1)) // LANES) * LANES
TILE_UNITS = TILE_SLOTS // UNIT
BLOCK_UNITS = TM_E // UNIT
SLOT_CHUNK = 512


def _split3(v):
    hi = v.astype(BF16).astype(F32)
    r = v - hi
    mid = r.astype(BF16).astype(F32)
    lo = (r - mid).astype(BF16).astype(F32)
    return hi, mid, lo


def _dot(a, b):
    return jnp.dot(a, b, preferred_element_type=F32)


def _sigmoid(x):
    return 0.5 * jnp.tanh(0.5 * x) + 0.5


def _in_proj_kernel(x_ref, ln_ref, w_ref, wqt_ref, wkt_ref, wvt_ref, wft_ref, bf_ref, qw_ref, kw_ref,
                    u_ref, g_ref, q_ref, k_ref, v_ref, carry_ref, *, tiles_per_seq):
    i = pl.program_id(0)
    tm = x_ref.shape[0]

    @pl.when(i % tiles_per_seq == 0)
    def _():
        carry_ref[...] = jnp.zeros_like(carry_ref)

    x = x_ref[...]
    ms = jnp.mean(x * x, axis=-1, keepdims=True)
    h = (x * lax.rsqrt(ms + EPS) * ln_ref[...]).astype(BF16)
    nt = (((1,), (1,)), ((), ()))

    f_t = lax.dot_general(wft_ref[...], h, nt, preferred_element_type=F32) + bf_ref[...]
    logf_t = jnp.minimum(f_t, 0.0) - jnp.log1p(jnp.exp(-jnp.abs(f_t)))
    row = lax.broadcasted_iota(jnp.int32, (tm, tm), 0)
    col = lax.broadcasted_iota(jnp.int32, (tm, tm), 1)
    upto = (row <= col).astype(BF16)
    hi, mid, lo = _split3(logf_t)
    c_t = (_dot(hi.astype(BF16), upto) + _dot(mid.astype(BF16), upto)
           + _dot(lo.astype(BF16), upto) + carry_ref[...])
    carry_ref[...] = c_t[:, tm - 1:tm]

    qt_all = lax.dot_general(wqt_ref[...], h, nt, preferred_element_type=F32)
    kt_all = lax.dot_general(wkt_ref[...], h, nt, preferred_element_type=F32)
    spare = lax.broadcasted_iota(jnp.int32, (SLAB - HEAD_DIM, tm), 0) + HEAD_DIM
    qw = qw_ref[...]
    kw = kw_ref[...]

    def normed(rows, weight):
        inv = lax.rsqrt(jnp.sum(rows * rows, axis=0, keepdims=True) * (1.0 / HEAD_DIM) + EPS)
        return rows * inv * weight

    for hd in range(N_HEADS):
        rows = slice(hd * HEAD_DIM, (hd + 1) * HEAD_DIM)
        c_hi, c_mid, c_lo = _split3(c_t[hd:hd + 1, :])
        q_extra = jnp.where(spare == L_C, c_hi, 0.0)
        q_extra = jnp.where(spare == L_C + 1, c_mid, q_extra)
        q_extra = jnp.where(spare == L_C + 2, c_lo, q_extra)
        q_extra = jnp.where((spare >= L_ONE) & (spare < L_ONE + 3), 1.0, q_extra)
        k_extra = jnp.where(spare < L_C + 3, 1.0, 0.0)
        k_extra = jnp.where(spare == L_ONE, -c_hi, k_extra)
        k_extra = jnp.where(spare == L_ONE + 1, -c_mid, k_extra)
        k_extra = jnp.where(spare == L_ONE + 2, -c_lo, k_extra)
        q_ref[0, hd] = jnp.concatenate([normed(qt_all[rows, :], qw), q_extra], axis=0).T.astype(BF16)
        k_ref[0, hd] = jnp.concatenate([normed(kt_all[rows, :], kw), k_extra], axis=0).T.astype(BF16)

    vt_all = lax.dot_general(wvt_ref[...], h, (((1,), (1,)), ((), ())), preferred_element_type=F32)
    spare = lax.broadcasted_iota(jnp.int32, (SLAB - HEAD_DIM, tm), 0)
    ones_row = jnp.where(spare == 0, 1.0, 0.0)
    for hd in range(N_HEADS):
        v_ref[0, hd] = jnp.concatenate(
            [vt_all[hd * HEAD_DIM:(hd + 1) * HEAD_DIM, :], ones_row], axis=0).astype(BF16)
    u_ref[...] = lax.dot_general(h, w_ref[O_U:O_U + D_REC, :], nt, preferred_element_type=F32)
    g_ref[...] = lax.dot_general(h, w_ref[O_G:O_G + D_REC, :], nt, preferred_element_type=F32)


def _in_proj(x2d, ln1_w, w_t, w_ft, bf_col, qw_col, kw_col, batch, seq):
    t = x2d.shape[0]
    rows = lambda first, n: pl.BlockSpec((n, D_MODEL), lambda i: (first // n, 0))
    o_q = N_PROJ
    o_k = o_q + D_ATT
    o_v = o_k + D_ATT
    tiles_per_seq = seq // TM_IN
    slab_shape = jax.ShapeDtypeStruct((batch, N_HEADS, seq, SLAB), BF16)
    slab_spec = pl.BlockSpec((1, N_HEADS, TM_IN, SLAB),
                             lambda i: (i // tiles_per_seq, 0, i % tiles_per_seq, 0))
    slab_t_shape = jax.ShapeDtypeStruct((batch, N_HEADS, SLAB, seq), BF16)
    slab_t_spec = pl.BlockSpec((1, N_HEADS, SLAB, TM_IN),
                               lambda i: (i // tiles_per_seq, 0, 0, i % tiles_per_seq))
    row_spec = pl.BlockSpec((TM_IN, D_REC), lambda i: (i, 0))
    const = lambda shape: pl.BlockSpec(shape, lambda i: (0,) * len(shape))
    return pl.pallas_call(
        functools.partial(_in_proj_kernel, tiles_per_seq=tiles_per_seq),
        out_shape=(jax.ShapeDtypeStruct((t, D_REC), F32), jax.ShapeDtypeStruct((t, D_REC), F32),
                   slab_shape, slab_shape, slab_t_shape),
        grid=(t // TM_IN,),
        in_specs=[pl.BlockSpec((TM_IN, D_MODEL), lambda i: (i, 0)),
                  const((1, D_MODEL)), rows(0, N_PROJ), rows(o_q, D_ATT),
                  rows(o_k, D_ATT), rows(o_v, D_ATT), const((LANES, D_MODEL)),
                  const((LANES, 1)), const((HEAD_DIM, 1)), const((HEAD_DIM, 1))],
        out_specs=(row_spec, row_spec, slab_spec, slab_spec, slab_t_spec),
        scratch_shapes=[pltpu.VMEM((LANES, 1), F32)],
        compiler_params=pltpu.CompilerParams(dimension_semantics=("arbitrary",),
                                             vmem_limit_bytes=VMEM_LIMIT),
        name="in_proj",
    )(x2d, ln1_w, w_t, w_t, w_t, w_t, w_ft, bf_col, qw_col, kw_col)


R_CONV_B = CONV_WIDTH
R_B_A = CONV_WIDTH + 1
R_B_I = CONV_WIDTH + 2
R_LAMBDA = CONV_WIDTH + 3
R_NORM_W = CONV_WIDTH + 4
N_REC_ROWS = CONV_WIDTH + 5


def _rglru_kernel(u_ref, g_ref, vec_ref, wg_ref, o_ref, ext_ref, a_ref, b_ref, h_ref, hcar_ref):
    s = pl.program_id(1)
    ts = u_ref.shape[1]
    pad = 8
    row = lambda r: vec_ref[r:r + 1, :]

    @pl.when(s == 0)
    def _():
        ext_ref[0:pad, :] = jnp.zeros((pad, D_REC), F32)
        hcar_ref[...] = jnp.zeros_like(hcar_ref)

    @pl.when(s != 0)
    def _():
        ext_ref[0:pad, :] = ext_ref[ts:ts + pad, :]

    u = u_ref[0]
    ext_ref[pad:pad + ts, :] = u
    xc = row(R_CONV_B) + u * row(CONV_WIDTH - 1)
    for tap in range(CONV_WIDTH - 1):
        d = CONV_WIDTH - 1 - tap
        xc = xc + ext_ref[pad - d:pad - d + ts, :] * row(tap)

    gates = _dot(xc.astype(BF16), wg_ref[...])
    r = _sigmoid(gates[:, :D_REC] + row(R_B_A))
    ig = _sigmoid(gates[:, D_REC:] + row(R_B_I))
    nlam = -row(R_LAMBDA)
    softplus = jnp.maximum(nlam, 0.0) + jnp.log1p(jnp.exp(-jnp.abs(nlam)))
    log_a = -RGLRU_C * r * softplus
    a = jnp.exp(log_a)
    b = jnp.sqrt(1.0 - jnp.exp(2.0 * log_a)) * (ig * xc)
    sub = ts // SCAN_WAYS
    side_by_side = lambda v: jnp.swapaxes(v.reshape(SCAN_WAYS, sub, D_REC), 0, 1)
    a_ref[...] = side_by_side(a)
    b_ref[...] = side_by_side(b)
    h = jnp.zeros((SCAN_WAYS, D_REC), F32)
    p = jnp.ones((SCAN_WAYS, D_REC), F32)
    for r in range(sub):
        a_r = a_ref[r]
        h = a_r * h + b_ref[r]
        p = a_r * p
        h_ref[r] = h
        a_ref[r] = p
    entering = hcar_ref[...]
    enter = []
    for w in range(SCAN_WAYS):
        enter.append(entering)
        entering = h[w:w + 1, :] + p[w:w + 1, :] * entering
    hcar_ref[...] = entering
    h_all = h_ref[...] + a_ref[...] * jnp.concatenate(enter, axis=0)[None]
    h_all = jnp.swapaxes(h_all, 0, 1).reshape(ts, D_REC)

    rec = h_all * jax.nn.gelu(g_ref[0])
    ms = jnp.mean(rec * rec, axis=-1, keepdims=True)
    o_ref[0] = (rec * lax.rsqrt(ms + EPS) * row(R_NORM_W)).astype(BF16)


def _rglru(u, g, rec_vec, w_gates):
    batch, seq, _ = u.shape
    seq_spec = pl.BlockSpec((1, TS_REC, D_REC), lambda b, s: (b, s, 0))
    const = lambda shape: pl.BlockSpec(shape, lambda b, s: (0,) * len(shape))
    return pl.pallas_call(
        _rglru_kernel,
        out_shape=jax.ShapeDtypeStruct((batch, seq, D_REC), BF16),
        grid=(batch, seq // TS_REC),
        in_specs=[seq_spec, seq_spec, const((N_REC_ROWS, D_REC)), const((D_REC, 2 * D_REC))],
        out_specs=seq_spec,
        scratch_shapes=[pltpu.VMEM((TS_REC + 8, D_REC), F32)]
        + [pltpu.VMEM((TS_REC // SCAN_WAYS, SCAN_WAYS, D_REC), F32)] * 3
        + [pltpu.VMEM((1, D_REC), F32)],
        compiler_params=pltpu.CompilerParams(dimension_semantics=("arbitrary", "arbitrary"),
                                             vmem_limit_bytes=VMEM_LIMIT),
        name="rglru",
    )(u, g, rec_vec, w_gates)


def _attn_kernel(q_ref, k_ref, v_ref, o_ref):
    qi = pl.program_id(2)
    hb = q_ref.shape[1]
    tq = q_ref.shape[2]
    qs = [q_ref[0, h] for h in range(hb)]

    def step(j, carry, masked):
        start = pl.multiple_of(j * TK, TK)
        scores = []
        for h in range(hb):
            kj = k_ref[0, h, pl.ds(start, TK), :]
            s = lax.dot_general(kj, qs[h], (((1,), (1,)), ((), ())), preferred_element_type=F32)
            if masked:
                key = lax.broadcasted_iota(jnp.int32, (TK, tq), 0)
                qry = lax.broadcasted_iota(jnp.int32, (TK, tq), 1)
                s = jnp.where(qry >= key, s, -jnp.inf)
            scores.append(s)
        m_news = [jnp.maximum(carry[h][0], jnp.max(scores[h], axis=0, keepdims=True))
                  for h in range(hb)]
        probs = [jnp.exp(scores[h] - m_news[h]).astype(BF16) for h in range(hb)]
        out = []
        for h in range(hb):
            m, acc = carry[h]
            vj = v_ref[0, h, :, pl.ds(start, TK)]
            out.append((m_news[h], jnp.exp(m - m_news[h]) * acc + _dot(vj, probs[h])))
        return tuple(out)

    init = tuple((jnp.full((1, tq), -jnp.inf, F32), jnp.zeros((SLAB, tq), F32)) for _ in range(hb))
    carry = lax.fori_loop(0, qi, lambda j, c: step(j, c, False), init)
    carry = step(qi, carry, True)
    outs = [carry[h][1][:HEAD_DIM, :] / carry[h][1][HEAD_DIM:HEAD_DIM + 1, :] for h in range(hb)]
    for pair in range(hb // 2):
        both = jnp.concatenate([outs[2 * pair], outs[2 * pair + 1]], axis=0)
        o_ref[0, :, pair * LANES:(pair + 1) * LANES] = both.T


def _attention(qp, kp, vp):
    batch, _, seq, _ = kp.shape
    hb = HEADS_PER_STEP
    kv_spec = pl.BlockSpec((1, hb, seq, SLAB), lambda b, h, i: (b, h, 0, 0))
    vt_spec = pl.BlockSpec((1, hb, SLAB, seq), lambda b, h, i: (b, h, 0, 0))
    return pl.pallas_call(
        _attn_kernel,
        out_shape=jax.ShapeDtypeStruct((batch, seq, D_ATT), F32),
        grid=(batch, N_HEADS // hb, seq // TQ),
        in_specs=[pl.BlockSpec((1, hb, TQ, SLAB), lambda b, h, i: (b, h, i, 0)), kv_spec, vt_spec],
        out_specs=pl.BlockSpec((1, TQ, hb * HEAD_DIM), lambda b, h, i: (b, i, h)),
        compiler_params=pltpu.CompilerParams(
            dimension_semantics=("arbitrary", "arbitrary", "arbitrary"),
            vmem_limit_bytes=VMEM_LIMIT),
        name="attn",
    )(qp, kp, vp)


def _out_proj_kernel(x_ref, rec_ref, att_ref, aw_ref, wr_ref, wa_ref, ln_ref, rwh_ref, rwl_ref, rb_ref,
                     x1_ref, xs_ref, meta_ref, units_ref):
    tm = x_ref.shape[0]

    att = att_ref[...]
    ms = jnp.sum(att * att, axis=-1, keepdims=True) * (1.0 / D_ATT)
    att_n = (att * lax.rsqrt(ms + EPS) * aw_ref[...]).astype(BF16)
    x1 = x_ref[...] + _dot(rec_ref[...], wr_ref[...]) + _dot(att_n, wa_ref[...])
    x1_ref[...] = x1

    ms2 = jnp.mean(x1 * x1, axis=-1, keepdims=True)
    h2 = x1 * lax.rsqrt(ms2 + EPS) * ln_ref[...]

    h_hi = h2.astype(BF16)
    h_lo = (h2 - h_hi.astype(F32)).astype(BF16)
    nt = (((1,), (1,)), ((), ()))
    logits = (lax.dot_general(rwh_ref[...], h_hi, nt, preferred_element_type=F32)
              + lax.dot_general(rwh_ref[...], h_lo, nt, preferred_element_type=F32)
              + lax.dot_general(rwl_ref[...], h_hi, nt, preferred_element_type=F32) + rb_ref[...])
    expert = lax.broadcasted_iota(jnp.int32, (LANES, tm), 0)
    expert_f = expert.astype(F32)
    l = jnp.where(expert < N_EXPERTS, logits, -jnp.inf)

    vals, idxs, sels = [], [], []
    for _ in range(TOP_K):
        m = jnp.max(l, axis=0, keepdims=True)
        idx = jnp.min(jnp.where(l == m, expert_f, float(LANES)), axis=0, keepdims=True)
        sel = expert_f == idx
        vals.append(m)
        idxs.append(idx)
        sels.append(sel)
        l = jnp.where(sel, -jnp.inf, l)
    exps = [jnp.exp(v - vals[0]) for v in vals]
    denom = exps[0] + exps[1] + exps[2] + exps[3]

    onehot = jnp.zeros((LANES, tm), F32)
    for sel in sels:
        onehot = jnp.where(sel, 1.0, onehot)
    row = lax.broadcasted_iota(jnp.int32, (tm, tm), 0)
    col = lax.broadcasted_iota(jnp.int32, (tm, tm), 1)
    earlier_token = (row < col).astype(BF16)
    rank = _dot(onehot.astype(BF16), earlier_token)
    count = jnp.sum(onehot, axis=1, keepdims=True)
    units = jnp.floor((count + (UNIT - 1)) * (1.0 / UNIT))
    e_row = lax.broadcasted_iota(jnp.int32, (LANES, LANES), 0)
    e_col = lax.broadcasted_iota(jnp.int32, (LANES, LANES), 1)
    earlier_expert = (e_col < e_row).astype(BF16)
    units_wide = jnp.broadcast_to(units, (LANES, LANES))
    first_unit = _dot(earlier_expert, units_wide.astype(BF16))[:, 0:1]
    units_ref[0] = units_wide.T[0:8, :]
    place = first_unit * float(UNIT) + rank

    meta_t = jnp.zeros((LANES, tm), F32)
    for k in range(TOP_K):
        slot = jnp.sum(jnp.where(sels[k], place, 0.0), axis=0, keepdims=True)
        meta_t = jnp.where(expert == k, idxs[k], meta_t)
        meta_t = jnp.where(expert == TOP_K + k, exps[k] / denom, meta_t)
        meta_t = jnp.where(expert == 2 * TOP_K + k, slot, meta_t)
    meta_ref[...] = meta_t.T

    slot_rows = [meta_t[2 * TOP_K + k:2 * TOP_K + k + 1, :].astype(jnp.int32) for k in range(TOP_K)]
    for c in range(TILE_SLOTS // SLOT_CHUNK):
        slot_id = lax.broadcasted_iota(jnp.int32, (SLOT_CHUNK, tm), 0) + c * SLOT_CHUNK
        hit = jnp.zeros((SLOT_CHUNK, tm), F32)
        for slot_row in slot_rows:
            hit = jnp.where(slot_id == slot_row, 1.0, hit)
        xs_ref[c * SLOT_CHUNK:(c + 1) * SLOT_CHUNK, :] = _dot(hit.astype(BF16), h_hi).astype(BF16)


def _out_proj(x2d, rec_n, att, aw_slab, w_out, ln2_w, rw_hi, rw_lo, rb_pad):
    assert D_REC == D_ATT
    t = x2d.shape[0]
    n_tiles = t // TM_OUT
    rows = lambda n: pl.BlockSpec((TM_OUT, n), lambda i: (i, 0))
    const = lambda shape: pl.BlockSpec(shape, lambda i: (0,) * len(shape))
    return pl.pallas_call(
        _out_proj_kernel,
        out_shape=(jax.ShapeDtypeStruct((t, D_MODEL), F32),
                   jax.ShapeDtypeStruct((n_tiles * TILE_SLOTS, D_MODEL), BF16),
                   jax.ShapeDtypeStruct((t, LANES), F32),
                   jax.ShapeDtypeStruct((n_tiles, 8, LANES), F32)),
        grid=(n_tiles,),
        in_specs=[rows(D_MODEL), rows(D_REC), rows(D_ATT), const((1, D_ATT)),
                  pl.BlockSpec((D_REC, D_MODEL), lambda i: (0, 0)),
                  pl.BlockSpec((D_ATT, D_MODEL), lambda i: (1, 0)), const((1, D_MODEL)),
                  const((LANES, D_MODEL)), const((LANES, D_MODEL)), const((LANES, 1))],
        out_specs=(rows(D_MODEL), pl.BlockSpec((TILE_SLOTS, D_MODEL), lambda i: (i, 0)), rows(LANES),
                   pl.BlockSpec((1, 8, LANES), lambda i: (i, 0, 0))),
        compiler_params=pltpu.CompilerParams(dimension_semantics=("arbitrary",),
                                             vmem_limit_bytes=VMEM_LIMIT),
        name="out_proj",
    )(x2d, rec_n, att, aw_slab, w_out, w_out, ln2_w, rw_hi, rw_lo, rb_pad)


def _start_units(unit_ref, base, n_units, src_hbm, dst, sem):
    def body(j, carry):
        src_row = pl.multiple_of(unit_ref[base + j] * UNIT, UNIT)
        dst_row = pl.multiple_of(j * UNIT, UNIT)
        pltpu.make_async_copy(src_hbm.at[pl.ds(src_row, UNIT), :],
                              dst.at[pl.ds(dst_row, UNIT), :], sem).start()
        return carry
    lax.fori_loop(0, n_units, body, 0, unroll=8)


def _wait_units(n_units, src_hbm, dst, sem):
    pltpu.make_async_copy(src_hbm.at[pl.ds(0, n_units * UNIT), :], dst, sem).wait()


def _experts_kernel(be_ref, nblk_ref, unit_ref, order_ref, next_ref, xs_hbm, wgu_hbm, bgu_ref, wd_hbm,
                    bd_ref, ys_ref, xbuf, wgu_f32, wd_f32, wgu_bf, wd_bf, sem, wsem):
    b = pl.program_id(0)
    nblk = nblk_ref[0]
    slot = b % 2
    e = be_ref[b]

    def weight_copies(expert, wslot):
        return (pltpu.make_async_copy(wgu_hbm.at[expert], wgu_f32.at[wslot], wsem.at[wslot]),
                pltpu.make_async_copy(wd_hbm.at[expert], wd_f32.at[wslot], wsem.at[wslot]))

    @pl.when(b == 0)
    def _():
        _start_units(unit_ref, 0, BLOCK_UNITS, xs_hbm, xbuf.at[0], sem.at[0])
        for copy in weight_copies(e, 0):
            copy.start()

    @pl.when(b + 1 < nblk)
    def _():
        _start_units(unit_ref, (b + 1) * BLOCK_UNITS, BLOCK_UNITS, xs_hbm, xbuf.at[1 - slot],
                     sem.at[1 - slot])

    new_expert = jnp.logical_or(b == 0, e != be_ref[jnp.maximum(b - 1, 0)])

    @pl.when(jnp.logical_and(b < nblk, new_expert))
    def _():
        wslot = order_ref[e] % 2
        for copy in weight_copies(e, wslot):
            copy.wait()
        wgu_bf[...] = wgu_f32[wslot].astype(BF16)
        wd_bf[...] = wd_f32[wslot].astype(BF16)
        nxt = next_ref[e]

        @pl.when(nxt >= 0)
        def _():
            for copy in weight_copies(nxt, 1 - wslot):
                copy.start()

    @pl.when(b < nblk)
    def _():
        _wait_units(BLOCK_UNITS, xs_hbm, xbuf.at[slot], sem.at[slot])
        gu = _dot(xbuf[slot], wgu_bf[...]) + bgu_ref[0]
        g = jnp.minimum(gu[:, :D_EXPERT], SWIGLU_LIMIT)
        u = jnp.clip(gu[:, D_EXPERT:], -SWIGLU_LIMIT, SWIGLU_LIMIT)
        act = (u + 1.0) * (g * _sigmoid(SWIGLU_ALPHA * g))
        ys_ref[...] = (_dot(act.astype(BF16), wd_bf[...]) + bd_ref[0]).astype(BF16)

    @pl.when(b >= nblk)
    def _():
        ys_ref[...] = jnp.zeros_like(ys_ref)


def _experts(block_expert, n_blocks_used, unit_src, expert_order, expert_next, xs, w_gate_up,
             b_gate_up, w_down, b_down):
    n_blocks = block_expert.shape[0]
    bias_map = lambda b, be, nb, us, eo, en: (be[b], 0, 0)
    grid_spec = pltpu.PrefetchScalarGridSpec(
        num_scalar_prefetch=5,
        grid=(n_blocks,),
        in_specs=[pl.BlockSpec(memory_space=pl.ANY),
                  pl.BlockSpec(memory_space=pl.ANY),
                  pl.BlockSpec((1, 1, 2 * D_EXPERT), bias_map),
                  pl.BlockSpec(memory_space=pl.ANY),
                  pl.BlockSpec((1, 1, D_MODEL), bias_map)],
        out_specs=pl.BlockSpec((TM_E, D_MODEL), lambda b, be, nb, us, eo, en: (b, 0)),
        scratch_shapes=[pltpu.VMEM((2, TM_E, D_MODEL), BF16),
                        pltpu.VMEM((2, D_MODEL, 2 * D_EXPERT), F32),
                        pltpu.VMEM((2, D_EXPERT, D_MODEL), F32),
                        pltpu.VMEM((D_MODEL, 2 * D_EXPERT), BF16),
                        pltpu.VMEM((D_EXPERT, D_MODEL), BF16),
                        pltpu.SemaphoreType.DMA((2,)),
                        pltpu.SemaphoreType.DMA((2,))],
    )
    return pl.pallas_call(
        _experts_kernel,
        out_shape=jax.ShapeDtypeStruct((n_blocks * TM_E, D_MODEL), BF16),
        grid_spec=grid_spec,
        compiler_params=pltpu.CompilerParams(dimension_semantics=("arbitrary",),
                                             vmem_limit_bytes=VMEM_LIMIT),
        name="experts",
    )(block_expert, n_blocks_used, unit_src, expert_order, expert_next, xs, w_gate_up,
      b_gate_up.reshape(N_EXPERTS, 1, 2 * D_EXPERT), w_down, b_down.reshape(N_EXPERTS, 1, D_MODEL))


def _combine_kernel(unit_ref, x1_ref, meta_ref, ys_hbm, o_ref, buf, sem):
    i = pl.program_id(0)
    n = pl.num_programs(0)
    slot = i % 2
    tm = x1_ref.shape[0]

    @pl.when(i == 0)
    def _():
        _start_units(unit_ref, 0, TILE_UNITS, ys_hbm, buf.at[0], sem.at[0])

    @pl.when(i + 1 < n)
    def _():
        _start_units(unit_ref, (i + 1) * TILE_UNITS, TILE_UNITS, ys_hbm, buf.at[1 - slot],
                     sem.at[1 - slot])

    meta = meta_ref[...]
    slots = [meta[:, 2 * TOP_K + k:2 * TOP_K + k + 1].astype(jnp.int32) for k in range(TOP_K)]
    gates = [meta[:, TOP_K + k:TOP_K + k + 1] for k in range(TOP_K)]
    _wait_units(TILE_UNITS, ys_hbm, buf.at[slot], sem.at[slot])

    acc = x1_ref[...]
    for c in range(TILE_SLOTS // SLOT_CHUNK):
        first = c * SLOT_CHUNK
        col = lax.broadcasted_iota(jnp.int32, (tm, SLOT_CHUNK), 1) + first
        weight = jnp.zeros((tm, SLOT_CHUNK), F32)
        for k in range(TOP_K):
            weight = jnp.where(col == slots[k], gates[k], weight)
        acc = acc + _dot(weight.astype(BF16), buf[slot, first:first + SLOT_CHUNK, :])
    o_ref[...] = acc


def _combine(tile_units, x1, meta, ys):
    t = x1.shape[0]
    grid_spec = pltpu.PrefetchScalarGridSpec(
        num_scalar_prefetch=1,
        grid=(t // TM_OUT,),
        in_specs=[pl.BlockSpec((TM_OUT, D_MODEL), lambda i, us: (i, 0)),
                  pl.BlockSpec((TM_OUT, LANES), lambda i, us: (i, 0)),
                  pl.BlockSpec(memory_space=pl.ANY)],
        out_specs=pl.BlockSpec((TM_OUT, D_MODEL), lambda i, us: (i, 0)),
        scratch_shapes=[pltpu.VMEM((2, TILE_SLOTS, D_MODEL), BF16),
                        pltpu.SemaphoreType.DMA((2,))],
    )
    return pl.pallas_call(
        _combine_kernel,
        out_shape=jax.ShapeDtypeStruct((t, D_MODEL), F32),
        grid_spec=grid_spec,
        compiler_params=pltpu.CompilerParams(dimension_semantics=("arbitrary",),
                                             vmem_limit_bytes=VMEM_LIMIT),
        name="combine",
    )(tile_units, x1, meta, ys)


def _gate_matrix(w_a, w_i):
    g = w_a.shape[0]
    eye = jnp.eye(g, dtype=w_a.dtype)
    both = jnp.stack([w_a, w_i])
    return jnp.einsum('sgij,gh->gishj', both, eye).reshape(g * REC_BLOCK, 2 * g * REC_BLOCK)


def _pad_lanes(v, n=LANES):
    return jnp.pad(v, (0, n - v.shape[0])).reshape(1, n)


def _excl_cumsum(a, axis):
    return jnp.cumsum(a, axis=axis) - a


def _pick(onehot, table):
    if table.ndim == 1:
        return jnp.sum(onehot * table[None, :], axis=1)
    return jnp.sum(onehot[:, :, None] * table[None, :, :], axis=1)


def _dispatch_tables(units, n_blocks):
    n_tiles = units.shape[0]
    i32 = jnp.int32
    tile_first = _excl_cumsum(units, 1)
    tile_end = tile_first + units
    run_first = _excl_cumsum(units, 0)
    run_end = run_first + units
    region_units = jnp.sum(units, axis=0)
    region_blocks = (region_units + BLOCK_UNITS - 1) // BLOCK_UNITS
    block_end = jnp.cumsum(region_blocks)
    region_first = (block_end - region_blocks) * BLOCK_UNITS
    n_used = block_end[-1:]
    experts = jnp.arange(N_EXPERTS, dtype=i32)
    tiles = jnp.arange(n_tiles, dtype=i32)

    blocks = jnp.arange(n_blocks, dtype=i32)
    block_expert = jnp.minimum(
        jnp.sum((block_end[None, :] <= blocks[:, None]).astype(i32), axis=1), N_EXPERTS - 1)

    g = jnp.arange(n_blocks * BLOCK_UNITS, dtype=i32).reshape(n_blocks, BLOCK_UNITS)
    e_hot = (block_expert[:, None] == experts[None, :]).astype(i32)
    local = g - _pick(e_hot, region_first)[:, None]
    valid = (local < _pick(e_hot, region_units)[:, None]) & (g < n_used[0] * BLOCK_UNITS)
    ends = _pick(e_hot, run_end.T)
    tile = jnp.minimum(jnp.sum((ends[:, None, :] <= local[:, :, None]).astype(i32), axis=2),
                       n_tiles - 1)
    t_hot = (tile[:, :, None] == tiles[None, None, :]).astype(i32)
    run_shift = _pick(e_hot, (tile_first - run_first).T)
    src = tile * TILE_UNITS + jnp.sum(t_hot * run_shift[:, None, :], axis=2) + local
    unit_src = jnp.where(valid, src, 0).astype(i32).reshape(-1)

    j = jnp.arange(TILE_UNITS, dtype=i32)
    ej = jnp.minimum(jnp.sum((tile_end[:, None, :] <= j[None, :, None]).astype(i32), axis=2),
                     N_EXPERTS - 1)
    ej_hot = (ej[:, :, None] == experts[None, None, :]).astype(i32)
    used = j[None, :] < jnp.sum(units, axis=1)[:, None]
    base = region_first[None, :] + run_first - tile_first
    dst = jnp.sum(ej_hot * base[:, None, :], axis=2) + j[None, :]
    tile_units = jnp.where(used, dst, 0).astype(i32).reshape(-1)
    owns = region_blocks > 0
    expert_order = _excl_cumsum(owns.astype(i32), 0)
    later = (experts[None, :] > experts[:, None]) & owns[None, :]
    expert_next = jnp.min(jnp.where(later, experts[None, :], N_EXPERTS), axis=1)
    expert_next = jnp.where(expert_next < N_EXPERTS, expert_next, -1).astype(i32)
    return (block_expert.astype(i32), n_used.astype(i32), unit_src, tile_units,
            expert_order.astype(i32), expert_next)


def _layer(x, ln1_w, w_in, conv_w, conv_b, w_rg_a, b_rg_a, w_rg_i, b_rg_i, rg_lambda, b_forget,
           q_norm_w, k_norm_w, rec_out_norm_w, attn_out_norm_w, w_out, ln2_w, w_router, b_router,
           w_gate_up, b_gate_up, w_down, b_down):
    batch, seq, _ = x.shape
    t = batch * seq
    x2d = x.reshape(t, D_MODEL)

    o_f = N_PROJ + 3 * D_ATT
    w_t = w_in.T.astype(BF16)
    w_ft = jnp.pad(w_t[o_f:], ((0, LANES - N_HEADS), (0, 0)))
    scale = HEAD_DIM ** -0.5
    u, g, qp, kp, vp = _in_proj(
        x2d, ln1_w.reshape(1, D_MODEL), w_t, w_ft, _pad_lanes(b_forget).reshape(LANES, 1),
        (q_norm_w * scale).reshape(HEAD_DIM, 1), k_norm_w.reshape(HEAD_DIM, 1), batch, seq)

    rec_vec = jnp.concatenate(
        [conv_w, jnp.stack([conv_b, b_rg_a, b_rg_i, rg_lambda, rec_out_norm_w])])
    rec_n = _rglru(u.reshape(batch, seq, D_REC), g.reshape(batch, seq, D_REC), rec_vec,
                   _gate_matrix(w_rg_a, w_rg_i).astype(BF16))

    att = _attention(qp, kp, vp)

    rw = jnp.pad(w_router.T, ((0, LANES - N_EXPERTS), (0, 0)))
    rw_hi = rw.astype(BF16)
    rw_lo = (rw - rw_hi.astype(F32)).astype(BF16)
    x1, xs, meta, units = _out_proj(
        x2d, rec_n.reshape(t, D_REC), att.reshape(t, D_ATT), attn_out_norm_w.reshape(1, D_ATT),
        w_out.astype(BF16), ln2_w.reshape(1, D_MODEL),
        rw_hi, rw_lo, _pad_lanes(b_router).reshape(LANES, 1))

    n_tiles = t // TM_OUT
    max_units = t * TOP_K // UNIT + n_tiles * N_EXPERTS
    n_blocks = -(-max_units // BLOCK_UNITS) + N_EXPERTS
    block_expert, n_used, unit_src, tile_units, expert_order, expert_next = _dispatch_tables(
        units[:, 0, :N_EXPERTS].astype(jnp.int32), n_blocks)

    ys = _experts(block_expert, n_used, unit_src, expert_order, expert_next, xs, w_gate_up, b_gate_up,
                  w_down, b_down)
    out = _combine(tile_units, x1, meta, ys)
    return out.reshape(batch, seq, D_MODEL)


def kernel(x, ln1_w, w_in, conv_w, conv_b, w_rg_a, b_rg_a, w_rg_i, b_rg_i, rg_lambda, b_forget,
           q_norm_w, k_norm_w, rec_out_norm_w, attn_out_norm_w, w_out, ln2_w, w_router, b_router,
           w_gate_up, b_gate_up, w_down, b_down):
    for l in range(ln1_w.shape[0]):
        x = _layer(x, ln1_w[l], w_in[l], conv_w[l], conv_b[l], w_rg_a[l], b_rg_a[l], w_rg_i[l],
                   b_rg_i[l], rg_lambda[l], b_forget[l], q_norm_w[l], k_norm_w[l], rec_out_norm_w[l],
                   attn_out_norm_w[l], w_out[l], ln2_w[l], w_router[l], b_router[l], w_gate_up[l],
                   b_gate_up[l], w_down[l], b_down[l])
    return x
```

```python
import functools

import jax
import jax.numpy as jnp
from jax import lax
from jax.experimental import pallas as pl
from jax.experimental.pallas import tpu as pltpu

D_MODEL = 1024
D_REC = 512
REC_BLOCK = 64
CONV_WIDTH = 4
RGLRU_C = 8.0
D_ATT = 512
HEAD_DIM = 64
N_HEADS = 8
N_EXPERTS = 32
TOP_K = 4
D_EXPERT = 1024
SWIGLU_LIMIT = 7.0
SWIGLU_ALPHA = 1.702
EPS = 1e-6

LANES = 128
SLAB = LANES
VMEM_LIMIT = 56 * 1024 * 1024

F32 = jnp.float32
BF16 = jnp.bfloat16

O_U = 0
O_G = O_U + D_REC
N_PROJ = O_G + D_REC

L_C = HEAD_DIM
L_ONE = HEAD_DIM + 3

TM_IN = 512
TS_REC = 512
SCAN_WAYS = 8
TQ = 512
TK = 512
HEADS_PER_STEP = 8
TM_OUT = 512
TM_E = 512

UNIT = 16
TILE_SLOTS = -(-(TM_OUT * TOP_K + N_EXPERTS * (UNIT - 1)) // LANES) * LANES
TILE_UNITS = TILE_SLOTS // UNIT
BLOCK_UNITS = TM_E // UNIT
SLOT_CHUNK = 512


def _split3(v):
    hi = v.astype(BF16).astype(F32)
    r = v - hi
    mid = r.astype(BF16).astype(F32)
    lo = (r - mid).astype(BF16).astype(F32)
    return hi, mid, lo


def _dot(a, b):
    return jnp.dot(a, b, preferred_element_type=F32)


def _sigmoid(x):
    return 0.5 * jnp.tanh(0.5 * x) + 0.5


def _in_proj_kernel(x_ref, ln_ref, w_ref, wqt_ref, wkt_ref, wvt_ref, wft_ref, bf_ref, qw_ref, kw_ref,
                    u_ref, g_ref, q_ref, k_ref, v_ref, carry_ref, *, tiles_per_seq):
    i = pl.program_id(0)
    tm = x_ref.shape[0]

    @pl.when(i % tiles_per_seq == 0)
    def _():
        carry_ref[...] = jnp.zeros_like(carry_ref)

    x = x_ref[...]
    ms = jnp.mean(x * x, axis=-1, keepdims=True)
    h = (x * lax.rsqrt(ms + EPS) * ln_ref[...]).astype(BF16)
    nt = (((1,), (1,)), ((), ()))

    f_t = lax.dot_general(wft_ref[...], h, nt, preferred_element_type=F32) + bf_ref[...]
    logf_t = jnp.minimum(f_t, 0.0) - jnp.log1p(jnp.exp(-jnp.abs(f_t)))
    row = lax.broadcasted_iota(jnp.int32, (tm, tm), 0)
    col = lax.broadcasted_iota(jnp.int32, (tm, tm), 1)
    upto = (row <= col).astype(BF16)
    hi, mid, lo = _split3(logf_t)
    c_t = (_dot(hi.astype(BF16), upto) + _dot(mid.astype(BF16), upto)
           + _dot(lo.astype(BF16), upto) + carry_ref[...])
    carry_ref[...] = c_t[:, tm - 1:tm]

    qt_all = lax.dot_general(wqt_ref[...], h, nt, preferred_element_type=F32)
    kt_all = lax.dot_general(wkt_ref[...], h, nt, preferred_element_type=F32)
    spare = lax.broadcasted_iota(jnp.int32, (SLAB - HEAD_DIM, tm), 0) + HEAD_DIM
    qw = qw_ref[...]
    kw = kw_ref[...]

    def normed(rows, weight):
        inv = lax.rsqrt(jnp.sum(rows * rows, axis=0, keepdims=True) * (1.0 / HEAD_DIM) + EPS)
        return rows * inv * weight

    for hd in range(N_HEADS):
        rows = slice(hd * HEAD_DIM, (hd + 1) * HEAD_DIM)
        c_hi, c_mid, c_lo = _split3(c_t[hd:hd + 1, :])
        q_extra = jnp.where(spare == L_C, c_hi, 0.0)
        q_extra = jnp.where(spare == L_C + 1, c_mid, q_extra)
        q_extra = jnp.where(spare == L_C + 2, c_lo, q_extra)
        q_extra = jnp.where((spare >= L_ONE) & (spare < L_ONE + 3), 1.0, q_extra)
        k_extra = jnp.where(spare < L_C + 3, 1.0, 0.0)
        k_extra = jnp.where(spare == L_ONE, -c_hi, k_extra)
        k_extra = jnp.where(spare == L_ONE + 1, -c_mid, k_extra)
        k_extra = jnp.where(spare == L_ONE + 2, -c_lo, k_extra)
        q_ref[0, hd] = jnp.concatenate([normed(qt_all[rows, :], qw), q_extra], axis=0).T.astype(BF16)
        k_ref[0, hd] = jnp.concatenate([normed(kt_all[rows, :], kw), k_extra], axis=0).T.astype(BF16)

    vt_all = lax.dot_general(wvt_ref[...], h, (((1,), (1,)), ((), ())), preferred_element_type=F32)
    spare = lax.broadcasted_iota(jnp.int32, (SLAB - HEAD_DIM, tm), 0)
    ones_row = jnp.where(spare == 0, 1.0, 0.0)
    for hd in range(N_HEADS):
        v_ref[0, hd] = jnp.concatenate(
            [vt_all[hd * HEAD_DIM:(hd + 1) * HEAD_DIM, :], ones_row], axis=0).astype(BF16)
    u_ref[...] = lax.dot_general(h, w_ref[O_U:O_U + D_REC, :], nt, preferred_element_type=F32)
    g_ref[...] = lax.dot_general(h, w_ref[O_G:O_G + D_REC, :], nt, preferred_element_type=F32)


def _in_proj(x2d, ln1_w, w_t, w_ft, bf_col, qw_col, kw_col, batch, seq):
    t = x2d.shape[0]
    rows = lambda first, n: pl.BlockSpec((n, D_MODEL), lambda i: (first // n, 0))
    o_q = N_PROJ
    o_k = o_q + D_ATT
    o_v = o_k + D_ATT
    tiles_per_seq = seq // TM_IN
    slab_shape = jax.ShapeDtypeStruct((batch, N_HEADS, seq, SLAB), BF16)
    slab_spec = pl.BlockSpec((1, N_HEADS, TM_IN, SLAB),
                             lambda i: (i // tiles_per_seq, 0, i % tiles_per_seq, 0))
    slab_t_shape = jax.ShapeDtypeStruct((batch, N_HEADS, SLAB, seq), BF16)
    slab_t_spec = pl.BlockSpec((1, N_HEADS, SLAB, TM_IN),
                               lambda i: (i // tiles_per_seq, 0, 0, i % tiles_per_seq))
    row_spec = pl.BlockSpec((TM_IN, D_REC), lambda i: (i, 0))
    const = lambda shape: pl.BlockSpec(shape, lambda i: (0,) * len(shape))
    return pl.pallas_call(
        functools.partial(_in_proj_kernel, tiles_per_seq=tiles_per_seq),
        out_shape=(jax.ShapeDtypeStruct((t, D_REC), F32), jax.ShapeDtypeStruct((t, D_REC), F32),
                   slab_shape, slab_shape, slab_t_shape),
        grid=(t // TM_IN,),
        in_specs=[pl.BlockSpec((TM_IN, D_MODEL), lambda i: (i, 0)),
                  const((1, D_MODEL)), rows(0, N_PROJ), rows(o_q, D_ATT),
                  rows(o_k, D_ATT), rows(o_v, D_ATT), const((LANES, D_MODEL)),
                  const((LANES, 1)), const((HEAD_DIM, 1)), const((HEAD_DIM, 1))],
        out_specs=(row_spec, row_spec, slab_spec, slab_spec, slab_t_spec),
        scratch_shapes=[pltpu.VMEM((LANES, 1), F32)],
        compiler_params=pltpu.CompilerParams(dimension_semantics=("arbitrary",),
                                             vmem_limit_bytes=VMEM_LIMIT),
        name="in_proj",
    )(x2d, ln1_w, w_t, w_t, w_t, w_t, w_ft, bf_col, qw_col, kw_col)


R_CONV_B = CONV_WIDTH
R_B_A = CONV_WIDTH + 1
R_B_I = CONV_WIDTH + 2
R_LAMBDA = CONV_WIDTH + 3
R_NORM_W = CONV_WIDTH + 4
N_REC_ROWS = CONV_WIDTH + 5


def _rglru_kernel(u_ref, g_ref, vec_ref, wg_ref, o_ref, ext_ref, a_ref, b_ref, h_ref, hcar_ref):
    s = pl.program_id(1)
    ts = u_ref.shape[1]
    pad = 8
    row = lambda r: vec_ref[r:r + 1, :]

    @pl.when(s == 0)
    def _():
        ext_ref[0:pad, :] = jnp.zeros((pad, D_REC), F32)
        hcar_ref[...] = jnp.zeros_like(hcar_ref)

    @pl.when(s != 0)
    def _():
        ext_ref[0:pad, :] = ext_ref[ts:ts + pad, :]

    u = u_ref[0]
    ext_ref[pad:pad + ts, :] = u
    xc = row(R_CONV_B) + u * row(CONV_WIDTH - 1)
    for tap in range(CONV_WIDTH - 1):
        d = CONV_WIDTH - 1 - tap
        xc = xc + ext_ref[pad - d:pad - d + ts, :] * row(tap)

    gates = _dot(xc.astype(BF16), wg_ref[...])
    r = _sigmoid(gates[:, :D_REC] + row(R_B_A))
    ig = _sigmoid(gates[:, D_REC:] + row(R_B_I))
    nlam = -row(R_LAMBDA)
    softplus = jnp.maximum(nlam, 0.0) + jnp.log1p(jnp.exp(-jnp.abs(nlam)))
    log_a = -RGLRU_C * r * softplus
    a = jnp.exp(log_a)
    b = jnp.sqrt(1.0 - jnp.exp(2.0 * log_a)) * (ig * xc)
    sub = ts // SCAN_WAYS
    side_by_side = lambda v: jnp.swapaxes(v.reshape(SCAN_WAYS, sub, D_REC), 0, 1)
    a_ref[...] = side_by_side(a)
    b_ref[...] = side_by_side(b)
    h = jnp.zeros((SCAN_WAYS, D_REC), F32)
    p = jnp.ones((SCAN_WAYS, D_REC), F32)
    for r in range(sub):
        a_r = a_ref[r]
        h = a_r * h + b_ref[r]
        p = a_r * p
        h_ref[r] = h
        a_ref[r] = p
    entering = hcar_ref[...]
    enter = []
    for w in range(SCAN_WAYS):
        enter.append(entering)
        entering = h[w:w + 1, :] + p[w:w + 1, :] * entering
    hcar_ref[...] = entering
    h_all = h_ref[...] + a_ref[...] * jnp.concatenate(enter, axis=0)[None]
    h_all = jnp.swapaxes(h_all, 0, 1).reshape(ts, D_REC)

    rec = h_all * jax.nn.gelu(g_ref[0])
    ms = jnp.mean(rec * rec, axis=-1, keepdims=True)
    o_ref[0] = (rec * lax.rsqrt(ms + EPS) * row(R_NORM_W)).astype(BF16)


def _rglru(u, g, rec_vec, w_gates):
    batch, seq, _ = u.shape
    seq_spec = pl.BlockSpec((1, TS_REC, D_REC), lambda b, s: (b, s, 0))
    const = lambda shape: pl.BlockSpec(shape, lambda b, s: (0,) * len(shape))
    return pl.pallas_call(
        _rglru_kernel,
        out_shape=jax.ShapeDtypeStruct((batch, seq, D_REC), BF16),
        grid=(batch, seq // TS_REC),
        in_specs=[seq_spec, seq_spec, const((N_REC_ROWS, D_REC)), const((D_REC, 2 * D_REC))],
        out_specs=seq_spec,
        scratch_shapes=[pltpu.VMEM((TS_REC + 8, D_REC), F32)]
        + [pltpu.VMEM((TS_REC // SCAN_WAYS, SCAN_WAYS, D_REC), F32)] * 3
        + [pltpu.VMEM((1, D_REC), F32)],
        compiler_params=pltpu.CompilerParams(dimension_semantics=("arbitrary", "arbitrary"),
                                             vmem_limit_bytes=VMEM_LIMIT),
        name="rglru",
    )(u, g, rec_vec, w_gates)


def _attn_kernel(q_ref, k_ref, v_ref, o_ref):
    qi = pl.program_id(2)
    hb = q_ref.shape[1]
    tq = q_ref.shape[2]

    def step(j, carry, masked):
        start = pl.multiple_of(j * TK, TK)
        scores = []
        for h in range(hb):
            kj = k_ref[0, h, pl.ds(start, TK), :]
            s = lax.dot_general(kj, q_ref[0, h], (((1,), (1,)), ((), ())),
                                preferred_element_type=F32)
            if masked:
                key = lax.broadcasted_iota(jnp.int32, (TK, tq), 0)
                qry = lax.broadcasted_iota(jnp.int32, (TK, tq), 1)
                s = jnp.where(qry >= key, s, -jnp.inf)
            scores.append(s)
        m_news = [jnp.maximum(carry[h][0], jnp.max(scores[h], axis=0, keepdims=True))
                  for h in range(hb)]
        probs = [jnp.exp(scores[h] - m_news[h]).astype(BF16) for h in range(hb)]
        out = []
        for h in range(hb):
            m, acc = carry[h]
            vj = v_ref[0, h, :, pl.ds(start, TK)]
            out.append((m_news[h], jnp.exp(m - m_news[h]) * acc + _dot(vj, probs[h])))
        return tuple(out)

    init = tuple((jnp.full((1, tq), -jnp.inf, F32), jnp.zeros((SLAB, tq), F32)) for _ in range(hb))
    carry = lax.fori_loop(0, qi, lambda j, c: step(j, c, False), init)
    carry = step(qi, carry, True)
    outs = [carry[h][1][:HEAD_DIM, :] / carry[h][1][HEAD_DIM:HEAD_DIM + 1, :] for h in range(hb)]
    for pair in range(hb // 2):
        both = jnp.concatenate([outs[2 * pair], outs[2 * pair + 1]], axis=0)
        o_ref[0, :, pair * LANES:(pair + 1) * LANES] = both.T


def _attention(qp, kp, vp):
    batch, _, seq, _ = kp.shape
    hb = HEADS_PER_STEP
    kv_spec = pl.BlockSpec((1, hb, seq, SLAB), lambda b, h, i: (b, h, 0, 0))
    vt_spec = pl.BlockSpec((1, hb, SLAB, seq), lambda b, h, i: (b, h, 0, 0))
    return pl.pallas_call(
        _attn_kernel,
        out_shape=jax.ShapeDtypeStruct((batch, seq, D_ATT), F32),
        grid=(batch, N_HEADS // hb, seq // TQ),
        in_specs=[pl.BlockSpec((1, hb, TQ, SLAB), lambda b, h, i: (b, h, i, 0)), kv_spec, vt_spec],
        out_specs=pl.BlockSpec((1, TQ, hb * HEAD_DIM), lambda b, h, i: (b, i, h)),
        compiler_params=pltpu.CompilerParams(
            dimension_semantics=("arbitrary", "arbitrary", "arbitrary"),
            vmem_limit_bytes=VMEM_LIMIT),
        name="attn",
    )(qp, kp, vp)


def _out_proj_kernel(x_ref, rec_ref, att_ref, aw_ref, wr_ref, wa_ref, ln_ref, rwh_ref, rwl_ref, rb_ref,
                     x1_ref, xs_ref, meta_ref, units_ref):
    tm = x_ref.shape[0]

    att = att_ref[...]
    ms = jnp.sum(att * att, axis=-1, keepdims=True) * (1.0 / D_ATT)
    att_n = (att * lax.rsqrt(ms + EPS) * aw_ref[...]).astype(BF16)
    x1 = x_ref[...] + _dot(rec_ref[...], wr_ref[...]) + _dot(att_n, wa_ref[...])
    x1_ref[...] = x1

    ms2 = jnp.mean(x1 * x1, axis=-1, keepdims=True)
    h2 = x1 * lax.rsqrt(ms2 + EPS) * ln_ref[...]

    h_hi = h2.astype(BF16)
    h_lo = (h2 - h_hi.astype(F32)).astype(BF16)
    nt = (((1,), (1,)), ((), ()))
    logits = (lax.dot_general(rwh_ref[...], h_hi, nt, preferred_element_type=F32)
              + lax.dot_general(rwh_ref[...], h_lo, nt, preferred_element_type=F32)
              + lax.dot_general(rwl_ref[...], h_hi, nt, preferred_element_type=F32) + rb_ref[...])
    expert = lax.broadcasted_iota(jnp.int32, (LANES, tm), 0)
    expert_f = expert.astype(F32)
    l = jnp.where(expert < N_EXPERTS, logits, -jnp.inf)

    vals, idxs, sels = [], [], []
    for _ in range(TOP_K):
        m = jnp.max(l, axis=0, keepdims=True)
        idx = jnp.min(jnp.where(l == m, expert_f, float(LANES)), axis=0, keepdims=True)
        sel = expert_f == idx
        vals.append(m)
        idxs.append(idx)
        sels.append(sel)
        l = jnp.where(sel, -jnp.inf, l)
    exps = [jnp.exp(v - vals[0]) for v in vals]
    denom = exps[0] + exps[1] + exps[2] + exps[3]

    onehot = jnp.zeros((LANES, tm), F32)
    for sel in sels:
        onehot = jnp.where(sel, 1.0, onehot)
    row = lax.broadcasted_iota(jnp.int32, (tm, tm), 0)
    col = lax.broadcasted_iota(jnp.int32, (tm, tm), 1)
    earlier_token = (row < col).astype(BF16)
    rank = _dot(onehot.astype(BF16), earlier_token)
    count = jnp.sum(onehot, axis=1, keepdims=True)
    units = jnp.floor((count + (UNIT - 1)) * (1.0 / UNIT))
    e_row = lax.broadcasted_iota(jnp.int32, (LANES, LANES), 0)
    e_col = lax.broadcasted_iota(jnp.int32, (LANES, LANES), 1)
    earlier_expert = (e_col < e_row).astype(BF16)
    units_wide = jnp.broadcast_to(units, (LANES, LANES))
    first_unit = _dot(earlier_expert, units_wide.astype(BF16))[:, 0:1]
    units_ref[0] = units_wide.T[0:8, :]
    place = first_unit * float(UNIT) + rank

    meta_t = jnp.zeros((LANES, tm), F32)
    for k in range(TOP_K):
        slot = jnp.sum(jnp.where(sels[k], place, 0.0), axis=0, keepdims=True)
        meta_t = jnp.where(expert == k, idxs[k], meta_t)
        meta_t = jnp.where(expert == TOP_K + k, exps[k] / denom, meta_t)
        meta_t = jnp.where(expert == 2 * TOP_K + k, slot, meta_t)
    meta_ref[...] = meta_t.T

    slot_rows = [meta_t[2 * TOP_K + k:2 * TOP_K + k + 1, :].astype(jnp.int32) for k in range(TOP_K)]
    for c in range(TILE_SLOTS // SLOT_CHUNK):
        slot_id = lax.broadcasted_iota(jnp.int32, (SLOT_CHUNK, tm), 0) + c * SLOT_CHUNK
        hit = jnp.zeros((SLOT_CHUNK, tm), F32)
        for slot_row in slot_rows:
            hit = jnp.where(slot_id == slot_row, 1.0, hit)
        xs_ref[c * SLOT_CHUNK:(c + 1) * SLOT_CHUNK, :] = _dot(hit.astype(BF16), h_hi).astype(BF16)


def _out_proj(x2d, rec_n, att, aw_slab, w_out, ln2_w, rw_hi, rw_lo, rb_pad):
    assert D_REC == D_ATT
    t = x2d.shape[0]
    n_tiles = t // TM_OUT
    rows = lambda n: pl.BlockSpec((TM_OUT, n), lambda i: (i, 0))
    const = lambda shape: pl.BlockSpec(shape, lambda i: (0,) * len(shape))
    return pl.pallas_call(
        _out_proj_kernel,
        out_shape=(jax.ShapeDtypeStruct((t, D_MODEL), F32),
                   jax.ShapeDtypeStruct((n_tiles * TILE_SLOTS, D_MODEL), BF16),
                   jax.ShapeDtypeStruct((t, LANES), F32),
                   jax.ShapeDtypeStruct((n_tiles, 8, LANES), F32)),
        grid=(n_tiles,),
        in_specs=[rows(D_MODEL), rows(D_REC), rows(D_ATT), const((1, D_ATT)),
                  pl.BlockSpec((D_REC, D_MODEL), lambda i: (0, 0)),
                  pl.BlockSpec((D_ATT, D_MODEL), lambda i: (1, 0)), const((1, D_MODEL)),
                  const((LANES, D_MODEL)), const((LANES, D_MODEL)), const((LANES, 1))],
        out_specs=(rows(D_MODEL), pl.BlockSpec((TILE_SLOTS, D_MODEL), lambda i: (i, 0)), rows(LANES),
                   pl.BlockSpec((1, 8, LANES), lambda i: (i, 0, 0))),
        compiler_params=pltpu.CompilerParams(dimension_semantics=("arbitrary",),
                                             vmem_limit_bytes=VMEM_LIMIT),
        name="out_proj",
    )(x2d, rec_n, att, aw_slab, w_out, w_out, ln2_w, rw_hi, rw_lo, rb_pad)


def _start_units(unit_ref, base, n_units, src_hbm, dst, sem):
    def body(j, carry):
        src_row = pl.multiple_of(unit_ref[base + j] * UNIT, UNIT)
        dst_row = pl.multiple_of(j * UNIT, UNIT)
        pltpu.make_async_copy(src_hbm.at[pl.ds(src_row, UNIT), :],
                              dst.at[pl.ds(dst_row, UNIT), :], sem).start()
        return carry
    lax.fori_loop(0, n_units, body, 0, unroll=8)


def _wait_units(n_units, src_hbm, dst, sem):
    pltpu.make_async_copy(src_hbm.at[pl.ds(0, n_units * UNIT), :], dst, sem).wait()


def _experts_kernel(be_ref, nblk_ref, unit_ref, order_ref, next_ref, xs_hbm, wgu_hbm, bgu_ref, wd_hbm,
                    bd_ref, ys_ref, xbuf, wgu_f32, wd_f32, wgu_bf, wd_bf, sem, wsem):
    b = pl.program_id(0)
    nblk = nblk_ref[0]
    slot = b % 2
    e = be_ref[b]

    def weight_copies(expert, wslot):
        return (pltpu.make_async_copy(wgu_hbm.at[expert], wgu_f32.at[wslot], wsem.at[wslot]),
                pltpu.make_async_copy(wd_hbm.at[expert], wd_f32.at[wslot], wsem.at[wslot]))

    @pl.when(b == 0)
    def _():
        _start_units(unit_ref, 0, BLOCK_UNITS, xs_hbm, xbuf.at[0], sem.at[0])
        for copy in weight_copies(e, 0):
            copy.start()

    @pl.when(b + 1 < nblk)
    def _():
        _start_units(unit_ref, (b + 1) * BLOCK_UNITS, BLOCK_UNITS, xs_hbm, xbuf.at[1 - slot],
                     sem.at[1 - slot])

    new_expert = jnp.logical_or(b == 0, e != be_ref[jnp.maximum(b - 1, 0)])

    @pl.when(jnp.logical_and(b < nblk, new_expert))
    def _():
        wslot = order_ref[e] % 2
        for copy in weight_copies(e, wslot):
            copy.wait()
        wgu_bf[...] = wgu_f32[wslot].astype(BF16)
        wd_bf[...] = wd_f32[wslot].astype(BF16)
        nxt = next_ref[e]

        @pl.when(nxt >= 0)
        def _():
            for copy in weight_copies(nxt, 1 - wslot):
                copy.start()

    @pl.when(b < nblk)
    def _():
        _wait_units(BLOCK_UNITS, xs_hbm, xbuf.at[slot], sem.at[slot])
        gu = _dot(xbuf[slot], wgu_bf[...]) + bgu_ref[0]
        g = jnp.minimum(gu[:, :D_EXPERT], SWIGLU_LIMIT)
        u = jnp.clip(gu[:, D_EXPERT:], -SWIGLU_LIMIT, SWIGLU_LIMIT)
        act = (u + 1.0) * (g * _sigmoid(SWIGLU_ALPHA * g))
        ys_ref[...] = (_dot(act.astype(BF16), wd_bf[...]) + bd_ref[0]).astype(BF16)

    @pl.when(b >= nblk)
    def _():
        ys_ref[...] = jnp.zeros_like(ys_ref)


def _experts(block_expert, n_blocks_used, unit_src, expert_order, expert_next, xs, w_gate_up,
             b_gate_up, w_down, b_down):
    n_blocks = block_expert.shape[0]
    bias_map = lambda b, be, nb, us, eo, en: (be[b], 0, 0)
    grid_spec = pltpu.PrefetchScalarGridSpec(
        num_scalar_prefetch=5,
        grid=(n_blocks,),
        in_specs=[pl.BlockSpec(memory_space=pl.ANY),
                  pl.BlockSpec(memory_space=pl.ANY),
                  pl.BlockSpec((1, 1, 2 * D_EXPERT), bias_map),
                  pl.BlockSpec(memory_space=pl.ANY),
                  pl.BlockSpec((1, 1, D_MODEL), bias_map)],
        out_specs=pl.BlockSpec((TM_E, D_MODEL), lambda b, be, nb, us, eo, en: (b, 0)),
        scratch_shapes=[pltpu.VMEM((2, TM_E, D_MODEL), BF16),
                        pltpu.VMEM((2, D_MODEL, 2 * D_EXPERT), F32),
                        pltpu.VMEM((2, D_EXPERT, D_MODEL), F32),
                        pltpu.VMEM((D_MODEL, 2 * D_EXPERT), BF16),
                        pltpu.VMEM((D_EXPERT, D_MODEL), BF16),
                        pltpu.SemaphoreType.DMA((2,)),
                        pltpu.SemaphoreType.DMA((2,))],
    )
    return pl.pallas_call(
        _experts_kernel,
        out_shape=jax.ShapeDtypeStruct((n_blocks * TM_E, D_MODEL), BF16),
        grid_spec=grid_spec,
        compiler_params=pltpu.CompilerParams(dimension_semantics=("arbitrary",),
                                             vmem_limit_bytes=VMEM_LIMIT),
        name="experts",
    )(block_expert, n_blocks_used, unit_src, expert_order, expert_next, xs, w_gate_up,
      b_gate_up.reshape(N_EXPERTS, 1, 2 * D_EXPERT), w_down, b_down.reshape(N_EXPERTS, 1, D_MODEL))


def _combine_kernel(unit_ref, x1_ref, meta_ref, ys_hbm, o_ref, buf, sem):
    i = pl.program_id(0)
    n = pl.num_programs(0)
    slot = i % 2
    tm = x1_ref.shape[0]

    @pl.when(i == 0)
    def _():
        _start_units(unit_ref, 0, TILE_UNITS, ys_hbm, buf.at[0], sem.at[0])

    @pl.when(i + 1 < n)
    def _():
        _start_units(unit_ref, (i + 1) * TILE_UNITS, TILE_UNITS, ys_hbm, buf.at[1 - slot],
                     sem.at[1 - slot])

    meta = meta_ref[...]
    slots = [meta[:, 2 * TOP_K + k:2 * TOP_K + k + 1].astype(jnp.int32) for k in range(TOP_K)]
    gates = [meta[:, TOP_K + k:TOP_K + k + 1] for k in range(TOP_K)]
    _wait_units(TILE_UNITS, ys_hbm, buf.at[slot], sem.at[slot])

    acc = x1_ref[...]
    for c in range(TILE_SLOTS // SLOT_CHUNK):
        first = c * SLOT_CHUNK
        col = lax.broadcasted_iota(jnp.int32, (tm, SLOT_CHUNK), 1) + first
        weight = jnp.zeros((tm, SLOT_CHUNK), F32)
        for k in range(TOP_K):
            weight = jnp.where(col == slots[k], gates[k], weight)
        acc = acc + _dot(weight.astype(BF16), buf[slot, first:first + SLOT_CHUNK, :])
    o_ref[...] = acc


def _combine(tile_units, x1, meta, ys):
    t = x1.shape[0]
    grid_spec = pltpu.PrefetchScalarGridSpec(
        num_scalar_prefetch=1,
        grid=(t // TM_OUT,),
        in_specs=[pl.BlockSpec((TM_OUT, D_MODEL), lambda i, us: (i, 0)),
                  pl.BlockSpec((TM_OUT, LANES), lambda i, us: (i, 0)),
                  pl.BlockSpec(memory_space=pl.ANY)],
        out_specs=pl.BlockSpec((TM_OUT, D_MODEL), lambda i, us: (i, 0)),
        scratch_shapes=[pltpu.VMEM((2, TILE_SLOTS, D_MODEL), BF16),
                        pltpu.SemaphoreType.DMA((2,))],
    )
    return pl.pallas_call(
        _combine_kernel,
        out_shape=jax.ShapeDtypeStruct((t, D_MODEL), F32),
        grid_spec=grid_spec,
        compiler_params=pltpu.CompilerParams(dimension_semantics=("arbitrary",),
                                             vmem_limit_bytes=VMEM_LIMIT),
        name="combine",
    )(tile_units, x1, meta, ys)


def _gate_matrix(w_a, w_i):
    g = w_a.shape[0]
    eye = jnp.eye(g, dtype=w_a.dtype)
    both = jnp.stack([w_a, w_i])
    return jnp.einsum('sgij,gh->gishj', both, eye).reshape(g * REC_BLOCK, 2 * g * REC_BLOCK)


def _pad_lanes(v, n=LANES):
    return jnp.pad(v, (0, n - v.shape[0])).reshape(1, n)


def _excl_cumsum(a, axis):
    return jnp.cumsum(a, axis=axis) - a


def _pick(onehot, table):
    if table.ndim == 1:
        return jnp.sum(onehot * table[None, :], axis=1)
    return jnp.sum(onehot[:, :, None] * table[None, :, :], axis=1)


def _dispatch_tables(units, n_blocks):
    n_tiles = units.shape[0]
    i32 = jnp.int32
    tile_first = _excl_cumsum(units, 1)
    tile_end = tile_first + units
    run_first = _excl_cumsum(units, 0)
    run_end = run_first + units
    region_units = jnp.sum(units, axis=0)
    region_blocks = (region_units + BLOCK_UNITS - 1) // BLOCK_UNITS
    block_end = jnp.cumsum(region_blocks)
    region_first = (block_end - region_blocks) * BLOCK_UNITS
    n_used = block_end[-1:]
    experts = jnp.arange(N_EXPERTS, dtype=i32)
    tiles = jnp.arange(n_tiles, dtype=i32)

    blocks = jnp.arange(n_blocks, dtype=i32)
    block_expert = jnp.minimum(
        jnp.sum((block_end[None, :] <= blocks[:, None]).astype(i32), axis=1), N_EXPERTS - 1)

    g = jnp.arange(n_blocks * BLOCK_UNITS, dtype=i32).reshape(n_blocks, BLOCK_UNITS)
    e_hot = (block_expert[:, None] == experts[None, :]).astype(i32)
    local = g - _pick(e_hot, region_first)[:, None]
    valid = (local < _pick(e_hot, region_units)[:, None]) & (g < n_used[0] * BLOCK_UNITS)
    ends = _pick(e_hot, run_end.T)
    tile = jnp.minimum(jnp.sum((ends[:, None, :] <= local[:, :, None]).astype(i32), axis=2),
                       n_tiles - 1)
    t_hot = (tile[:, :, None] == tiles[None, None, :]).astype(i32)
    run_shift = _pick(e_hot, (tile_first - run_first).T)
    src = tile * TILE_UNITS + jnp.sum(t_hot * run_shift[:, None, :], axis=2) + local
    unit_src = jnp.where(valid, src, 0).astype(i32).reshape(-1)

    j = jnp.arange(TILE_UNITS, dtype=i32)
    ej = jnp.minimum(jnp.sum((tile_end[:, None, :] <= j[None, :, None]).astype(i32), axis=2),
                     N_EXPERTS - 1)
    ej_hot = (ej[:, :, None] == experts[None, None, :]).astype(i32)
    used = j[None, :] < jnp.sum(units, axis=1)[:, None]
    base = region_first[None, :] + run_first - tile_first
    dst = jnp.sum(ej_hot * base[:, None, :], axis=2) + j[None, :]
    tile_units = jnp.where(used, dst, 0).astype(i32).reshape(-1)
    owns = region_blocks > 0
    expert_order = _excl_cumsum(owns.astype(i32), 0)
    later = (experts[None, :] > experts[:, None]) & owns[None, :]
    expert_next = jnp.min(jnp.where(later, experts[None, :], N_EXPERTS), axis=1)
    expert_next = jnp.where(expert_next < N_EXPERTS, expert_next, -1).astype(i32)
    return (block_expert.astype(i32), n_used.astype(i32), unit_src, tile_units,
            expert_order.astype(i32), expert_next)


def _layer(x, ln1_w, w_in, conv_w, conv_b, w_rg_a, b_rg_a, w_rg_i, b_rg_i, rg_lambda, b_forget,
           q_norm_w, k_norm_w, rec_out_norm_w, attn_out_norm_w, w_out, ln2_w, w_router, b_router,
           w_gate_up, b_gate_up, w_down, b_down):
    batch, seq, _ = x.shape
    t = batch * seq
    x2d = x.reshape(t, D_MODEL)

    o_f = N_PROJ + 3 * D_ATT
    w_t = w_in.T.astype(BF16)
    w_ft = jnp.pad(w_t[o_f:], ((0, LANES - N_HEADS), (0, 0)))
    scale = HEAD_DIM ** -0.5
    u, g, qp, kp, vp = _in_proj(
        x2d, ln1_w.reshape(1, D_MODEL), w_t, w_ft, _pad_lanes(b_forget).reshape(LANES, 1),
        (q_norm_w * scale).reshape(HEAD_DIM, 1), k_norm_w.reshape(HEAD_DIM, 1), batch, seq)

    rec_vec = jnp.concatenate(
        [conv_w, jnp.stack([conv_b, b_rg_a, b_rg_i, rg_lambda, rec_out_norm_w])])
    rec_n = _rglru(u.reshape(batch, seq, D_REC), g.reshape(batch, seq, D_REC), rec_vec,
                   _gate_matrix(w_rg_a, w_rg_i).astype(BF16))

    att = _attention(qp, kp, vp)

    rw = jnp.pad(w_router.T, ((0, LANES - N_EXPERTS), (0, 0)))
    rw_hi = rw.astype(BF16)
    rw_lo = (rw - rw_hi.astype(F32)).astype(BF16)
    x1, xs, meta, units = _out_proj(
        x2d, rec_n.reshape(t, D_REC), att.reshape(t, D_ATT), attn_out_norm_w.reshape(1, D_ATT),
        w_out.astype(BF16), ln2_w.reshape(1, D_MODEL),
        rw_hi, rw_lo, _pad_lanes(b_router).reshape(LANES, 1))

    n_tiles = t // TM_OUT
    max_units = t * TOP_K // UNIT + n_tiles * N_EXPERTS
    n_blocks = -(-max_units // BLOCK_UNITS) + N_EXPERTS
    block_expert, n_used, unit_src, tile_units, expert_order, expert_next = _dispatch_tables(
        units[:, 0, :N_EXPERTS].astype(jnp.int32), n_blocks)

    ys = _experts(block_expert, n_used, unit_src, expert_order, expert_next, xs, w_gate_up, b_gate_up,
                  w_down, b_down)
    out = _combine(tile_units, x1, meta, ys)
    return out.reshape(batch, seq, D_MODEL)


def kernel(x, ln1_w, w_in, conv_w, conv_b, w_rg_a, b_rg_a, w_rg_i, b_rg_i, rg_lambda, b_forget,
           q_norm_w, k_norm_w, rec_out_norm_w, attn_out_norm_w, w_out, ln2_w, w_router, b_router,
           w_gate_up, b_gate_up, w_down, b_down):
    for l in range(ln1_w.shape[0]):
        x = _layer(x, ln1_w[l], w_in[l], conv_w[l], conv_b[l], w_rg_a[l], b_rg_a[l], w_rg_i[l],
                   b_rg_i[l], rg_lambda[l], b_forget[l], q_norm_w[l], k_norm_w[l], rec_out_norm_w[l],
                   attn_out_norm_w[l], w_out[l], ln2_w[l], w_router[l], b_router[l], w_gate_up[l],
                   b_gate_up[l], w_down[l], b_down[l])
    return x
```
